```python
import math
import jax, jax.numpy as jnp
from jax import lax
import numpy as np

D_MODEL = 1024
BATCH = 8
SEQ = 2048
DEPTH = 1
DEC_BATCH = 128
DEC_SEQ = 8
PAST_LEN = 16384
PAGE_SIZE = 128

D_MIX = D_MODEL
D_REC = D_MIX // 2
D_CONV = D_MIX - D_REC
H_REC = 4
DK = D_REC // H_REC
DV = D_REC // H_REC
CONV_W = 3
D_FF = 4 * D_MODEL
CHUNK = 64
N_MOD = 6
N_IN = 4 * D_REC + 3 * D_CONV
SPLITS = (D_REC, 2 * D_REC, 3 * D_REC, 4 * D_REC,
          4 * D_REC + D_CONV, 4 * D_REC + 2 * D_CONV)
EPS = 1e-6

kernel_name = "hymba_hgrn2_shortconv_adaln_step"


def rms_norm(x):
    xf = x.astype(jnp.float32)
    y = xf * lax.rsqrt(jnp.mean(xf * xf, axis=-1, keepdims=True) + EPS)
    return y.astype(x.dtype)


def hgrn2_chunked(q, k, logf, v, s0):
    b, l, h, dk = q.shape
    dv = v.shape[-1]
    c = math.gcd(l, CHUNK)
    n = l // c
    f32 = jnp.float32
    q = q.astype(f32).reshape(b, n, c, h, dk)
    k = k.astype(f32).reshape(b, n, c, h, dk)
    logf = logf.astype(f32).reshape(b, n, c, h, dk)
    v = v.astype(f32).reshape(b, n, c, h, dv)
    cum = jnp.cumsum(logf, axis=2)
    last = cum[:, :, -1:]
    q_dec = q * jnp.exp(cum)
    k_dec = k * jnp.exp(-cum)
    scores = jnp.einsum("bntHk,bnsHk->bnHts", q_dec, k_dec)
    causal = jnp.tril(jnp.ones((c, c), dtype=bool))
    scores = jnp.where(causal, scores, 0.0)
    o_intra = jnp.einsum("bnHts,bnsHv->bntHv", scores, v)
    k_end = k * jnp.exp(last - cum)
    upd = jnp.einsum("bnsHk,bnsHv->nbHkv", k_end, v)
    decay = jnp.exp(last[:, :, 0]).transpose(1, 0, 2, 3)

    def step(s, inp):
        d, u = inp
        return d[..., None] * s + u, s

    s_fin, s_prev = lax.scan(step, s0.astype(f32), (decay, upd))
    o_inter = jnp.einsum("bntHk,nbHkv->bntHv", q_dec, s_prev)
    o = (o_intra + o_inter).reshape(b, l, h, dv)
    return o, s_fin


def short_conv(u, buf, w):
    l = u.shape[1]
    full = jnp.concatenate([buf.astype(u.dtype), u], axis=1)
    y = w[0] * full[:, 0:l]
    for j in range(1, CONV_W):
        y = y + w[j] * full[:, j:j + l]
    return y, full[:, -(CONV_W - 1):]


def hybrid_layer(x, c, s_rec, s_conv, lb, w_ada, b_ada, w_in, w_conv, g_onorm,
                 w_out, w_up, w_down):
    bsz, l, _ = x.shape
    mod = jax.nn.silu(c) @ w_ada + b_ada
    sh1, sc1, g1, sh2, sc2, g2 = jnp.split(mod[:, None, :], N_MOD, axis=-1)

    h = rms_norm(x) * (1 + sc1) + sh1
    proj = h @ w_in
    q, fz, i, g, gb, gc, hv = jnp.split(proj, SPLITS, axis=-1)

    fz32 = fz.astype(jnp.float32).reshape(bsz, l, H_REC, DK)
    lb_h = lb.reshape(H_REC, DK)
    logf = jnp.log(lb_h + (1.0 - lb_h) * jax.nn.sigmoid(fz32))
    k = (1.0 - lb_h) * jax.nn.sigmoid(-fz32)
    o_rec, s_rec_new = hgrn2_chunked(q.reshape(bsz, l, H_REC, DK), k, logf,
                                     i.reshape(bsz, l, H_REC, DV), s_rec)
    o_rec = o_rec * lax.rsqrt(jnp.mean(o_rec * o_rec, axis=-1, keepdims=True) + EPS)
    o_rec = (o_rec.reshape(bsz, l, D_REC) * g_onorm).astype(x.dtype) * jax.nn.silu(g)

    y_conv, s_conv_new = short_conv(gc * hv, s_conv, w_conv)
    o_conv = gb * y_conv

    mix = jnp.concatenate([o_rec, o_conv], axis=-1) @ w_out
    x = x + g1 * mix

    h2 = rms_norm(x) * (1 + sc2) + sh2
    x = x + g2 * (jnp.square(jax.nn.relu(h2 @ w_up)) @ w_down)
    return x, s_rec_new.astype(s_rec.dtype), s_conv_new.astype(s_conv.dtype)


def setup_inputs(seed: int = 0) -> dict:
    key = jax.random.key(seed)
    ks = jax.random.split(key, 16)
    f32 = jnp.float32
    nrm = lambda k, shp, s: jax.random.normal(k, shp, f32) * s
    return {
        "x_prompt": nrm(ks[0], (BATCH, SEQ, D_MODEL), 1.0),
        "x_sample": nrm(ks[1], (DEC_BATCH, DEC_SEQ, D_MODEL), 1.0),
        "state_rec": nrm(ks[2], (DEPTH, DEC_BATCH, H_REC, DK, DV), 1.0),
        "state_conv": nrm(ks[3], (DEPTH, DEC_BATCH, CONV_W - 1, D_CONV), 1.0),
        "c_prompt": nrm(ks[4], (BATCH, D_MODEL), 1.0),
        "c_sample": nrm(ks[5], (DEC_BATCH, D_MODEL), 1.0),
        "lower_bounds": nrm(ks[6], (DEPTH + 1, D_REC), 0.1),
        "w_ada": nrm(ks[7], (DEPTH, D_MODEL, N_MOD * D_MODEL), 0.5 * D_MODEL ** -0.5),
        "b_ada": nrm(ks[8], (DEPTH, N_MOD * D_MODEL), 0.02),
        "w_in": nrm(ks[9], (DEPTH, D_MODEL, N_IN), D_MODEL ** -0.5),
        "w_conv": nrm(ks[10], (DEPTH, CONV_W, D_CONV), CONV_W ** -0.5),
        "g_onorm": 1.0 + nrm(ks[11], (DEPTH, D_REC), 0.02),
        "w_out": nrm(ks[12], (DEPTH, D_MIX, D_MODEL), D_MIX ** -0.5),
        "w_up": nrm(ks[13], (DEPTH, D_MODEL, D_FF), D_MODEL ** -0.5),
        "w_down": nrm(ks[14], (DEPTH, D_FF, D_MODEL), D_FF ** -0.5),
        "g_final": 1.0 + nrm(ks[15], (D_MODEL,), 0.02),
    }


def reference(x_prompt, x_sample, state_rec, state_conv, c_prompt, c_sample,
              lower_bounds, w_ada, b_ada, w_in, w_conv, g_onorm, w_out, w_up,
              w_down, g_final):
    lbs = jnp.cumsum(jax.nn.softmax(lower_bounds.astype(jnp.float32), axis=0), axis=0)
    xp, xs = x_prompt, x_sample
    rec_p, conv_p, rec_s, conv_s = [], [], [], []
    for layer in range(DEPTH):
        wl = (w_ada[layer], b_ada[layer], w_in[layer], w_conv[layer], g_onorm[layer],
              w_out[layer], w_up[layer], w_down[layer])
        s_rec0 = jnp.zeros((xp.shape[0], H_REC, DK, DV), state_rec.dtype)
        s_conv0 = jnp.zeros((xp.shape[0], CONV_W - 1, D_CONV), state_conv.dtype)
        xp, sr_p, sc_p = hybrid_layer(xp, c_prompt, s_rec0, s_conv0, lbs[layer], *wl)
        xs, sr_s, sc_s = hybrid_layer(xs, c_sample, state_rec[layer], state_conv[layer],
                                      lbs[layer], *wl)
        rec_p.append(sr_p)
        conv_p.append(sc_p)
        rec_s.append(sr_s)
        conv_s.append(sc_s)
    y_prompt = rms_norm(xp) * g_final
    y_sample = rms_norm(xs) * g_final
    new_rec_prompt = jnp.stack(rec_p)
    new_conv_prompt = jnp.stack(conv_p)
    new_rec_sample = jnp.stack(rec_s)
    new_conv_sample = jnp.stack(conv_s)
    return (y_prompt, y_sample, new_rec_prompt, new_conv_prompt, new_rec_sample, new_conv_sample)
```

```python
import functools

import jax
import jax.numpy as jnp
from jax import lax
from jax.experimental import pallas as pl
from jax.experimental.pallas import tpu as pltpu

F32 = jnp.float32
BF16 = jnp.bfloat16

EPS = 1e-6
H_REC = 4
CONV_W = 3
N_MOD = 6
CHUNK = 64

SUBLANES = 8
VMEM_LIMIT_BYTES = 56 * 1024 * 1024

PROMPT_TILE = 256
SAMPLE_TILE_B = 16
MOD_TILE_N = 1024


def _const_spec(shape):
    zeros = (0,) * len(shape)
    return pl.BlockSpec(shape, lambda *_: zeros, pipeline_mode=pl.Buffered(1))


def _rms(x):
    return x * lax.rsqrt(jnp.mean(x * x, axis=-1, keepdims=True) + EPS)


def _silu(x):
    return x * jax.nn.sigmoid(x)


def _bdot(a, b):
    return jnp.dot(a, b, preferred_element_type=F32)


def _dot_nt(a, b):
    return lax.dot_general(a, b, (((1,), (1,)), ((), ())), preferred_element_type=F32)


def _dot_tn(a, b):
    return lax.dot_general(a, b, (((0,), (0,)), ((), ())), preferred_element_type=F32)


def _exact_dot(a, b):
    return jnp.dot(a, b, preferred_element_type=F32, precision=lax.Precision.HIGHEST)


def _layer_lower_bound(lb_ref, layer):
    lb = lb_ref[...]
    e = jnp.exp(lb - jnp.max(lb, axis=0, keepdims=True))
    sm = e / jnp.sum(e, axis=0, keepdims=True)
    return jnp.sum(sm[: layer + 1], axis=0, keepdims=True)


def _gates(fz, lb):
    logf = jnp.log(lb + (1.0 - lb) * jax.nn.sigmoid(fz))
    k = (1.0 - lb) * jax.nn.sigmoid(-fz)
    return logf, k


def _head_out(o, g, g_onorm):
    dv = o.shape[-1] // H_REC
    pieces = []
    for h in range(H_REC):
        sl = slice(h * dv, (h + 1) * dv)
        pieces.append(_rms(o[:, sl]))
    return (jnp.concatenate(pieces, axis=-1) * g_onorm) * _silu(g)


def _mlp_tail(x, mixed, g1, sh2, sc2, g2, w_out_ref, w_up_ref, w_down_ref, g_final_ref):
    x = x + g1 * _bdot(mixed.astype(BF16), w_out_ref[...])
    h2 = _rms(x) * (1.0 + sc2) + sh2
    up = jnp.maximum(_bdot(h2.astype(BF16), w_up_ref[...]), 0.0)
    x = x + g2 * _bdot((up * up).astype(BF16), w_down_ref[...])
    return _rms(x) * g_final_ref[...]


def _mod_kernel(c_ref, w_ref, b_ref, o_ref):
    s = _silu(c_ref[...]).astype(BF16)
    o_ref[...] = _bdot(s, w_ref[...].astype(BF16)) + b_ref[...]


def _modulation(c_all, w_ada, b_ada):
    rows, d = c_all.shape
    n = w_ada.shape[1]
    return pl.pallas_call(
        _mod_kernel,
        grid=(n // MOD_TILE_N,),
        in_specs=[
            pl.BlockSpec((rows, d), lambda j: (0, 0)),
            pl.BlockSpec((d, MOD_TILE_N), lambda j: (0, j)),
            pl.BlockSpec((1, MOD_TILE_N), lambda j: (0, j)),
        ],
        out_specs=pl.BlockSpec((rows, MOD_TILE_N), lambda j: (0, j)),
        out_shape=jax.ShapeDtypeStruct((rows, n), F32),
        compiler_params=pltpu.CompilerParams(
            dimension_semantics=("arbitrary",), vmem_limit_bytes=VMEM_LIMIT_BYTES),
        name="adaln_modulation",
    )(c_all, w_ada, b_ada)


def _prompt_kernel(x_ref, mod_ref, lb_ref, w_in_ref, w_conv_ref, g_onorm_ref, w_out_ref,
                   w_up_ref, w_down_ref, g_final_ref,
                   y_ref, rec_ref, conv_ref,
                   st_ref, ubuf_ref, mix_ref, *, n_tiles):
    l = pl.program_id(1)
    tl, d = x_ref.shape
    d_rec = lb_ref.shape[1]
    dk = d_rec // H_REC

    @pl.when(l == 0)
    def _():
        st_ref[...] = jnp.zeros_like(st_ref)
        ubuf_ref[0:SUBLANES, :] = jnp.zeros((SUBLANES, ubuf_ref.shape[1]), F32)

    mod = mod_ref[...]
    sh1, sc1, g1, sh2, sc2, g2 = (mod[:, i * d:(i + 1) * d] for i in range(N_MOD))

    x = x_ref[...]
    h = _rms(x) * (1.0 + sc1) + sh1
    proj = _bdot(h.astype(BF16), w_in_ref[...])
    q, fz, iv, g = (proj[:, i * d_rec:(i + 1) * d_rec] for i in range(4))
    d_conv = (proj.shape[1] - 4 * d_rec) // 3
    gb, gc, hv = (proj[:, 4 * d_rec + i * d_conv:4 * d_rec + (i + 1) * d_conv] for i in range(3))

    lb = _layer_lower_bound(lb_ref, 0)
    row = lax.broadcasted_iota(jnp.int32, (CHUNK, CHUNK), 0)
    col = lax.broadcasted_iota(jnp.int32, (CHUNK, CHUNK), 1)
    causal = row >= col
    tril = causal.astype(F32)
    o_chunks = []
    for c0 in range(0, tl, CHUNK):
        rows = slice(c0, c0 + CHUNK)
        logf, k = _gates(fz[rows], lb)
        cum = _exact_dot(tril, logf)
        last = cum[CHUNK - 1:CHUNK, :]
        q_dec = (q[rows] * jnp.exp(cum)).astype(BF16)
        k_dec = (k * jnp.exp(-cum)).astype(BF16)
        k_end = (k * jnp.exp(last - cum)).astype(BF16)
        decay = jnp.exp(last)
        v = iv[rows]
        o_heads = []
        for hd in range(H_REC):
            sl = slice(hd * dk, (hd + 1) * dk)
            scores = jnp.where(causal, _dot_nt(q_dec[:, sl], k_dec[:, sl]), 0.0)
            st = st_ref[hd]
            o_heads.append(_bdot(scores.astype(BF16), v[:, sl].astype(BF16))
                           + _dot_nt(q_dec[:, sl], st.astype(BF16)))
            st_ref[hd] = st * decay[:, sl] + _dot_tn(v[:, sl].astype(BF16), k_end[:, sl])
        o_chunks.append(jnp.concatenate(o_heads, axis=-1))
    o_rec = _head_out(jnp.concatenate(o_chunks, axis=0), g, g_onorm_ref[...])
    mix_ref[:, 0:d_rec] = o_rec

    u = gc * hv
    ubuf_ref[SUBLANES:SUBLANES + tl, :] = u
    w_conv = w_conv_ref[...]
    y_conv = (w_conv[0:1] * ubuf_ref[SUBLANES - 2:SUBLANES - 2 + tl, :]
              + w_conv[1:2] * ubuf_ref[SUBLANES - 1:SUBLANES - 1 + tl, :]
              + w_conv[2:3] * u)
    mix_ref[:, d_rec:d_rec + d_conv] = gb * y_conv
    tail = u[tl - (CONV_W - 1):tl]
    ubuf_ref[SUBLANES - (CONV_W - 1):SUBLANES, :] = tail

    y_ref[...] = _mlp_tail(x, mix_ref[...], g1, sh2, sc2, g2,
                           w_out_ref, w_up_ref, w_down_ref, g_final_ref)

    @pl.when(l == n_tiles - 1)
    def _():
        for hd in range(H_REC):
            rec_ref[hd] = st_ref[hd].T
        conv_ref[...] = tail


def _prompt_layer(x, mod_p, lower_bounds, w_in, w_conv, g_onorm, w_out, w_up, w_down, g_final):
    bsz, seq, d = x.shape
    d_rec = lower_bounds.shape[1]
    dk = d_rec // H_REC
    d_conv = w_conv.shape[1]
    tl = PROMPT_TILE
    n_tiles = seq // tl
    assert seq % tl == 0 and tl % CHUNK == 0
    return pl.pallas_call(
        functools.partial(_prompt_kernel, n_tiles=n_tiles),
        grid=(bsz, n_tiles),
        in_specs=[
            pl.BlockSpec((None, tl, d), lambda b, l: (b, l, 0)),
            pl.BlockSpec((None, 1, N_MOD * d), lambda b, l: (b, 0, 0)),
            _const_spec(lower_bounds.shape),
            _const_spec(w_in.shape),
            _const_spec(w_conv.shape),
            _const_spec(g_onorm.shape),
            _const_spec(w_out.shape),
            _const_spec(w_up.shape),
            _const_spec(w_down.shape),
            _const_spec(g_final.shape),
        ],
        out_specs=[
            pl.BlockSpec((None, tl, d), lambda b, l: (b, l, 0)),
            pl.BlockSpec((None, H_REC, dk, dk), lambda b, l: (b, 0, 0, 0)),
            pl.BlockSpec((None, CONV_W - 1, d_conv), lambda b, l: (b, 0, 0)),
        ],
        out_shape=[
            jax.ShapeDtypeStruct((bsz, seq, d), F32),
            jax.ShapeDtypeStruct((bsz, H_REC, dk, dk), F32),
            jax.ShapeDtypeStruct((bsz, CONV_W - 1, d_conv), F32),
        ],
        scratch_shapes=[
            pltpu.VMEM((H_REC, dk, dk), F32),
            pltpu.VMEM((SUBLANES + tl, d_conv), F32),
            pltpu.VMEM((tl, d_rec + d_conv), F32),
        ],
        compiler_params=pltpu.CompilerParams(
            dimension_semantics=("arbitrary", "arbitrary"), vmem_limit_bytes=VMEM_LIMIT_BYTES),
        name="prompt_layer",
    )(x, mod_p, lower_bounds, w_in, w_conv, g_onorm, w_out, w_up, w_down, g_final)


def _sample_kernel(x_ref, mod_ref, lb_ref, w_in_ref, w_conv_ref, g_onorm_ref, w_out_ref,
                   w_up_ref, w_down_ref, g_final_ref, rec_in_ref, conv_in_ref,
                   y_ref, rec_ref, conv_ref,
                   modx_ref, cbx_ref, ubuf_ref, mix_ref, *, seq):
    rows, d = x_ref.shape
    tb = rows // seq
    d_rec = lb_ref.shape[1]
    dk = d_rec // H_REC
    d_conv = conv_in_ref.shape[2]

    for b in range(tb):
        r = slice(b * seq, (b + 1) * seq)
        modx_ref[r, :] = jnp.broadcast_to(mod_ref[b:b + 1, :], (seq, mod_ref.shape[1]))
        for j in range(CONV_W - 1):
            cbx_ref[j, r, :] = jnp.broadcast_to(conv_in_ref[b, j:j + 1, :], (seq, d_conv))

    sh1, sc1, g1, sh2, sc2, g2 = (modx_ref[:, i * d:(i + 1) * d] for i in range(N_MOD))

    x = x_ref[...]
    h = _rms(x) * (1.0 + sc1) + sh1
    proj = _bdot(h.astype(BF16), w_in_ref[...])
    q, fz, iv, g = (proj[:, i * d_rec:(i + 1) * d_rec] for i in range(4))
    gb, gc, hv = (proj[:, 4 * d_rec + i * d_conv:4 * d_rec + (i + 1) * d_conv] for i in range(3))

    lb = _layer_lower_bound(lb_ref, 0)
    row = lax.broadcasted_iota(jnp.int32, (rows, rows), 0)
    col = lax.broadcasted_iota(jnp.int32, (rows, rows), 1)
    same = (row // seq) == (col // seq)
    causal = same & (row >= col)
    logf, k = _gates(fz, lb)
    cum = _exact_dot(causal.astype(F32), logf)
    last = _exact_dot(same.astype(F32), logf)
    q_dec = (q * jnp.exp(cum)).astype(BF16)
    k_dec = (k * jnp.exp(-cum)).astype(BF16)
    k_end = k * jnp.exp(last - cum)
    decay = jnp.exp(last)
    vb = iv.astype(BF16)

    d_hi = decay.astype(BF16).astype(F32)
    r1 = decay - d_hi
    d_mid = r1.astype(BF16).astype(F32)
    d_lo = r1 - d_mid
    tok_rec = lax.broadcasted_iota(jnp.int32, (rows, d_rec), 0) % seq
    dec3 = jnp.where(tok_rec == 0, d_hi,
                     jnp.where(tok_rec == 1, d_mid, jnp.where(tok_rec == 2, d_lo, 0.0)))
    sub = lax.broadcasted_iota(jnp.int32, (seq, dk), 0)
    ones3 = jnp.where(sub < 3, 1.0, 0.0)
    zeros_blk = jnp.zeros((seq, dk), F32)
    rhs_bottom = jnp.concatenate([zeros_blk, ones3], axis=1)

    o_intra = []
    for hd in range(H_REC):
        sl = slice(hd * dk, (hd + 1) * dk)
        scores = jnp.where(causal, _dot_nt(q_dec[:, sl], k_dec[:, sl]), 0.0)
        o_intra.append(_bdot(scores.astype(BF16), vb[:, sl]))
    o_intra = jnp.concatenate(o_intra, axis=-1)

    o_rows = []
    for b in range(tb):
        r0 = b * seq
        r = slice(r0, r0 + seq)
        o_heads = []
        for hd in range(H_REC):
            sl = slice(hd * dk, (hd + 1) * dk)
            s0 = rec_in_ref[b, hd]
            o_heads.append(_bdot(q_dec[r, sl], s0.astype(BF16)))
            lhs = jnp.concatenate([k_end[r, sl], dec3[r, sl]], axis=0)
            rhs = jnp.concatenate(
                [jnp.concatenate([iv[r, sl], zeros_blk], axis=1), rhs_bottom], axis=0)
            both = _dot_tn(lhs.astype(BF16), rhs.astype(BF16))
            rec_ref[b, hd] = both[:, dk:] * s0 + both[:, :dk]
        o_rows.append(jnp.concatenate(o_heads, axis=-1))
    o_rec = _head_out(o_intra + jnp.concatenate(o_rows, axis=0), g, g_onorm_ref[...])
    mix_ref[:, 0:d_rec] = o_rec

    u = gc * hv
    ubuf_ref[SUBLANES:SUBLANES + rows, :] = u
    tok = lax.broadcasted_iota(jnp.int32, (rows, d_conv), 0) % seq
    u_m1 = jnp.where(tok >= 1, ubuf_ref[SUBLANES - 1:SUBLANES - 1 + rows, :], cbx_ref[1])
    u_m2 = jnp.where(tok >= 2, ubuf_ref[SUBLANES - 2:SUBLANES - 2 + rows, :],
                     jnp.where(tok == 1, cbx_ref[1], cbx_ref[0]))
    w_conv = w_conv_ref[...]
    y_conv = w_conv[0:1] * u_m2 + w_conv[1:2] * u_m1 + w_conv[2:3] * u
    mix_ref[:, d_rec:d_rec + d_conv] = gb * y_conv
    for b in range(tb):
        conv_ref[b] = u[(b + 1) * seq - (CONV_W - 1):(b + 1) * seq]

    y_ref[...] = _mlp_tail(x, mix_ref[...], g1, sh2, sc2, g2,
                           w_out_ref, w_up_ref, w_down_ref, g_final_ref)


def _sample_layer(x2d, seq, mod, lower_bounds, w_in, w_conv, g_onorm, w_out, w_up, w_down,
                  g_final, rec_in, conv_in):
    n_rows, d = x2d.shape
    bsz = n_rows // seq
    d_rec = lower_bounds.shape[1]
    dk = d_rec // H_REC
    d_conv = w_conv.shape[1]
    tb = SAMPLE_TILE_B
    rows = tb * seq
    assert bsz % tb == 0 and seq == SUBLANES and seq >= CONV_W
    return pl.pallas_call(
        functools.partial(_sample_kernel, seq=seq),
        grid=(bsz // tb,),
        in_specs=[
            pl.BlockSpec((rows, d), lambda i: (i, 0)),
            pl.BlockSpec((tb, N_MOD * d), lambda i: (i, 0)),
            _const_spec(lower_bounds.shape),
            _const_spec(w_in.shape),
            _const_spec(w_conv.shape),
            _const_spec(g_onorm.shape),
            _const_spec(w_out.shape),
            _const_spec(w_up.shape),
            _const_spec(w_down.shape),
            _const_spec(g_final.shape),
            pl.BlockSpec((tb, H_REC, dk, dk), lambda i: (i, 0, 0, 0)),
            pl.BlockSpec((tb, CONV_W - 1, d_conv), lambda i: (i, 0, 0)),
        ],
        out_specs=[
            pl.BlockSpec((rows, d), lambda i: (i, 0)),
            pl.BlockSpec((tb, H_REC, dk, dk), lambda i: (i, 0, 0, 0)),
            pl.BlockSpec((tb, CONV_W - 1, d_conv), lambda i: (i, 0, 0)),
        ],
        out_shape=[
            jax.ShapeDtypeStruct((n_rows, d), F32),
            jax.ShapeDtypeStruct((bsz, H_REC, dk, dk), F32),
            jax.ShapeDtypeStruct((bsz, CONV_W - 1, d_conv), F32),
        ],
        scratch_shapes=[
            pltpu.VMEM((rows, N_MOD * d), F32),
            pltpu.VMEM((CONV_W - 1, rows, d_conv), F32),
            pltpu.VMEM((SUBLANES + rows, d_conv), F32),
            pltpu.VMEM((rows, d_rec + d_conv), F32),
        ],
        compiler_params=pltpu.CompilerParams(
            dimension_semantics=("arbitrary",), vmem_limit_bytes=VMEM_LIMIT_BYTES),
        name="sample_layer",
    )(x2d, mod, lower_bounds, w_in, w_conv, g_onorm, w_out, w_up, w_down, g_final, rec_in, conv_in)


def kernel(x_prompt, x_sample, state_rec, state_conv, c_prompt, c_sample, lower_bounds, w_ada,
           b_ada, w_in, w_conv, g_onorm, w_out, w_up, w_down, g_final):
    depth = w_in.shape[0]
    assert depth == 1, "single-layer trunk"
    bsz_s, seq_s, d = x_sample.shape
    bsz_p = x_prompt.shape[0]

    c_all = jnp.concatenate([c_sample, c_prompt], axis=0)
    mod = _modulation(c_all, w_ada[0], b_ada)
    mod_p = mod[bsz_s:bsz_s + bsz_p].reshape(bsz_p, 1, N_MOD * d)

    weights = (w_in[0].astype(BF16), w_conv[0], g_onorm, w_out[0].astype(BF16),
               w_up[0].astype(BF16), w_down[0].astype(BF16), g_final.reshape(1, d))

    y_p, rec_p, conv_p = _prompt_layer(x_prompt, mod_p, lower_bounds, *weights)
    y_s, rec_s, conv_s = _sample_layer(
        x_sample.reshape(bsz_s * seq_s, d), seq_s, mod, lower_bounds, *weights,
        state_rec[0], state_conv[0])
    return (y_p, y_s.reshape(bsz_s, seq_s, d), rec_p[None], conv_p[None], rec_s[None],
            conv_s[None])
```

```python
import functools

import jax
import jax.numpy as jnp
from jax import lax
from jax.experimental import pallas as pl
from jax.experimental.pallas import tpu as pltpu

F32 = jnp.float32
BF16 = jnp.bfloat16

EPS = 1e-6
H_REC = 4
CONV_W = 3
N_MOD = 6
CHUNK = 64

SUBLANES = 8
VMEM_LIMIT_BYTES = 56 * 1024 * 1024

PROMPT_TILE = 256
SAMPLE_TILE_B = 16
MOD_TILE_N = 1024
MLP_BLOCKS = 4


def _const_spec(shape):
    zeros = (0,) * len(shape)
    return pl.BlockSpec(shape, lambda *_: zeros, pipeline_mode=pl.Buffered(1))


def _rms(x):
    return x * lax.rsqrt(jnp.mean(x * x, axis=-1, keepdims=True) + EPS)


def _silu(x):
    return x * jax.nn.sigmoid(x)


def _bdot(a, b):
    return jnp.dot(a, b, preferred_element_type=F32)


def _dot_nt(a, b):
    return lax.dot_general(a, b, (((1,), (1,)), ((), ())), preferred_element_type=F32)


def _dot_tn(a, b):
    return lax.dot_general(a, b, (((0,), (0,)), ((), ())), preferred_element_type=F32)


def _group_pos(shape, group):
    return lax.broadcasted_iota(jnp.int32, shape, 0) % group


def _group_cumsum(x, group):
    pos = _group_pos(x.shape, group)
    step = 1
    while step < group:
        x = x + jnp.where(pos >= step, pltpu.roll(x, step, axis=0), 0.0)
        step *= 2
    return x


def _group_last(x, group):
    rows = x.shape[0]
    pos = _group_pos(x.shape, group)
    x = jnp.where(pos == group - 1, x, 0.0)
    step = 1
    while step < group:
        x = x + jnp.where(pos + step < group, pltpu.roll(x, rows - step, axis=0), 0.0)
        step *= 2
    return x


def _layer_lower_bound(lb_ref, layer):
    lb = lb_ref[...]
    e = jnp.exp(lb - jnp.max(lb, axis=0, keepdims=True))
    sm = e / jnp.sum(e, axis=0, keepdims=True)
    return jnp.sum(sm[: layer + 1], axis=0, keepdims=True)


def _gates(fz, lb):
    logf = jnp.log(lb + (1.0 - lb) * jax.nn.sigmoid(fz))
    k = (1.0 - lb) * jax.nn.sigmoid(-fz)
    return logf, k


def _head_out(o, g, g_onorm):
    dv = o.shape[-1] // H_REC
    pieces = []
    for h in range(H_REC):
        sl = slice(h * dv, (h + 1) * dv)
        pieces.append(_rms(o[:, sl]))
    return (jnp.concatenate(pieces, axis=-1) * g_onorm) * _silu(g)


def _modulate(x, shift, scale):
    return (_rms(x) * (1.0 + scale) + shift).astype(BF16)


def _mlp_cols(w_up_ref, j):
    blk = w_up_ref.shape[1] // MLP_BLOCKS
    return slice(j * blk, (j + 1) * blk)


def _mlp_up(h2, w_up_ref, j):
    up = jnp.maximum(_bdot(h2, w_up_ref[:, _mlp_cols(w_up_ref, j)]), 0.0)
    return (up * up).astype(BF16)


def _mlp_down(up, w_up_ref, w_down_ref, j):
    return _bdot(up, w_down_ref[_mlp_cols(w_up_ref, j), :])


def _final_norm(x, g2, mlp, g_final_ref):
    return _rms(x + g2 * mlp) * g_final_ref[...]


def _mod_kernel(c_ref, w_ref, b_ref, o_ref):
    s = _silu(c_ref[...]).astype(BF16)
    o_ref[...] = _bdot(s, w_ref[...].astype(BF16)) + b_ref[...]


def _modulation(c_all, w_ada, b_ada):
    rows, d = c_all.shape
    n = w_ada.shape[1]
    return pl.pallas_call(
        _mod_kernel,
        grid=(n // MOD_TILE_N,),
        in_specs=[
            pl.BlockSpec((rows, d), lambda j: (0, 0)),
            pl.BlockSpec((d, MOD_TILE_N), lambda j: (0, j)),
            pl.BlockSpec((1, MOD_TILE_N), lambda j: (0, j)),
        ],
        out_specs=pl.BlockSpec((rows, MOD_TILE_N), lambda j: (0, j)),
        out_shape=jax.ShapeDtypeStruct((rows, n), F32),
        compiler_params=pltpu.CompilerParams(
            dimension_semantics=("arbitrary",), vmem_limit_bytes=VMEM_LIMIT_BYTES),
        name="adaln_modulation",
    )(c_all, w_ada, b_ada)


def _prompt_kernel(x_ref, mod_ref, lb_ref, w_in_ref, w_conv_ref, g_onorm_ref, w_out_ref,
                   w_up_ref, w_down_ref, g_final_ref,
                   y_ref, rec_ref, conv_ref,
                   st_ref, ubuf_ref, mix_ref, x1_ref, h2_ref, *, n_tiles, n_steps):
    s = pl.program_id(0)
    tile = jnp.minimum(s, n_steps - 1)
    l = lax.rem(tile, n_tiles)
    live = s < n_steps
    tl, d = x_ref.shape
    d_rec = lb_ref.shape[1]
    dk = d_rec // H_REC
    n_chunks = tl // CHUNK
    heads = [slice(hd * dk, (hd + 1) * dk) for hd in range(H_REC)]

    @pl.when(s == 0)
    def _():
        x1_ref[...] = jnp.zeros_like(x1_ref)
        h2_ref[...] = jnp.zeros_like(h2_ref)

    @pl.when(l == 0)
    def _():
        st_ref[...] = jnp.zeros_like(st_ref)
        ubuf_ref[0:SUBLANES, :] = jnp.zeros((SUBLANES, ubuf_ref.shape[1]), F32)

    mod_prev = mod_ref[lax.div(jnp.maximum(s - 1, 0), n_tiles)]
    g2_prev = mod_prev[:, (N_MOD - 1) * d:]
    mod = mod_ref[lax.div(tile, n_tiles)]
    sh1, sc1, g1, sh2, sc2 = (mod[:, i * d:(i + 1) * d] for i in range(N_MOD - 1))

    h2 = h2_ref[...]
    up = _mlp_up(h2, w_up_ref, 0)

    x = x_ref[...]
    proj = _bdot(_modulate(x, sh1, sc1), w_in_ref[...])
    q, fz, iv, g = (proj[:, i * d_rec:(i + 1) * d_rec] for i in range(4))
    d_conv = (proj.shape[1] - 4 * d_rec) // 3
    gb, gc, hv = (proj[:, 4 * d_rec + i * d_conv:4 * d_rec + (i + 1) * d_conv] for i in range(3))

    mlp = _mlp_down(up, w_up_ref, w_down_ref, 0)
    up = _mlp_up(h2, w_up_ref, 1)

    lb = _layer_lower_bound(lb_ref, 0)
    row = lax.broadcasted_iota(jnp.int32, (CHUNK, CHUNK), 0)
    col = lax.broadcasted_iota(jnp.int32, (CHUNK, CHUNK), 1)
    causal = row >= col
    logf, k_all = _gates(fz, lb)
    cum_all = _group_cumsum(logf, CHUNK)
    q_dec, decay, scores, upd = [], [], [], []
    for c in range(n_chunks):
        rows = slice(c * CHUNK, (c + 1) * CHUNK)
        cum = cum_all[rows]
        last = cum[CHUNK - 1:CHUNK, :]
        q_dec.append((q[rows] * jnp.exp(cum)).astype(BF16))
        k_dec = (k_all[rows] * jnp.exp(-cum)).astype(BF16)
        k_end = (k_all[rows] * jnp.exp(last - cum)).astype(BF16)
        decay.append(jnp.exp(last))
        v = iv[rows].astype(BF16)
        scores.append([jnp.where(causal, _dot_nt(q_dec[c][:, sl], k_dec[:, sl]), 0.0).astype(BF16)
                       for sl in heads])
        upd.append([_dot_tn(v[:, sl], k_end[:, sl]) for sl in heads])

    mlp += _mlp_down(up, w_up_ref, w_down_ref, 1)
    up = _mlp_up(h2, w_up_ref, 2)

    st = [st_ref[hd] for hd in range(H_REC)]
    o_chunks = []
    for c in range(n_chunks):
        v = iv[c * CHUNK:(c + 1) * CHUNK].astype(BF16)
        o_heads = []
        for hd, sl in enumerate(heads):
            o_heads.append(_bdot(scores[c][hd], v[:, sl])
                           + _dot_nt(q_dec[c][:, sl], st[hd].astype(BF16)))
            st[hd] = st[hd] * decay[c][:, sl] + upd[c][hd]
        o_chunks.append(jnp.concatenate(o_heads, axis=-1))
    for hd in range(H_REC):
        st_ref[hd] = st[hd]

    mlp += _mlp_down(up, w_up_ref, w_down_ref, 2)
    up = _mlp_up(h2, w_up_ref, 3)

    o_rec = _head_out(jnp.concatenate(o_chunks, axis=0), g, g_onorm_ref[...])
    mix_ref[:, 0:d_rec] = o_rec.astype(BF16)

    u = gc * hv
    ubuf_ref[SUBLANES:SUBLANES + tl, :] = u
    w_conv = w_conv_ref[...]
    y_conv = (w_conv[0:1] * ubuf_ref[SUBLANES - 2:SUBLANES - 2 + tl, :]
              + w_conv[1:2] * ubuf_ref[SUBLANES - 1:SUBLANES - 1 + tl, :]
              + w_conv[2:3] * u)
    mix_ref[:, d_rec:d_rec + d_conv] = (gb * y_conv).astype(BF16)
    tail = u[tl - (CONV_W - 1):tl]
    ubuf_ref[SUBLANES - (CONV_W - 1):SUBLANES, :] = tail

    mlp += _mlp_down(up, w_up_ref, w_down_ref, 3)
    y_ref[...] = _final_norm(x1_ref[...], g2_prev, mlp, g_final_ref)

    x1 = x + g1 * _bdot(mix_ref[...], w_out_ref[...])
    x1_ref[...] = x1
    h2_ref[...] = _modulate(x1, sh2, sc2)

    @pl.when(live & (l == n_tiles - 1))
    def _():
        for hd in range(H_REC):
            rec_ref[hd] = st_ref[hd].T
        conv_ref[...] = tail


def _prompt_layer(x, mod_p, lower_bounds, w_in, w_conv, g_onorm, w_out, w_up, w_down, g_final):
    bsz, seq, d = x.shape
    d_rec = lower_bounds.shape[1]
    dk = d_rec // H_REC
    d_conv = w_conv.shape[1]
    tl = PROMPT_TILE
    n_tiles = seq // tl
    n_steps = bsz * n_tiles
    assert seq % tl == 0 and tl % CHUNK == 0 and MLP_BLOCKS == 4

    def mix_tile(s):
        t = jnp.minimum(s, n_steps - 1)
        return lax.div(t, n_tiles), lax.rem(t, n_tiles)

    def mlp_tile(s):
        t = jnp.maximum(s - 1, 0)
        return lax.div(t, n_tiles), lax.rem(t, n_tiles)

    return pl.pallas_call(
        functools.partial(_prompt_kernel, n_tiles=n_tiles, n_steps=n_steps),
        grid=(n_steps + 1,),
        in_specs=[
            pl.BlockSpec((None, tl, d), lambda s: (*mix_tile(s), 0)),
            _const_spec(mod_p.shape),
            _const_spec(lower_bounds.shape),
            _const_spec(w_in.shape),
            _const_spec(w_conv.shape),
            _const_spec(g_onorm.shape),
            _const_spec(w_out.shape),
            _const_spec(w_up.shape),
            _const_spec(w_down.shape),
            _const_spec(g_final.shape),
        ],
        out_specs=[
            pl.BlockSpec((None, tl, d), lambda s: (*mlp_tile(s), 0)),
            pl.BlockSpec((None, H_REC, dk, dk), lambda s: (mix_tile(s)[0], 0, 0, 0)),
            pl.BlockSpec((None, CONV_W - 1, d_conv), lambda s: (mix_tile(s)[0], 0, 0)),
        ],
        out_shape=[
            jax.ShapeDtypeStruct((bsz, seq, d), F32),
            jax.ShapeDtypeStruct((bsz, H_REC, dk, dk), F32),
            jax.ShapeDtypeStruct((bsz, CONV_W - 1, d_conv), F32),
        ],
        scratch_shapes=[
            pltpu.VMEM((H_REC, dk, dk), F32),
            pltpu.VMEM((SUBLANES + tl, d_conv), F32),
            pltpu.VMEM((tl, d_rec + d_conv), BF16),
            pltpu.VMEM((tl, d), F32),
            pltpu.VMEM((tl, d), BF16),
        ],
        compiler_params=pltpu.CompilerParams(
            dimension_semantics=("arbitrary",), vmem_limit_bytes=VMEM_LIMIT_BYTES),
        name="prompt_layer",
    )(x, mod_p, lower_bounds, w_in, w_conv, g_onorm, w_out, w_up, w_down, g_final)


def _sample_kernel(x_ref, mod_ref, lb_ref, w_in_ref, w_conv_ref, g_onorm_ref, w_out_ref,
                   w_up_ref, w_down_ref, g_final_ref, rec_in_ref, conv_in_ref,
                   y_ref, rec_ref, conv_ref,
                   modx_ref, cbx_ref, ubuf_ref, mix_ref, *, seq):
    rows, d = x_ref.shape
    tb = rows // seq
    d_rec = lb_ref.shape[1]
    dk = d_rec // H_REC
    d_conv = conv_in_ref.shape[2]

    for b in range(tb):
        r = slice(b * seq, (b + 1) * seq)
        modx_ref[r, :] = jnp.broadcast_to(mod_ref[b:b + 1, :], (seq, mod_ref.shape[1]))
        for j in range(CONV_W - 1):
            cbx_ref[j, r, :] = jnp.broadcast_to(conv_in_ref[b, j:j + 1, :], (seq, d_conv))

    sh1, sc1, g1, sh2, sc2, g2 = (modx_ref[:, i * d:(i + 1) * d] for i in range(N_MOD))

    x = x_ref[...]
    proj = _bdot(_modulate(x, sh1, sc1), w_in_ref[...])
    q, fz, iv, g = (proj[:, i * d_rec:(i + 1) * d_rec] for i in range(4))
    gb, gc, hv = (proj[:, 4 * d_rec + i * d_conv:4 * d_rec + (i + 1) * d_conv] for i in range(3))

    lb = _layer_lower_bound(lb_ref, 0)
    row = lax.broadcasted_iota(jnp.int32, (rows, rows), 0)
    col = lax.broadcasted_iota(jnp.int32, (rows, rows), 1)
    causal = ((row // seq) == (col // seq)) & (row >= col)
    logf, k = _gates(fz, lb)
    cum = _group_cumsum(logf, seq)
    last = _group_last(cum, seq)
    q_dec = (q * jnp.exp(cum)).astype(BF16)
    k_dec = (k * jnp.exp(-cum)).astype(BF16)
    k_end = k * jnp.exp(last - cum)
    decay = jnp.exp(last)
    vb = iv.astype(BF16)

    d_hi = decay.astype(BF16).astype(F32)
    r1 = decay - d_hi
    d_mid = r1.astype(BF16).astype(F32)
    d_lo = r1 - d_mid
    tok_rec = _group_pos((rows, d_rec), seq)
    dec3 = jnp.where(tok_rec == 0, d_hi,
                     jnp.where(tok_rec == 1, d_mid, jnp.where(tok_rec == 2, d_lo, 0.0)))
    sub = lax.broadcasted_iota(jnp.int32, (seq, dk), 0)
    ones3 = jnp.where(sub < 3, 1.0, 0.0)
    zeros_blk = jnp.zeros((seq, dk), F32)
    rhs_bottom = jnp.concatenate([zeros_blk, ones3], axis=1)

    o_intra = []
    for hd in range(H_REC):
        sl = slice(hd * dk, (hd + 1) * dk)
        scores = jnp.where(causal, _dot_nt(q_dec[:, sl], k_dec[:, sl]), 0.0)
        o_intra.append(_bdot(scores.astype(BF16), vb[:, sl]))
    o_intra = jnp.concatenate(o_intra, axis=-1)

    o_rows = []
    for b in range(tb):
        r0 = b * seq
        r = slice(r0, r0 + seq)
        o_heads = []
        for hd in range(H_REC):
            sl = slice(hd * dk, (hd + 1) * dk)
            s0 = rec_in_ref[b, hd]
            o_heads.append(_bdot(q_dec[r, sl], s0.astype(BF16)))
            lhs = jnp.concatenate([k_end[r, sl], dec3[r, sl]], axis=0)
            rhs = jnp.concatenate(
                [jnp.concatenate([iv[r, sl], zeros_blk], axis=1), rhs_bottom], axis=0)
            both = _dot_tn(lhs.astype(BF16), rhs.astype(BF16))
            rec_ref[b, hd] = both[:, dk:] * s0 + both[:, :dk]
        o_rows.append(jnp.concatenate(o_heads, axis=-1))
    o_rec = _head_out(o_intra + jnp.concatenate(o_rows, axis=0), g, g_onorm_ref[...])
    mix_ref[:, 0:d_rec] = o_rec.astype(BF16)

    u = gc * hv
    ubuf_ref[SUBLANES:SUBLANES + rows, :] = u
    tok = _group_pos((rows, d_conv), seq)
    u_m1 = jnp.where(tok >= 1, ubuf_ref[SUBLANES - 1:SUBLANES - 1 + rows, :], cbx_ref[1])
    u_m2 = jnp.where(tok >= 2, ubuf_ref[SUBLANES - 2:SUBLANES - 2 + rows, :],
                     jnp.where(tok == 1, cbx_ref[1], cbx_ref[0]))
    w_conv = w_conv_ref[...]
    y_conv = w_conv[0:1] * u_m2 + w_conv[1:2] * u_m1 + w_conv[2:3] * u
    mix_ref[:, d_rec:d_rec + d_conv] = (gb * y_conv).astype(BF16)
    for b in range(tb):
        conv_ref[b] = u[(b + 1) * seq - (CONV_W - 1):(b + 1) * seq]

    x1 = x + g1 * _bdot(mix_ref[...], w_out_ref[...])
    h2 = _modulate(x1, sh2, sc2)
    mlp = _mlp_down(_mlp_up(h2, w_up_ref, 0), w_up_ref, w_down_ref, 0)
    for j in range(1, MLP_BLOCKS):
        mlp += _mlp_down(_mlp_up(h2, w_up_ref, j), w_up_ref, w_down_ref, j)
    y_ref[...] = _final_norm(x1, g2, mlp, g_final_ref)


def _sample_layer(x2d, seq, mod, lower_bounds, w_in, w_conv, g_onorm, w_out, w_up, w_down,
                  g_final, rec_in, conv_in):
    n_rows, d = x2d.shape
    bsz = n_rows // seq
    d_rec = lower_bounds.shape[1]
    dk = d_rec // H_REC
    d_conv = w_conv.shape[1]
    tb = SAMPLE_TILE_B
    rows = tb * seq
    assert bsz % tb == 0 and seq == SUBLANES and seq >= CONV_W
    return pl.pallas_call(
        functools.partial(_sample_kernel, seq=seq),
        grid=(bsz // tb,),
        in_specs=[
            pl.BlockSpec((rows, d), lambda i: (i, 0)),
            pl.BlockSpec((tb, N_MOD * d), lambda i: (i, 0)),
            _const_spec(lower_bounds.shape),
            _const_spec(w_in.shape),
            _const_spec(w_conv.shape),
            _const_spec(g_onorm.shape),
            _const_spec(w_out.shape),
            _const_spec(w_up.shape),
            _const_spec(w_down.shape),
            _const_spec(g_final.shape),
            pl.BlockSpec((tb, H_REC, dk, dk), lambda i: (i, 0, 0, 0)),
            pl.BlockSpec((tb, CONV_W - 1, d_conv), lambda i: (i, 0, 0)),
        ],
        out_specs=[
            pl.BlockSpec((rows, d), lambda i: (i, 0)),
            pl.BlockSpec((tb, H_REC, dk, dk), lambda i: (i, 0, 0, 0)),
            pl.BlockSpec((tb, CONV_W - 1, d_conv), lambda i: (i, 0, 0)),
        ],
        out_shape=[
            jax.ShapeDtypeStruct((n_rows, d), F32),
            jax.ShapeDtypeStruct((bsz, H_REC, dk, dk), F32),
            jax.ShapeDtypeStruct((bsz, CONV_W - 1, d_conv), F32),
        ],
        scratch_shapes=[
            pltpu.VMEM((rows, N_MOD * d), F32),
            pltpu.VMEM((CONV_W - 1, rows, d_conv), F32),
            pltpu.VMEM((SUBLANES + rows, d_conv), F32),
            pltpu.VMEM((rows, d_rec + d_conv), BF16),
        ],
        compiler_params=pltpu.CompilerParams(
            dimension_semantics=("arbitrary",), vmem_limit_bytes=VMEM_LIMIT_BYTES),
        name="sample_layer",
    )(x2d, mod, lower_bounds, w_in, w_conv, g_onorm, w_out, w_up, w_down, g_final, rec_in, conv_in)


def kernel(x_prompt, x_sample, state_rec, state_conv, c_prompt, c_sample, lower_bounds, w_ada,
           b_ada, w_in, w_conv, g_onorm, w_out, w_up, w_down, g_final):
    depth = w_in.shape[0]
    assert depth == 1, "single-layer trunk"
    bsz_s, seq_s, d = x_sample.shape
    bsz_p = x_prompt.shape[0]

    c_all = jnp.concatenate([c_sample, c_prompt], axis=0)
    mod = _modulation(c_all, w_ada[0], b_ada)
    mod_p = mod[bsz_s:bsz_s + bsz_p].reshape(bsz_p, 1, N_MOD * d)

    weights = (w_in[0].astype(BF16), w_conv[0], g_onorm, w_out[0].astype(BF16),
               w_up[0].astype(BF16), w_down[0].astype(BF16), g_final.reshape(1, d))

    y_p, rec_p, conv_p = _prompt_layer(x_prompt, mod_p, lower_bounds, *weights)
    y_s, rec_s, conv_s = _sample_layer(
        x_sample.reshape(bsz_s * seq_s, d), seq_s, mod, lower_bounds, *weights,
        state_rec[0], state_conv[0])
    return (y_p, y_s.reshape(bsz_s, seq_s, d), rec_p[None], conv_p[None], rec_s[None],
            conv_s[None])
```

```python
import functools

import jax
import jax.numpy as jnp
from jax import lax
from jax.experimental import pallas as pl
from jax.experimental.pallas import tpu as pltpu

F32 = jnp.float32
BF16 = jnp.bfloat16

EPS = 1e-6
H_REC = 4
CONV_W = 3
N_MOD = 6
CHUNK = 64

SUBLANES = 8
VMEM_LIMIT_BYTES = 56 * 1024 * 1024

PROMPT_TILE = 512
SAMPLE_TILE_B = 16
MOD_TILE_N = 1024
MLP_BLOCKS = 4


def _const_spec(shape):
    zeros = (0,) * len(shape)
    return pl.BlockSpec(shape, lambda *_: zeros, pipeline_mode=pl.Buffered(1))


def _rms(x):
    return x * lax.rsqrt(jnp.mean(x * x, axis=-1, keepdims=True) + EPS)


def _silu(x):
    return x * jax.nn.sigmoid(x)


def _bdot(a, b):
    return jnp.dot(a, b, preferred_element_type=F32)


def _dot_nt(a, b):
    return lax.dot_general(a, b, (((1,), (1,)), ((), ())), preferred_element_type=F32)


def _dot_tn(a, b):
    return lax.dot_general(a, b, (((0,), (0,)), ((), ())), preferred_element_type=F32)


def _group_pos(shape, group):
    return lax.broadcasted_iota(jnp.int32, shape, 0) % group


def _group_cumsum(x, group):
    pos = _group_pos(x.shape, group)
    step = 1
    while step < group:
        x = x + jnp.where(pos >= step, pltpu.roll(x, step, axis=0), 0.0)
        step *= 2
    return x


def _group_last(x, group):
    rows = x.shape[0]
    pos = _group_pos(x.shape, group)
    x = jnp.where(pos == group - 1, x, 0.0)
    step = 1
    while step < group:
        x = x + jnp.where(pos + step < group, pltpu.roll(x, rows - step, axis=0), 0.0)
        step *= 2
    return x


def _layer_lower_bound(lb_ref, layer):
    lb = lb_ref[...]
    e = jnp.exp(lb - jnp.max(lb, axis=0, keepdims=True))
    sm = e / jnp.sum(e, axis=0, keepdims=True)
    return jnp.sum(sm[: layer + 1], axis=0, keepdims=True)


def _gates(fz, lb):
    sig = jax.nn.sigmoid(fz)
    logf = jnp.log(lb + (1.0 - lb) * sig)
    k = (1.0 - lb) * (1.0 - sig)
    return logf, k


def _head_out(o, g, g_onorm):
    dv = o.shape[-1] // H_REC
    pieces = []
    for h in range(H_REC):
        sl = slice(h * dv, (h + 1) * dv)
        pieces.append(_rms(o[:, sl]))
    return (jnp.concatenate(pieces, axis=-1) * g_onorm) * _silu(g)


def _modulate(x, shift, scale):
    return (_rms(x) * (1.0 + scale) + shift).astype(BF16)


def _mlp_cols(w_up_ref, j):
    blk = w_up_ref.shape[1] // MLP_BLOCKS
    return slice(j * blk, (j + 1) * blk)


def _mlp_up(h2, w_up_ref, j):
    up = jnp.maximum(_bdot(h2, w_up_ref[:, _mlp_cols(w_up_ref, j)]), 0.0)
    return (up * up).astype(BF16)


def _mlp_down(up, w_up_ref, w_down_ref, j):
    return _bdot(up, w_down_ref[_mlp_cols(w_up_ref, j), :])


def _final_norm(x, g2, mlp, g_final_ref):
    return _rms(x + g2 * mlp) * g_final_ref[...]


def _mod_kernel(c_ref, w_ref, b_ref, o_ref):
    s = _silu(c_ref[...]).astype(BF16)
    o_ref[...] = _bdot(s, w_ref[...].astype(BF16)) + b_ref[...]


def _modulation(c_all, w_ada, b_ada):
    rows, d = c_all.shape
    n = w_ada.shape[1]
    return pl.pallas_call(
        _mod_kernel,
        grid=(n // MOD_TILE_N,),
        in_specs=[
            pl.BlockSpec((rows, d), lambda j: (0, 0)),
            pl.BlockSpec((d, MOD_TILE_N), lambda j: (0, j)),
            pl.BlockSpec((1, MOD_TILE_N), lambda j: (0, j)),
        ],
        out_specs=pl.BlockSpec((rows, MOD_TILE_N), lambda j: (0, j)),
        out_shape=jax.ShapeDtypeStruct((rows, n), F32),
        compiler_params=pltpu.CompilerParams(
            dimension_semantics=("arbitrary",), vmem_limit_bytes=VMEM_LIMIT_BYTES),
        name="adaln_modulation",
    )(c_all, w_ada, b_ada)


def _prompt_kernel(x_ref, mod_ref, lb_ref, w_in_ref, w_conv_ref, g_onorm_ref, w_out_ref,
                   w_up_ref, w_down_ref, g_final_ref,
                   y_ref, rec_ref, conv_ref,
                   st_ref, ubuf_ref, mix_ref, x1_ref, h2_ref, *, n_tiles, n_steps):
    s = pl.program_id(0)
    tile = jnp.minimum(s, n_steps - 1)
    l = lax.rem(tile, n_tiles)
    live = s < n_steps
    tl, d = x_ref.shape
    d_rec = lb_ref.shape[1]
    dk = d_rec // H_REC
    n_chunks = tl // CHUNK
    heads = [slice(hd * dk, (hd + 1) * dk) for hd in range(H_REC)]

    @pl.when(s == 0)
    def _():
        x1_ref[...] = jnp.zeros_like(x1_ref)
        h2_ref[...] = jnp.zeros_like(h2_ref)

    @pl.when(l == 0)
    def _():
        st_ref[...] = jnp.zeros_like(st_ref)
        ubuf_ref[0:SUBLANES, :] = jnp.zeros((SUBLANES, ubuf_ref.shape[1]), F32)

    mod_prev = mod_ref[lax.div(jnp.maximum(s - 1, 0), n_tiles)]
    g2_prev = mod_prev[:, (N_MOD - 1) * d:]
    mod = mod_ref[lax.div(tile, n_tiles)]
    sh1, sc1, g1, sh2, sc2 = (mod[:, i * d:(i + 1) * d] for i in range(N_MOD - 1))

    h2 = h2_ref[...]
    up = _mlp_up(h2, w_up_ref, 0)

    x = x_ref[...]
    proj = _bdot(_modulate(x, sh1, sc1), w_in_ref[...])
    q, fz, iv, g = (proj[:, i * d_rec:(i + 1) * d_rec] for i in range(4))
    d_conv = (proj.shape[1] - 4 * d_rec) // 3
    gb, gc, hv = (proj[:, 4 * d_rec + i * d_conv:4 * d_rec + (i + 1) * d_conv] for i in range(3))

    up_next = _mlp_up(h2, w_up_ref, 1)
    mlp = _mlp_down(up, w_up_ref, w_down_ref, 0)
    up = up_next

    lb = _layer_lower_bound(lb_ref, 0)
    row = lax.broadcasted_iota(jnp.int32, (CHUNK, CHUNK), 0)
    col = lax.broadcasted_iota(jnp.int32, (CHUNK, CHUNK), 1)
    causal = row >= col
    logf, k_all = _gates(fz, lb)
    cum_all = _group_cumsum(logf, CHUNK)
    q_dec, decay, scores, upd = [], [], [], []
    for c in range(n_chunks):
        rows = slice(c * CHUNK, (c + 1) * CHUNK)
        cum = cum_all[rows]
        last = cum[CHUNK - 1:CHUNK, :]
        q_dec.append((q[rows] * jnp.exp(cum)).astype(BF16))
        k_dec = (k_all[rows] * jnp.exp(-cum)).astype(BF16)
        k_end = (k_all[rows] * jnp.exp(last - cum)).astype(BF16)
        decay.append(jnp.exp(last))
        v = iv[rows].astype(BF16)
        scores.append([jnp.where(causal, _dot_nt(q_dec[c][:, sl], k_dec[:, sl]), 0.0).astype(BF16)
                       for sl in heads])
        upd.append([_dot_tn(v[:, sl], k_end[:, sl]) for sl in heads])

    up_next = _mlp_up(h2, w_up_ref, 2)
    mlp += _mlp_down(up, w_up_ref, w_down_ref, 1)
    up = up_next

    st = [st_ref[hd] for hd in range(H_REC)]
    o_chunks = []
    for c in range(n_chunks):
        v = iv[c * CHUNK:(c + 1) * CHUNK].astype(BF16)
        o_heads = []
        for hd, sl in enumerate(heads):
            o_heads.append(_bdot(scores[c][hd], v[:, sl])
                           + _dot_nt(q_dec[c][:, sl], st[hd].astype(BF16)))
            st[hd] = st[hd] * decay[c][:, sl] + upd[c][hd]
        o_chunks.append(jnp.concatenate(o_heads, axis=-1))
    for hd in range(H_REC):
        st_ref[hd] = st[hd]

    up_next = _mlp_up(h2, w_up_ref, 3)
    mlp += _mlp_down(up, w_up_ref, w_down_ref, 2)
    up = up_next

    o_rec = _head_out(jnp.concatenate(o_chunks, axis=0), g, g_onorm_ref[...])
    mix_ref[:, 0:d_rec] = o_rec.astype(BF16)

    u = gc * hv
    ubuf_ref[SUBLANES:SUBLANES + tl, :] = u
    w_conv = w_conv_ref[...]
    y_conv = (w_conv[0:1] * ubuf_ref[SUBLANES - 2:SUBLANES - 2 + tl, :]
              + w_conv[1:2] * ubuf_ref[SUBLANES - 1:SUBLANES - 1 + tl, :]
              + w_conv[2:3] * u)
    mix_ref[:, d_rec:d_rec + d_conv] = (gb * y_conv).astype(BF16)
    tail = u[tl - (CONV_W - 1):tl]
    ubuf_ref[SUBLANES - (CONV_W - 1):SUBLANES, :] = tail

    mlp += _mlp_down(up, w_up_ref, w_down_ref, 3)
    y_ref[...] = _final_norm(x1_ref[...], g2_prev, mlp, g_final_ref)

    x1 = x + g1 * _bdot(mix_ref[...], w_out_ref[...])
    x1_ref[...] = x1
    h2_ref[...] = _modulate(x1, sh2, sc2)

    @pl.when(live & (l == n_tiles - 1))
    def _():
        for hd in range(H_REC):
            rec_ref[hd] = st_ref[hd].T
        conv_ref[...] = tail


def _prompt_layer(x, mod_p, lower_bounds, w_in, w_conv, g_onorm, w_out, w_up, w_down, g_final):
    bsz, seq, d = x.shape
    d_rec = lower_bounds.shape[1]
    dk = d_rec // H_REC
    d_conv = w_conv.shape[1]
    tl = PROMPT_TILE
    n_tiles = seq // tl
    n_steps = bsz * n_tiles
    assert seq % tl == 0 and tl % CHUNK == 0 and MLP_BLOCKS == 4

    def mix_tile(s):
        t = jnp.minimum(s, n_steps - 1)
        return lax.div(t, n_tiles), lax.rem(t, n_tiles)

    def mlp_tile(s):
        t = jnp.maximum(s - 1, 0)
        return lax.div(t, n_tiles), lax.rem(t, n_tiles)

    return pl.pallas_call(
        functools.partial(_prompt_kernel, n_tiles=n_tiles, n_steps=n_steps),
        grid=(n_steps + 1,),
        in_specs=[
            pl.BlockSpec((None, tl, d), lambda s: (*mix_tile(s), 0)),
            _const_spec(mod_p.shape),
            _const_spec(lower_bounds.shape),
            _const_spec(w_in.shape),
            _const_spec(w_conv.shape),
            _const_spec(g_onorm.shape),
            _const_spec(w_out.shape),
            _const_spec(w_up.shape),
            _const_spec(w_down.shape),
            _const_spec(g_final.shape),
        ],
        out_specs=[
            pl.BlockSpec((None, tl, d), lambda s: (*mlp_tile(s), 0)),
            pl.BlockSpec((None, H_REC, dk, dk), lambda s: (mix_tile(s)[0], 0, 0, 0)),
            pl.BlockSpec((None, CONV_W - 1, d_conv), lambda s: (mix_tile(s)[0], 0, 0)),
        ],
        out_shape=[
            jax.ShapeDtypeStruct((bsz, seq, d), F32),
            jax.ShapeDtypeStruct((bsz, H_REC, dk, dk), F32),
            jax.ShapeDtypeStruct((bsz, CONV_W - 1, d_conv), F32),
        ],
        scratch_shapes=[
            pltpu.VMEM((H_REC, dk, dk), F32),
            pltpu.VMEM((SUBLANES + tl, d_conv), F32),
            pltpu.VMEM((tl, d_rec + d_conv), BF16),
            pltpu.VMEM((tl, d), F32),
            pltpu.VMEM((tl, d), BF16),
        ],
        compiler_params=pltpu.CompilerParams(
            dimension_semantics=("arbitrary",), vmem_limit_bytes=VMEM_LIMIT_BYTES),
        name="prompt_layer",
    )(x, mod_p, lower_bounds, w_in, w_conv, g_onorm, w_out, w_up, w_down, g_final)


def _sample_kernel(x_ref, mod_ref, lb_ref, w_in_ref, w_conv_ref, g_onorm_ref, w_out_ref,
                   w_up_ref, w_down_ref, g_final_ref, rec_in_ref, conv_in_ref,
                   y_ref, rec_ref, conv_ref,
                   modx_ref, cbx_ref, ubuf_ref, mix_ref, *, seq):
    rows, d = x_ref.shape
    tb = rows // seq
    d_rec = lb_ref.shape[1]
    dk = d_rec // H_REC
    d_conv = conv_in_ref.shape[2]

    for b in range(tb):
        r = slice(b * seq, (b + 1) * seq)
        modx_ref[r, :] = jnp.broadcast_to(mod_ref[b:b + 1, :], (seq, mod_ref.shape[1]))
        for j in range(CONV_W - 1):
            cbx_ref[j, r, :] = jnp.broadcast_to(conv_in_ref[b, j:j + 1, :], (seq, d_conv))

    sh1, sc1, g1, sh2, sc2, g2 = (modx_ref[:, i * d:(i + 1) * d] for i in range(N_MOD))

    x = x_ref[...]
    proj = _bdot(_modulate(x, sh1, sc1), w_in_ref[...])
    q, fz, iv, g = (proj[:, i * d_rec:(i + 1) * d_rec] for i in range(4))
    gb, gc, hv = (proj[:, 4 * d_rec + i * d_conv:4 * d_rec + (i + 1) * d_conv] for i in range(3))

    lb = _layer_lower_bound(lb_ref, 0)
    row = lax.broadcasted_iota(jnp.int32, (rows, rows), 0)
    col = lax.broadcasted_iota(jnp.int32, (rows, rows), 1)
    causal = ((row // seq) == (col // seq)) & (row >= col)
    logf, k = _gates(fz, lb)
    cum = _group_cumsum(logf, seq)
    last = _group_last(cum, seq)
    q_dec = (q * jnp.exp(cum)).astype(BF16)
    k_dec = (k * jnp.exp(-cum)).astype(BF16)
    k_end = k * jnp.exp(last - cum)
    decay = jnp.exp(last)
    vb = iv.astype(BF16)

    d_hi = decay.astype(BF16).astype(F32)
    r1 = decay - d_hi
    d_mid = r1.astype(BF16).astype(F32)
    d_lo = r1 - d_mid
    tok_rec = _group_pos((rows, d_rec), seq)
    dec3 = jnp.where(tok_rec == 0, d_hi,
                     jnp.where(tok_rec == 1, d_mid, jnp.where(tok_rec == 2, d_lo, 0.0)))
    sub = lax.broadcasted_iota(jnp.int32, (seq, dk), 0)
    ones3 = jnp.where(sub < 3, 1.0, 0.0)
    zeros_blk = jnp.zeros((seq, dk), F32)
    rhs_bottom = jnp.concatenate([zeros_blk, ones3], axis=1)

    o_intra = []
    for hd in range(H_REC):
        sl = slice(hd * dk, (hd + 1) * dk)
        scores = jnp.where(causal, _dot_nt(q_dec[:, sl], k_dec[:, sl]), 0.0)
        o_intra.append(_bdot(scores.astype(BF16), vb[:, sl]))
    o_intra = jnp.concatenate(o_intra, axis=-1)

    o_rows = []
    for b in range(tb):
        r0 = b * seq
        r = slice(r0, r0 + seq)
        o_heads = []
        for hd in range(H_REC):
            sl = slice(hd * dk, (hd + 1) * dk)
            s0 = rec_in_ref[b, hd]
            o_heads.append(_bdot(q_dec[r, sl], s0.astype(BF16)))
            lhs = jnp.concatenate([k_end[r, sl], dec3[r, sl]], axis=0)
            rhs = jnp.concatenate(
                [jnp.concatenate([iv[r, sl], zeros_blk], axis=1), rhs_bottom], axis=0)
            both = _dot_tn(lhs.astype(BF16), rhs.astype(BF16))
            rec_ref[b, hd] = both[:, dk:] * s0 + both[:, :dk]
        o_rows.append(jnp.concatenate(o_heads, axis=-1))
    o_rec = _head_out(o_intra + jnp.concatenate(o_rows, axis=0), g, g_onorm_ref[...])
    mix_ref[:, 0:d_rec] = o_rec.astype(BF16)

    u = gc * hv
    ubuf_ref[SUBLANES:SUBLANES + rows, :] = u
    tok = _group_pos((rows, d_conv), seq)
    u_m1 = jnp.where(tok >= 1, ubuf_ref[SUBLANES - 1:SUBLANES - 1 + rows, :], cbx_ref[1])
    u_m2 = jnp.where(tok >= 2, ubuf_ref[SUBLANES - 2:SUBLANES - 2 + rows, :],
                     jnp.where(tok == 1, cbx_ref[1], cbx_ref[0]))
    w_conv = w_conv_ref[...]
    y_conv = w_conv[0:1] * u_m2 + w_conv[1:2] * u_m1 + w_conv[2:3] * u
    mix_ref[:, d_rec:d_rec + d_conv] = (gb * y_conv).astype(BF16)
    for b in range(tb):
        conv_ref[b] = u[(b + 1) * seq - (CONV_W - 1):(b + 1) * seq]

    x1 = x + g1 * _bdot(mix_ref[...], w_out_ref[...])
    h2 = _modulate(x1, sh2, sc2)
    mlp = _mlp_down(_mlp_up(h2, w_up_ref, 0), w_up_ref, w_down_ref, 0)
    for j in range(1, MLP_BLOCKS):
        mlp += _mlp_down(_mlp_up(h2, w_up_ref, j), w_up_ref, w_down_ref, j)
    y_ref[...] = _final_norm(x1, g2, mlp, g_final_ref)


def _sample_layer(x2d, seq, mod, lower_bounds, w_in, w_conv, g_onorm, w_out, w_up, w_down,
                  g_final, rec_in, conv_in):
    n_rows, d = x2d.shape
    bsz = n_rows // seq
    d_rec = lower_bounds.shape[1]
    dk = d_rec // H_REC
    d_conv = w_conv.shape[1]
    tb = SAMPLE_TILE_B
    rows = tb * seq
    assert bsz % tb == 0 and seq == SUBLANES and seq >= CONV_W
    return pl.pallas_call(
        functools.partial(_sample_kernel, seq=seq),
        grid=(bsz // tb,),
        in_specs=[
            pl.BlockSpec((rows, d), lambda i: (i, 0)),
            pl.BlockSpec((tb, N_MOD * d), lambda i: (i, 0)),
            _const_spec(lower_bounds.shape),
            _const_spec(w_in.shape),
            _const_spec(w_conv.shape),
            _const_spec(g_onorm.shape),
            _const_spec(w_out.shape),
            _const_spec(w_up.shape),
            _const_spec(w_down.shape),
            _const_spec(g_final.shape),
            pl.BlockSpec((tb, H_REC, dk, dk), lambda i: (i, 0, 0, 0)),
            pl.BlockSpec((tb, CONV_W - 1, d_conv), lambda i: (i, 0, 0)),
        ],
        out_specs=[
            pl.BlockSpec((rows, d), lambda i: (i, 0)),
            pl.BlockSpec((tb, H_REC, dk, dk), lambda i: (i, 0, 0, 0)),
            pl.BlockSpec((tb, CONV_W - 1, d_conv), lambda i: (i, 0, 0)),
        ],
        out_shape=[
            jax.ShapeDtypeStruct((n_rows, d), F32),
            jax.ShapeDtypeStruct((bsz, H_REC, dk, dk), F32),
            jax.ShapeDtypeStruct((bsz, CONV_W - 1, d_conv), F32),
        ],
        scratch_shapes=[
            pltpu.VMEM((rows, N_MOD * d), F32),
            pltpu.VMEM((CONV_W - 1, rows, d_conv), F32),
            pltpu.VMEM((SUBLANES + rows, d_conv), F32),
            pltpu.VMEM((rows, d_rec + d_conv), BF16),
        ],
        compiler_params=pltpu.CompilerParams(
            dimension_semantics=("arbitrary",), vmem_limit_bytes=VMEM_LIMIT_BYTES),
        name="sample_layer",
    )(x2d, mod, lower_bounds, w_in, w_conv, g_onorm, w_out, w_up, w_down, g_final, rec_in, conv_in)


def kernel(x_prompt, x_sample, state_rec, state_conv, c_prompt, c_sample, lower_bounds, w_ada,
           b_ada, w_in, w_conv, g_onorm, w_out, w_up, w_down, g_final):
    depth = w_in.shape[0]
    assert depth == 1, "single-layer trunk"
    bsz_s, seq_s, d = x_sample.shape
    bsz_p = x_prompt.shape[0]

    c_all = jnp.concatenate([c_sample, c_prompt], axis=0)
    mod = _modulation(c_all, w_ada[0], b_ada)
    mod_p = mod[bsz_s:bsz_s + bsz_p].reshape(bsz_p, 1, N_MOD * d)

    weights = (w_in[0].astype(BF16), w_conv[0], g_onorm, w_out[0].astype(BF16),
               w_up[0].astype(BF16), w_down[0].astype(BF16), g_final.reshape(1, d))

    y_p, rec_p, conv_p = _prompt_layer(x_prompt, mod_p, lower_bounds, *weights)
    y_s, rec_s, conv_s = _sample_layer(
        x_sample.reshape(bsz_s * seq_s, d), seq_s, mod, lower_bounds, *weights,
        state_rec[0], state_conv[0])
    return (y_p, y_s.reshape(bsz_s, seq_s, d), rec_p[None], conv_p[None], rec_s[None],
            conv_s[None])
```

```python
import functools

import jax
import jax.numpy as jnp
from jax import lax
from jax.experimental import pallas as pl
from jax.experimental.pallas import tpu as pltpu

F32 = jnp.float32
BF16 = jnp.bfloat16

EPS = 1e-6
H_REC = 4
CONV_W = 3
N_MOD = 6
CHUNK = 64

SUBLANES = 8
VMEM_LIMIT_BYTES = 56 * 1024 * 1024

PROMPT_TILE = 512
SAMPLE_TILE_B = 16
MOD_TILE_N = 1024
MLP_BLOCKS = 4
WEIGHT_STAGE_SHAPE = (1024, 512)


def _const_spec(shape):
    zeros = (0,) * len(shape)
    return pl.BlockSpec(shape, lambda *_: zeros, pipeline_mode=pl.Buffered(1))


def _rms(x):
    return x * lax.rsqrt(jnp.mean(x * x, axis=-1, keepdims=True) + EPS)


def _silu(x):
    return x * jax.nn.sigmoid(x)


def _bdot(a, b):
    return jnp.dot(a, b, preferred_element_type=F32)


def _dot_nt(a, b):
    return lax.dot_general(a, b, (((1,), (1,)), ((), ())), preferred_element_type=F32)


def _dot_tn(a, b):
    return lax.dot_general(a, b, (((0,), (0,)), ((), ())), preferred_element_type=F32)


def _group_pos(shape, group):
    return lax.broadcasted_iota(jnp.int32, shape, 0) % group


def _group_cumsum(x, group):
    pos = _group_pos(x.shape, group)
    step = 1
    while step < group:
        x = x + jnp.where(pos >= step, pltpu.roll(x, step, axis=0), 0.0)
        step *= 2
    return x


def _group_last(x, group):
    rows = x.shape[0]
    pos = _group_pos(x.shape, group)
    x = jnp.where(pos == group - 1, x, 0.0)
    step = 1
    while step < group:
        x = x + jnp.where(pos + step < group, pltpu.roll(x, rows - step, axis=0), 0.0)
        step *= 2
    return x


def _layer_lower_bound(lb_ref, layer):
    lb = lb_ref[...]
    e = jnp.exp(lb - jnp.max(lb, axis=0, keepdims=True))
    sm = e / jnp.sum(e, axis=0, keepdims=True)
    return jnp.sum(sm[: layer + 1], axis=0, keepdims=True)


def _gates(fz, lb):
    sig = jax.nn.sigmoid(fz)
    logf = jnp.log(lb + (1.0 - lb) * sig)
    k = (1.0 - lb) * (1.0 - sig)
    return logf, k


def _head_out(o, g, g_onorm):
    dv = o.shape[-1] // H_REC
    pieces = []
    for h in range(H_REC):
        sl = slice(h * dv, (h + 1) * dv)
        pieces.append(_rms(o[:, sl]))
    return (jnp.concatenate(pieces, axis=-1) * g_onorm) * _silu(g)


def _modulate(x, shift, scale):
    return (_rms(x) * (1.0 + scale) + shift).astype(BF16)


def _mlp_cols(w_up_ref, j):
    blk = w_up_ref.shape[1] // MLP_BLOCKS
    return slice(j * blk, (j + 1) * blk)


def _mlp_up(h2, w_up_ref, j):
    up = jnp.maximum(_bdot(h2, w_up_ref[:, _mlp_cols(w_up_ref, j)]), 0.0)
    return (up * up).astype(BF16)


def _mlp_down(up, w_up_ref, w_down_ref, j):
    return _bdot(up, w_down_ref[_mlp_cols(w_up_ref, j), :])


def _final_norm(x, g2, mlp, g_final_ref):
    return _rms(x + g2 * mlp) * g_final_ref[...]


def _mod_kernel(c_ref, w_ref, b_ref, o_ref):
    s = _silu(c_ref[...]).astype(BF16)
    o_ref[...] = _bdot(s, w_ref[...].astype(BF16)) + b_ref[...]


def _modulation(c_all, w_ada, b_ada):
    rows, d = c_all.shape
    n = w_ada.shape[1]
    return pl.pallas_call(
        _mod_kernel,
        grid=(n // MOD_TILE_N,),
        in_specs=[
            pl.BlockSpec((rows, d), lambda j: (0, 0)),
            pl.BlockSpec((d, MOD_TILE_N), lambda j: (0, j)),
            pl.BlockSpec((1, MOD_TILE_N), lambda j: (0, j)),
        ],
        out_specs=pl.BlockSpec((rows, MOD_TILE_N), lambda j: (0, j)),
        out_shape=jax.ShapeDtypeStruct((rows, n), F32),
        compiler_params=pltpu.CompilerParams(
            dimension_semantics=("arbitrary",), vmem_limit_bytes=VMEM_LIMIT_BYTES),
        name="adaln_modulation",
    )(c_all, w_ada, b_ada)


def _weight_blocks(w_hbm, w_vmem):
    rows, cols = w_hbm.shape
    br, bc = WEIGHT_STAGE_SHAPE
    assert rows % br == 0 and cols % bc == 0
    return [(w_hbm.at[r:r + br, c:c + bc], w_vmem.at[r:r + br, c:c + bc])
            for r in range(0, rows, br) for c in range(0, cols, bc)]


def _stage_copy(src, stage_ref, sem_ref, i):
    return pltpu.make_async_copy(src, stage_ref.at[i % 2], sem_ref.at[i % 2])


def _load_weights_as_bf16(w_hbm_refs, w_vmem_refs, stage_ref, sem_ref):
    blocks = [blk for w_hbm, w_vmem in zip(w_hbm_refs, w_vmem_refs)
              for blk in _weight_blocks(w_hbm, w_vmem)]
    _stage_copy(blocks[0][0], stage_ref, sem_ref, 0).start()
    for i, (src, dst) in enumerate(blocks):
        if i + 1 < len(blocks):
            _stage_copy(blocks[i + 1][0], stage_ref, sem_ref, i + 1).start()
        _stage_copy(src, stage_ref, sem_ref, i).wait()
        dst[...] = stage_ref[i % 2].astype(BF16)


def _prompt_kernel(x_ref, mod_ref, lb_ref, w_in_hbm, w_conv_ref, g_onorm_ref, w_out_hbm,
                   w_up_hbm, w_down_hbm, g_final_ref,
                   y_ref, rec_ref, conv_ref, w_in_out, w_out_out, w_up_out, w_down_out,
                   st_ref, ubuf_ref, mix_ref, x1_ref, h2_ref,
                   w_in_ref, w_out_ref, w_up_ref, w_down_ref, stage_ref, stage_sem, out_sem,
                   *, n_tiles, n_steps):
    s = pl.program_id(0)
    tile = jnp.minimum(s, n_steps - 1)
    l = lax.rem(tile, n_tiles)
    live = s < n_steps
    tl, d = x_ref.shape
    d_rec = lb_ref.shape[1]
    dk = d_rec // H_REC
    n_chunks = tl // CHUNK
    heads = [slice(hd * dk, (hd + 1) * dk) for hd in range(H_REC)]

    w_vmem = (w_in_ref, w_out_ref, w_up_ref, w_down_ref)
    w_outs = (w_in_out, w_out_out, w_up_out, w_down_out)

    def bf16_export(i):
        return pltpu.make_async_copy(w_vmem[i], w_outs[i], out_sem.at[i])

    @pl.when(s == 0)
    def _():
        x1_ref[...] = jnp.zeros_like(x1_ref)
        h2_ref[...] = jnp.zeros_like(h2_ref)
        _load_weights_as_bf16((w_in_hbm, w_out_hbm, w_up_hbm, w_down_hbm), w_vmem,
                              stage_ref, stage_sem)
        for i in range(len(w_vmem)):
            bf16_export(i).start()

    @pl.when(s == n_steps)
    def _():
        for i in range(len(w_vmem)):
            bf16_export(i).wait()

    @pl.when(l == 0)
    def _():
        st_ref[...] = jnp.zeros_like(st_ref)
        ubuf_ref[0:SUBLANES, :] = jnp.zeros((SUBLANES, ubuf_ref.shape[1]), F32)

    mod_prev = mod_ref[lax.div(jnp.maximum(s - 1, 0), n_tiles)]
    g2_prev = mod_prev[:, (N_MOD - 1) * d:]
    mod = mod_ref[lax.div(tile, n_tiles)]
    sh1, sc1, g1, sh2, sc2 = (mod[:, i * d:(i + 1) * d] for i in range(N_MOD - 1))

    h2 = h2_ref[...]
    up = _mlp_up(h2, w_up_ref, 0)

    x = x_ref[...]
    proj = _bdot(_modulate(x, sh1, sc1), w_in_ref[...])
    q, fz, iv, g = (proj[:, i * d_rec:(i + 1) * d_rec] for i in range(4))
    d_conv = (proj.shape[1] - 4 * d_rec) // 3
    gb, gc, hv = (proj[:, 4 * d_rec + i * d_conv:4 * d_rec + (i + 1) * d_conv] for i in range(3))

    up_next = _mlp_up(h2, w_up_ref, 1)
    mlp = _mlp_down(up, w_up_ref, w_down_ref, 0)
    up = up_next

    lb = _layer_lower_bound(lb_ref, 0)
    row = lax.broadcasted_iota(jnp.int32, (CHUNK, CHUNK), 0)
    col = lax.broadcasted_iota(jnp.int32, (CHUNK, CHUNK), 1)
    causal = row >= col
    logf, k_all = _gates(fz, lb)
    cum_all = _group_cumsum(logf, CHUNK)
    q_dec, decay, scores, upd = [], [], [], []
    for c in range(n_chunks):
        rows = slice(c * CHUNK, (c + 1) * CHUNK)
        cum = cum_all[rows]
        last = cum[CHUNK - 1:CHUNK, :]
        q_dec.append((q[rows] * jnp.exp(cum)).astype(BF16))
        k_dec = (k_all[rows] * jnp.exp(-cum)).astype(BF16)
        k_end = (k_all[rows] * jnp.exp(last - cum)).astype(BF16)
        decay.append(jnp.exp(last))
        v = iv[rows].astype(BF16)
        scores.append([jnp.where(causal, _dot_nt(q_dec[c][:, sl], k_dec[:, sl]), 0.0).astype(BF16)
                       for sl in heads])
        upd.append([_dot_tn(v[:, sl], k_end[:, sl]) for sl in heads])

    up_next = _mlp_up(h2, w_up_ref, 2)
    mlp += _mlp_down(up, w_up_ref, w_down_ref, 1)
    up = up_next

    st = [st_ref[hd] for hd in range(H_REC)]
    o_chunks = []
    for c in range(n_chunks):
        v = iv[c * CHUNK:(c + 1) * CHUNK].astype(BF16)
        o_heads = []
        for hd, sl in enumerate(heads):
            o_heads.append(_bdot(scores[c][hd], v[:, sl])
                           + _dot_nt(q_dec[c][:, sl], st[hd].astype(BF16)))
            st[hd] = st[hd] * decay[c][:, sl] + upd[c][hd]
        o_chunks.append(jnp.concatenate(o_heads, axis=-1))
    for hd in range(H_REC):
        st_ref[hd] = st[hd]

    up_next = _mlp_up(h2, w_up_ref, 3)
    mlp += _mlp_down(up, w_up_ref, w_down_ref, 2)
    up = up_next

    o_rec = _head_out(jnp.concatenate(o_chunks, axis=0), g, g_onorm_ref[...])
    mix_ref[:, 0:d_rec] = o_rec.astype(BF16)

    u = gc * hv
    ubuf_ref[SUBLANES:SUBLANES + tl, :] = u
    w_conv = w_conv_ref[...]
    y_conv = (w_conv[0:1] * ubuf_ref[SUBLANES - 2:SUBLANES - 2 + tl, :]
              + w_conv[1:2] * ubuf_ref[SUBLANES - 1:SUBLANES - 1 + tl, :]
              + w_conv[2:3] * u)
    mix_ref[:, d_rec:d_rec + d_conv] = (gb * y_conv).astype(BF16)
    tail = u[tl - (CONV_W - 1):tl]
    ubuf_ref[SUBLANES - (CONV_W - 1):SUBLANES, :] = tail

    mlp += _mlp_down(up, w_up_ref, w_down_ref, 3)
    y_ref[...] = _final_norm(x1_ref[...], g2_prev, mlp, g_final_ref)

    x1 = x + g1 * _bdot(mix_ref[...], w_out_ref[...])
    x1_ref[...] = x1
    h2_ref[...] = _modulate(x1, sh2, sc2)

    @pl.when(live & (l == n_tiles - 1))
    def _():
        for hd in range(H_REC):
            rec_ref[hd] = st_ref[hd].T
        conv_ref[...] = tail


def _prompt_layer(x, mod_p, lower_bounds, w_in, w_conv, g_onorm, w_out, w_up, w_down, g_final):
    bsz, seq, d = x.shape
    d_rec = lower_bounds.shape[1]
    dk = d_rec // H_REC
    d_conv = w_conv.shape[1]
    tl = PROMPT_TILE
    n_tiles = seq // tl
    n_steps = bsz * n_tiles
    assert seq % tl == 0 and tl % CHUNK == 0 and MLP_BLOCKS == 4
    mats = (w_in, w_out, w_up, w_down)
    hbm = pl.BlockSpec(memory_space=pl.ANY)

    def mix_tile(s):
        t = jnp.minimum(s, n_steps - 1)
        return lax.div(t, n_tiles), lax.rem(t, n_tiles)

    def mlp_tile(s):
        t = jnp.maximum(s - 1, 0)
        return lax.div(t, n_tiles), lax.rem(t, n_tiles)

    return pl.pallas_call(
        functools.partial(_prompt_kernel, n_tiles=n_tiles, n_steps=n_steps),
        grid=(n_steps + 1,),
        in_specs=[
            pl.BlockSpec((None, tl, d), lambda s: (*mix_tile(s), 0)),
            _const_spec(mod_p.shape),
            _const_spec(lower_bounds.shape),
            hbm,
            _const_spec(w_conv.shape),
            _const_spec(g_onorm.shape),
            hbm,
            hbm,
            hbm,
            _const_spec(g_final.shape),
        ],
        out_specs=[
            pl.BlockSpec((None, tl, d), lambda s: (*mlp_tile(s), 0)),
            pl.BlockSpec((None, H_REC, dk, dk), lambda s: (mix_tile(s)[0], 0, 0, 0)),
            pl.BlockSpec((None, CONV_W - 1, d_conv), lambda s: (mix_tile(s)[0], 0, 0)),
        ] + [hbm] * len(mats),
        out_shape=[
            jax.ShapeDtypeStruct((bsz, seq, d), F32),
            jax.ShapeDtypeStruct((bsz, H_REC, dk, dk), F32),
            jax.ShapeDtypeStruct((bsz, CONV_W - 1, d_conv), F32),
        ] + [jax.ShapeDtypeStruct(w.shape, BF16) for w in mats],
        scratch_shapes=[
            pltpu.VMEM((H_REC, dk, dk), F32),
            pltpu.VMEM((SUBLANES + tl, d_conv), F32),
            pltpu.VMEM((tl, d_rec + d_conv), BF16),
            pltpu.VMEM((tl, d), F32),
            pltpu.VMEM((tl, d), BF16),
        ] + [pltpu.VMEM(w.shape, BF16) for w in mats] + [
            pltpu.VMEM((2, *WEIGHT_STAGE_SHAPE), F32),
            pltpu.SemaphoreType.DMA((2,)),
            pltpu.SemaphoreType.DMA((len(mats),)),
        ],
        compiler_params=pltpu.CompilerParams(
            dimension_semantics=("arbitrary",), vmem_limit_bytes=VMEM_LIMIT_BYTES),
        name="prompt_layer",
    )(x, mod_p, lower_bounds, w_in, w_conv, g_onorm, w_out, w_up, w_down, g_final)


def _sample_kernel(x_ref, mod_ref, lb_ref, w_in_ref, w_conv_ref, g_onorm_ref, w_out_ref,
                   w_up_ref, w_down_ref, g_final_ref, rec_in_ref, conv_in_ref,
                   y_ref, rec_ref, conv_ref,
                   modx_ref, cbx_ref, ubuf_ref, mix_ref, *, seq):
    rows, d = x_ref.shape
    tb = rows // seq
    d_rec = lb_ref.shape[1]
    dk = d_rec // H_REC
    d_conv = conv_in_ref.shape[2]

    for b in range(tb):
        r = slice(b * seq, (b + 1) * seq)
        modx_ref[r, :] = jnp.broadcast_to(mod_ref[b:b + 1, :], (seq, mod_ref.shape[1]))
        for j in range(CONV_W - 1):
            cbx_ref[j, r, :] = jnp.broadcast_to(conv_in_ref[b, j:j + 1, :], (seq, d_conv))

    sh1, sc1, g1, sh2, sc2, g2 = (modx_ref[:, i * d:(i + 1) * d] for i in range(N_MOD))

    x = x_ref[...]
    proj = _bdot(_modulate(x, sh1, sc1), w_in_ref[...])
    q, fz, iv, g = (proj[:, i * d_rec:(i + 1) * d_rec] for i in range(4))
    gb, gc, hv = (proj[:, 4 * d_rec + i * d_conv:4 * d_rec + (i + 1) * d_conv] for i in range(3))

    lb = _layer_lower_bound(lb_ref, 0)
    row = lax.broadcasted_iota(jnp.int32, (rows, rows), 0)
    col = lax.broadcasted_iota(jnp.int32, (rows, rows), 1)
    causal = ((row // seq) == (col // seq)) & (row >= col)
    logf, k = _gates(fz, lb)
    cum = _group_cumsum(logf, seq)
    last = _group_last(cum, seq)
    q_dec = (q * jnp.exp(cum)).astype(BF16)
    k_dec = (k * jnp.exp(-cum)).astype(BF16)
    k_end = k * jnp.exp(last - cum)
    decay = jnp.exp(last)
    vb = iv.astype(BF16)

    d_hi = decay.astype(BF16).astype(F32)
    r1 = decay - d_hi
    d_mid = r1.astype(BF16).astype(F32)
    d_lo = r1 - d_mid
    tok_rec = _group_pos((rows, d_rec), seq)
    dec3 = jnp.where(tok_rec == 0, d_hi,
                     jnp.where(tok_rec == 1, d_mid, jnp.where(tok_rec == 2, d_lo, 0.0)))
    sub = lax.broadcasted_iota(jnp.int32, (seq, dk), 0)
    ones3 = jnp.where(sub < 3, 1.0, 0.0)
    zeros_blk = jnp.zeros((seq, dk), F32)
    rhs_bottom = jnp.concatenate([zeros_blk, ones3], axis=1)

    o_intra = []
    for hd in range(H_REC):
        sl = slice(hd * dk, (hd + 1) * dk)
        scores = jnp.where(causal, _dot_nt(q_dec[:, sl], k_dec[:, sl]), 0.0)
        o_intra.append(_bdot(scores.astype(BF16), vb[:, sl]))
    o_intra = jnp.concatenate(o_intra, axis=-1)

    o_rows = []
    for b in range(tb):
        r0 = b * seq
        r = slice(r0, r0 + seq)
        o_heads = []
        for hd in range(H_REC):
            sl = slice(hd * dk, (hd + 1) * dk)
            s0 = rec_in_ref[b, hd]
            o_heads.append(_bdot(q_dec[r, sl], s0.astype(BF16)))
            lhs = jnp.concatenate([k_end[r, sl], dec3[r, sl]], axis=0)
            rhs = jnp.concatenate(
                [jnp.concatenate([iv[r, sl], zeros_blk], axis=1), rhs_bottom], axis=0)
            both = _dot_tn(lhs.astype(BF16), rhs.astype(BF16))
            rec_ref[b, hd] = both[:, dk:] * s0 + both[:, :dk]
        o_rows.append(jnp.concatenate(o_heads, axis=-1))
    o_rec = _head_out(o_intra + jnp.concatenate(o_rows, axis=0), g, g_onorm_ref[...])
    mix_ref[:, 0:d_rec] = o_rec.astype(BF16)

    u = gc * hv
    ubuf_ref[SUBLANES:SUBLANES + rows, :] = u
    tok = _group_pos((rows, d_conv), seq)
    u_m1 = jnp.where(tok >= 1, ubuf_ref[SUBLANES - 1:SUBLANES - 1 + rows, :], cbx_ref[1])
    u_m2 = jnp.where(tok >= 2, ubuf_ref[SUBLANES - 2:SUBLANES - 2 + rows, :],
                     jnp.where(tok == 1, cbx_ref[1], cbx_ref[0]))
    w_conv = w_conv_ref[...]
    y_conv = w_conv[0:1] * u_m2 + w_conv[1:2] * u_m1 + w_conv[2:3] * u
    mix_ref[:, d_rec:d_rec + d_conv] = (gb * y_conv).astype(BF16)
    for b in range(tb):
        conv_ref[b] = u[(b + 1) * seq - (CONV_W - 1):(b + 1) * seq]

    x1 = x + g1 * _bdot(mix_ref[...], w_out_ref[...])
    h2 = _modulate(x1, sh2, sc2)
    mlp = _mlp_down(_mlp_up(h2, w_up_ref, 0), w_up_ref, w_down_ref, 0)
    for j in range(1, MLP_BLOCKS):
        mlp += _mlp_down(_mlp_up(h2, w_up_ref, j), w_up_ref, w_down_ref, j)
    y_ref[...] = _final_norm(x1, g2, mlp, g_final_ref)


def _sample_layer(x2d, seq, mod, lower_bounds, w_in, w_conv, g_onorm, w_out, w_up, w_down,
                  g_final, rec_in, conv_in):
    n_rows, d = x2d.shape
    bsz = n_rows // seq
    d_rec = lower_bounds.shape[1]
    dk = d_rec // H_REC
    d_conv = w_conv.shape[1]
    tb = SAMPLE_TILE_B
    rows = tb * seq
    assert bsz % tb == 0 and seq == SUBLANES and seq >= CONV_W
    return pl.pallas_call(
        functools.partial(_sample_kernel, seq=seq),
        grid=(bsz // tb,),
        in_specs=[
            pl.BlockSpec((rows, d), lambda i: (i, 0)),
            pl.BlockSpec((tb, N_MOD * d), lambda i: (i, 0)),
            _const_spec(lower_bounds.shape),
            _const_spec(w_in.shape),
            _const_spec(w_conv.shape),
            _const_spec(g_onorm.shape),
            _const_spec(w_out.shape),
            _const_spec(w_up.shape),
            _const_spec(w_down.shape),
            _const_spec(g_final.shape),
            pl.BlockSpec((tb, H_REC, dk, dk), lambda i: (i, 0, 0, 0)),
            pl.BlockSpec((tb, CONV_W - 1, d_conv), lambda i: (i, 0, 0)),
        ],
        out_specs=[
            pl.BlockSpec((rows, d), lambda i: (i, 0)),
            pl.BlockSpec((tb, H_REC, dk, dk), lambda i: (i, 0, 0, 0)),
            pl.BlockSpec((tb, CONV_W - 1, d_conv), lambda i: (i, 0, 0)),
        ],
        out_shape=[
            jax.ShapeDtypeStruct((n_rows, d), F32),
            jax.ShapeDtypeStruct((bsz, H_REC, dk, dk), F32),
            jax.ShapeDtypeStruct((bsz, CONV_W - 1, d_conv), F32),
        ],
        scratch_shapes=[
            pltpu.VMEM((rows, N_MOD * d), F32),
            pltpu.VMEM((CONV_W - 1, rows, d_conv), F32),
            pltpu.VMEM((SUBLANES + rows, d_conv), F32),
            pltpu.VMEM((rows, d_rec + d_conv), BF16),
        ],
        compiler_params=pltpu.CompilerParams(
            dimension_semantics=("arbitrary",), vmem_limit_bytes=VMEM_LIMIT_BYTES),
        name="sample_layer",
    )(x2d, mod, lower_bounds, w_in, w_conv, g_onorm, w_out, w_up, w_down, g_final, rec_in, conv_in)


def kernel(x_prompt, x_sample, state_rec, state_conv, c_prompt, c_sample, lower_bounds, w_ada,
           b_ada, w_in, w_conv, g_onorm, w_out, w_up, w_down, g_final):
    depth = w_in.shape[0]
    assert depth == 1, "single-layer trunk"
    bsz_s, seq_s, d = x_sample.shape
    bsz_p = x_prompt.shape[0]

    c_all = jnp.concatenate([c_sample, c_prompt], axis=0)
    mod = _modulation(c_all, w_ada[0], b_ada)
    mod_p = mod[bsz_s:bsz_s + bsz_p].reshape(bsz_p, 1, N_MOD * d)

    g_final = g_final.reshape(1, d)
    y_p, rec_p, conv_p, w_in_b, w_out_b, w_up_b, w_down_b = _prompt_layer(
        x_prompt, mod_p, lower_bounds, w_in[0], w_conv[0], g_onorm, w_out[0], w_up[0], w_down[0],
        g_final)
    y_s, rec_s, conv_s = _sample_layer(
        x_sample.reshape(bsz_s * seq_s, d), seq_s, mod, lower_bounds, w_in_b, w_conv[0], g_onorm,
        w_out_b, w_up_b, w_down_b, g_final, state_rec[0], state_conv[0])
    return (y_p, y_s.reshape(bsz_s, seq_s, d), rec_p[None], conv_p[None], rec_s[None],
            conv_s[None])
```

```python
import functools

import jax
import jax.numpy as jnp
from jax import lax
from jax.experimental import pallas as pl
from jax.experimental.pallas import tpu as pltpu

F32 = jnp.float32
BF16 = jnp.bfloat16

EPS = 1e-6
H_REC = 4
CONV_W = 3
N_MOD = 6
CHUNK = 64

SUBLANES = 8
VMEM_LIMIT_BYTES = 56 * 1024 * 1024

PROMPT_TILE = 512
SAMPLE_TILE_B = 16
MOD_TILE_N = 1024
MLP_BLOCKS = 4
WEIGHT_STAGE_SHAPE = (512, 512)
WEIGHT_STAGE_SLOTS = 4


def _const_spec(shape):
    zeros = (0,) * len(shape)
    return pl.BlockSpec(shape, lambda *_: zeros, pipeline_mode=pl.Buffered(1))


def _rms(x):
    return x * lax.rsqrt(jnp.mean(x * x, axis=-1, keepdims=True) + EPS)


def _silu(x):
    return x * jax.nn.sigmoid(x)


def _bdot(a, b):
    return jnp.dot(a, b, preferred_element_type=F32)


def _dot_nt(a, b):
    return lax.dot_general(a, b, (((1,), (1,)), ((), ())), preferred_element_type=F32)


def _dot_tn(a, b):
    return lax.dot_general(a, b, (((0,), (0,)), ((), ())), preferred_element_type=F32)


def _group_pos(shape, group):
    return lax.broadcasted_iota(jnp.int32, shape, 0) % group


def _group_cumsum(x, group):
    pos = _group_pos(x.shape, group)
    step = 1
    while step < group:
        x = x + jnp.where(pos >= step, pltpu.roll(x, step, axis=0), 0.0)
        step *= 2
    return x


def _group_last(x, group):
    rows = x.shape[0]
    pos = _group_pos(x.shape, group)
    x = jnp.where(pos == group - 1, x, 0.0)
    step = 1
    while step < group:
        x = x + jnp.where(pos + step < group, pltpu.roll(x, rows - step, axis=0), 0.0)
        step *= 2
    return x


def _layer_lower_bound(lb_ref, layer):
    lb = lb_ref[...]
    e = jnp.exp(lb - jnp.max(lb, axis=0, keepdims=True))
    sm = e / jnp.sum(e, axis=0, keepdims=True)
    return jnp.sum(sm[: layer + 1], axis=0, keepdims=True)


def _gates(fz, lb):
    sig = jax.nn.sigmoid(fz)
    logf = jnp.log(lb + (1.0 - lb) * sig)
    k = (1.0 - lb) * (1.0 - sig)
    return logf, k


def _head_out(o, g, g_onorm):
    dv = o.shape[-1] // H_REC
    pieces = []
    for h in range(H_REC):
        sl = slice(h * dv, (h + 1) * dv)
        pieces.append(_rms(o[:, sl]))
    return (jnp.concatenate(pieces, axis=-1) * g_onorm) * _silu(g)


def _modulate(x, shift, scale):
    return (_rms(x) * (1.0 + scale) + shift).astype(BF16)


def _mlp_cols(w_up_ref, j):
    blk = w_up_ref.shape[1] // MLP_BLOCKS
    return slice(j * blk, (j + 1) * blk)


def _mlp_up(h2, w_up_ref, j):
    up = jnp.maximum(_bdot(h2, w_up_ref[:, _mlp_cols(w_up_ref, j)]), 0.0)
    return (up * up).astype(BF16)


def _mlp_down(up, w_up_ref, w_down_ref, j):
    return _bdot(up, w_down_ref[_mlp_cols(w_up_ref, j), :])


def _final_norm(x, g2, mlp, g_final_ref):
    return _rms(x + g2 * mlp) * g_final_ref[...]


def _mod_kernel(c_ref, w_ref, b_ref, o_ref):
    s = _silu(c_ref[...]).astype(BF16)
    o_ref[...] = _bdot(s, w_ref[...].astype(BF16)) + b_ref[...]


def _modulation(c_all, w_ada, b_ada):
    rows, d = c_all.shape
    n = w_ada.shape[1]
    return pl.pallas_call(
        _mod_kernel,
        grid=(n // MOD_TILE_N,),
        in_specs=[
            pl.BlockSpec((rows, d), lambda j: (0, 0)),
            pl.BlockSpec((d, MOD_TILE_N), lambda j: (0, j)),
            pl.BlockSpec((1, MOD_TILE_N), lambda j: (0, j)),
        ],
        out_specs=pl.BlockSpec((rows, MOD_TILE_N), lambda j: (0, j)),
        out_shape=jax.ShapeDtypeStruct((rows, n), F32),
        compiler_params=pltpu.CompilerParams(
            dimension_semantics=("arbitrary",), vmem_limit_bytes=VMEM_LIMIT_BYTES),
        name="adaln_modulation",
    )(c_all, w_ada, b_ada)


def _weight_blocks(w_hbm, w_vmem):
    rows, cols = w_hbm.shape
    br, bc = WEIGHT_STAGE_SHAPE
    assert rows % br == 0 and cols % bc == 0
    return [(w_hbm.at[r:r + br, c:c + bc], w_vmem.at[r:r + br, c:c + bc])
            for r in range(0, rows, br) for c in range(0, cols, bc)]


def _stage_copy(src, stage_ref, sem_ref, i):
    slot = i % WEIGHT_STAGE_SLOTS
    return pltpu.make_async_copy(src, stage_ref.at[slot], sem_ref.at[slot])


def _load_weights_as_bf16(w_hbm_refs, w_vmem_refs, stage_ref, sem_ref):
    blocks = [blk for w_hbm, w_vmem in zip(w_hbm_refs, w_vmem_refs)
              for blk in _weight_blocks(w_hbm, w_vmem)]
    ahead = WEIGHT_STAGE_SLOTS - 1
    for i in range(min(ahead, len(blocks))):
        _stage_copy(blocks[i][0], stage_ref, sem_ref, i).start()
    for i, (src, dst) in enumerate(blocks):
        if i + ahead < len(blocks):
            _stage_copy(blocks[i + ahead][0], stage_ref, sem_ref, i + ahead).start()
        _stage_copy(src, stage_ref, sem_ref, i).wait()
        dst[...] = stage_ref[i % WEIGHT_STAGE_SLOTS].astype(BF16)


def _prompt_kernel(x_ref, mod_ref, lb_ref, w_in_hbm, w_conv_ref, g_onorm_ref, w_out_hbm,
                   w_up_hbm, w_down_hbm, g_final_ref,
                   y_ref, rec_ref, conv_ref, w_in_out, w_out_out, w_up_out, w_down_out,
                   st_ref, ubuf_ref, mix_ref, x1_ref, h2_ref,
                   w_in_ref, w_out_ref, w_up_ref, w_down_ref, stage_ref, stage_sem, out_sem,
                   *, n_tiles, n_steps):
    s = pl.program_id(0)
    tile = jnp.minimum(s, n_steps - 1)
    l = lax.rem(tile, n_tiles)
    live = s < n_steps
    tl, d = x_ref.shape
    d_rec = lb_ref.shape[1]
    dk = d_rec // H_REC
    n_chunks = tl // CHUNK
    heads = [slice(hd * dk, (hd + 1) * dk) for hd in range(H_REC)]

    w_vmem = (w_in_ref, w_out_ref, w_up_ref, w_down_ref)
    w_outs = (w_in_out, w_out_out, w_up_out, w_down_out)

    def bf16_export(i):
        return pltpu.make_async_copy(w_vmem[i], w_outs[i], out_sem.at[i])

    @pl.when(s == 0)
    def _():
        x1_ref[...] = jnp.zeros_like(x1_ref)
        h2_ref[...] = jnp.zeros_like(h2_ref)
        _load_weights_as_bf16((w_in_hbm, w_out_hbm, w_up_hbm, w_down_hbm), w_vmem,
                              stage_ref, stage_sem)
        for i in range(len(w_vmem)):
            bf16_export(i).start()

    @pl.when(s == n_steps)
    def _():
        for i in range(len(w_vmem)):
            bf16_export(i).wait()

    @pl.when(l == 0)
    def _():
        st_ref[...] = jnp.zeros_like(st_ref)
        ubuf_ref[0:SUBLANES, :] = jnp.zeros((SUBLANES, ubuf_ref.shape[1]), F32)

    mod_prev = mod_ref[lax.div(jnp.maximum(s - 1, 0), n_tiles)]
    g2_prev = mod_prev[:, (N_MOD - 1) * d:]
    mod = mod_ref[lax.div(tile, n_tiles)]
    sh1, sc1, g1, sh2, sc2 = (mod[:, i * d:(i + 1) * d] for i in range(N_MOD - 1))

    h2 = h2_ref[...]
    up = _mlp_up(h2, w_up_ref, 0)

    x = x_ref[...]
    proj = _bdot(_modulate(x, sh1, sc1), w_in_ref[...])
    q, fz, iv, g = (proj[:, i * d_rec:(i + 1) * d_rec] for i in range(4))
    d_conv = (proj.shape[1] - 4 * d_rec) // 3
    gb, gc, hv = (proj[:, 4 * d_rec + i * d_conv:4 * d_rec + (i + 1) * d_conv] for i in range(3))

    up_next = _mlp_up(h2, w_up_ref, 1)
    mlp = _mlp_down(up, w_up_ref, w_down_ref, 0)
    up = up_next

    lb = _layer_lower_bound(lb_ref, 0)
    row = lax.broadcasted_iota(jnp.int32, (CHUNK, CHUNK), 0)
    col = lax.broadcasted_iota(jnp.int32, (CHUNK, CHUNK), 1)
    causal = row >= col
    logf, k_all = _gates(fz, lb)
    cum_all = _group_cumsum(logf, CHUNK)
    q_dec, decay, scores, upd = [], [], [], []
    for c in range(n_chunks):
        rows = slice(c * CHUNK, (c + 1) * CHUNK)
        cum = cum_all[rows]
        last = cum[CHUNK - 1:CHUNK, :]
        q_dec.append((q[rows] * jnp.exp(cum)).astype(BF16))
        k_dec = (k_all[rows] * jnp.exp(-cum)).astype(BF16)
        k_end = (k_all[rows] * jnp.exp(last - cum)).astype(BF16)
        decay.append(jnp.exp(last))
        v = iv[rows].astype(BF16)
        scores.append([jnp.where(causal, _dot_nt(q_dec[c][:, sl], k_dec[:, sl]), 0.0).astype(BF16)
                       for sl in heads])
        upd.append([_dot_tn(v[:, sl], k_end[:, sl]) for sl in heads])

    up_next = _mlp_up(h2, w_up_ref, 2)
    mlp += _mlp_down(up, w_up_ref, w_down_ref, 1)
    up = up_next

    st = [st_ref[hd] for hd in range(H_REC)]
    o_chunks = []
    for c in range(n_chunks):
        v = iv[c * CHUNK:(c + 1) * CHUNK].astype(BF16)
        o_heads = []
        for hd, sl in enumerate(heads):
            o_heads.append(_bdot(scores[c][hd], v[:, sl])
                           + _dot_nt(q_dec[c][:, sl], st[hd].astype(BF16)))
            st[hd] = st[hd] * decay[c][:, sl] + upd[c][hd]
        o_chunks.append(jnp.concatenate(o_heads, axis=-1))
    for hd in range(H_REC):
        st_ref[hd] = st[hd]

    up_next = _mlp_up(h2, w_up_ref, 3)
    mlp += _mlp_down(up, w_up_ref, w_down_ref, 2)
    up = up_next

    o_rec = _head_out(jnp.concatenate(o_chunks, axis=0), g, g_onorm_ref[...])
    mix_ref[:, 0:d_rec] = o_rec.astype(BF16)

    u = gc * hv
    ubuf_ref[SUBLANES:SUBLANES + tl, :] = u
    w_conv = w_conv_ref[...]
    y_conv = (w_conv[0:1] * ubuf_ref[SUBLANES - 2:SUBLANES - 2 + tl, :]
              + w_conv[1:2] * ubuf_ref[SUBLANES - 1:SUBLANES - 1 + tl, :]
              + w_conv[2:3] * u)
    mix_ref[:, d_rec:d_rec + d_conv] = (gb * y_conv).astype(BF16)
    tail = u[tl - (CONV_W - 1):tl]
    ubuf_ref[SUBLANES - (CONV_W - 1):SUBLANES, :] = tail

    mlp += _mlp_down(up, w_up_ref, w_down_ref, 3)
    y_ref[...] = _final_norm(x1_ref[...], g2_prev, mlp, g_final_ref)

    x1 = x + g1 * _bdot(mix_ref[...], w_out_ref[...])
    x1_ref[...] = x1
    h2_ref[...] = _modulate(x1, sh2, sc2)

    @pl.when(live & (l == n_tiles - 1))
    def _():
        for hd in range(H_REC):
            rec_ref[hd] = st_ref[hd].T
        conv_ref[...] = tail


def _prompt_layer(x, mod_p, lower_bounds, w_in, w_conv, g_onorm, w_out, w_up, w_down, g_final):
    bsz, seq, d = x.shape
    d_rec = lower_bounds.shape[1]
    dk = d_rec // H_REC
    d_conv = w_conv.shape[1]
    tl = PROMPT_TILE
    n_tiles = seq // tl
    n_steps = bsz * n_tiles
    assert seq % tl == 0 and tl % CHUNK == 0 and MLP_BLOCKS == 4
    mats = (w_in, w_out, w_up, w_down)
    hbm = pl.BlockSpec(memory_space=pl.ANY)

    def mix_tile(s):
        t = jnp.minimum(s, n_steps - 1)
        return lax.div(t, n_tiles), lax.rem(t, n_tiles)

    def mlp_tile(s):
        t = jnp.maximum(s - 1, 0)
        return lax.div(t, n_tiles), lax.rem(t, n_tiles)

    return pl.pallas_call(
        functools.partial(_prompt_kernel, n_tiles=n_tiles, n_steps=n_steps),
        grid=(n_steps + 1,),
        in_specs=[
            pl.BlockSpec((None, tl, d), lambda s: (*mix_tile(s), 0)),
            _const_spec(mod_p.shape),
            _const_spec(lower_bounds.shape),
            hbm,
            _const_spec(w_conv.shape),
            _const_spec(g_onorm.shape),
            hbm,
            hbm,
            hbm,
            _const_spec(g_final.shape),
        ],
        out_specs=[
            pl.BlockSpec((None, tl, d), lambda s: (*mlp_tile(s), 0)),
            pl.BlockSpec((None, H_REC, dk, dk), lambda s: (mix_tile(s)[0], 0, 0, 0)),
            pl.BlockSpec((None, CONV_W - 1, d_conv), lambda s: (mix_tile(s)[0], 0, 0)),
        ] + [hbm] * len(mats),
        out_shape=[
            jax.ShapeDtypeStruct((bsz, seq, d), F32),
            jax.ShapeDtypeStruct((bsz, H_REC, dk, dk), F32),
            jax.ShapeDtypeStruct((bsz, CONV_W - 1, d_conv), F32),
        ] + [jax.ShapeDtypeStruct(w.shape, BF16) for w in mats],
        scratch_shapes=[
            pltpu.VMEM((H_REC, dk, dk), F32),
            pltpu.VMEM((SUBLANES + tl, d_conv), F32),
            pltpu.VMEM((tl, d_rec + d_conv), BF16),
            pltpu.VMEM((tl, d), F32),
            pltpu.VMEM((tl, d), BF16),
        ] + [pltpu.VMEM(w.shape, BF16) for w in mats] + [
            pltpu.VMEM((WEIGHT_STAGE_SLOTS, *WEIGHT_STAGE_SHAPE), F32),
            pltpu.SemaphoreType.DMA((WEIGHT_STAGE_SLOTS,)),
            pltpu.SemaphoreType.DMA((len(mats),)),
        ],
        compiler_params=pltpu.CompilerParams(
            dimension_semantics=("arbitrary",), vmem_limit_bytes=VMEM_LIMIT_BYTES),
        name="prompt_layer",
    )(x, mod_p, lower_bounds, w_in, w_conv, g_onorm, w_out, w_up, w_down, g_final)


def _sample_kernel(x_ref, mod_ref, lb_ref, w_in_ref, w_conv_ref, g_onorm_ref, w_out_ref,
                   w_up_ref, w_down_ref, g_final_ref, rec_in_ref, conv_in_ref,
                   y_ref, rec_ref, conv_ref,
                   modx_ref, cbx_ref, ubuf_ref, mix_ref, *, seq):
    rows, d = x_ref.shape
    tb = rows // seq
    d_rec = lb_ref.shape[1]
    dk = d_rec // H_REC
    d_conv = conv_in_ref.shape[2]

    for b in range(tb):
        r = slice(b * seq, (b + 1) * seq)
        modx_ref[r, :] = jnp.broadcast_to(mod_ref[b:b + 1, :], (seq, mod_ref.shape[1]))
        for j in range(CONV_W - 1):
            cbx_ref[j, r, :] = jnp.broadcast_to(conv_in_ref[b, j:j + 1, :], (seq, d_conv))

    sh1, sc1, g1, sh2, sc2, g2 = (modx_ref[:, i * d:(i + 1) * d] for i in range(N_MOD))

    x = x_ref[...]
    proj = _bdot(_modulate(x, sh1, sc1), w_in_ref[...])
    q, fz, iv, g = (proj[:, i * d_rec:(i + 1) * d_rec] for i in range(4))
    gb, gc, hv = (proj[:, 4 * d_rec + i * d_conv:4 * d_rec + (i + 1) * d_conv] for i in range(3))

    lb = _layer_lower_bound(lb_ref, 0)
    row = lax.broadcasted_iota(jnp.int32, (rows, rows), 0)
    col = lax.broadcasted_iota(jnp.int32, (rows, rows), 1)
    causal = ((row // seq) == (col // seq)) & (row >= col)
    logf, k = _gates(fz, lb)
    cum = _group_cumsum(logf, seq)
    last = _group_last(cum, seq)
    q_dec = (q * jnp.exp(cum)).astype(BF16)
    k_dec = (k * jnp.exp(-cum)).astype(BF16)
    k_end = k * jnp.exp(last - cum)
    decay = jnp.exp(last)
    vb = iv.astype(BF16)

    d_hi = decay.astype(BF16).astype(F32)
    r1 = decay - d_hi
    d_mid = r1.astype(BF16).astype(F32)
    d_lo = r1 - d_mid
    tok_rec = _group_pos((rows, d_rec), seq)
    dec3 = jnp.where(tok_rec == 0, d_hi,
                     jnp.where(tok_rec == 1, d_mid, jnp.where(tok_rec == 2, d_lo, 0.0)))
    sub = lax.broadcasted_iota(jnp.int32, (seq, dk), 0)
    ones3 = jnp.where(sub < 3, 1.0, 0.0)
    zeros_blk = jnp.zeros((seq, dk), F32)
    rhs_bottom = jnp.concatenate([zeros_blk, ones3], axis=1)

    o_intra = []
    for hd in range(H_REC):
        sl = slice(hd * dk, (hd + 1) * dk)
        scores = jnp.where(causal, _dot_nt(q_dec[:, sl], k_dec[:, sl]), 0.0)
        o_intra.append(_bdot(scores.astype(BF16), vb[:, sl]))
    o_intra = jnp.concatenate(o_intra, axis=-1)

    o_rows = []
    for b in range(tb):
        r0 = b * seq
        r = slice(r0, r0 + seq)
        o_heads = []
        for hd in range(H_REC):
            sl = slice(hd * dk, (hd + 1) * dk)
            s0 = rec_in_ref[b, hd]
            o_heads.append(_bdot(q_dec[r, sl], s0.astype(BF16)))
            lhs = jnp.concatenate([k_end[r, sl], dec3[r, sl]], axis=0)
            rhs = jnp.concatenate(
                [jnp.concatenate([iv[r, sl], zeros_blk], axis=1), rhs_bottom], axis=0)
            both = _dot_tn(lhs.astype(BF16), rhs.astype(BF16))
            rec_ref[b, hd] = both[:, dk:] * s0 + both[:, :dk]
        o_rows.append(jnp.concatenate(o_heads, axis=-1))
    o_rec = _head_out(o_intra + jnp.concatenate(o_rows, axis=0), g, g_onorm_ref[...])
    mix_ref[:, 0:d_rec] = o_rec.astype(BF16)

    u = gc * hv
    ubuf_ref[SUBLANES:SUBLANES + rows, :] = u
    tok = _group_pos((rows, d_conv), seq)
    u_m1 = jnp.where(tok >= 1, ubuf_ref[SUBLANES - 1:SUBLANES - 1 + rows, :], cbx_ref[1])
    u_m2 = jnp.where(tok >= 2, ubuf_ref[SUBLANES - 2:SUBLANES - 2 + rows, :],
                     jnp.where(tok == 1, cbx_ref[1], cbx_ref[0]))
    w_conv = w_conv_ref[...]
    y_conv = w_conv[0:1] * u_m2 + w_conv[1:2] * u_m1 + w_conv[2:3] * u
    mix_ref[:, d_rec:d_rec + d_conv] = (gb * y_conv).astype(BF16)
    for b in range(tb):
        conv_ref[b] = u[(b + 1) * seq - (CONV_W - 1):(b + 1) * seq]

    x1 = x + g1 * _bdot(mix_ref[...], w_out_ref[...])
    h2 = _modulate(x1, sh2, sc2)
    mlp = _mlp_down(_mlp_up(h2, w_up_ref, 0), w_up_ref, w_down_ref, 0)
    for j in range(1, MLP_BLOCKS):
        mlp += _mlp_down(_mlp_up(h2, w_up_ref, j), w_up_ref, w_down_ref, j)
    y_ref[...] = _final_norm(x1, g2, mlp, g_final_ref)


def _sample_layer(x2d, seq, mod, lower_bounds, w_in, w_conv, g_onorm, w_out, w_up, w_down,
                  g_final, rec_in, conv_in):
    n_rows, d = x2d.shape
    bsz = n_rows // seq
    d_rec = lower_bounds.shape[1]
    dk = d_rec // H_REC
    d_conv = w_conv.shape[1]
    tb = SAMPLE_TILE_B
    rows = tb * seq
    assert bsz % tb == 0 and seq == SUBLANES and seq >= CONV_W
    return pl.pallas_call(
        functools.partial(_sample_kernel, seq=seq),
        grid=(bsz // tb,),
        in_specs=[
            pl.BlockSpec((rows, d), lambda i: (i, 0)),
            pl.BlockSpec((tb, N_MOD * d), lambda i: (i, 0)),
            _const_spec(lower_bounds.shape),
            _const_spec(w_in.shape),
            _const_spec(w_conv.shape),
            _const_spec(g_onorm.shape),
            _const_spec(w_out.shape),
            _const_spec(w_up.shape),
            _const_spec(w_down.shape),
            _const_spec(g_final.shape),
            pl.BlockSpec((tb, H_REC, dk, dk), lambda i: (i, 0, 0, 0)),
            pl.BlockSpec((tb, CONV_W - 1, d_conv), lambda i: (i, 0, 0)),
        ],
        out_specs=[
            pl.BlockSpec((rows, d), lambda i: (i, 0)),
            pl.BlockSpec((tb, H_REC, dk, dk), lambda i: (i, 0, 0, 0)),
            pl.BlockSpec((tb, CONV_W - 1, d_conv), lambda i: (i, 0, 0)),
        ],
        out_shape=[
            jax.ShapeDtypeStruct((n_rows, d), F32),
            jax.ShapeDtypeStruct((bsz, H_REC, dk, dk), F32),
            jax.ShapeDtypeStruct((bsz, CONV_W - 1, d_conv), F32),
        ],
        scratch_shapes=[
            pltpu.VMEM((rows, N_MOD * d), F32),
            pltpu.VMEM((CONV_W - 1, rows, d_conv), F32),
            pltpu.VMEM((SUBLANES + rows, d_conv), F32),
            pltpu.VMEM((rows, d_rec + d_conv), BF16),
        ],
        compiler_params=pltpu.CompilerParams(
            dimension_semantics=("arbitrary",), vmem_limit_bytes=VMEM_LIMIT_BYTES),
        name="sample_layer",
    )(x2d, mod, lower_bounds, w_in, w_conv, g_onorm, w_out, w_up, w_down, g_final, rec_in, conv_in)


def kernel(x_prompt, x_sample, state_rec, state_conv, c_prompt, c_sample, lower_bounds, w_ada,
           b_ada, w_in, w_conv, g_onorm, w_out, w_up, w_down, g_final):
    depth = w_in.shape[0]
    assert depth == 1, "single-layer trunk"
    bsz_s, seq_s, d = x_sample.shape
    bsz_p = x_prompt.shape[0]

    c_all = jnp.concatenate([c_sample, c_prompt], axis=0)
    mod = _modulation(c_all, w_ada[0], b_ada)
    mod_p = mod[bsz_s:bsz_s + bsz_p].reshape(bsz_p, 1, N_MOD * d)

    g_final = g_final.reshape(1, d)
    y_p, rec_p, conv_p, w_in_b, w_out_b, w_up_b, w_down_b = _prompt_layer(
        x_prompt, mod_p, lower_bounds, w_in[0], w_conv[0], g_onorm, w_out[0], w_up[0], w_down[0],
        g_final)
    y_s, rec_s, conv_s = _sample_layer(
        x_sample.reshape(bsz_s * seq_s, d), seq_s, mod, lower_bounds, w_in_b, w_conv[0], g_onorm,
        w_out_b, w_up_b, w_down_b, g_final, state_rec[0], state_conv[0])
    return (y_p, y_s.reshape(bsz_s, seq_s, d), rec_p[None], conv_p[None], rec_s[None],
            conv_s[None])
```

```python
import functools

import jax
import jax.numpy as jnp
from jax import lax
from jax.experimental import pallas as pl
from jax.experimental.pallas import tpu as pltpu

F32 = jnp.float32
BF16 = jnp.bfloat16

EPS = 1e-6
H_REC = 4
CONV_W = 3
N_MOD = 6
CHUNK = 64

SUBLANES = 8
VMEM_LIMIT_BYTES = 56 * 1024 * 1024

PROMPT_TILE = 512
SAMPLE_TILE_B = 16
MOD_TILE_N = 1024
MLP_BLOCKS = 4
WEIGHT_STAGE_SHAPE = (512, 512)
WEIGHT_STAGE_SLOTS = 4
PROJ_STORE_ORDER = (4, 5, 6, 3, 2, 0, 1)
N_PROJ_GROUPS = len(PROJ_STORE_ORDER)
N_WEIGHT_EXPORTS = N_PROJ_GROUPS + 3


def _const_spec(shape):
    zeros = (0,) * len(shape)
    return pl.BlockSpec(shape, lambda *_: zeros, pipeline_mode=pl.Buffered(1))


def _rms(x):
    return x * lax.rsqrt(jnp.mean(x * x, axis=-1, keepdims=True) + EPS)


def _silu(x):
    return x * jax.nn.sigmoid(x)


def _bdot(a, b):
    return jnp.dot(a, b, preferred_element_type=F32)


def _dot_nt(a, b):
    return lax.dot_general(a, b, (((1,), (1,)), ((), ())), preferred_element_type=F32)


def _dot_tn(a, b):
    return lax.dot_general(a, b, (((0,), (0,)), ((), ())), preferred_element_type=F32)


def _group_pos(shape, group):
    return lax.broadcasted_iota(jnp.int32, shape, 0) % group


def _group_cumsum(x, group):
    pos = _group_pos(x.shape, group)
    step = 1
    while step < group:
        x = x + jnp.where(pos >= step, pltpu.roll(x, step, axis=0), 0.0)
        step *= 2
    return x


def _group_last(x, group):
    rows = x.shape[0]
    pos = _group_pos(x.shape, group)
    x = jnp.where(pos == group - 1, x, 0.0)
    step = 1
    while step < group:
        x = x + jnp.where(pos + step < group, pltpu.roll(x, rows - step, axis=0), 0.0)
        step *= 2
    return x


def _layer_lower_bound(lb_ref, layer):
    lb = lb_ref[...]
    e = jnp.exp(lb - jnp.max(lb, axis=0, keepdims=True))
    sm = e / jnp.sum(e, axis=0, keepdims=True)
    return jnp.sum(sm[: layer + 1], axis=0, keepdims=True)


def _gates(fz, lb):
    sig = jax.nn.sigmoid(fz)
    logf = jnp.log(lb + (1.0 - lb) * sig)
    k = (1.0 - lb) * (1.0 - sig)
    return logf, k


def _head_out(o, g, g_onorm):
    dv = o.shape[-1] // H_REC
    pieces = []
    for h in range(H_REC):
        sl = slice(h * dv, (h + 1) * dv)
        pieces.append(_rms(o[:, sl]))
    return (jnp.concatenate(pieces, axis=-1) * g_onorm) * _silu(g)


def _modulate(x, shift, scale):
    return (_rms(x) * (1.0 + scale) + shift).astype(BF16)


def _mlp_cols(w_up_ref, j):
    blk = w_up_ref.shape[1] // MLP_BLOCKS
    return slice(j * blk, (j + 1) * blk)


def _mlp_up(h2, w_up_ref, j):
    up = jnp.maximum(_bdot(h2, w_up_ref[:, _mlp_cols(w_up_ref, j)]), 0.0)
    return (up * up).astype(BF16)


def _mlp_down(up, w_up_ref, w_down_ref, j):
    return _bdot(up, w_down_ref[_mlp_cols(w_up_ref, j), :])


def _final_norm(x, g2, mlp, g_final_ref):
    return _rms(x + g2 * mlp) * g_final_ref[...]


def _mod_kernel(c_ref, w_ref, b_ref, o_ref):
    s = _silu(c_ref[...]).astype(BF16)
    o_ref[...] = _bdot(s, w_ref[...].astype(BF16)) + b_ref[...]


def _modulation(c_all, w_ada, b_ada):
    rows, d = c_all.shape
    n = w_ada.shape[1]
    return pl.pallas_call(
        _mod_kernel,
        grid=(n // MOD_TILE_N,),
        in_specs=[
            pl.BlockSpec((rows, d), lambda j: (0, 0)),
            pl.BlockSpec((d, MOD_TILE_N), lambda j: (0, j)),
            pl.BlockSpec((1, MOD_TILE_N), lambda j: (0, j)),
        ],
        out_specs=pl.BlockSpec((rows, MOD_TILE_N), lambda j: (0, j)),
        out_shape=jax.ShapeDtypeStruct((rows, n), F32),
        compiler_params=pltpu.CompilerParams(
            dimension_semantics=("arbitrary",), vmem_limit_bytes=VMEM_LIMIT_BYTES),
        name="adaln_modulation",
    )(c_all, w_ada, b_ada)


def _split_proj(proj, store_order=None):
    store_order = store_order or tuple(range(N_PROJ_GROUPS))
    width = proj.shape[1] // N_PROJ_GROUPS
    at = [store_order.index(grp) for grp in range(N_PROJ_GROUPS)]
    return tuple(proj[:, p * width:(p + 1) * width] for p in at)


def _weight_blocks(w_hbm, w_vmem, col_order=None):
    rows, cols = w_hbm.shape
    br, bc = WEIGHT_STAGE_SHAPE
    assert rows % br == 0 and cols % bc == 0
    col_order = col_order or tuple(range(cols // bc))
    return [(w_hbm.at[r:r + br, src * bc:(src + 1) * bc], w_vmem.at[r:r + br, j * bc:(j + 1) * bc])
            for r in range(0, rows, br) for j, src in enumerate(col_order)]


def _stage_copy(src, stage_ref, sem_ref, i):
    slot = i % WEIGHT_STAGE_SLOTS
    return pltpu.make_async_copy(src, stage_ref.at[slot], sem_ref.at[slot])


def _load_weights_as_bf16(w_hbm_refs, w_vmem_refs, stage_ref, sem_ref):
    assert w_hbm_refs[0].shape[1] == N_PROJ_GROUPS * WEIGHT_STAGE_SHAPE[1]
    orders = (PROJ_STORE_ORDER,) + (None,) * (len(w_hbm_refs) - 1)
    blocks = [blk for w_hbm, w_vmem, order in zip(w_hbm_refs, w_vmem_refs, orders)
              for blk in _weight_blocks(w_hbm, w_vmem, order)]
    ahead = WEIGHT_STAGE_SLOTS - 1
    for i in range(min(ahead, len(blocks))):
        _stage_copy(blocks[i][0], stage_ref, sem_ref, i).start()
    for i, (src, dst) in enumerate(blocks):
        if i + ahead < len(blocks):
            _stage_copy(blocks[i + ahead][0], stage_ref, sem_ref, i + ahead).start()
        _stage_copy(src, stage_ref, sem_ref, i).wait()
        dst[...] = stage_ref[i % WEIGHT_STAGE_SLOTS].astype(BF16)


def _prompt_kernel(x_ref, mod_ref, lb_ref, w_in_hbm, w_conv_ref, g_onorm_ref, w_out_hbm,
                   w_up_hbm, w_down_hbm, g_final_ref,
                   y_ref, rec_ref, conv_ref, w_in_out, w_out_out, w_up_out, w_down_out,
                   st_ref, ubuf_ref, mix_ref, x1_ref, h2_ref,
                   w_in_ref, w_out_ref, w_up_ref, w_down_ref, stage_ref, stage_sem, out_sem,
                   *, n_tiles, n_steps):
    s = pl.program_id(0)
    tile = jnp.minimum(s, n_steps - 1)
    l = lax.rem(tile, n_tiles)
    live = s < n_steps
    tl, d = x_ref.shape
    d_rec = lb_ref.shape[1]
    dk = d_rec // H_REC
    n_chunks = tl // CHUNK
    heads = [slice(hd * dk, (hd + 1) * dk) for hd in range(H_REC)]

    w_vmem = (w_in_ref, w_out_ref, w_up_ref, w_down_ref)
    w_outs = (w_in_out, w_out_out, w_up_out, w_down_out)

    gw = w_in_ref.shape[1] // N_PROJ_GROUPS
    exports = [(w_in_ref.at[:, j * gw:(j + 1) * gw], w_in_out.at[:, grp * gw:(grp + 1) * gw])
               for j, grp in enumerate(PROJ_STORE_ORDER)]
    exports += list(zip(w_vmem[1:], w_outs[1:]))
    assert len(exports) == N_WEIGHT_EXPORTS

    def bf16_export(i):
        return pltpu.make_async_copy(*exports[i], out_sem.at[i])

    @pl.when(s == 0)
    def _():
        x1_ref[...] = jnp.zeros_like(x1_ref)
        h2_ref[...] = jnp.zeros_like(h2_ref)
        _load_weights_as_bf16((w_in_hbm, w_out_hbm, w_up_hbm, w_down_hbm), w_vmem,
                              stage_ref, stage_sem)
        for i in range(N_WEIGHT_EXPORTS):
            bf16_export(i).start()

    @pl.when(s == n_steps)
    def _():
        for i in range(N_WEIGHT_EXPORTS):
            bf16_export(i).wait()

    @pl.when(l == 0)
    def _():
        st_ref[...] = jnp.zeros_like(st_ref)
        ubuf_ref[0:SUBLANES, :] = jnp.zeros((SUBLANES, ubuf_ref.shape[1]), F32)

    mod_prev = mod_ref[lax.div(jnp.maximum(s - 1, 0), n_tiles)]
    g2_prev = mod_prev[:, (N_MOD - 1) * d:]
    mod = mod_ref[lax.div(tile, n_tiles)]
    sh1, sc1, g1, sh2, sc2 = (mod[:, i * d:(i + 1) * d] for i in range(N_MOD - 1))

    h2 = h2_ref[...]
    up = _mlp_up(h2, w_up_ref, 0)

    x = x_ref[...]
    proj = _bdot(_modulate(x, sh1, sc1), w_in_ref[...])
    q, fz, iv, g, gb, gc, hv = _split_proj(proj, PROJ_STORE_ORDER)
    d_conv = gb.shape[1]

    up_next = _mlp_up(h2, w_up_ref, 1)
    mlp = _mlp_down(up, w_up_ref, w_down_ref, 0)
    up = up_next

    lb = _layer_lower_bound(lb_ref, 0)
    row = lax.broadcasted_iota(jnp.int32, (CHUNK, CHUNK), 0)
    col = lax.broadcasted_iota(jnp.int32, (CHUNK, CHUNK), 1)
    causal = row >= col
    logf, k_all = _gates(fz, lb)
    cum_all = _group_cumsum(logf, CHUNK)
    q_dec, decay, scores, upd = [], [], [], []
    for c in range(n_chunks):
        rows = slice(c * CHUNK, (c + 1) * CHUNK)
        cum = cum_all[rows]
        last = cum[CHUNK - 1:CHUNK, :]
        q_dec.append((q[rows] * jnp.exp(cum)).astype(BF16))
        k_dec = (k_all[rows] * jnp.exp(-cum)).astype(BF16)
        k_end = (k_all[rows] * jnp.exp(last - cum)).astype(BF16)
        decay.append(jnp.exp(last))
        v = iv[rows].astype(BF16)
        scores.append([jnp.where(causal, _dot_nt(q_dec[c][:, sl], k_dec[:, sl]), 0.0).astype(BF16)
                       for sl in heads])
        upd.append([_dot_tn(v[:, sl], k_end[:, sl]) for sl in heads])

    up_next = _mlp_up(h2, w_up_ref, 2)
    mlp += _mlp_down(up, w_up_ref, w_down_ref, 1)
    up = up_next

    st = [st_ref[hd] for hd in range(H_REC)]
    o_chunks = []
    for c in range(n_chunks):
        v = iv[c * CHUNK:(c + 1) * CHUNK].astype(BF16)
        o_heads = []
        for hd, sl in enumerate(heads):
            o_heads.append(_bdot(scores[c][hd], v[:, sl])
                           + _dot_nt(q_dec[c][:, sl], st[hd].astype(BF16)))
            st[hd] = st[hd] * decay[c][:, sl] + upd[c][hd]
        o_chunks.append(jnp.concatenate(o_heads, axis=-1))
    for hd in range(H_REC):
        st_ref[hd] = st[hd]

    up_next = _mlp_up(h2, w_up_ref, 3)
    mlp += _mlp_down(up, w_up_ref, w_down_ref, 2)
    up = up_next

    o_rec = _head_out(jnp.concatenate(o_chunks, axis=0), g, g_onorm_ref[...])
    mix_ref[:, 0:d_rec] = o_rec.astype(BF16)

    u = gc * hv
    ubuf_ref[SUBLANES:SUBLANES + tl, :] = u
    w_conv = w_conv_ref[...]
    y_conv = (w_conv[0:1] * ubuf_ref[SUBLANES - 2:SUBLANES - 2 + tl, :]
              + w_conv[1:2] * ubuf_ref[SUBLANES - 1:SUBLANES - 1 + tl, :]
              + w_conv[2:3] * u)
    mix_ref[:, d_rec:d_rec + d_conv] = (gb * y_conv).astype(BF16)
    tail = u[tl - (CONV_W - 1):tl]
    ubuf_ref[SUBLANES - (CONV_W - 1):SUBLANES, :] = tail

    mlp += _mlp_down(up, w_up_ref, w_down_ref, 3)
    y_ref[...] = _final_norm(x1_ref[...], g2_prev, mlp, g_final_ref)

    x1 = x + g1 * _bdot(mix_ref[...], w_out_ref[...])
    x1_ref[...] = x1
    h2_ref[...] = _modulate(x1, sh2, sc2)

    @pl.when(live & (l == n_tiles - 1))
    def _():
        for hd in range(H_REC):
            rec_ref[hd] = st_ref[hd].T
        conv_ref[...] = tail


def _prompt_layer(x, mod_p, lower_bounds, w_in, w_conv, g_onorm, w_out, w_up, w_down, g_final):
    bsz, seq, d = x.shape
    d_rec = lower_bounds.shape[1]
    dk = d_rec // H_REC
    d_conv = w_conv.shape[1]
    tl = PROMPT_TILE
    n_tiles = seq // tl
    n_steps = bsz * n_tiles
    assert seq % tl == 0 and tl % CHUNK == 0 and MLP_BLOCKS == 4
    mats = (w_in, w_out, w_up, w_down)
    hbm = pl.BlockSpec(memory_space=pl.ANY)

    def mix_tile(s):
        t = jnp.minimum(s, n_steps - 1)
        return lax.div(t, n_tiles), lax.rem(t, n_tiles)

    def mlp_tile(s):
        t = jnp.maximum(s - 1, 0)
        return lax.div(t, n_tiles), lax.rem(t, n_tiles)

    return pl.pallas_call(
        functools.partial(_prompt_kernel, n_tiles=n_tiles, n_steps=n_steps),
        grid=(n_steps + 1,),
        in_specs=[
            pl.BlockSpec((None, tl, d), lambda s: (*mix_tile(s), 0)),
            _const_spec(mod_p.shape),
            _const_spec(lower_bounds.shape),
            hbm,
            _const_spec(w_conv.shape),
            _const_spec(g_onorm.shape),
            hbm,
            hbm,
            hbm,
            _const_spec(g_final.shape),
        ],
        out_specs=[
            pl.BlockSpec((None, tl, d), lambda s: (*mlp_tile(s), 0)),
            pl.BlockSpec((None, H_REC, dk, dk), lambda s: (mix_tile(s)[0], 0, 0, 0)),
            pl.BlockSpec((None, CONV_W - 1, d_conv), lambda s: (mix_tile(s)[0], 0, 0)),
        ] + [hbm] * len(mats),
        out_shape=[
            jax.ShapeDtypeStruct((bsz, seq, d), F32),
            jax.ShapeDtypeStruct((bsz, H_REC, dk, dk), F32),
            jax.ShapeDtypeStruct((bsz, CONV_W - 1, d_conv), F32),
        ] + [jax.ShapeDtypeStruct(w.shape, BF16) for w in mats],
        scratch_shapes=[
            pltpu.VMEM((H_REC, dk, dk), F32),
            pltpu.VMEM((SUBLANES + tl, d_conv), F32),
            pltpu.VMEM((tl, d_rec + d_conv), BF16),
            pltpu.VMEM((tl, d), F32),
            pltpu.VMEM((tl, d), BF16),
        ] + [pltpu.VMEM(w.shape, BF16) for w in mats] + [
            pltpu.VMEM((WEIGHT_STAGE_SLOTS, *WEIGHT_STAGE_SHAPE), F32),
            pltpu.SemaphoreType.DMA((WEIGHT_STAGE_SLOTS,)),
            pltpu.SemaphoreType.DMA((N_WEIGHT_EXPORTS,)),
        ],
        compiler_params=pltpu.CompilerParams(
            dimension_semantics=("arbitrary",), vmem_limit_bytes=VMEM_LIMIT_BYTES),
        name="prompt_layer",
    )(x, mod_p, lower_bounds, w_in, w_conv, g_onorm, w_out, w_up, w_down, g_final)


def _sample_kernel(x_ref, mod_ref, lb_ref, w_in_ref, w_conv_ref, g_onorm_ref, w_out_ref,
                   w_up_ref, w_down_ref, g_final_ref, rec_in_ref, conv_in_ref,
                   y_ref, rec_ref, conv_ref,
                   modx_ref, cbx_ref, ubuf_ref, mix_ref, *, seq):
    rows, d = x_ref.shape
    tb = rows // seq
    d_rec = lb_ref.shape[1]
    dk = d_rec // H_REC
    d_conv = conv_in_ref.shape[2]

    for b in range(tb):
        r = slice(b * seq, (b + 1) * seq)
        modx_ref[r, :] = jnp.broadcast_to(mod_ref[b:b + 1, :], (seq, mod_ref.shape[1]))
        for j in range(CONV_W - 1):
            cbx_ref[j, r, :] = jnp.broadcast_to(conv_in_ref[b, j:j + 1, :], (seq, d_conv))

    sh1, sc1, g1, sh2, sc2, g2 = (modx_ref[:, i * d:(i + 1) * d] for i in range(N_MOD))

    x = x_ref[...]
    proj = _bdot(_modulate(x, sh1, sc1), w_in_ref[...])
    q, fz, iv, g, gb, gc, hv = _split_proj(proj)

    lb = _layer_lower_bound(lb_ref, 0)
    row = lax.broadcasted_iota(jnp.int32, (rows, rows), 0)
    col = lax.broadcasted_iota(jnp.int32, (rows, rows), 1)
    causal = ((row // seq) == (col // seq)) & (row >= col)
    logf, k = _gates(fz, lb)
    cum = _group_cumsum(logf, seq)
    last = _group_last(cum, seq)
    q_dec = (q * jnp.exp(cum)).astype(BF16)
    k_dec = (k * jnp.exp(-cum)).astype(BF16)
    k_end = k * jnp.exp(last - cum)
    decay = jnp.exp(last)
    vb = iv.astype(BF16)

    d_hi = decay.astype(BF16).astype(F32)
    r1 = decay - d_hi
    d_mid = r1.astype(BF16).astype(F32)
    d_lo = r1 - d_mid
    tok_rec = _group_pos((rows, d_rec), seq)
    dec3 = jnp.where(tok_rec == 0, d_hi,
                     jnp.where(tok_rec == 1, d_mid, jnp.where(tok_rec == 2, d_lo, 0.0)))
    sub = lax.broadcasted_iota(jnp.int32, (seq, dk), 0)
    ones3 = jnp.where(sub < 3, 1.0, 0.0)
    zeros_blk = jnp.zeros((seq, dk), F32)
    rhs_bottom = jnp.concatenate([zeros_blk, ones3], axis=1)

    o_intra = []
    for hd in range(H_REC):
        sl = slice(hd * dk, (hd + 1) * dk)
        scores = jnp.where(causal, _dot_nt(q_dec[:, sl], k_dec[:, sl]), 0.0)
        o_intra.append(_bdot(scores.astype(BF16), vb[:, sl]))
    o_intra = jnp.concatenate(o_intra, axis=-1)

    o_rows = []
    for b in range(tb):
        r0 = b * seq
        r = slice(r0, r0 + seq)
        o_heads = []
        for hd in range(H_REC):
            sl = slice(hd * dk, (hd + 1) * dk)
            s0 = rec_in_ref[b, hd]
            o_heads.append(_bdot(q_dec[r, sl], s0.astype(BF16)))
            lhs = jnp.concatenate([k_end[r, sl], dec3[r, sl]], axis=0)
            rhs = jnp.concatenate(
                [jnp.concatenate([iv[r, sl], zeros_blk], axis=1), rhs_bottom], axis=0)
            both = _dot_tn(lhs.astype(BF16), rhs.astype(BF16))
            rec_ref[b, hd] = both[:, dk:] * s0 + both[:, :dk]
        o_rows.append(jnp.concatenate(o_heads, axis=-1))
    o_rec = _head_out(o_intra + jnp.concatenate(o_rows, axis=0), g, g_onorm_ref[...])
    mix_ref[:, 0:d_rec] = o_rec.astype(BF16)

    u = gc * hv
    ubuf_ref[SUBLANES:SUBLANES + rows, :] = u
    tok = _group_pos((rows, d_conv), seq)
    u_m1 = jnp.where(tok >= 1, ubuf_ref[SUBLANES - 1:SUBLANES - 1 + rows, :], cbx_ref[1])
    u_m2 = jnp.where(tok >= 2, ubuf_ref[SUBLANES - 2:SUBLANES - 2 + rows, :],
                     jnp.where(tok == 1, cbx_ref[1], cbx_ref[0]))
    w_conv = w_conv_ref[...]
    y_conv = w_conv[0:1] * u_m2 + w_conv[1:2] * u_m1 + w_conv[2:3] * u
    mix_ref[:, d_rec:d_rec + d_conv] = (gb * y_conv).astype(BF16)
    for b in range(tb):
        conv_ref[b] = u[(b + 1) * seq - (CONV_W - 1):(b + 1) * seq]

    x1 = x + g1 * _bdot(mix_ref[...], w_out_ref[...])
    h2 = _modulate(x1, sh2, sc2)
    mlp = _mlp_down(_mlp_up(h2, w_up_ref, 0), w_up_ref, w_down_ref, 0)
    for j in range(1, MLP_BLOCKS):
        mlp += _mlp_down(_mlp_up(h2, w_up_ref, j), w_up_ref, w_down_ref, j)
    y_ref[...] = _final_norm(x1, g2, mlp, g_final_ref)


def _sample_layer(x2d, seq, mod, lower_bounds, w_in, w_conv, g_onorm, w_out, w_up, w_down,
                  g_final, rec_in, conv_in):
    n_rows, d = x2d.shape
    bsz = n_rows // seq
    d_rec = lower_bounds.shape[1]
    dk = d_rec // H_REC
    d_conv = w_conv.shape[1]
    tb = SAMPLE_TILE_B
    rows = tb * seq
    assert bsz % tb == 0 and seq == SUBLANES and seq >= CONV_W
    return pl.pallas_call(
        functools.partial(_sample_kernel, seq=seq),
        grid=(bsz // tb,),
        in_specs=[
            pl.BlockSpec((rows, d), lambda i: (i, 0)),
            pl.BlockSpec((tb, N_MOD * d), lambda i: (i, 0)),
            _const_spec(lower_bounds.shape),
            _const_spec(w_in.shape),
            _const_spec(w_conv.shape),
            _const_spec(g_onorm.shape),
            _const_spec(w_out.shape),
            _const_spec(w_up.shape),
            _const_spec(w_down.shape),
            _const_spec(g_final.shape),
            pl.BlockSpec((tb, H_REC, dk, dk), lambda i: (i, 0, 0, 0)),
            pl.BlockSpec((tb, CONV_W - 1, d_conv), lambda i: (i, 0, 0)),
        ],
        out_specs=[
            pl.BlockSpec((rows, d), lambda i: (i, 0)),
            pl.BlockSpec((tb, H_REC, dk, dk), lambda i: (i, 0, 0, 0)),
            pl.BlockSpec((tb, CONV_W - 1, d_conv), lambda i: (i, 0, 0)),
        ],
        out_shape=[
            jax.ShapeDtypeStruct((n_rows, d), F32),
            jax.ShapeDtypeStruct((bsz, H_REC, dk, dk), F32),
            jax.ShapeDtypeStruct((bsz, CONV_W - 1, d_conv), F32),
        ],
        scratch_shapes=[
            pltpu.VMEM((rows, N_MOD * d), F32),
            pltpu.VMEM((CONV_W - 1, rows, d_conv), F32),
            pltpu.VMEM((SUBLANES + rows, d_conv), F32),
            pltpu.VMEM((rows, d_rec + d_conv), BF16),
        ],
        compiler_params=pltpu.CompilerParams(
            dimension_semantics=("arbitrary",), vmem_limit_bytes=VMEM_LIMIT_BYTES),
        name="sample_layer",
    )(x2d, mod, lower_bounds, w_in, w_conv, g_onorm, w_out, w_up, w_down, g_final, rec_in, conv_in)


def kernel(x_prompt, x_sample, state_rec, state_conv, c_prompt, c_sample, lower_bounds, w_ada,
           b_ada, w_in, w_conv, g_onorm, w_out, w_up, w_down, g_final):
    depth = w_in.shape[0]
    assert depth == 1, "single-layer trunk"
    bsz_s, seq_s, d = x_sample.shape
    bsz_p = x_prompt.shape[0]

    c_all = jnp.concatenate([c_sample, c_prompt], axis=0)
    mod = _modulation(c_all, w_ada[0], b_ada)
    mod_p = mod[bsz_s:bsz_s + bsz_p].reshape(bsz_p, 1, N_MOD * d)

    g_final = g_final.reshape(1, d)
    y_p, rec_p, conv_p, w_in_b, w_out_b, w_up_b, w_down_b = _prompt_layer(
        x_prompt, mod_p, lower_bounds, w_in[0], w_conv[0], g_onorm, w_out[0], w_up[0], w_down[0],
        g_final)
    y_s, rec_s, conv_s = _sample_layer(
        x_sample.reshape(bsz_s * seq_s, d), seq_s, mod, lower_bounds, w_in_b, w_conv[0], g_onorm,
        w_out_b, w_up_b, w_down_b, g_final, state_rec[0], state_conv[0])
    return (y_p, y_s.reshape(bsz_s, seq_s, d), rec_p[None], conv_p[None], rec_s[None],
            conv_s[None])
```

```python
import functools

import jax
import jax.numpy as jnp
from jax import lax
from jax.experimental import pallas as pl
from jax.experimental.pallas import tpu as pltpu

F32 = jnp.float32
BF16 = jnp.bfloat16

EPS = 1e-6
H_REC = 4
CONV_W = 3
N_MOD = 6
N_PROJ_GROUPS = 7
CHUNK = 64

SUBLANES = 8
VMEM_LIMIT_BYTES = 56 * 1024 * 1024

PROMPT_TILE = 512
SAMPLE_TILE_B = 16
MOD_TILE_N = 2048
MLP_BLOCKS = 4
WEIGHT_STAGE_SHAPE = (512, 512)
WEIGHT_STAGE_SLOTS = 4

def _const_spec(shape):
    zeros = (0,) * len(shape)
    return pl.BlockSpec(shape, lambda *_: zeros, pipeline_mode=pl.Buffered(1))


def _rms(x):
    return x * lax.rsqrt(jnp.mean(x * x, axis=-1, keepdims=True) + EPS)


def _silu(x):
    return x * jax.nn.sigmoid(x)


def _bdot(a, b):
    return jnp.dot(a, b, preferred_element_type=F32)


def _dot_nt(a, b):
    return lax.dot_general(a, b, (((1,), (1,)), ((), ())), preferred_element_type=F32)


def _dot_tn(a, b):
    return lax.dot_general(a, b, (((0,), (0,)), ((), ())), preferred_element_type=F32)


def _group_pos(shape, group):
    return lax.broadcasted_iota(jnp.int32, shape, 0) % group


def _group_cumsum(x, group):
    pos = _group_pos(x.shape, group)
    step = 1
    while step < group:
        x = x + jnp.where(pos >= step, pltpu.roll(x, step, axis=0), 0.0)
        step *= 2
    return x


def _group_last(x, group):
    rows = x.shape[0]
    pos = _group_pos(x.shape, group)
    x = jnp.where(pos == group - 1, x, 0.0)
    step = 1
    while step < group:
        x = x + jnp.where(pos + step < group, pltpu.roll(x, rows - step, axis=0), 0.0)
        step *= 2
    return x


def _layer_lower_bound(lb_ref, layer):
    lb = lb_ref[...]
    e = jnp.exp(lb - jnp.max(lb, axis=0, keepdims=True))
    sm = e / jnp.sum(e, axis=0, keepdims=True)
    return jnp.sum(sm[: layer + 1], axis=0, keepdims=True)


def _gates(fz, lb):
    sig = jax.nn.sigmoid(fz)
    logf = jnp.log(lb + (1.0 - lb) * sig)
    k = (1.0 - lb) * (1.0 - sig)
    return logf, k


def _head_out(o, g, g_onorm):
    dv = o.shape[-1] // H_REC
    pieces = []
    for h in range(H_REC):
        sl = slice(h * dv, (h + 1) * dv)
        pieces.append(_rms(o[:, sl]))
    return (jnp.concatenate(pieces, axis=-1) * g_onorm) * _silu(g)


def _modulate(x, shift, scale):
    return (_rms(x) * (1.0 + scale) + shift).astype(BF16)


def _mlp_cols(w_up_ref, j):
    blk = w_up_ref.shape[1] // MLP_BLOCKS
    return slice(j * blk, (j + 1) * blk)


def _mlp_up(h2, w_up_ref, j):
    up = jnp.maximum(_bdot(h2, w_up_ref[:, _mlp_cols(w_up_ref, j)]), 0.0)
    return (up * up).astype(BF16)


def _mlp_down(up, w_up_ref, w_down_ref, j):
    return _bdot(up, w_down_ref[_mlp_cols(w_up_ref, j), :])


def _final_norm(x, g2, mlp, g_final_ref):
    return _rms(x + g2 * mlp) * g_final_ref[...]


def _mod_kernel(c_s_ref, c_p_ref, w_ref, b_ref, o_s_ref, o_p_ref):
    n_s = c_s_ref.shape[0]
    c = jnp.concatenate([c_s_ref[...], c_p_ref[...]], axis=0)
    mod = _bdot(_silu(c).astype(BF16), w_ref[...].astype(BF16)) + b_ref[...]
    o_s_ref[...] = mod[:n_s]
    o_p_ref[...] = mod[n_s:]


def _modulation(c_s, c_p, w_ada, b_ada):
    (n_s, d), n_p = c_s.shape, c_p.shape[0]
    n = w_ada.shape[1]
    assert n_s % SUBLANES == 0 and n_p % SUBLANES == 0
    return pl.pallas_call(
        _mod_kernel,
        grid=(n // MOD_TILE_N,),
        in_specs=[
            pl.BlockSpec((n_s, d), lambda j: (0, 0)),
            pl.BlockSpec((n_p, d), lambda j: (0, 0)),
            pl.BlockSpec((d, MOD_TILE_N), lambda j: (0, j)),
            pl.BlockSpec((1, MOD_TILE_N), lambda j: (0, j)),
        ],
        out_specs=[
            pl.BlockSpec((n_s, MOD_TILE_N), lambda j: (0, j)),
            pl.BlockSpec((n_p, MOD_TILE_N), lambda j: (0, j)),
        ],
        out_shape=[
            jax.ShapeDtypeStruct((n_s, n), F32),
            jax.ShapeDtypeStruct((n_p, n), F32),
        ],
        compiler_params=pltpu.CompilerParams(
            dimension_semantics=("arbitrary",), vmem_limit_bytes=VMEM_LIMIT_BYTES),
        name="adaln_modulation",
    )(c_s, c_p, w_ada, b_ada)


def _split_proj(proj):
    width = proj.shape[1] // N_PROJ_GROUPS
    return tuple(proj[:, p * width:(p + 1) * width] for p in range(N_PROJ_GROUPS))


def _weight_blocks(w_hbm, w_vmem):
    rows, cols = w_hbm.shape
    br, bc = WEIGHT_STAGE_SHAPE
    assert rows % br == 0 and cols % bc == 0
    return [(w_hbm.at[r:r + br, c:c + bc], w_vmem.at[r:r + br, c:c + bc])
            for r in range(0, rows, br) for c in range(0, cols, bc)]


def _stage_copy(src, stage_ref, sem_ref, i):
    slot = i % WEIGHT_STAGE_SLOTS
    return pltpu.make_async_copy(src, stage_ref.at[slot], sem_ref.at[slot])


def _load_weights_as_bf16(w_hbm_refs, w_vmem_refs, stage_ref, sem_ref):
    blocks = [blk for w_hbm, w_vmem in zip(w_hbm_refs, w_vmem_refs)
              for blk in _weight_blocks(w_hbm, w_vmem)]
    ahead = WEIGHT_STAGE_SLOTS - 1
    for i in range(min(ahead, len(blocks))):
        _stage_copy(blocks[i][0], stage_ref, sem_ref, i).start()
    for i, (src, dst) in enumerate(blocks):
        if i + ahead < len(blocks):
            _stage_copy(blocks[i + ahead][0], stage_ref, sem_ref, i + ahead).start()
        _stage_copy(src, stage_ref, sem_ref, i).wait()
        dst[...] = stage_ref[i % WEIGHT_STAGE_SLOTS].astype(BF16)


def _prompt_kernel(x_ref, mod_ref, lb_ref, w_in_hbm, w_conv_ref, g_onorm_ref, w_out_hbm,
                   w_up_hbm, w_down_hbm, g_final_ref,
                   y_ref, rec_ref, conv_ref, w_in_out, w_out_out, w_up_out, w_down_out,
                   st_ref, ubuf_ref, mix_ref, x1_ref, h2_ref,
                   w_in_ref, w_out_ref, w_up_ref, w_down_ref, stage_ref, stage_sem, out_sem,
                   *, n_tiles, n_steps):
    s = pl.program_id(0)
    tile = jnp.minimum(s, n_steps - 1)
    l = lax.rem(tile, n_tiles)
    live = s < n_steps
    tl, d = x_ref.shape
    d_rec = lb_ref.shape[1]
    dk = d_rec // H_REC
    n_chunks = tl // CHUNK
    heads = [slice(hd * dk, (hd + 1) * dk) for hd in range(H_REC)]

    w_vmem = (w_in_ref, w_out_ref, w_up_ref, w_down_ref)
    w_outs = (w_in_out, w_out_out, w_up_out, w_down_out)

    def bf16_export(i):
        return pltpu.make_async_copy(w_vmem[i], w_outs[i], out_sem.at[i])

    @pl.when(s == 0)
    def _():
        x1_ref[...] = jnp.zeros_like(x1_ref)
        h2_ref[...] = jnp.zeros_like(h2_ref)
        _load_weights_as_bf16((w_in_hbm, w_out_hbm, w_up_hbm, w_down_hbm), w_vmem,
                              stage_ref, stage_sem)
        for i in range(len(w_vmem)):
            bf16_export(i).start()

    @pl.when(s == n_steps)
    def _():
        for i in range(len(w_vmem)):
            bf16_export(i).wait()

    @pl.when(l == 0)
    def _():
        st_ref[...] = jnp.zeros_like(st_ref)
        ubuf_ref[0:SUBLANES, :] = jnp.zeros((SUBLANES, ubuf_ref.shape[1]), F32)

    mod_prev = mod_ref[pl.ds(lax.div(jnp.maximum(s - 1, 0), n_tiles), 1), :]
    g2_prev = mod_prev[:, (N_MOD - 1) * d:]
    mod = mod_ref[pl.ds(lax.div(tile, n_tiles), 1), :]
    sh1, sc1, g1, sh2, sc2 = (mod[:, i * d:(i + 1) * d] for i in range(N_MOD - 1))

    h2 = h2_ref[...]
    up = _mlp_up(h2, w_up_ref, 0)

    x = x_ref[...]
    proj = _bdot(_modulate(x, sh1, sc1), w_in_ref[...])
    q, fz, iv, g, gb, gc, hv = _split_proj(proj)
    d_conv = gb.shape[1]

    up_next = _mlp_up(h2, w_up_ref, 1)
    mlp = _mlp_down(up, w_up_ref, w_down_ref, 0)
    up = up_next

    lb = _layer_lower_bound(lb_ref, 0)
    row = lax.broadcasted_iota(jnp.int32, (CHUNK, CHUNK), 0)
    col = lax.broadcasted_iota(jnp.int32, (CHUNK, CHUNK), 1)
    causal = row >= col
    logf, k_all = _gates(fz, lb)
    cum_all = _group_cumsum(logf, CHUNK)
    q_dec, decay, scores, upd = [], [], [], []
    for c in range(n_chunks):
        rows = slice(c * CHUNK, (c + 1) * CHUNK)
        cum = cum_all[rows]
        last = cum[CHUNK - 1:CHUNK, :]
        q_dec.append((q[rows] * jnp.exp(cum)).astype(BF16))
        k_dec = (k_all[rows] * jnp.exp(-cum)).astype(BF16)
        k_end = (k_all[rows] * jnp.exp(last - cum)).astype(BF16)
        decay.append(jnp.exp(last))
        v = iv[rows].astype(BF16)
        scores.append([jnp.where(causal, _dot_nt(q_dec[c][:, sl], k_dec[:, sl]), 0.0).astype(BF16)
                       for sl in heads])
        upd.append([_dot_tn(v[:, sl], k_end[:, sl]) for sl in heads])

    up_next = _mlp_up(h2, w_up_ref, 2)
    mlp += _mlp_down(up, w_up_ref, w_down_ref, 1)
    up = up_next

    st = [st_ref[hd] for hd in range(H_REC)]
    o_chunks = []
    for c in range(n_chunks):
        v = iv[c * CHUNK:(c + 1) * CHUNK].astype(BF16)
        o_heads = []
        for hd, sl in enumerate(heads):
            o_heads.append(_bdot(scores[c][hd], v[:, sl])
                           + _dot_nt(q_dec[c][:, sl], st[hd].astype(BF16)))
            st[hd] = st[hd] * decay[c][:, sl] + upd[c][hd]
        o_chunks.append(jnp.concatenate(o_heads, axis=-1))
    for hd in range(H_REC):
        st_ref[hd] = st[hd]

    up_next = _mlp_up(h2, w_up_ref, 3)
    mlp += _mlp_down(up, w_up_ref, w_down_ref, 2)
    up = up_next

    o_rec = _head_out(jnp.concatenate(o_chunks, axis=0), g, g_onorm_ref[...])
    mix_ref[:, 0:d_rec] = o_rec.astype(BF16)

    u = gc * hv
    ubuf_ref[SUBLANES:SUBLANES + tl, :] = u
    w_conv = w_conv_ref[...]
    y_conv = (w_conv[0:1] * ubuf_ref[SUBLANES - 2:SUBLANES - 2 + tl, :]
              + w_conv[1:2] * ubuf_ref[SUBLANES - 1:SUBLANES - 1 + tl, :]
              + w_conv[2:3] * u)
    mix_ref[:, d_rec:d_rec + d_conv] = (gb * y_conv).astype(BF16)
    tail = u[tl - (CONV_W - 1):tl]
    ubuf_ref[SUBLANES - (CONV_W - 1):SUBLANES, :] = tail

    mlp += _mlp_down(up, w_up_ref, w_down_ref, 3)
    y_ref[...] = _final_norm(x1_ref[...], g2_prev, mlp, g_final_ref)

    x1 = x + g1 * _bdot(mix_ref[...], w_out_ref[...])
    x1_ref[...] = x1
    h2_ref[...] = _modulate(x1, sh2, sc2)

    @pl.when(live & (l == n_tiles - 1))
    def _():
        for hd in range(H_REC):
            rec_ref[hd] = st_ref[hd].T
        conv_ref[...] = tail


def _prompt_layer(x, mod_p, lower_bounds, w_in, w_conv, g_onorm, w_out, w_up, w_down, g_final):
    bsz, seq, d = x.shape
    d_rec = lower_bounds.shape[1]
    dk = d_rec // H_REC
    d_conv = w_conv.shape[1]
    tl = PROMPT_TILE
    n_tiles = seq // tl
    n_steps = bsz * n_tiles
    assert seq % tl == 0 and tl % CHUNK == 0 and MLP_BLOCKS == 4
    mats = (w_in, w_out, w_up, w_down)
    hbm = pl.BlockSpec(memory_space=pl.ANY)

    def mix_tile(s):
        t = jnp.minimum(s, n_steps - 1)
        return lax.div(t, n_tiles), lax.rem(t, n_tiles)

    def mlp_tile(s):
        t = jnp.maximum(s - 1, 0)
        return lax.div(t, n_tiles), lax.rem(t, n_tiles)

    return pl.pallas_call(
        functools.partial(_prompt_kernel, n_tiles=n_tiles, n_steps=n_steps),
        grid=(n_steps + 1,),
        in_specs=[
            pl.BlockSpec((None, tl, d), lambda s: (*mix_tile(s), 0)),
            _const_spec(mod_p.shape),
            _const_spec(lower_bounds.shape),
            hbm,
            _const_spec(w_conv.shape),
            _const_spec(g_onorm.shape),
            hbm,
            hbm,
            hbm,
            _const_spec(g_final.shape),
        ],
        out_specs=[
            pl.BlockSpec((None, tl, d), lambda s: (*mlp_tile(s), 0)),
            pl.BlockSpec((None, H_REC, dk, dk), lambda s: (mix_tile(s)[0], 0, 0, 0)),
            pl.BlockSpec((None, CONV_W - 1, d_conv), lambda s: (mix_tile(s)[0], 0, 0)),
        ] + [hbm] * len(mats),
        out_shape=[
            jax.ShapeDtypeStruct((bsz, seq, d), F32),
            jax.ShapeDtypeStruct((bsz, H_REC, dk, dk), F32),
            jax.ShapeDtypeStruct((bsz, CONV_W - 1, d_conv), F32),
        ] + [jax.ShapeDtypeStruct(w.shape, BF16) for w in mats],
        scratch_shapes=[
            pltpu.VMEM((H_REC, dk, dk), F32),
            pltpu.VMEM((SUBLANES + tl, d_conv), F32),
            pltpu.VMEM((tl, d_rec + d_conv), BF16),
            pltpu.VMEM((tl, d), F32),
            pltpu.VMEM((tl, d), BF16),
        ] + [pltpu.VMEM(w.shape, BF16) for w in mats] + [
            pltpu.VMEM((WEIGHT_STAGE_SLOTS, *WEIGHT_STAGE_SHAPE), F32),
            pltpu.SemaphoreType.DMA((WEIGHT_STAGE_SLOTS,)),
            pltpu.SemaphoreType.DMA((len(mats),)),
        ],
        compiler_params=pltpu.CompilerParams(
            dimension_semantics=("arbitrary",), vmem_limit_bytes=VMEM_LIMIT_BYTES),
        name="prompt_layer",
    )(x, mod_p, lower_bounds, w_in, w_conv, g_onorm, w_out, w_up, w_down, g_final)


def _sample_kernel(x_ref, mod_ref, lb_ref, w_in_ref, w_conv_ref, g_onorm_ref, w_out_ref,
                   w_up_ref, w_down_ref, g_final_ref, rec_in_ref, conv_in_ref,
                   y_ref, rec_ref, conv_ref,
                   modx_ref, cbx_ref, ubuf_ref, mix_ref, *, seq):
    rows, d = x_ref.shape
    tb = rows // seq
    d_rec = lb_ref.shape[1]
    dk = d_rec // H_REC
    d_conv = conv_in_ref.shape[2]

    for b in range(tb):
        r = slice(b * seq, (b + 1) * seq)
        modx_ref[r, :] = jnp.broadcast_to(mod_ref[b:b + 1, :], (seq, mod_ref.shape[1]))
        for j in range(CONV_W - 1):
            cbx_ref[j, r, :] = jnp.broadcast_to(conv_in_ref[b, j:j + 1, :], (seq, d_conv))

    sh1, sc1, g1, sh2, sc2, g2 = (modx_ref[:, i * d:(i + 1) * d] for i in range(N_MOD))

    x = x_ref[...]
    proj = _bdot(_modulate(x, sh1, sc1), w_in_ref[...])
    q, fz, iv, g, gb, gc, hv = _split_proj(proj)

    lb = _layer_lower_bound(lb_ref, 0)
    row = lax.broadcasted_iota(jnp.int32, (rows, rows), 0)
    col = lax.broadcasted_iota(jnp.int32, (rows, rows), 1)
    causal = ((row // seq) == (col // seq)) & (row >= col)
    logf, k = _gates(fz, lb)
    cum = _group_cumsum(logf, seq)
    last = _group_last(cum, seq)
    q_dec = (q * jnp.exp(cum)).astype(BF16)
    k_dec = (k * jnp.exp(-cum)).astype(BF16)
    k_end = k * jnp.exp(last - cum)
    decay = jnp.exp(last)
    vb = iv.astype(BF16)

    d_hi = decay.astype(BF16).astype(F32)
    r1 = decay - d_hi
    d_mid = r1.astype(BF16).astype(F32)
    d_lo = r1 - d_mid
    tok_rec = _group_pos((rows, d_rec), seq)
    dec3 = jnp.where(tok_rec == 0, d_hi,
                     jnp.where(tok_rec == 1, d_mid, jnp.where(tok_rec == 2, d_lo, 0.0)))
    sub = lax.broadcasted_iota(jnp.int32, (seq, dk), 0)
    ones3 = jnp.where(sub < 3, 1.0, 0.0)
    zeros_blk = jnp.zeros((seq, dk), F32)
    rhs_bottom = jnp.concatenate([zeros_blk, ones3], axis=1)

    o_intra = []
    for hd in range(H_REC):
        sl = slice(hd * dk, (hd + 1) * dk)
        scores = jnp.where(causal, _dot_nt(q_dec[:, sl], k_dec[:, sl]), 0.0)
        o_intra.append(_bdot(scores.astype(BF16), vb[:, sl]))
    o_intra = jnp.concatenate(o_intra, axis=-1)

    o_rows = []
    for b in range(tb):
        r0 = b * seq
        r = slice(r0, r0 + seq)
        o_heads = []
        for hd in range(H_REC):
            sl = slice(hd * dk, (hd + 1) * dk)
            s0 = rec_in_ref[b, hd]
            o_heads.append(_bdot(q_dec[r, sl], s0.astype(BF16)))
            lhs = jnp.concatenate([k_end[r, sl], dec3[r, sl]], axis=0)
            rhs = jnp.concatenate(
                [jnp.concatenate([iv[r, sl], zeros_blk], axis=1), rhs_bottom], axis=0)
            both = _dot_tn(lhs.astype(BF16), rhs.astype(BF16))
            rec_ref[b, hd] = both[:, dk:] * s0 + both[:, :dk]
        o_rows.append(jnp.concatenate(o_heads, axis=-1))
    o_rec = _head_out(o_intra + jnp.concatenate(o_rows, axis=0), g, g_onorm_ref[...])
    mix_ref[:, 0:d_rec] = o_rec.astype(BF16)

    u = gc * hv
    ubuf_ref[SUBLANES:SUBLANES + rows, :] = u
    tok = _group_pos((rows, d_conv), seq)
    u_m1 = jnp.where(tok >= 1, ubuf_ref[SUBLANES - 1:SUBLANES - 1 + rows, :], cbx_ref[1])
    u_m2 = jnp.where(tok >= 2, ubuf_ref[SUBLANES - 2:SUBLANES - 2 + rows, :],
                     jnp.where(tok == 1, cbx_ref[1], cbx_ref[0]))
    w_conv = w_conv_ref[...]
    y_conv = w_conv[0:1] * u_m2 + w_conv[1:2] * u_m1 + w_conv[2:3] * u
    mix_ref[:, d_rec:d_rec + d_conv] = (gb * y_conv).astype(BF16)
    for b in range(tb):
        conv_ref[b] = u[(b + 1) * seq - (CONV_W - 1):(b + 1) * seq]

    x1 = x + g1 * _bdot(mix_ref[...], w_out_ref[...])
    h2 = _modulate(x1, sh2, sc2)
    mlp = _mlp_down(_mlp_up(h2, w_up_ref, 0), w_up_ref, w_down_ref, 0)
    for j in range(1, MLP_BLOCKS):
        mlp += _mlp_down(_mlp_up(h2, w_up_ref, j), w_up_ref, w_down_ref, j)
    y_ref[...] = _final_norm(x1, g2, mlp, g_final_ref)


def _sample_layer(x2d, seq, mod, lower_bounds, w_in, w_conv, g_onorm, w_out, w_up, w_down,
                  g_final, rec_in, conv_in):
    n_rows, d = x2d.shape
    bsz = n_rows // seq
    d_rec = lower_bounds.shape[1]
    dk = d_rec // H_REC
    d_conv = w_conv.shape[1]
    tb = SAMPLE_TILE_B
    rows = tb * seq
    assert bsz % tb == 0 and seq == SUBLANES and seq >= CONV_W
    return pl.pallas_call(
        functools.partial(_sample_kernel, seq=seq),
        grid=(bsz // tb,),
        in_specs=[
            pl.BlockSpec((rows, d), lambda i: (i, 0)),
            pl.BlockSpec((tb, N_MOD * d), lambda i: (i, 0)),
            _const_spec(lower_bounds.shape),
            _const_spec(w_in.shape),
            _const_spec(w_conv.shape),
            _const_spec(g_onorm.shape),
            _const_spec(w_out.shape),
            _const_spec(w_up.shape),
            _const_spec(w_down.shape),
            _const_spec(g_final.shape),
            pl.BlockSpec((tb, H_REC, dk, dk), lambda i: (i, 0, 0, 0)),
            pl.BlockSpec((tb, CONV_W - 1, d_conv), lambda i: (i, 0, 0)),
        ],
        out_specs=[
            pl.BlockSpec((rows, d), lambda i: (i, 0)),
            pl.BlockSpec((tb, H_REC, dk, dk), lambda i: (i, 0, 0, 0)),
            pl.BlockSpec((tb, CONV_W - 1, d_conv), lambda i: (i, 0, 0)),
        ],
        out_shape=[
            jax.ShapeDtypeStruct((n_rows, d), F32),
            jax.ShapeDtypeStruct((bsz, H_REC, dk, dk), F32),
            jax.ShapeDtypeStruct((bsz, CONV_W - 1, d_conv), F32),
        ],
        scratch_shapes=[
            pltpu.VMEM((rows, N_MOD * d), F32),
            pltpu.VMEM((CONV_W - 1, rows, d_conv), F32),
            pltpu.VMEM((SUBLANES + rows, d_conv), F32),
            pltpu.VMEM((rows, d_rec + d_conv), BF16),
        ],
        compiler_params=pltpu.CompilerParams(
            dimension_semantics=("arbitrary",), vmem_limit_bytes=VMEM_LIMIT_BYTES),
        name="sample_layer",
    )(x2d, mod, lower_bounds, w_in, w_conv, g_onorm, w_out, w_up, w_down, g_final, rec_in, conv_in)


def kernel(x_prompt, x_sample, state_rec, state_conv, c_prompt, c_sample, lower_bounds, w_ada,
           b_ada, w_in, w_conv, g_onorm, w_out, w_up, w_down, g_final):
    depth = w_in.shape[0]
    assert depth == 1, "single-layer trunk"
    bsz_s, seq_s, d = x_sample.shape

    mod_s, mod_p = _modulation(c_sample, c_prompt, w_ada[0], b_ada)

    g_final = g_final.reshape(1, d)
    y_p, rec_p, conv_p, w_in_b, w_out_b, w_up_b, w_down_b = _prompt_layer(
        x_prompt, mod_p, lower_bounds, w_in[0], w_conv[0], g_onorm, w_out[0], w_up[0], w_down[0],
        g_final)
    y_s, rec_s, conv_s = _sample_layer(
        x_sample.reshape(bsz_s * seq_s, d), seq_s, mod_s, lower_bounds, w_in_b, w_conv[0], g_onorm,
        w_out_b, w_up_b, w_down_b, g_final, state_rec[0], state_conv[0])
    return (y_p, y_s.reshape(bsz_s, seq_s, d), rec_p[None], conv_p[None], rec_s[None],
            conv_s[None])
```

```python
import functools

import jax
import jax.numpy as jnp
from jax import lax
from jax.experimental import pallas as pl
from jax.experimental.pallas import tpu as pltpu

F32 = jnp.float32
BF16 = jnp.bfloat16

EPS = 1e-6
H_REC = 4
CONV_W = 3
N_MOD = 6
N_PROJ_GROUPS = 7
CHUNK = 64

SUBLANES = 8
VMEM_LIMIT_BYTES = 56 * 1024 * 1024

PROMPT_TILE = 512
SAMPLE_TILE_B = 16
MOD_TILE_N = 2048
MOD_K_SPLIT = 4
MLP_BLOCKS = 4
WEIGHT_STAGE_SHAPE = (512, 512)
WEIGHT_STAGE_SLOTS = 4

def _const_spec(shape):
    zeros = (0,) * len(shape)
    return pl.BlockSpec(shape, lambda *_: zeros, pipeline_mode=pl.Buffered(1))


def _rms(x):
    return x * lax.rsqrt(jnp.mean(x * x, axis=-1, keepdims=True) + EPS)


def _silu(x):
    return x * jax.nn.sigmoid(x)


def _bdot(a, b):
    return jnp.dot(a, b, preferred_element_type=F32)


def _dot_nt(a, b):
    return lax.dot_general(a, b, (((1,), (1,)), ((), ())), preferred_element_type=F32)


def _dot_tn(a, b):
    return lax.dot_general(a, b, (((0,), (0,)), ((), ())), preferred_element_type=F32)


def _group_pos(shape, group):
    return lax.broadcasted_iota(jnp.int32, shape, 0) % group


def _group_cumsum(x, group):
    pos = _group_pos(x.shape, group)
    step = 1
    while step < group:
        x = x + jnp.where(pos >= step, pltpu.roll(x, step, axis=0), 0.0)
        step *= 2
    return x


def _group_last(x, group):
    rows = x.shape[0]
    pos = _group_pos(x.shape, group)
    x = jnp.where(pos == group - 1, x, 0.0)
    step = 1
    while step < group:
        x = x + jnp.where(pos + step < group, pltpu.roll(x, rows - step, axis=0), 0.0)
        step *= 2
    return x


def _layer_lower_bound(lb_ref, layer):
    lb = lb_ref[...]
    e = jnp.exp(lb - jnp.max(lb, axis=0, keepdims=True))
    sm = e / jnp.sum(e, axis=0, keepdims=True)
    return jnp.sum(sm[: layer + 1], axis=0, keepdims=True)


def _gates(fz, lb):
    sig = jax.nn.sigmoid(fz)
    logf = jnp.log(lb + (1.0 - lb) * sig)
    k = (1.0 - lb) * (1.0 - sig)
    return logf, k


def _head_out(o, g, g_onorm):
    dv = o.shape[-1] // H_REC
    pieces = []
    for h in range(H_REC):
        sl = slice(h * dv, (h + 1) * dv)
        pieces.append(_rms(o[:, sl]))
    return (jnp.concatenate(pieces, axis=-1) * g_onorm) * _silu(g)


def _modulate(x, shift, scale):
    return (_rms(x) * (1.0 + scale) + shift).astype(BF16)


def _mlp_cols(w_up_ref, j):
    blk = w_up_ref.shape[1] // MLP_BLOCKS
    return slice(j * blk, (j + 1) * blk)


def _mlp_up(h2, w_up_ref, j):
    up = jnp.maximum(_bdot(h2, w_up_ref[:, _mlp_cols(w_up_ref, j)]), 0.0)
    return (up * up).astype(BF16)


def _mlp_down(up, w_up_ref, w_down_ref, j):
    return _bdot(up, w_down_ref[_mlp_cols(w_up_ref, j), :])


def _final_norm(x, g2, mlp, g_final_ref):
    return _rms(x + g2 * mlp) * g_final_ref[...]


def _mod_kernel(c_s_ref, c_p_ref, *refs):
    w_refs, (b_ref, o_s_ref, o_p_ref) = refs[:MOD_K_SPLIT], refs[MOD_K_SPLIT:]
    n_s = c_s_ref.shape[0]
    kb = w_refs[0].shape[0]
    c = _silu(jnp.concatenate([c_s_ref[...], c_p_ref[...]], axis=0)).astype(BF16)
    mod = b_ref[...]
    for k, w_ref in enumerate(w_refs):
        mod = mod + _bdot(c[:, k * kb:(k + 1) * kb], w_ref[...].astype(BF16))
    o_s_ref[...] = mod[:n_s]
    o_p_ref[...] = mod[n_s:]


def _modulation(c_s, c_p, w_ada, b_ada):
    (n_s, d), n_p = c_s.shape, c_p.shape[0]
    n = w_ada.shape[1]
    assert n_s % SUBLANES == 0 and n_p % SUBLANES == 0 and d % MOD_K_SPLIT == 0
    kb = d // MOD_K_SPLIT
    w_specs = [pl.BlockSpec((kb, MOD_TILE_N), functools.partial(lambda k, j: (k, j), k))
               for k in range(MOD_K_SPLIT)]
    return pl.pallas_call(
        _mod_kernel,
        grid=(n // MOD_TILE_N,),
        in_specs=[
            pl.BlockSpec((n_s, d), lambda j: (0, 0)),
            pl.BlockSpec((n_p, d), lambda j: (0, 0)),
            *w_specs,
            pl.BlockSpec((1, MOD_TILE_N), lambda j: (0, j)),
        ],
        out_specs=[
            pl.BlockSpec((n_s, MOD_TILE_N), lambda j: (0, j)),
            pl.BlockSpec((n_p, MOD_TILE_N), lambda j: (0, j)),
        ],
        out_shape=[
            jax.ShapeDtypeStruct((n_s, n), F32),
            jax.ShapeDtypeStruct((n_p, n), F32),
        ],
        compiler_params=pltpu.CompilerParams(
            dimension_semantics=("arbitrary",), vmem_limit_bytes=VMEM_LIMIT_BYTES),
        name="adaln_modulation",
    )(c_s, c_p, *([w_ada] * MOD_K_SPLIT), b_ada)


def _split_proj(proj):
    width = proj.shape[1] // N_PROJ_GROUPS
    return tuple(proj[:, p * width:(p + 1) * width] for p in range(N_PROJ_GROUPS))


def _weight_blocks(w_hbm, w_vmem):
    rows, cols = w_hbm.shape
    br, bc = WEIGHT_STAGE_SHAPE
    assert rows % br == 0 and cols % bc == 0
    return [(w_hbm.at[r:r + br, c:c + bc], w_vmem.at[r:r + br, c:c + bc])
            for r in range(0, rows, br) for c in range(0, cols, bc)]


def _stage_copy(src, stage_ref, sem_ref, i):
    slot = i % WEIGHT_STAGE_SLOTS
    return pltpu.make_async_copy(src, stage_ref.at[slot], sem_ref.at[slot])


def _load_weights_as_bf16(w_hbm_refs, w_vmem_refs, stage_ref, sem_ref):
    blocks = [blk for w_hbm, w_vmem in zip(w_hbm_refs, w_vmem_refs)
              for blk in _weight_blocks(w_hbm, w_vmem)]
    ahead = WEIGHT_STAGE_SLOTS - 1
    for i in range(min(ahead, len(blocks))):
        _stage_copy(blocks[i][0], stage_ref, sem_ref, i).start()
    for i, (src, dst) in enumerate(blocks):
        if i + ahead < len(blocks):
            _stage_copy(blocks[i + ahead][0], stage_ref, sem_ref, i + ahead).start()
        _stage_copy(src, stage_ref, sem_ref, i).wait()
        dst[...] = stage_ref[i % WEIGHT_STAGE_SLOTS].astype(BF16)


def _prompt_kernel(x_ref, mod_ref, lb_ref, w_in_hbm, w_conv_ref, g_onorm_ref, w_out_hbm,
                   w_up_hbm, w_down_hbm, g_final_ref,
                   y_ref, rec_ref, conv_ref, w_in_out, w_out_out, w_up_out, w_down_out,
                   st_ref, ubuf_ref, mix_ref, x1_ref, h2_ref,
                   w_in_ref, w_out_ref, w_up_ref, w_down_ref, stage_ref, stage_sem, out_sem,
                   *, n_tiles, n_steps):
    s = pl.program_id(0)
    tile = jnp.minimum(s, n_steps - 1)
    l = lax.rem(tile, n_tiles)
    live = s < n_steps
    tl, d = x_ref.shape
    d_rec = lb_ref.shape[1]
    dk = d_rec // H_REC
    n_chunks = tl // CHUNK
    heads = [slice(hd * dk, (hd + 1) * dk) for hd in range(H_REC)]

    w_vmem = (w_in_ref, w_out_ref, w_up_ref, w_down_ref)
    w_outs = (w_in_out, w_out_out, w_up_out, w_down_out)

    def bf16_export(i):
        return pltpu.make_async_copy(w_vmem[i], w_outs[i], out_sem.at[i])

    @pl.when(s == 0)
    def _():
        x1_ref[...] = jnp.zeros_like(x1_ref)
        h2_ref[...] = jnp.zeros_like(h2_ref)
        _load_weights_as_bf16((w_in_hbm, w_out_hbm, w_up_hbm, w_down_hbm), w_vmem,
                              stage_ref, stage_sem)
        for i in range(len(w_vmem)):
            bf16_export(i).start()

    @pl.when(s == n_steps)
    def _():
        for i in range(len(w_vmem)):
            bf16_export(i).wait()

    @pl.when(l == 0)
    def _():
        st_ref[...] = jnp.zeros_like(st_ref)
        ubuf_ref[0:SUBLANES, :] = jnp.zeros((SUBLANES, ubuf_ref.shape[1]), F32)

    mod_prev = mod_ref[pl.ds(lax.div(jnp.maximum(s - 1, 0), n_tiles), 1), :]
    g2_prev = mod_prev[:, (N_MOD - 1) * d:]
    mod = mod_ref[pl.ds(lax.div(tile, n_tiles), 1), :]
    sh1, sc1, g1, sh2, sc2 = (mod[:, i * d:(i + 1) * d] for i in range(N_MOD - 1))

    h2 = h2_ref[...]
    up = _mlp_up(h2, w_up_ref, 0)

    x = x_ref[...]
    proj = _bdot(_modulate(x, sh1, sc1), w_in_ref[...])
    q, fz, iv, g, gb, gc, hv = _split_proj(proj)
    d_conv = gb.shape[1]

    up_next = _mlp_up(h2, w_up_ref, 1)
    mlp = _mlp_down(up, w_up_ref, w_down_ref, 0)
    up = up_next

    lb = _layer_lower_bound(lb_ref, 0)
    row = lax.broadcasted_iota(jnp.int32, (CHUNK, CHUNK), 0)
    col = lax.broadcasted_iota(jnp.int32, (CHUNK, CHUNK), 1)
    causal = row >= col
    logf, k_all = _gates(fz, lb)
    cum_all = _group_cumsum(logf, CHUNK)
    q_dec, decay, scores, upd = [], [], [], []
    for c in range(n_chunks):
        rows = slice(c * CHUNK, (c + 1) * CHUNK)
        cum = cum_all[rows]
        last = cum[CHUNK - 1:CHUNK, :]
        q_dec.append((q[rows] * jnp.exp(cum)).astype(BF16))
        k_dec = (k_all[rows] * jnp.exp(-cum)).astype(BF16)
        k_end = (k_all[rows] * jnp.exp(last - cum)).astype(BF16)
        decay.append(jnp.exp(last))
        v = iv[rows].astype(BF16)
        scores.append([jnp.where(causal, _dot_nt(q_dec[c][:, sl], k_dec[:, sl]), 0.0).astype(BF16)
                       for sl in heads])
        upd.append([_dot_tn(v[:, sl], k_end[:, sl]) for sl in heads])

    up_next = _mlp_up(h2, w_up_ref, 2)
    mlp += _mlp_down(up, w_up_ref, w_down_ref, 1)
    up = up_next

    st = [st_ref[hd] for hd in range(H_REC)]
    o_chunks = []
    for c in range(n_chunks):
        v = iv[c * CHUNK:(c + 1) * CHUNK].astype(BF16)
        o_heads = []
        for hd, sl in enumerate(heads):
            o_heads.append(_bdot(scores[c][hd], v[:, sl])
                           + _dot_nt(q_dec[c][:, sl], st[hd].astype(BF16)))
            st[hd] = st[hd] * decay[c][:, sl] + upd[c][hd]
        o_chunks.append(jnp.concatenate(o_heads, axis=-1))
    for hd in range(H_REC):
        st_ref[hd] = st[hd]

    up_next = _mlp_up(h2, w_up_ref, 3)
    mlp += _mlp_down(up, w_up_ref, w_down_ref, 2)
    up = up_next

    o_rec = _head_out(jnp.concatenate(o_chunks, axis=0), g, g_onorm_ref[...])
    mix_ref[:, 0:d_rec] = o_rec.astype(BF16)

    u = gc * hv
    ubuf_ref[SUBLANES:SUBLANES + tl, :] = u
    w_conv = w_conv_ref[...]
    y_conv = (w_conv[0:1] * ubuf_ref[SUBLANES - 2:SUBLANES - 2 + tl, :]
              + w_conv[1:2] * ubuf_ref[SUBLANES - 1:SUBLANES - 1 + tl, :]
              + w_conv[2:3] * u)
    mix_ref[:, d_rec:d_rec + d_conv] = (gb * y_conv).astype(BF16)
    tail = u[tl - (CONV_W - 1):tl]
    ubuf_ref[SUBLANES - (CONV_W - 1):SUBLANES, :] = tail

    mlp += _mlp_down(up, w_up_ref, w_down_ref, 3)
    y_ref[...] = _final_norm(x1_ref[...], g2_prev, mlp, g_final_ref)

    x1 = x + g1 * _bdot(mix_ref[...], w_out_ref[...])
    x1_ref[...] = x1
    h2_ref[...] = _modulate(x1, sh2, sc2)

    @pl.when(live & (l == n_tiles - 1))
    def _():
        for hd in range(H_REC):
            rec_ref[hd] = st_ref[hd].T
        conv_ref[...] = tail


def _prompt_layer(x, mod_p, lower_bounds, w_in, w_conv, g_onorm, w_out, w_up, w_down, g_final):
    bsz, seq, d = x.shape
    d_rec = lower_bounds.shape[1]
    dk = d_rec // H_REC
    d_conv = w_conv.shape[1]
    tl = PROMPT_TILE
    n_tiles = seq // tl
    n_steps = bsz * n_tiles
    assert seq % tl == 0 and tl % CHUNK == 0 and MLP_BLOCKS == 4
    mats = (w_in, w_out, w_up, w_down)
    hbm = pl.BlockSpec(memory_space=pl.ANY)

    def mix_tile(s):
        t = jnp.minimum(s, n_steps - 1)
        return lax.div(t, n_tiles), lax.rem(t, n_tiles)

    def mlp_tile(s):
        t = jnp.maximum(s - 1, 0)
        return lax.div(t, n_tiles), lax.rem(t, n_tiles)

    return pl.pallas_call(
        functools.partial(_prompt_kernel, n_tiles=n_tiles, n_steps=n_steps),
        grid=(n_steps + 1,),
        in_specs=[
            pl.BlockSpec((None, tl, d), lambda s: (*mix_tile(s), 0)),
            _const_spec(mod_p.shape),
            _const_spec(lower_bounds.shape),
            hbm,
            _const_spec(w_conv.shape),
            _const_spec(g_onorm.shape),
            hbm,
            hbm,
            hbm,
            _const_spec(g_final.shape),
        ],
        out_specs=[
            pl.BlockSpec((None, tl, d), lambda s: (*mlp_tile(s), 0)),
            pl.BlockSpec((None, H_REC, dk, dk), lambda s: (mix_tile(s)[0], 0, 0, 0)),
            pl.BlockSpec((None, CONV_W - 1, d_conv), lambda s: (mix_tile(s)[0], 0, 0)),
        ] + [hbm] * len(mats),
        out_shape=[
            jax.ShapeDtypeStruct((bsz, seq, d), F32),
            jax.ShapeDtypeStruct((bsz, H_REC, dk, dk), F32),
            jax.ShapeDtypeStruct((bsz, CONV_W - 1, d_conv), F32),
        ] + [jax.ShapeDtypeStruct(w.shape, BF16) for w in mats],
        scratch_shapes=[
            pltpu.VMEM((H_REC, dk, dk), F32),
            pltpu.VMEM((SUBLANES + tl, d_conv), F32),
            pltpu.VMEM((tl, d_rec + d_conv), BF16),
            pltpu.VMEM((tl, d), F32),
            pltpu.VMEM((tl, d), BF16),
        ] + [pltpu.VMEM(w.shape, BF16) for w in mats] + [
            pltpu.VMEM((WEIGHT_STAGE_SLOTS, *WEIGHT_STAGE_SHAPE), F32),
            pltpu.SemaphoreType.DMA((WEIGHT_STAGE_SLOTS,)),
            pltpu.SemaphoreType.DMA((len(mats),)),
        ],
        compiler_params=pltpu.CompilerParams(
            dimension_semantics=("arbitrary",), vmem_limit_bytes=VMEM_LIMIT_BYTES),
        name="prompt_layer",
    )(x, mod_p, lower_bounds, w_in, w_conv, g_onorm, w_out, w_up, w_down, g_final)


def _sample_kernel(x_ref, mod_ref, lb_ref, w_in_ref, w_conv_ref, g_onorm_ref, w_out_ref,
                   w_up_ref, w_down_ref, g_final_ref, rec_in_ref, conv_in_ref,
                   y_ref, rec_ref, conv_ref,
                   modx_ref, cbx_ref, ubuf_ref, mix_ref, *, seq):
    rows, d = x_ref.shape
    tb = rows // seq
    d_rec = lb_ref.shape[1]
    dk = d_rec // H_REC
    d_conv = conv_in_ref.shape[2]

    for b in range(tb):
        r = slice(b * seq, (b + 1) * seq)
        modx_ref[r, :] = jnp.broadcast_to(mod_ref[b:b + 1, :], (seq, mod_ref.shape[1]))
        for j in range(CONV_W - 1):
            cbx_ref[j, r, :] = jnp.broadcast_to(conv_in_ref[b, j:j + 1, :], (seq, d_conv))

    sh1, sc1, g1, sh2, sc2, g2 = (modx_ref[:, i * d:(i + 1) * d] for i in range(N_MOD))

    x = x_ref[...]
    proj = _bdot(_modulate(x, sh1, sc1), w_in_ref[...])
    q, fz, iv, g, gb, gc, hv = _split_proj(proj)

    lb = _layer_lower_bound(lb_ref, 0)
    row = lax.broadcasted_iota(jnp.int32, (rows, rows), 0)
    col = lax.broadcasted_iota(jnp.int32, (rows, rows), 1)
    causal = ((row // seq) == (col // seq)) & (row >= col)
    logf, k = _gates(fz, lb)
    cum = _group_cumsum(logf, seq)
    last = _group_last(cum, seq)
    q_dec = (q * jnp.exp(cum)).astype(BF16)
    k_dec = (k * jnp.exp(-cum)).astype(BF16)
    k_end = k * jnp.exp(last - cum)
    decay = jnp.exp(last)
    vb = iv.astype(BF16)

    d_hi = decay.astype(BF16).astype(F32)
    r1 = decay - d_hi
    d_mid = r1.astype(BF16).astype(F32)
    d_lo = r1 - d_mid
    tok_rec = _group_pos((rows, d_rec), seq)
    dec3 = jnp.where(tok_rec == 0, d_hi,
                     jnp.where(tok_rec == 1, d_mid, jnp.where(tok_rec == 2, d_lo, 0.0)))
    sub = lax.broadcasted_iota(jnp.int32, (seq, dk), 0)
    ones3 = jnp.where(sub < 3, 1.0, 0.0)
    zeros_blk = jnp.zeros((seq, dk), F32)
    rhs_bottom = jnp.concatenate([zeros_blk, ones3], axis=1)

    o_intra = []
    for hd in range(H_REC):
        sl = slice(hd * dk, (hd + 1) * dk)
        scores = jnp.where(causal, _dot_nt(q_dec[:, sl], k_dec[:, sl]), 0.0)
        o_intra.append(_bdot(scores.astype(BF16), vb[:, sl]))
    o_intra = jnp.concatenate(o_intra, axis=-1)

    o_rows = []
    for b in range(tb):
        r0 = b * seq
        r = slice(r0, r0 + seq)
        o_heads = []
        for hd in range(H_REC):
            sl = slice(hd * dk, (hd + 1) * dk)
            s0 = rec_in_ref[b, hd]
            o_heads.append(_bdot(q_dec[r, sl], s0.astype(BF16)))
            lhs = jnp.concatenate([k_end[r, sl], dec3[r, sl]], axis=0)
            rhs = jnp.concatenate(
                [jnp.concatenate([iv[r, sl], zeros_blk], axis=1), rhs_bottom], axis=0)
            both = _dot_tn(lhs.astype(BF16), rhs.astype(BF16))
            rec_ref[b, hd] = both[:, dk:] * s0 + both[:, :dk]
        o_rows.append(jnp.concatenate(o_heads, axis=-1))
    o_rec = _head_out(o_intra + jnp.concatenate(o_rows, axis=0), g, g_onorm_ref[...])
    mix_ref[:, 0:d_rec] = o_rec.astype(BF16)

    u = gc * hv
    ubuf_ref[SUBLANES:SUBLANES + rows, :] = u
    tok = _group_pos((rows, d_conv), seq)
    u_m1 = jnp.where(tok >= 1, ubuf_ref[SUBLANES - 1:SUBLANES - 1 + rows, :], cbx_ref[1])
    u_m2 = jnp.where(tok >= 2, ubuf_ref[SUBLANES - 2:SUBLANES - 2 + rows, :],
                     jnp.where(tok == 1, cbx_ref[1], cbx_ref[0]))
    w_conv = w_conv_ref[...]
    y_conv = w_conv[0:1] * u_m2 + w_conv[1:2] * u_m1 + w_conv[2:3] * u
    mix_ref[:, d_rec:d_rec + d_conv] = (gb * y_conv).astype(BF16)
    for b in range(tb):
        conv_ref[b] = u[(b + 1) * seq - (CONV_W - 1):(b + 1) * seq]

    x1 = x + g1 * _bdot(mix_ref[...], w_out_ref[...])
    h2 = _modulate(x1, sh2, sc2)
    mlp = _mlp_down(_mlp_up(h2, w_up_ref, 0), w_up_ref, w_down_ref, 0)
    for j in range(1, MLP_BLOCKS):
        mlp += _mlp_down(_mlp_up(h2, w_up_ref, j), w_up_ref, w_down_ref, j)
    y_ref[...] = _final_norm(x1, g2, mlp, g_final_ref)


def _sample_layer(x2d, seq, mod, lower_bounds, w_in, w_conv, g_onorm, w_out, w_up, w_down,
                  g_final, rec_in, conv_in):
    n_rows, d = x2d.shape
    bsz = n_rows // seq
    d_rec = lower_bounds.shape[1]
    dk = d_rec // H_REC
    d_conv = w_conv.shape[1]
    tb = SAMPLE_TILE_B
    rows = tb * seq
    assert bsz % tb == 0 and seq == SUBLANES and seq >= CONV_W
    return pl.pallas_call(
        functools.partial(_sample_kernel, seq=seq),
        grid=(bsz // tb,),
        in_specs=[
            pl.BlockSpec((rows, d), lambda i: (i, 0)),
            pl.BlockSpec((tb, N_MOD * d), lambda i: (i, 0)),
            _const_spec(lower_bounds.shape),
            _const_spec(w_in.shape),
            _const_spec(w_conv.shape),
            _const_spec(g_onorm.shape),
            _const_spec(w_out.shape),
            _const_spec(w_up.shape),
            _const_spec(w_down.shape),
            _const_spec(g_final.shape),
            pl.BlockSpec((tb, H_REC, dk, dk), lambda i: (i, 0, 0, 0)),
            pl.BlockSpec((tb, CONV_W - 1, d_conv), lambda i: (i, 0, 0)),
        ],
        out_specs=[
            pl.BlockSpec((rows, d), lambda i: (i, 0)),
            pl.BlockSpec((tb, H_REC, dk, dk), lambda i: (i, 0, 0, 0)),
            pl.BlockSpec((tb, CONV_W - 1, d_conv), lambda i: (i, 0, 0)),
        ],
        out_shape=[
            jax.ShapeDtypeStruct((n_rows, d), F32),
            jax.ShapeDtypeStruct((bsz, H_REC, dk, dk), F32),
            jax.ShapeDtypeStruct((bsz, CONV_W - 1, d_conv), F32),
        ],
        scratch_shapes=[
            pltpu.VMEM((rows, N_MOD * d), F32),
            pltpu.VMEM((CONV_W - 1, rows, d_conv), F32),
            pltpu.VMEM((SUBLANES + rows, d_conv), F32),
            pltpu.VMEM((rows, d_rec + d_conv), BF16),
        ],
        compiler_params=pltpu.CompilerParams(
            dimension_semantics=("arbitrary",), vmem_limit_bytes=VMEM_LIMIT_BYTES),
        name="sample_layer",
    )(x2d, mod, lower_bounds, w_in, w_conv, g_onorm, w_out, w_up, w_down, g_final, rec_in, conv_in)


def kernel(x_prompt, x_sample, state_rec, state_conv, c_prompt, c_sample, lower_bounds, w_ada,
           b_ada, w_in, w_conv, g_onorm, w_out, w_up, w_down, g_final):
    depth = w_in.shape[0]
    assert depth == 1, "single-layer trunk"
    bsz_s, seq_s, d = x_sample.shape

    mod_s, mod_p = _modulation(c_sample, c_prompt, w_ada[0], b_ada)

    g_final = g_final.reshape(1, d)
    y_p, rec_p, conv_p, w_in_b, w_out_b, w_up_b, w_down_b = _prompt_layer(
        x_prompt, mod_p, lower_bounds, w_in[0], w_conv[0], g_onorm, w_out[0], w_up[0], w_down[0],
        g_final)
    y_s, rec_s, conv_s = _sample_layer(
        x_sample.reshape(bsz_s * seq_s, d), seq_s, mod_s, lower_bounds, w_in_b, w_conv[0], g_onorm,
        w_out_b, w_up_b, w_down_b, g_final, state_rec[0], state_conv[0])
    return (y_p, y_s.reshape(bsz_s, seq_s, d), rec_p[None], conv_p[None], rec_s[None],
            conv_s[None])
```

```python
import functools

import jax
import jax.numpy as jnp
from jax import lax
from jax.experimental import pallas as pl
from jax.experimental.pallas import tpu as pltpu

F32 = jnp.float32
BF16 = jnp.bfloat16

EPS = 1e-6
H_REC = 4
CONV_W = 3
N_MOD = 6
N_PROJ_GROUPS = 7
CHUNK = 64

SUBLANES = 8
VMEM_LIMIT_BYTES = 56 * 1024 * 1024

PROMPT_TILE = 512
SAMPLE_TILE_B = 16
MOD_TILE_K = 256
MLP_BLOCKS = 4
WEIGHT_STAGE_SHAPE = (512, 512)
WEIGHT_STAGE_SLOTS = 4

def _const_spec(shape):
    zeros = (0,) * len(shape)
    return pl.BlockSpec(shape, lambda *_: zeros, pipeline_mode=pl.Buffered(1))


def _rms(x):
    return x * lax.rsqrt(jnp.mean(x * x, axis=-1, keepdims=True) + EPS)


def _silu(x):
    return x * jax.nn.sigmoid(x)


def _bdot(a, b):
    return jnp.dot(a, b, preferred_element_type=F32)


def _dot_nt(a, b):
    return lax.dot_general(a, b, (((1,), (1,)), ((), ())), preferred_element_type=F32)


def _dot_tn(a, b):
    return lax.dot_general(a, b, (((0,), (0,)), ((), ())), preferred_element_type=F32)


def _group_pos(shape, group):
    return lax.broadcasted_iota(jnp.int32, shape, 0) % group


def _group_cumsum(x, group):
    pos = _group_pos(x.shape, group)
    step = 1
    while step < group:
        x = x + jnp.where(pos >= step, pltpu.roll(x, step, axis=0), 0.0)
        step *= 2
    return x


def _group_last(x, group):
    rows = x.shape[0]
    pos = _group_pos(x.shape, group)
    x = jnp.where(pos == group - 1, x, 0.0)
    step = 1
    while step < group:
        x = x + jnp.where(pos + step < group, pltpu.roll(x, rows - step, axis=0), 0.0)
        step *= 2
    return x


def _layer_lower_bound(lb_ref, layer):
    lb = lb_ref[...]
    e = jnp.exp(lb - jnp.max(lb, axis=0, keepdims=True))
    sm = e / jnp.sum(e, axis=0, keepdims=True)
    return jnp.sum(sm[: layer + 1], axis=0, keepdims=True)


def _gates(fz, lb):
    sig = jax.nn.sigmoid(fz)
    logf = jnp.log(lb + (1.0 - lb) * sig)
    k = (1.0 - lb) * (1.0 - sig)
    return logf, k


def _head_out(o, g, g_onorm):
    dv = o.shape[-1] // H_REC
    pieces = []
    for h in range(H_REC):
        sl = slice(h * dv, (h + 1) * dv)
        pieces.append(_rms(o[:, sl]))
    return (jnp.concatenate(pieces, axis=-1) * g_onorm) * _silu(g)


def _modulate(x, shift, scale):
    return (_rms(x) * (1.0 + scale) + shift).astype(BF16)


def _mlp_cols(w_up_ref, j):
    blk = w_up_ref.shape[1] // MLP_BLOCKS
    return slice(j * blk, (j + 1) * blk)


def _mlp_up(h2, w_up_ref, j):
    up = jnp.maximum(_bdot(h2, w_up_ref[:, _mlp_cols(w_up_ref, j)]), 0.0)
    return (up * up).astype(BF16)


def _mlp_down(up, w_up_ref, w_down_ref, j):
    return _bdot(up, w_down_ref[_mlp_cols(w_up_ref, j), :])


def _final_norm(x, g2, mlp, g_final_ref):
    return _rms(x + g2 * mlp) * g_final_ref[...]


def _mod_kernel(c_s_ref, c_p_ref, w_ref, b_ref, o_s_ref, o_p_ref):
    k = pl.program_id(0)
    n_s = c_s_ref.shape[0]
    kb = w_ref.shape[0]
    cols = pl.ds(pl.multiple_of(k * kb, kb), kb)
    c = jnp.concatenate([c_s_ref[:, cols], c_p_ref[:, cols]], axis=0)
    part = _bdot(_silu(c).astype(BF16), w_ref[...].astype(BF16))

    @pl.when(k == 0)
    def _():
        o_s_ref[...] = part[:n_s] + b_ref[...]
        o_p_ref[...] = part[n_s:] + b_ref[...]

    @pl.when(k > 0)
    def _():
        o_s_ref[...] += part[:n_s]
        o_p_ref[...] += part[n_s:]


def _modulation(c_s, c_p, w_ada, b_ada):
    (n_s, d), n_p = c_s.shape, c_p.shape[0]
    n = w_ada.shape[1]
    assert n_s % SUBLANES == 0 and n_p % SUBLANES == 0 and d % MOD_TILE_K == 0
    return pl.pallas_call(
        _mod_kernel,
        grid=(d // MOD_TILE_K,),
        in_specs=[
            pl.BlockSpec((n_s, d), lambda k: (0, 0)),
            pl.BlockSpec((n_p, d), lambda k: (0, 0)),
            pl.BlockSpec((MOD_TILE_K, n), lambda k: (k, 0)),
            pl.BlockSpec((1, n), lambda k: (0, 0)),
        ],
        out_specs=[
            pl.BlockSpec((n_s, n), lambda k: (0, 0)),
            pl.BlockSpec((n_p, n), lambda k: (0, 0)),
        ],
        out_shape=[
            jax.ShapeDtypeStruct((n_s, n), F32),
            jax.ShapeDtypeStruct((n_p, n), F32),
        ],
        compiler_params=pltpu.CompilerParams(
            dimension_semantics=("arbitrary",), vmem_limit_bytes=VMEM_LIMIT_BYTES),
        name="adaln_modulation",
    )(c_s, c_p, w_ada, b_ada)


def _split_proj(proj):
    width = proj.shape[1] // N_PROJ_GROUPS
    return tuple(proj[:, p * width:(p + 1) * width] for p in range(N_PROJ_GROUPS))


def _weight_blocks(w_hbm, w_vmem):
    rows, cols = w_hbm.shape
    br, bc = WEIGHT_STAGE_SHAPE
    assert rows % br == 0 and cols % bc == 0
    return [(w_hbm.at[r:r + br, c:c + bc], w_vmem.at[r:r + br, c:c + bc])
            for r in range(0, rows, br) for c in range(0, cols, bc)]


def _stage_copy(src, stage_ref, sem_ref, i):
    slot = i % WEIGHT_STAGE_SLOTS
    return pltpu.make_async_copy(src, stage_ref.at[slot], sem_ref.at[slot])


def _load_weights_as_bf16(w_hbm_refs, w_vmem_refs, stage_ref, sem_ref):
    blocks = [blk for w_hbm, w_vmem in zip(w_hbm_refs, w_vmem_refs)
              for blk in _weight_blocks(w_hbm, w_vmem)]
    ahead = WEIGHT_STAGE_SLOTS - 1
    for i in range(min(ahead, len(blocks))):
        _stage_copy(blocks[i][0], stage_ref, sem_ref, i).start()
    for i, (src, dst) in enumerate(blocks):
        if i + ahead < len(blocks):
            _stage_copy(blocks[i + ahead][0], stage_ref, sem_ref, i + ahead).start()
        _stage_copy(src, stage_ref, sem_ref, i).wait()
        dst[...] = stage_ref[i % WEIGHT_STAGE_SLOTS].astype(BF16)


def _prompt_kernel(x_ref, mod_ref, lb_ref, w_in_hbm, w_conv_ref, g_onorm_ref, w_out_hbm,
                   w_up_hbm, w_down_hbm, g_final_ref,
                   y_ref, rec_ref, conv_ref, w_in_out, w_out_out, w_up_out, w_down_out,
                   st_ref, ubuf_ref, mix_ref, x1_ref, h2_ref,
                   w_in_ref, w_out_ref, w_up_ref, w_down_ref, stage_ref, stage_sem, out_sem,
                   *, n_tiles, n_steps):
    s = pl.program_id(0)
    tile = jnp.minimum(s, n_steps - 1)
    l = lax.rem(tile, n_tiles)
    live = s < n_steps
    tl, d = x_ref.shape
    d_rec = lb_ref.shape[1]
    dk = d_rec // H_REC
    n_chunks = tl // CHUNK
    heads = [slice(hd * dk, (hd + 1) * dk) for hd in range(H_REC)]

    w_vmem = (w_in_ref, w_out_ref, w_up_ref, w_down_ref)
    w_outs = (w_in_out, w_out_out, w_up_out, w_down_out)

    def bf16_export(i):
        return pltpu.make_async_copy(w_vmem[i], w_outs[i], out_sem.at[i])

    @pl.when(s == 0)
    def _():
        x1_ref[...] = jnp.zeros_like(x1_ref)
        h2_ref[...] = jnp.zeros_like(h2_ref)
        _load_weights_as_bf16((w_in_hbm, w_out_hbm, w_up_hbm, w_down_hbm), w_vmem,
                              stage_ref, stage_sem)
        for i in range(len(w_vmem)):
            bf16_export(i).start()

    @pl.when(s == n_steps)
    def _():
        for i in range(len(w_vmem)):
            bf16_export(i).wait()

    @pl.when(l == 0)
    def _():
        st_ref[...] = jnp.zeros_like(st_ref)
        ubuf_ref[0:SUBLANES, :] = jnp.zeros((SUBLANES, ubuf_ref.shape[1]), F32)

    mod_prev = mod_ref[pl.ds(lax.div(jnp.maximum(s - 1, 0), n_tiles), 1), :]
    g2_prev = mod_prev[:, (N_MOD - 1) * d:]
    mod = mod_ref[pl.ds(lax.div(tile, n_tiles), 1), :]
    sh1, sc1, g1, sh2, sc2 = (mod[:, i * d:(i + 1) * d] for i in range(N_MOD - 1))

    h2 = h2_ref[...]
    up = _mlp_up(h2, w_up_ref, 0)

    x = x_ref[...]
    proj = _bdot(_modulate(x, sh1, sc1), w_in_ref[...])
    q, fz, iv, g, gb, gc, hv = _split_proj(proj)
    d_conv = gb.shape[1]

    up_next = _mlp_up(h2, w_up_ref, 1)
    mlp = _mlp_down(up, w_up_ref, w_down_ref, 0)
    up = up_next

    lb = _layer_lower_bound(lb_ref, 0)
    row = lax.broadcasted_iota(jnp.int32, (CHUNK, CHUNK), 0)
    col = lax.broadcasted_iota(jnp.int32, (CHUNK, CHUNK), 1)
    causal = row >= col
    logf, k_all = _gates(fz, lb)
    cum_all = _group_cumsum(logf, CHUNK)
    q_dec, decay, scores, upd = [], [], [], []
    for c in range(n_chunks):
        rows = slice(c * CHUNK, (c + 1) * CHUNK)
        cum = cum_all[rows]
        last = cum[CHUNK - 1:CHUNK, :]
        q_dec.append((q[rows] * jnp.exp(cum)).astype(BF16))
        k_dec = (k_all[rows] * jnp.exp(-cum)).astype(BF16)
        k_end = (k_all[rows] * jnp.exp(last - cum)).astype(BF16)
        decay.append(jnp.exp(last))
        v = iv[rows].astype(BF16)
        scores.append([jnp.where(causal, _dot_nt(q_dec[c][:, sl], k_dec[:, sl]), 0.0).astype(BF16)
                       for sl in heads])
        upd.append([_dot_tn(v[:, sl], k_end[:, sl]) for sl in heads])

    up_next = _mlp_up(h2, w_up_ref, 2)
    mlp += _mlp_down(up, w_up_ref, w_down_ref, 1)
    up = up_next

    st = [st_ref[hd] for hd in range(H_REC)]
    o_chunks = []
    for c in range(n_chunks):
        v = iv[c * CHUNK:(c + 1) * CHUNK].astype(BF16)
        o_heads = []
        for hd, sl in enumerate(heads):
            o_heads.append(_bdot(scores[c][hd], v[:, sl])
                           + _dot_nt(q_dec[c][:, sl], st[hd].astype(BF16)))
            st[hd] = st[hd] * decay[c][:, sl] + upd[c][hd]
        o_chunks.append(jnp.concatenate(o_heads, axis=-1))
    for hd in range(H_REC):
        st_ref[hd] = st[hd]

    up_next = _mlp_up(h2, w_up_ref, 3)
    mlp += _mlp_down(up, w_up_ref, w_down_ref, 2)
    up = up_next

    o_rec = _head_out(jnp.concatenate(o_chunks, axis=0), g, g_onorm_ref[...])
    mix_ref[:, 0:d_rec] = o_rec.astype(BF16)

    u = gc * hv
    ubuf_ref[SUBLANES:SUBLANES + tl, :] = u
    w_conv = w_conv_ref[...]
    y_conv = (w_conv[0:1] * ubuf_ref[SUBLANES - 2:SUBLANES - 2 + tl, :]
              + w_conv[1:2] * ubuf_ref[SUBLANES - 1:SUBLANES - 1 + tl, :]
              + w_conv[2:3] * u)
    mix_ref[:, d_rec:d_rec + d_conv] = (gb * y_conv).astype(BF16)
    tail = u[tl - (CONV_W - 1):tl]
    ubuf_ref[SUBLANES - (CONV_W - 1):SUBLANES, :] = tail

    mlp += _mlp_down(up, w_up_ref, w_down_ref, 3)
    y_ref[...] = _final_norm(x1_ref[...], g2_prev, mlp, g_final_ref)

    x1 = x + g1 * _bdot(mix_ref[...], w_out_ref[...])
    x1_ref[...] = x1
    h2_ref[...] = _modulate(x1, sh2, sc2)

    @pl.when(live & (l == n_tiles - 1))
    def _():
        for hd in range(H_REC):
            rec_ref[hd] = st_ref[hd].T
        conv_ref[...] = tail


def _prompt_layer(x, mod_p, lower_bounds, w_in, w_conv, g_onorm, w_out, w_up, w_down, g_final):
    bsz, seq, d = x.shape
    d_rec = lower_bounds.shape[1]
    dk = d_rec // H_REC
    d_conv = w_conv.shape[1]
    tl = PROMPT_TILE
    n_tiles = seq // tl
    n_steps = bsz * n_tiles
    assert seq % tl == 0 and tl % CHUNK == 0 and MLP_BLOCKS == 4
    mats = (w_in, w_out, w_up, w_down)
    hbm = pl.BlockSpec(memory_space=pl.ANY)

    def mix_tile(s):
        t = jnp.minimum(s, n_steps - 1)
        return lax.div(t, n_tiles), lax.rem(t, n_tiles)

    def mlp_tile(s):
        t = jnp.maximum(s - 1, 0)
        return lax.div(t, n_tiles), lax.rem(t, n_tiles)

    return pl.pallas_call(
        functools.partial(_prompt_kernel, n_tiles=n_tiles, n_steps=n_steps),
        grid=(n_steps + 1,),
        in_specs=[
            pl.BlockSpec((None, tl, d), lambda s: (*mix_tile(s), 0)),
            _const_spec(mod_p.shape),
            _const_spec(lower_bounds.shape),
            hbm,
            _const_spec(w_conv.shape),
            _const_spec(g_onorm.shape),
            hbm,
            hbm,
            hbm,
            _const_spec(g_final.shape),
        ],
        out_specs=[
            pl.BlockSpec((None, tl, d), lambda s: (*mlp_tile(s), 0)),
            pl.BlockSpec((None, H_REC, dk, dk), lambda s: (mix_tile(s)[0], 0, 0, 0)),
            pl.BlockSpec((None, CONV_W - 1, d_conv), lambda s: (mix_tile(s)[0], 0, 0)),
        ] + [hbm] * len(mats),
        out_shape=[
            jax.ShapeDtypeStruct((bsz, seq, d), F32),
            jax.ShapeDtypeStruct((bsz, H_REC, dk, dk), F32),
            jax.ShapeDtypeStruct((bsz, CONV_W - 1, d_conv), F32),
        ] + [jax.ShapeDtypeStruct(w.shape, BF16) for w in mats],
        scratch_shapes=[
            pltpu.VMEM((H_REC, dk, dk), F32),
            pltpu.VMEM((SUBLANES + tl, d_conv), F32),
            pltpu.VMEM((tl, d_rec + d_conv), BF16),
            pltpu.VMEM((tl, d), F32),
            pltpu.VMEM((tl, d), BF16),
        ] + [pltpu.VMEM(w.shape, BF16) for w in mats] + [
            pltpu.VMEM((WEIGHT_STAGE_SLOTS, *WEIGHT_STAGE_SHAPE), F32),
            pltpu.SemaphoreType.DMA((WEIGHT_STAGE_SLOTS,)),
            pltpu.SemaphoreType.DMA((len(mats),)),
        ],
        compiler_params=pltpu.CompilerParams(
            dimension_semantics=("arbitrary",), vmem_limit_bytes=VMEM_LIMIT_BYTES),
        name="prompt_layer",
    )(x, mod_p, lower_bounds, w_in, w_conv, g_onorm, w_out, w_up, w_down, g_final)


def _sample_kernel(x_ref, mod_ref, lb_ref, w_in_ref, w_conv_ref, g_onorm_ref, w_out_ref,
                   w_up_ref, w_down_ref, g_final_ref, rec_in_ref, conv_in_ref,
                   y_ref, rec_ref, conv_ref,
                   modx_ref, cbx_ref, ubuf_ref, mix_ref, *, seq):
    rows, d = x_ref.shape
    tb = rows // seq
    d_rec = lb_ref.shape[1]
    dk = d_rec // H_REC
    d_conv = conv_in_ref.shape[2]

    for b in range(tb):
        r = slice(b * seq, (b + 1) * seq)
        modx_ref[r, :] = jnp.broadcast_to(mod_ref[b:b + 1, :], (seq, mod_ref.shape[1]))
        for j in range(CONV_W - 1):
            cbx_ref[j, r, :] = jnp.broadcast_to(conv_in_ref[b, j:j + 1, :], (seq, d_conv))

    sh1, sc1, g1, sh2, sc2, g2 = (modx_ref[:, i * d:(i + 1) * d] for i in range(N_MOD))

    x = x_ref[...]
    proj = _bdot(_modulate(x, sh1, sc1), w_in_ref[...])
    q, fz, iv, g, gb, gc, hv = _split_proj(proj)

    lb = _layer_lower_bound(lb_ref, 0)
    row = lax.broadcasted_iota(jnp.int32, (rows, rows), 0)
    col = lax.broadcasted_iota(jnp.int32, (rows, rows), 1)
    causal = ((row // seq) == (col // seq)) & (row >= col)
    logf, k = _gates(fz, lb)
    cum = _group_cumsum(logf, seq)
    last = _group_last(cum, seq)
    q_dec = (q * jnp.exp(cum)).astype(BF16)
    k_dec = (k * jnp.exp(-cum)).astype(BF16)
    k_end = k * jnp.exp(last - cum)
    decay = jnp.exp(last)
    vb = iv.astype(BF16)

    d_hi = decay.astype(BF16).astype(F32)
    r1 = decay - d_hi
    d_mid = r1.astype(BF16).astype(F32)
    d_lo = r1 - d_mid
    tok_rec = _group_pos((rows, d_rec), seq)
    dec3 = jnp.where(tok_rec == 0, d_hi,
                     jnp.where(tok_rec == 1, d_mid, jnp.where(tok_rec == 2, d_lo, 0.0)))
    sub = lax.broadcasted_iota(jnp.int32, (seq, dk), 0)
    ones3 = jnp.where(sub < 3, 1.0, 0.0)
    zeros_blk = jnp.zeros((seq, dk), F32)
    rhs_bottom = jnp.concatenate([zeros_blk, ones3], axis=1)

    o_intra = []
    for hd in range(H_REC):
        sl = slice(hd * dk, (hd + 1) * dk)
        scores = jnp.where(causal, _dot_nt(q_dec[:, sl], k_dec[:, sl]), 0.0)
        o_intra.append(_bdot(scores.astype(BF16), vb[:, sl]))
    o_intra = jnp.concatenate(o_intra, axis=-1)

    o_rows = []
    for b in range(tb):
        r0 = b * seq
        r = slice(r0, r0 + seq)
        o_heads = []
        for hd in range(H_REC):
            sl = slice(hd * dk, (hd + 1) * dk)
            s0 = rec_in_ref[b, hd]
            o_heads.append(_bdot(q_dec[r, sl], s0.astype(BF16)))
            lhs = jnp.concatenate([k_end[r, sl], dec3[r, sl]], axis=0)
            rhs = jnp.concatenate(
                [jnp.concatenate([iv[r, sl], zeros_blk], axis=1), rhs_bottom], axis=0)
            both = _dot_tn(lhs.astype(BF16), rhs.astype(BF16))
            rec_ref[b, hd] = both[:, dk:] * s0 + both[:, :dk]
        o_rows.append(jnp.concatenate(o_heads, axis=-1))
    o_rec = _head_out(o_intra + jnp.concatenate(o_rows, axis=0), g, g_onorm_ref[...])
    mix_ref[:, 0:d_rec] = o_rec.astype(BF16)

    u = gc * hv
    ubuf_ref[SUBLANES:SUBLANES + rows, :] = u
    tok = _group_pos((rows, d_conv), seq)
    u_m1 = jnp.where(tok >= 1, ubuf_ref[SUBLANES - 1:SUBLANES - 1 + rows, :], cbx_ref[1])
    u_m2 = jnp.where(tok >= 2, ubuf_ref[SUBLANES - 2:SUBLANES - 2 + rows, :],
                     jnp.where(tok == 1, cbx_ref[1], cbx_ref[0]))
    w_conv = w_conv_ref[...]
    y_conv = w_conv[0:1] * u_m2 + w_conv[1:2] * u_m1 + w_conv[2:3] * u
    mix_ref[:, d_rec:d_rec + d_conv] = (gb * y_conv).astype(BF16)
    for b in range(tb):
        conv_ref[b] = u[(b + 1) * seq - (CONV_W - 1):(b + 1) * seq]

    x1 = x + g1 * _bdot(mix_ref[...], w_out_ref[...])
    h2 = _modulate(x1, sh2, sc2)
    mlp = _mlp_down(_mlp_up(h2, w_up_ref, 0), w_up_ref, w_down_ref, 0)
    for j in range(1, MLP_BLOCKS):
        mlp += _mlp_down(_mlp_up(h2, w_up_ref, j), w_up_ref, w_down_ref, j)
    y_ref[...] = _final_norm(x1, g2, mlp, g_final_ref)


def _sample_layer(x2d, seq, mod, lower_bounds, w_in, w_conv, g_onorm, w_out, w_up, w_down,
                  g_final, rec_in, conv_in):
    n_rows, d = x2d.shape
    bsz = n_rows // seq
    d_rec = lower_bounds.shape[1]
    dk = d_rec // H_REC
    d_conv = w_conv.shape[1]
    tb = SAMPLE_TILE_B
    rows = tb * seq
    assert bsz % tb == 0 and seq == SUBLANES and seq >= CONV_W
    return pl.pallas_call(
        functools.partial(_sample_kernel, seq=seq),
        grid=(bsz // tb,),
        in_specs=[
            pl.BlockSpec((rows, d), lambda i: (i, 0)),
            pl.BlockSpec((tb, N_MOD * d), lambda i: (i, 0)),
            _const_spec(lower_bounds.shape),
            _const_spec(w_in.shape),
            _const_spec(w_conv.shape),
            _const_spec(g_onorm.shape),
            _const_spec(w_out.shape),
            _const_spec(w_up.shape),
            _const_spec(w_down.shape),
            _const_spec(g_final.shape),
            pl.BlockSpec((tb, H_REC, dk, dk), lambda i: (i, 0, 0, 0)),
            pl.BlockSpec((tb, CONV_W - 1, d_conv), lambda i: (i, 0, 0)),
        ],
        out_specs=[
            pl.BlockSpec((rows, d), lambda i: (i, 0)),
            pl.BlockSpec((tb, H_REC, dk, dk), lambda i: (i, 0, 0, 0)),
            pl.BlockSpec((tb, CONV_W - 1, d_conv), lambda i: (i, 0, 0)),
        ],
        out_shape=[
            jax.ShapeDtypeStruct((n_rows, d), F32),
            jax.ShapeDtypeStruct((bsz, H_REC, dk, dk), F32),
            jax.ShapeDtypeStruct((bsz, CONV_W - 1, d_conv), F32),
        ],
        scratch_shapes=[
            pltpu.VMEM((rows, N_MOD * d), F32),
            pltpu.VMEM((CONV_W - 1, rows, d_conv), F32),
            pltpu.VMEM((SUBLANES + rows, d_conv), F32),
            pltpu.VMEM((rows, d_rec + d_conv), BF16),
        ],
        compiler_params=pltpu.CompilerParams(
            dimension_semantics=("arbitrary",), vmem_limit_bytes=VMEM_LIMIT_BYTES),
        name="sample_layer",
    )(x2d, mod, lower_bounds, w_in, w_conv, g_onorm, w_out, w_up, w_down, g_final, rec_in, conv_in)


def kernel(x_prompt, x_sample, state_rec, state_conv, c_prompt, c_sample, lower_bounds, w_ada,
           b_ada, w_in, w_conv, g_onorm, w_out, w_up, w_down, g_final):
    depth = w_in.shape[0]
    assert depth == 1, "single-layer trunk"
    bsz_s, seq_s, d = x_sample.shape

    mod_s, mod_p = _modulation(c_sample, c_prompt, w_ada[0], b_ada)

    g_final = g_final.reshape(1, d)
    y_p, rec_p, conv_p, w_in_b, w_out_b, w_up_b, w_down_b = _prompt_layer(
        x_prompt, mod_p, lower_bounds, w_in[0], w_conv[0], g_onorm, w_out[0], w_up[0], w_down[0],
        g_final)
    y_s, rec_s, conv_s = _sample_layer(
        x_sample.reshape(bsz_s * seq_s, d), seq_s, mod_s, lower_bounds, w_in_b, w_conv[0], g_onorm,
        w_out_b, w_up_b, w_down_b, g_final, state_rec[0], state_conv[0])
    return (y_p, y_s.reshape(bsz_s, seq_s, d), rec_p[None], conv_p[None], rec_s[None],
            conv_s[None])
```

```python
import functools

import jax
import jax.numpy as jnp
from jax import lax
from jax.experimental import pallas as pl
from jax.experimental.pallas import tpu as pltpu

F32 = jnp.float32
BF16 = jnp.bfloat16

EPS = 1e-6
H_REC = 4
CONV_W = 3
N_MOD = 6
N_PROJ_GROUPS = 7
CHUNK = 64

SUBLANES = 8
VMEM_LIMIT_BYTES = 56 * 1024 * 1024

PROMPT_TILE = 512
SAMPLE_TILE_B = 16
MOD_TILE_K = 256
MLP_BLOCKS = 4
WEIGHT_STAGE_SHAPE = (512, 512)
WEIGHT_STAGE_SLOTS = 4

def _const_spec(shape):
    zeros = (0,) * len(shape)
    return pl.BlockSpec(shape, lambda *_: zeros, pipeline_mode=pl.Buffered(1))


def _rms(x):
    return x * lax.rsqrt(jnp.mean(x * x, axis=-1, keepdims=True) + EPS)


def _silu(x):
    return x * jax.nn.sigmoid(x)


def _bdot(a, b):
    return jnp.dot(a, b, preferred_element_type=F32)


def _dot_nt(a, b):
    return lax.dot_general(a, b, (((1,), (1,)), ((), ())), preferred_element_type=F32)


def _dot_tn(a, b):
    return lax.dot_general(a, b, (((0,), (0,)), ((), ())), preferred_element_type=F32)


def _group_pos(shape, group):
    return lax.broadcasted_iota(jnp.int32, shape, 0) % group


def _group_cumsum(x, group):
    pos = _group_pos(x.shape, group)
    step = 1
    while step < group:
        x = x + jnp.where(pos >= step, pltpu.roll(x, step, axis=0), 0.0)
        step *= 2
    return x


def _group_last(x, group):
    rows = x.shape[0]
    pos = _group_pos(x.shape, group)
    x = jnp.where(pos == group - 1, x, 0.0)
    step = 1
    while step < group:
        x = x + jnp.where(pos + step < group, pltpu.roll(x, rows - step, axis=0), 0.0)
        step *= 2
    return x


def _layer_lower_bound(lb_ref, layer):
    lb = lb_ref[...]
    e = jnp.exp(lb - jnp.max(lb, axis=0, keepdims=True))
    sm = e / jnp.sum(e, axis=0, keepdims=True)
    return jnp.sum(sm[: layer + 1], axis=0, keepdims=True)


def _gates(fz, lb):
    sig = jax.nn.sigmoid(fz)
    logf = jnp.log(lb + (1.0 - lb) * sig)
    k = (1.0 - lb) * (1.0 - sig)
    return logf, k


def _head_out(o, g, g_onorm):
    dv = o.shape[-1] // H_REC
    pieces = []
    for h in range(H_REC):
        sl = slice(h * dv, (h + 1) * dv)
        pieces.append(_rms(o[:, sl]))
    return (jnp.concatenate(pieces, axis=-1) * g_onorm) * _silu(g)


def _modulate(x, shift, scale):
    return (_rms(x) * (1.0 + scale) + shift).astype(BF16)


def _mlp_cols(w_up_ref, j):
    blk = w_up_ref.shape[1] // MLP_BLOCKS
    return slice(j * blk, (j + 1) * blk)


def _mlp_up(h2, w_up_ref, j):
    up = jnp.maximum(_bdot(h2, w_up_ref[:, _mlp_cols(w_up_ref, j)]), 0.0)
    return (up * up).astype(BF16)


def _mlp_down(up, w_up_ref, w_down_ref, j):
    return _bdot(up, w_down_ref[_mlp_cols(w_up_ref, j), :])


def _final_norm(x, g2, mlp, g_final_ref):
    return _rms(x + g2 * mlp) * g_final_ref[...]


def _mod_kernel(c_s_ref, c_p_ref, w_ref, b_ref, o_s_ref, o_p_ref):
    k = pl.program_id(0)
    n_s = c_s_ref.shape[0]
    kb = w_ref.shape[0]
    cols = pl.ds(pl.multiple_of(k * kb, kb), kb)
    c = jnp.concatenate([c_s_ref[:, cols], c_p_ref[:, cols]], axis=0)
    part = _bdot(_silu(c).astype(BF16), w_ref[...].astype(BF16))

    @pl.when(k == 0)
    def _():
        o_s_ref[...] = part[:n_s] + b_ref[...]
        o_p_ref[...] = part[n_s:] + b_ref[...]

    @pl.when(k > 0)
    def _():
        o_s_ref[...] += part[:n_s]
        o_p_ref[...] += part[n_s:]


def _modulation(c_s, c_p, w_ada, b_ada):
    (n_s, d), n_p = c_s.shape, c_p.shape[0]
    n = w_ada.shape[1]
    assert n_s % SUBLANES == 0 and n_p % SUBLANES == 0 and d % MOD_TILE_K == 0
    return pl.pallas_call(
        _mod_kernel,
        grid=(d // MOD_TILE_K,),
        in_specs=[
            pl.BlockSpec((n_s, d), lambda k: (0, 0)),
            pl.BlockSpec((n_p, d), lambda k: (0, 0)),
            pl.BlockSpec((MOD_TILE_K, n), lambda k: (k, 0)),
            pl.BlockSpec((1, n), lambda k: (0, 0)),
        ],
        out_specs=[
            pl.BlockSpec((n_s, n), lambda k: (0, 0)),
            pl.BlockSpec((n_p, n), lambda k: (0, 0)),
        ],
        out_shape=[
            jax.ShapeDtypeStruct((n_s, n), F32),
            jax.ShapeDtypeStruct((n_p, n), F32),
        ],
        compiler_params=pltpu.CompilerParams(
            dimension_semantics=("arbitrary",), vmem_limit_bytes=VMEM_LIMIT_BYTES),
        name="adaln_modulation",
    )(c_s, c_p, w_ada, b_ada)


def _split_proj(proj):
    width = proj.shape[1] // N_PROJ_GROUPS
    return tuple(proj[:, p * width:(p + 1) * width] for p in range(N_PROJ_GROUPS))


def _weight_blocks(w_hbm, w_vmem):
    rows, cols = w_hbm.shape
    br, bc = WEIGHT_STAGE_SHAPE
    assert rows % br == 0 and cols % bc == 0
    return [(w_hbm.at[r:r + br, c:c + bc], w_vmem.at[r:r + br, c:c + bc])
            for r in range(0, rows, br) for c in range(0, cols, bc)]


def _stage_copy(src, stage_ref, sem_ref, i):
    slot = i % WEIGHT_STAGE_SLOTS
    return pltpu.make_async_copy(src, stage_ref.at[slot], sem_ref.at[slot])


def _load_weights_as_bf16(w_hbm_refs, w_vmem_refs, stage_ref, sem_ref):
    blocks = [blk for w_hbm, w_vmem in zip(w_hbm_refs, w_vmem_refs)
              for blk in _weight_blocks(w_hbm, w_vmem)]
    ahead = WEIGHT_STAGE_SLOTS - 1
    for i in range(min(ahead, len(blocks))):
        _stage_copy(blocks[i][0], stage_ref, sem_ref, i).start()
    for i, (src, dst) in enumerate(blocks):
        if i + ahead < len(blocks):
            _stage_copy(blocks[i + ahead][0], stage_ref, sem_ref, i + ahead).start()
        _stage_copy(src, stage_ref, sem_ref, i).wait()
        dst[...] = stage_ref[i % WEIGHT_STAGE_SLOTS].astype(BF16)


def _prompt_kernel(x_ref, mod_ref, lb_ref, w_in_hbm, w_conv_ref, g_onorm_ref, w_out_hbm,
                   w_up_hbm, w_down_hbm, g_final_ref,
                   y_ref, rec_ref, conv_ref, w_in_out, w_out_out, w_up_out, w_down_out,
                   st_ref, ubuf_ref, mix_ref, x1_ref, h2_ref,
                   w_in_ref, w_out_ref, w_up_ref, w_down_ref, stage_ref, stage_sem, out_sem,
                   *, n_tiles, n_steps):
    s = pl.program_id(0)
    tile = jnp.minimum(s, n_steps - 1)
    l = lax.rem(tile, n_tiles)
    live = s < n_steps
    tl, d = x_ref.shape
    d_rec = lb_ref.shape[1]
    dk = d_rec // H_REC
    n_chunks = tl // CHUNK
    heads = [slice(hd * dk, (hd + 1) * dk) for hd in range(H_REC)]

    w_vmem = (w_in_ref, w_out_ref, w_up_ref, w_down_ref)
    w_outs = (w_in_out, w_out_out, w_up_out, w_down_out)

    def bf16_export(i):
        return pltpu.make_async_copy(w_vmem[i], w_outs[i], out_sem.at[i])

    @pl.when(s == 0)
    def _():
        x1_ref[...] = jnp.zeros_like(x1_ref)
        h2_ref[...] = jnp.zeros_like(h2_ref)
        _load_weights_as_bf16((w_in_hbm, w_out_hbm, w_up_hbm, w_down_hbm), w_vmem,
                              stage_ref, stage_sem)
        for i in range(len(w_vmem)):
            bf16_export(i).start()

    @pl.when(s == n_steps)
    def _():
        for i in range(len(w_vmem)):
            bf16_export(i).wait()

    @pl.when(l == 0)
    def _():
        st_ref[...] = jnp.zeros_like(st_ref)
        ubuf_ref[0:SUBLANES, :] = jnp.zeros((SUBLANES, ubuf_ref.shape[1]), F32)

    mod_prev = mod_ref[pl.ds(lax.div(jnp.maximum(s - 1, 0), n_tiles), 1), :]
    g2_prev = mod_prev[:, (N_MOD - 1) * d:]
    mod = mod_ref[pl.ds(lax.div(tile, n_tiles), 1), :]
    sh1, sc1, g1, sh2, sc2 = (mod[:, i * d:(i + 1) * d] for i in range(N_MOD - 1))

    h2 = h2_ref[...]
    up = _mlp_up(h2, w_up_ref, 0)

    x = x_ref[...]
    proj = _bdot(_modulate(x, sh1, sc1), w_in_ref[...])
    q, fz, iv, g, gb, gc, hv = _split_proj(proj)
    d_conv = gb.shape[1]

    up_next = _mlp_up(h2, w_up_ref, 1)
    mlp = _mlp_down(up, w_up_ref, w_down_ref, 0)
    up = up_next

    lb = _layer_lower_bound(lb_ref, 0)
    row = lax.broadcasted_iota(jnp.int32, (CHUNK, CHUNK), 0)
    col = lax.broadcasted_iota(jnp.int32, (CHUNK, CHUNK), 1)
    causal = row >= col
    logf, k_all = _gates(fz, lb)
    cum_all = _group_cumsum(logf, CHUNK)
    q_dec, decay, scores, upd = [], [], [], []
    for c in range(n_chunks):
        rows = slice(c * CHUNK, (c + 1) * CHUNK)
        cum = cum_all[rows]
        last = cum[CHUNK - 1:CHUNK, :]
        q_dec.append((q[rows] * jnp.exp(cum)).astype(BF16))
        k_dec = (k_all[rows] * jnp.exp(-cum)).astype(BF16)
        k_end = (k_all[rows] * jnp.exp(last - cum)).astype(BF16)
        decay.append(jnp.exp(last))
        v = iv[rows].astype(BF16)
        scores.append([jnp.where(causal, _dot_nt(q_dec[c][:, sl], k_dec[:, sl]), 0.0).astype(BF16)
                       for sl in heads])
        upd.append([_dot_tn(v[:, sl], k_end[:, sl]) for sl in heads])

    up_next = _mlp_up(h2, w_up_ref, 2)
    mlp += _mlp_down(up, w_up_ref, w_down_ref, 1)
    up = up_next

    st = [st_ref[hd] for hd in range(H_REC)]
    o_chunks = []
    for c in range(n_chunks):
        v = iv[c * CHUNK:(c + 1) * CHUNK].astype(BF16)
        o_heads = []
        for hd, sl in enumerate(heads):
            o_heads.append(_bdot(scores[c][hd], v[:, sl])
                           + _dot_nt(q_dec[c][:, sl], st[hd].astype(BF16)))
            st[hd] = st[hd] * decay[c][:, sl] + upd[c][hd]
        o_chunks.append(jnp.concatenate(o_heads, axis=-1))
    for hd in range(H_REC):
        st_ref[hd] = st[hd]

    up_next = _mlp_up(h2, w_up_ref, 3)
    mlp += _mlp_down(up, w_up_ref, w_down_ref, 2)
    up = up_next

    o_rec = _head_out(jnp.concatenate(o_chunks, axis=0), g, g_onorm_ref[...])
    mix_ref[:, 0:d_rec] = o_rec.astype(BF16)

    u = gc * hv
    ubuf_ref[SUBLANES:SUBLANES + tl, :] = u
    w_conv = w_conv_ref[...]
    y_conv = (w_conv[0:1] * ubuf_ref[SUBLANES - 2:SUBLANES - 2 + tl, :]
              + w_conv[1:2] * ubuf_ref[SUBLANES - 1:SUBLANES - 1 + tl, :]
              + w_conv[2:3] * u)
    mix_ref[:, d_rec:d_rec + d_conv] = (gb * y_conv).astype(BF16)
    tail = u[tl - (CONV_W - 1):tl]
    ubuf_ref[SUBLANES - (CONV_W - 1):SUBLANES, :] = tail

    mlp += _mlp_down(up, w_up_ref, w_down_ref, 3)
    y_ref[...] = _final_norm(x1_ref[...], g2_prev, mlp, g_final_ref)

    x1 = x + g1 * _bdot(mix_ref[...], w_out_ref[...])
    x1_ref[...] = x1
    h2_ref[...] = _modulate(x1, sh2, sc2)

    @pl.when(live & (l == n_tiles - 1))
    def _():
        for hd in range(H_REC):
            rec_ref[hd] = st_ref[hd].T
        conv_ref[...] = tail


def _prompt_layer(x, mod_p, lower_bounds, w_in, w_conv, g_onorm, w_out, w_up, w_down, g_final):
    bsz, seq, d = x.shape
    d_rec = lower_bounds.shape[1]
    dk = d_rec // H_REC
    d_conv = w_conv.shape[1]
    tl = PROMPT_TILE
    n_tiles = seq // tl
    n_steps = bsz * n_tiles
    assert seq % tl == 0 and tl % CHUNK == 0 and MLP_BLOCKS == 4
    mats = (w_in, w_out, w_up, w_down)
    hbm = pl.BlockSpec(memory_space=pl.ANY)

    def mix_tile(s):
        t = jnp.minimum(s, n_steps - 1)
        return lax.div(t, n_tiles), lax.rem(t, n_tiles)

    def mlp_tile(s):
        t = jnp.maximum(s - 1, 0)
        return lax.div(t, n_tiles), lax.rem(t, n_tiles)

    return pl.pallas_call(
        functools.partial(_prompt_kernel, n_tiles=n_tiles, n_steps=n_steps),
        grid=(n_steps + 1,),
        in_specs=[
            pl.BlockSpec((None, tl, d), lambda s: (*mix_tile(s), 0)),
            _const_spec(mod_p.shape),
            _const_spec(lower_bounds.shape),
            hbm,
            _const_spec(w_conv.shape),
            _const_spec(g_onorm.shape),
            hbm,
            hbm,
            hbm,
            _const_spec(g_final.shape),
        ],
        out_specs=[
            pl.BlockSpec((None, tl, d), lambda s: (*mlp_tile(s), 0)),
            pl.BlockSpec((None, H_REC, dk, dk), lambda s: (mix_tile(s)[0], 0, 0, 0)),
            pl.BlockSpec((None, CONV_W - 1, d_conv), lambda s: (mix_tile(s)[0], 0, 0)),
        ] + [hbm] * len(mats),
        out_shape=[
            jax.ShapeDtypeStruct((bsz, seq, d), F32),
            jax.ShapeDtypeStruct((bsz, H_REC, dk, dk), F32),
            jax.ShapeDtypeStruct((bsz, CONV_W - 1, d_conv), F32),
        ] + [jax.ShapeDtypeStruct(w.shape, BF16) for w in mats],
        scratch_shapes=[
            pltpu.VMEM((H_REC, dk, dk), F32),
            pltpu.VMEM((SUBLANES + tl, d_conv), F32),
            pltpu.VMEM((tl, d_rec + d_conv), BF16),
            pltpu.VMEM((tl, d), F32),
            pltpu.VMEM((tl, d), BF16),
        ] + [pltpu.VMEM(w.shape, BF16) for w in mats] + [
            pltpu.VMEM((WEIGHT_STAGE_SLOTS, *WEIGHT_STAGE_SHAPE), F32),
            pltpu.SemaphoreType.DMA((WEIGHT_STAGE_SLOTS,)),
            pltpu.SemaphoreType.DMA((len(mats),)),
        ],
        compiler_params=pltpu.CompilerParams(
            dimension_semantics=("arbitrary",), vmem_limit_bytes=VMEM_LIMIT_BYTES),
        name="prompt_layer",
    )(x, mod_p, lower_bounds, w_in, w_conv, g_onorm, w_out, w_up, w_down, g_final)


def _sample_kernel(x_ref, mod_ref, lb_ref, w_in_ref, w_conv_ref, g_onorm_ref, w_out_ref,
                   w_up_ref, w_down_ref, g_final_ref, rec_in_ref, conv_in_ref,
                   y_ref, rec_ref, conv_ref,
                   modx_ref, cbx_ref, ubuf_ref, mix_ref, *, seq):
    rows, d = x_ref.shape
    tb = rows // seq
    d_rec = lb_ref.shape[1]
    dk = d_rec // H_REC
    d_conv = conv_in_ref.shape[2]

    for b in range(tb):
        r = slice(b * seq, (b + 1) * seq)
        modx_ref[r, :] = jnp.broadcast_to(mod_ref[b:b + 1, :], (seq, mod_ref.shape[1]))
        for j in range(CONV_W - 1):
            cbx_ref[j, r, :] = jnp.broadcast_to(conv_in_ref[b, j:j + 1, :], (seq, d_conv))

    sh1, sc1, g1, sh2, sc2, g2 = (modx_ref[:, i * d:(i + 1) * d] for i in range(N_MOD))

    x = x_ref[...]
    proj = _bdot(_modulate(x, sh1, sc1), w_in_ref[...])
    q, fz, iv, g, gb, gc, hv = _split_proj(proj)

    lb = _layer_lower_bound(lb_ref, 0)
    row = lax.broadcasted_iota(jnp.int32, (rows, rows), 0)
    col = lax.broadcasted_iota(jnp.int32, (rows, rows), 1)
    causal = ((row // seq) == (col // seq)) & (row >= col)
    logf, k = _gates(fz, lb)
    cum = _group_cumsum(logf, seq)
    last = _group_last(cum, seq)
    q_dec = (q * jnp.exp(cum)).astype(BF16)
    k_dec = (k * jnp.exp(-cum)).astype(BF16)
    k_end = k * jnp.exp(last - cum)
    decay = jnp.exp(last)
    vb = iv.astype(BF16)

    d_hi = decay.astype(BF16).astype(F32)
    r1 = decay - d_hi
    d_mid = r1.astype(BF16).astype(F32)
    d_lo = r1 - d_mid
    tok_rec = _group_pos((rows, d_rec), seq)
    dec3 = jnp.where(tok_rec == 0, d_hi,
                     jnp.where(tok_rec == 1, d_mid, jnp.where(tok_rec == 2, d_lo, 0.0)))
    sub = lax.broadcasted_iota(jnp.int32, (seq, dk), 0)
    ones3 = jnp.where(sub < 3, 1.0, 0.0)
    zeros_blk = jnp.zeros((seq, dk), F32)
    rhs_bottom = jnp.concatenate([zeros_blk, ones3], axis=1)

    heads = [slice(hd * dk, (hd + 1) * dk) for hd in range(H_REC)]

    def update_states(elements):
        for b in elements:
            r = slice(b * seq, (b + 1) * seq)
            for hd, sl in enumerate(heads):
                lhs = jnp.concatenate([k_end[r, sl], dec3[r, sl]], axis=0)
                rhs = jnp.concatenate(
                    [jnp.concatenate([iv[r, sl], zeros_blk], axis=1), rhs_bottom], axis=0)
                both = _dot_tn(lhs.astype(BF16), rhs.astype(BF16))
                rec_ref[b, hd] = both[:, dk:] * rec_in_ref[b, hd] + both[:, :dk]

    scores = [jnp.where(causal, _dot_nt(q_dec[:, sl], k_dec[:, sl]), 0.0).astype(BF16)
              for sl in heads]
    o_inter = jnp.concatenate(
        [jnp.concatenate([_bdot(q_dec[b * seq:(b + 1) * seq, sl], rec_in_ref[b, hd].astype(BF16))
                          for hd, sl in enumerate(heads)], axis=-1)
         for b in range(tb)], axis=0)
    o_intra = jnp.concatenate([_bdot(scores[hd], vb[:, sl]) for hd, sl in enumerate(heads)],
                              axis=-1)
    o_rec = _head_out(o_intra + o_inter, g, g_onorm_ref[...])
    mix_ref[:, 0:d_rec] = o_rec.astype(BF16)

    u = gc * hv
    ubuf_ref[SUBLANES:SUBLANES + rows, :] = u
    tok = _group_pos((rows, d_conv), seq)
    u_m1 = jnp.where(tok >= 1, ubuf_ref[SUBLANES - 1:SUBLANES - 1 + rows, :], cbx_ref[1])
    u_m2 = jnp.where(tok >= 2, ubuf_ref[SUBLANES - 2:SUBLANES - 2 + rows, :],
                     jnp.where(tok == 1, cbx_ref[1], cbx_ref[0]))
    w_conv = w_conv_ref[...]
    y_conv = w_conv[0:1] * u_m2 + w_conv[1:2] * u_m1 + w_conv[2:3] * u
    mix_ref[:, d_rec:d_rec + d_conv] = (gb * y_conv).astype(BF16)
    for b in range(tb):
        conv_ref[b] = u[(b + 1) * seq - (CONV_W - 1):(b + 1) * seq]

    x1 = x + g1 * _bdot(mix_ref[...], w_out_ref[...])
    update_states(range(tb // 2))
    h2 = _modulate(x1, sh2, sc2)
    mlp = _mlp_down(_mlp_up(h2, w_up_ref, 0), w_up_ref, w_down_ref, 0)
    for j in range(1, MLP_BLOCKS):
        mlp += _mlp_down(_mlp_up(h2, w_up_ref, j), w_up_ref, w_down_ref, j)
    update_states(range(tb // 2, tb))
    y_ref[...] = _final_norm(x1, g2, mlp, g_final_ref)


def _sample_layer(x2d, seq, mod, lower_bounds, w_in, w_conv, g_onorm, w_out, w_up, w_down,
                  g_final, rec_in, conv_in):
    n_rows, d = x2d.shape
    bsz = n_rows // seq
    d_rec = lower_bounds.shape[1]
    dk = d_rec // H_REC
    d_conv = w_conv.shape[1]
    tb = SAMPLE_TILE_B
    rows = tb * seq
    assert bsz % tb == 0 and seq == SUBLANES and seq >= CONV_W
    return pl.pallas_call(
        functools.partial(_sample_kernel, seq=seq),
        grid=(bsz // tb,),
        in_specs=[
            pl.BlockSpec((rows, d), lambda i: (i, 0)),
            pl.BlockSpec((tb, N_MOD * d), lambda i: (i, 0)),
            _const_spec(lower_bounds.shape),
            _const_spec(w_in.shape),
            _const_spec(w_conv.shape),
            _const_spec(g_onorm.shape),
            _const_spec(w_out.shape),
            _const_spec(w_up.shape),
            _const_spec(w_down.shape),
            _const_spec(g_final.shape),
            pl.BlockSpec((tb, H_REC, dk, dk), lambda i: (i, 0, 0, 0)),
            pl.BlockSpec((tb, CONV_W - 1, d_conv), lambda i: (i, 0, 0)),
        ],
        out_specs=[
            pl.BlockSpec((rows, d), lambda i: (i, 0)),
            pl.BlockSpec((tb, H_REC, dk, dk), lambda i: (i, 0, 0, 0)),
            pl.BlockSpec((tb, CONV_W - 1, d_conv), lambda i: (i, 0, 0)),
        ],
        out_shape=[
            jax.ShapeDtypeStruct((n_rows, d), F32),
            jax.ShapeDtypeStruct((bsz, H_REC, dk, dk), F32),
            jax.ShapeDtypeStruct((bsz, CONV_W - 1, d_conv), F32),
        ],
        scratch_shapes=[
            pltpu.VMEM((rows, N_MOD * d), F32),
            pltpu.VMEM((CONV_W - 1, rows, d_conv), F32),
            pltpu.VMEM((SUBLANES + rows, d_conv), F32),
            pltpu.VMEM((rows, d_rec + d_conv), BF16),
        ],
        compiler_params=pltpu.CompilerParams(
            dimension_semantics=("arbitrary",), vmem_limit_bytes=VMEM_LIMIT_BYTES),
        name="sample_layer",
    )(x2d, mod, lower_bounds, w_in, w_conv, g_onorm, w_out, w_up, w_down, g_final, rec_in, conv_in)


def kernel(x_prompt, x_sample, state_rec, state_conv, c_prompt, c_sample, lower_bounds, w_ada,
           b_ada, w_in, w_conv, g_onorm, w_out, w_up, w_down, g_final):
    depth = w_in.shape[0]
    assert depth == 1, "single-layer trunk"
    bsz_s, seq_s, d = x_sample.shape

    mod_s, mod_p = _modulation(c_sample, c_prompt, w_ada[0], b_ada)

    g_final = g_final.reshape(1, d)
    y_p, rec_p, conv_p, w_in_b, w_out_b, w_up_b, w_down_b = _prompt_layer(
        x_prompt, mod_p, lower_bounds, w_in[0], w_conv[0], g_onorm, w_out[0], w_up[0], w_down[0],
        g_final)
    y_s, rec_s, conv_s = _sample_layer(
        x_sample.reshape(bsz_s * seq_s, d), seq_s, mod_s, lower_bounds, w_in_b, w_conv[0], g_onorm,
        w_out_b, w_up_b, w_down_b, g_final, state_rec[0], state_conv[0])
    return (y_p, y_s.reshape(bsz_s, seq_s, d), rec_p[None], conv_p[None], rec_s[None],
            conv_s[None])
```

```python
import functools

import jax
import jax.numpy as jnp
from jax import lax
from jax.experimental import pallas as pl
from jax.experimental.pallas import tpu as pltpu

F32 = jnp.float32
BF16 = jnp.bfloat16

EPS = 1e-6
H_REC = 4
CONV_W = 3
N_MOD = 6
N_PROJ_GROUPS = 7
CHUNK = 64

SUBLANES = 8
BF16_SUBLANES = 16
VMEM_LIMIT_BYTES = 56 * 1024 * 1024

PROMPT_TILE = 512
SAMPLE_TILE_B = 16
MOD_TILE_K = 256
MLP_BLOCKS = 4
WEIGHT_STAGE_SHAPE = (512, 512)
WEIGHT_STAGE_SLOTS = 4

def _const_spec(shape):
    zeros = (0,) * len(shape)
    return pl.BlockSpec(shape, lambda *_: zeros, pipeline_mode=pl.Buffered(1))


def _rms(x):
    return x * lax.rsqrt(jnp.mean(x * x, axis=-1, keepdims=True) + EPS)


def _silu(x):
    return x * jax.nn.sigmoid(x)


def _bdot(a, b):
    return jnp.dot(a, b, preferred_element_type=F32)


def _dot_nt(a, b):
    return lax.dot_general(a, b, (((1,), (1,)), ((), ())), preferred_element_type=F32)


def _dot_tn(a, b):
    return lax.dot_general(a, b, (((0,), (0,)), ((), ())), preferred_element_type=F32)


def _group_pos(shape, group):
    return lax.broadcasted_iota(jnp.int32, shape, 0) % group


def _group_cumsum(x, group):
    pos = _group_pos(x.shape, group)
    step = 1
    while step < group:
        x = x + jnp.where(pos >= step, pltpu.roll(x, step, axis=0), 0.0)
        step *= 2
    return x


def _group_last(x, group):
    rows = x.shape[0]
    pos = _group_pos(x.shape, group)
    x = jnp.where(pos == group - 1, x, 0.0)
    step = 1
    while step < group:
        x = x + jnp.where(pos + step < group, pltpu.roll(x, rows - step, axis=0), 0.0)
        step *= 2
    return x


def _layer_lower_bound(lb_ref, layer):
    lb = lb_ref[...]
    e = jnp.exp(lb - jnp.max(lb, axis=0, keepdims=True))
    sm = e / jnp.sum(e, axis=0, keepdims=True)
    return jnp.sum(sm[: layer + 1], axis=0, keepdims=True)


def _gates(fz, lb):
    sig = jax.nn.sigmoid(fz)
    logf = jnp.log(lb + (1.0 - lb) * sig)
    k = (1.0 - lb) * (1.0 - sig)
    return logf, k


def _decay_terms(decay, row):
    hi = decay.astype(BF16).astype(F32)
    rest = decay - hi
    mid = rest.astype(BF16).astype(F32)
    lo = rest - mid
    return jnp.where(row == 0, hi, jnp.where(row == 1, mid, jnp.where(row == 2, lo, 0.0)))


def _decay_selector(n_rows, width):
    sub = lax.broadcasted_iota(jnp.int32, (n_rows, width), 0)
    return jnp.concatenate(
        [jnp.zeros((n_rows, width), F32), jnp.where(sub < 3, 1.0, 0.0)], axis=1)


def _head_out(o, g, g_onorm):
    dv = o.shape[-1] // H_REC
    pieces = []
    for h in range(H_REC):
        sl = slice(h * dv, (h + 1) * dv)
        pieces.append(_rms(o[:, sl]))
    return (jnp.concatenate(pieces, axis=-1) * g_onorm) * _silu(g)


def _modulate(x, shift, scale):
    return (_rms(x) * (1.0 + scale) + shift).astype(BF16)


def _mlp_cols(w_up_ref, j):
    blk = w_up_ref.shape[1] // MLP_BLOCKS
    return slice(j * blk, (j + 1) * blk)


def _mlp_up(h2, w_up_ref, j):
    up = jnp.maximum(_bdot(h2, w_up_ref[:, _mlp_cols(w_up_ref, j)]), 0.0)
    return (up * up).astype(BF16)


def _mlp_down(up, w_up_ref, w_down_ref, j):
    return _bdot(up, w_down_ref[_mlp_cols(w_up_ref, j), :])


def _final_norm(x, g2, mlp, g_final_ref):
    return _rms(x + g2 * mlp) * g_final_ref[...]


def _mod_kernel(c_s_ref, c_p_ref, w_ref, b_ref, o_s_ref, o_p_ref):
    k = pl.program_id(0)
    n_s = c_s_ref.shape[0]
    kb = w_ref.shape[0]
    cols = pl.ds(pl.multiple_of(k * kb, kb), kb)
    c = jnp.concatenate([c_s_ref[:, cols], c_p_ref[:, cols]], axis=0)
    part = _bdot(_silu(c).astype(BF16), w_ref[...].astype(BF16))

    @pl.when(k == 0)
    def _():
        o_s_ref[...] = part[:n_s] + b_ref[...]
        o_p_ref[...] = part[n_s:] + b_ref[...]

    @pl.when(k > 0)
    def _():
        o_s_ref[...] += part[:n_s]
        o_p_ref[...] += part[n_s:]


def _modulation(c_s, c_p, w_ada, b_ada):
    (n_s, d), n_p = c_s.shape, c_p.shape[0]
    n = w_ada.shape[1]
    assert n_s % SUBLANES == 0 and n_p % SUBLANES == 0 and d % MOD_TILE_K == 0
    return pl.pallas_call(
        _mod_kernel,
        grid=(d // MOD_TILE_K,),
        in_specs=[
            pl.BlockSpec((n_s, d), lambda k: (0, 0)),
            pl.BlockSpec((n_p, d), lambda k: (0, 0)),
            pl.BlockSpec((MOD_TILE_K, n), lambda k: (k, 0)),
            pl.BlockSpec((1, n), lambda k: (0, 0)),
        ],
        out_specs=[
            pl.BlockSpec((n_s, n), lambda k: (0, 0)),
            pl.BlockSpec((n_p, n), lambda k: (0, 0)),
        ],
        out_shape=[
            jax.ShapeDtypeStruct((n_s, n), F32),
            jax.ShapeDtypeStruct((n_p, n), F32),
        ],
        compiler_params=pltpu.CompilerParams(
            dimension_semantics=("arbitrary",), vmem_limit_bytes=VMEM_LIMIT_BYTES),
        name="adaln_modulation",
    )(c_s, c_p, w_ada, b_ada)


def _split_proj(proj):
    width = proj.shape[1] // N_PROJ_GROUPS
    return tuple(proj[:, p * width:(p + 1) * width] for p in range(N_PROJ_GROUPS))


def _weight_blocks(w_hbm, w_vmem):
    rows, cols = w_hbm.shape
    br, bc = WEIGHT_STAGE_SHAPE
    assert rows % br == 0 and cols % bc == 0
    return [(w_hbm.at[r:r + br, c:c + bc], w_vmem.at[r:r + br, c:c + bc])
            for r in range(0, rows, br) for c in range(0, cols, bc)]


def _stage_copy(src, stage_ref, sem_ref, i):
    slot = i % WEIGHT_STAGE_SLOTS
    return pltpu.make_async_copy(src, stage_ref.at[slot], sem_ref.at[slot])


def _load_weights_as_bf16(w_hbm_refs, w_vmem_refs, stage_ref, sem_ref):
    blocks = [blk for w_hbm, w_vmem in zip(w_hbm_refs, w_vmem_refs)
              for blk in _weight_blocks(w_hbm, w_vmem)]
    ahead = WEIGHT_STAGE_SLOTS - 1
    for i in range(min(ahead, len(blocks))):
        _stage_copy(blocks[i][0], stage_ref, sem_ref, i).start()
    for i, (src, dst) in enumerate(blocks):
        if i + ahead < len(blocks):
            _stage_copy(blocks[i + ahead][0], stage_ref, sem_ref, i + ahead).start()
        _stage_copy(src, stage_ref, sem_ref, i).wait()
        dst[...] = stage_ref[i % WEIGHT_STAGE_SLOTS].astype(BF16)


def _prompt_kernel(x_ref, mod_ref, lb_ref, w_in_hbm, w_conv_ref, g_onorm_ref, w_out_hbm,
                   w_up_hbm, w_down_hbm, g_final_ref,
                   y_ref, rec_ref, conv_ref, w_in_out, w_out_out, w_up_out, w_down_out,
                   st_ref, ubuf_ref, mix_ref, x1_ref, h2_ref,
                   w_in_ref, w_out_ref, w_up_ref, w_down_ref, stage_ref, stage_sem, out_sem,
                   *, n_tiles, n_steps):
    s = pl.program_id(0)
    tile = jnp.minimum(s, n_steps - 1)
    l = lax.rem(tile, n_tiles)
    live = s < n_steps
    tl, d = x_ref.shape
    d_rec = lb_ref.shape[1]
    dk = d_rec // H_REC
    n_chunks = tl // CHUNK
    heads = [slice(hd * dk, (hd + 1) * dk) for hd in range(H_REC)]

    w_vmem = (w_in_ref, w_out_ref, w_up_ref, w_down_ref)
    w_outs = (w_in_out, w_out_out, w_up_out, w_down_out)

    def bf16_export(i):
        return pltpu.make_async_copy(w_vmem[i], w_outs[i], out_sem.at[i])

    @pl.when(s == 0)
    def _():
        x1_ref[...] = jnp.zeros_like(x1_ref)
        h2_ref[...] = jnp.zeros_like(h2_ref)
        _load_weights_as_bf16((w_in_hbm, w_out_hbm, w_up_hbm, w_down_hbm), w_vmem,
                              stage_ref, stage_sem)
        for i in range(len(w_vmem)):
            bf16_export(i).start()

    @pl.when(s == n_steps)
    def _():
        for i in range(len(w_vmem)):
            bf16_export(i).wait()

    @pl.when(l == 0)
    def _():
        st_ref[...] = jnp.zeros_like(st_ref)
        ubuf_ref[0:SUBLANES, :] = jnp.zeros((SUBLANES, ubuf_ref.shape[1]), F32)

    mod_prev = mod_ref[pl.ds(lax.div(jnp.maximum(s - 1, 0), n_tiles), 1), :]
    g2_prev = mod_prev[:, (N_MOD - 1) * d:]
    mod = mod_ref[pl.ds(lax.div(tile, n_tiles), 1), :]
    sh1, sc1, g1, sh2, sc2 = (mod[:, i * d:(i + 1) * d] for i in range(N_MOD - 1))

    h2 = h2_ref[...]
    up = _mlp_up(h2, w_up_ref, 0)

    x = x_ref[...]
    proj = _bdot(_modulate(x, sh1, sc1), w_in_ref[...])
    q, fz, iv, g, gb, gc, hv = _split_proj(proj)
    d_conv = gb.shape[1]

    up_next = _mlp_up(h2, w_up_ref, 1)
    mlp = _mlp_down(up, w_up_ref, w_down_ref, 0)
    up = up_next

    lb = _layer_lower_bound(lb_ref, 0)
    row = lax.broadcasted_iota(jnp.int32, (CHUNK, CHUNK), 0)
    col = lax.broadcasted_iota(jnp.int32, (CHUNK, CHUNK), 1)
    causal = row >= col
    logf, k_all = _gates(fz, lb)
    cum_all = _group_cumsum(logf, CHUNK)
    dec_row = lax.broadcasted_iota(jnp.int32, (BF16_SUBLANES, d_rec), 0)
    rhs_bottom = _decay_selector(BF16_SUBLANES, dk)
    zeros_v = jnp.zeros((CHUNK, dk), F32)
    q_dec, scores, upd = [], [], []
    for c in range(n_chunks):
        rows = slice(c * CHUNK, (c + 1) * CHUNK)
        cum = cum_all[rows]
        last = cum[CHUNK - 1:CHUNK, :]
        q_dec.append((q[rows] * jnp.exp(cum)).astype(BF16))
        k_dec = (k_all[rows] * jnp.exp(-cum)).astype(BF16)
        k_end = k_all[rows] * jnp.exp(last - cum)
        dec3 = _decay_terms(jnp.exp(last), dec_row)
        v = iv[rows]
        scores.append([jnp.where(causal, _dot_nt(q_dec[c][:, sl], k_dec[:, sl]), 0.0).astype(BF16)
                       for sl in heads])
        upd.append([])
        for sl in heads:
            lhs = jnp.concatenate([k_end[:, sl], dec3[:, sl]], axis=0)
            rhs = jnp.concatenate(
                [jnp.concatenate([v[:, sl], zeros_v], axis=1), rhs_bottom], axis=0)
            upd[c].append(_dot_tn(lhs.astype(BF16), rhs.astype(BF16)))

    up_next = _mlp_up(h2, w_up_ref, 2)
    mlp += _mlp_down(up, w_up_ref, w_down_ref, 1)
    up = up_next

    st = [st_ref[hd] for hd in range(H_REC)]
    o_chunks = []
    for c in range(n_chunks):
        v = iv[c * CHUNK:(c + 1) * CHUNK].astype(BF16)
        o_heads = []
        for hd, sl in enumerate(heads):
            o_heads.append(_bdot(scores[c][hd], v[:, sl])
                           + _bdot(q_dec[c][:, sl], st[hd].astype(BF16)))
            st[hd] = upd[c][hd][:, dk:] * st[hd] + upd[c][hd][:, :dk]
        o_chunks.append(jnp.concatenate(o_heads, axis=-1))
    for hd in range(H_REC):
        st_ref[hd] = st[hd]

    up_next = _mlp_up(h2, w_up_ref, 3)
    mlp += _mlp_down(up, w_up_ref, w_down_ref, 2)
    up = up_next

    o_rec = _head_out(jnp.concatenate(o_chunks, axis=0), g, g_onorm_ref[...])
    mix_ref[:, 0:d_rec] = o_rec.astype(BF16)

    u = gc * hv
    ubuf_ref[SUBLANES:SUBLANES + tl, :] = u
    w_conv = w_conv_ref[...]
    y_conv = (w_conv[0:1] * ubuf_ref[SUBLANES - 2:SUBLANES - 2 + tl, :]
              + w_conv[1:2] * ubuf_ref[SUBLANES - 1:SUBLANES - 1 + tl, :]
              + w_conv[2:3] * u)
    mix_ref[:, d_rec:d_rec + d_conv] = (gb * y_conv).astype(BF16)
    tail = u[tl - (CONV_W - 1):tl]
    ubuf_ref[SUBLANES - (CONV_W - 1):SUBLANES, :] = tail

    mlp += _mlp_down(up, w_up_ref, w_down_ref, 3)
    y_ref[...] = _final_norm(x1_ref[...], g2_prev, mlp, g_final_ref)

    x1 = x + g1 * _bdot(mix_ref[...], w_out_ref[...])
    x1_ref[...] = x1
    h2_ref[...] = _modulate(x1, sh2, sc2)

    @pl.when(live & (l == n_tiles - 1))
    def _():
        for hd in range(H_REC):
            rec_ref[hd] = st_ref[hd]
        conv_ref[...] = tail


def _prompt_layer(x, mod_p, lower_bounds, w_in, w_conv, g_onorm, w_out, w_up, w_down, g_final):
    bsz, seq, d = x.shape
    d_rec = lower_bounds.shape[1]
    dk = d_rec // H_REC
    d_conv = w_conv.shape[1]
    tl = PROMPT_TILE
    n_tiles = seq // tl
    n_steps = bsz * n_tiles
    assert seq % tl == 0 and tl % CHUNK == 0 and MLP_BLOCKS == 4
    mats = (w_in, w_out, w_up, w_down)
    hbm = pl.BlockSpec(memory_space=pl.ANY)

    def mix_tile(s):
        t = jnp.minimum(s, n_steps - 1)
        return lax.div(t, n_tiles), lax.rem(t, n_tiles)

    def mlp_tile(s):
        t = jnp.maximum(s - 1, 0)
        return lax.div(t, n_tiles), lax.rem(t, n_tiles)

    return pl.pallas_call(
        functools.partial(_prompt_kernel, n_tiles=n_tiles, n_steps=n_steps),
        grid=(n_steps + 1,),
        in_specs=[
            pl.BlockSpec((None, tl, d), lambda s: (*mix_tile(s), 0)),
            _const_spec(mod_p.shape),
            _const_spec(lower_bounds.shape),
            hbm,
            _const_spec(w_conv.shape),
            _const_spec(g_onorm.shape),
            hbm,
            hbm,
            hbm,
            _const_spec(g_final.shape),
        ],
        out_specs=[
            pl.BlockSpec((None, tl, d), lambda s: (*mlp_tile(s), 0)),
            pl.BlockSpec((None, H_REC, dk, dk), lambda s: (mix_tile(s)[0], 0, 0, 0)),
            pl.BlockSpec((None, CONV_W - 1, d_conv), lambda s: (mix_tile(s)[0], 0, 0)),
        ] + [hbm] * len(mats),
        out_shape=[
            jax.ShapeDtypeStruct((bsz, seq, d), F32),
            jax.ShapeDtypeStruct((bsz, H_REC, dk, dk), F32),
            jax.ShapeDtypeStruct((bsz, CONV_W - 1, d_conv), F32),
        ] + [jax.ShapeDtypeStruct(w.shape, BF16) for w in mats],
        scratch_shapes=[
            pltpu.VMEM((H_REC, dk, dk), F32),
            pltpu.VMEM((SUBLANES + tl, d_conv), F32),
            pltpu.VMEM((tl, d_rec + d_conv), BF16),
            pltpu.VMEM((tl, d), F32),
            pltpu.VMEM((tl, d), BF16),
        ] + [pltpu.VMEM(w.shape, BF16) for w in mats] + [
            pltpu.VMEM((WEIGHT_STAGE_SLOTS, *WEIGHT_STAGE_SHAPE), F32),
            pltpu.SemaphoreType.DMA((WEIGHT_STAGE_SLOTS,)),
            pltpu.SemaphoreType.DMA((len(mats),)),
        ],
        compiler_params=pltpu.CompilerParams(
            dimension_semantics=("arbitrary",), vmem_limit_bytes=VMEM_LIMIT_BYTES),
        name="prompt_layer",
    )(x, mod_p, lower_bounds, w_in, w_conv, g_onorm, w_out, w_up, w_down, g_final)


def _sample_kernel(x_ref, mod_ref, lb_ref, w_in_ref, w_conv_ref, g_onorm_ref, w_out_ref,
                   w_up_ref, w_down_ref, g_final_ref, rec_in_ref, conv_in_ref,
                   y_ref, rec_ref, conv_ref,
                   modx_ref, cbx_ref, ubuf_ref, mix_ref, *, seq):
    rows, d = x_ref.shape
    tb = rows // seq
    d_rec = lb_ref.shape[1]
    dk = d_rec // H_REC
    d_conv = conv_in_ref.shape[2]

    for b in range(tb):
        r = slice(b * seq, (b + 1) * seq)
        modx_ref[r, :] = jnp.broadcast_to(mod_ref[b:b + 1, :], (seq, mod_ref.shape[1]))
        for j in range(CONV_W - 1):
            cbx_ref[j, r, :] = jnp.broadcast_to(conv_in_ref[b, j:j + 1, :], (seq, d_conv))

    sh1, sc1, g1, sh2, sc2, g2 = (modx_ref[:, i * d:(i + 1) * d] for i in range(N_MOD))

    x = x_ref[...]
    proj = _bdot(_modulate(x, sh1, sc1), w_in_ref[...])
    q, fz, iv, g, gb, gc, hv = _split_proj(proj)

    lb = _layer_lower_bound(lb_ref, 0)
    row = lax.broadcasted_iota(jnp.int32, (rows, rows), 0)
    col = lax.broadcasted_iota(jnp.int32, (rows, rows), 1)
    causal = ((row // seq) == (col // seq)) & (row >= col)
    logf, k = _gates(fz, lb)
    cum = _group_cumsum(logf, seq)
    last = _group_last(cum, seq)
    q_dec = (q * jnp.exp(cum)).astype(BF16)
    k_dec = (k * jnp.exp(-cum)).astype(BF16)
    k_end = k * jnp.exp(last - cum)
    decay = jnp.exp(last)
    vb = iv.astype(BF16)

    dec3 = _decay_terms(decay, _group_pos((rows, d_rec), seq))
    zeros_blk = jnp.zeros((seq, dk), F32)
    rhs_bottom = _decay_selector(seq, dk)

    heads = [slice(hd * dk, (hd + 1) * dk) for hd in range(H_REC)]

    def update_states(elements):
        for b in elements:
            r = slice(b * seq, (b + 1) * seq)
            for hd, sl in enumerate(heads):
                lhs = jnp.concatenate([k_end[r, sl], dec3[r, sl]], axis=0)
                rhs = jnp.concatenate(
                    [jnp.concatenate([iv[r, sl], zeros_blk], axis=1), rhs_bottom], axis=0)
                both = _dot_tn(lhs.astype(BF16), rhs.astype(BF16))
                rec_ref[b, hd] = both[:, dk:] * rec_in_ref[b, hd] + both[:, :dk]

    scores = [jnp.where(causal, _dot_nt(q_dec[:, sl], k_dec[:, sl]), 0.0).astype(BF16)
              for sl in heads]
    o_inter = jnp.concatenate(
        [jnp.concatenate([_bdot(q_dec[b * seq:(b + 1) * seq, sl], rec_in_ref[b, hd].astype(BF16))
                          for hd, sl in enumerate(heads)], axis=-1)
         for b in range(tb)], axis=0)
    o_intra = jnp.concatenate([_bdot(scores[hd], vb[:, sl]) for hd, sl in enumerate(heads)],
                              axis=-1)
    o_rec = _head_out(o_intra + o_inter, g, g_onorm_ref[...])
    mix_ref[:, 0:d_rec] = o_rec.astype(BF16)

    u = gc * hv
    ubuf_ref[SUBLANES:SUBLANES + rows, :] = u
    tok = _group_pos((rows, d_conv), seq)
    u_m1 = jnp.where(tok >= 1, ubuf_ref[SUBLANES - 1:SUBLANES - 1 + rows, :], cbx_ref[1])
    u_m2 = jnp.where(tok >= 2, ubuf_ref[SUBLANES - 2:SUBLANES - 2 + rows, :],
                     jnp.where(tok == 1, cbx_ref[1], cbx_ref[0]))
    w_conv = w_conv_ref[...]
    y_conv = w_conv[0:1] * u_m2 + w_conv[1:2] * u_m1 + w_conv[2:3] * u
    mix_ref[:, d_rec:d_rec + d_conv] = (gb * y_conv).astype(BF16)
    for b in range(tb):
        conv_ref[b] = u[(b + 1) * seq - (CONV_W - 1):(b + 1) * seq]

    x1 = x + g1 * _bdot(mix_ref[...], w_out_ref[...])
    update_states(range(tb // 2))
    h2 = _modulate(x1, sh2, sc2)
    mlp = _mlp_down(_mlp_up(h2, w_up_ref, 0), w_up_ref, w_down_ref, 0)
    for j in range(1, MLP_BLOCKS):
        mlp += _mlp_down(_mlp_up(h2, w_up_ref, j), w_up_ref, w_down_ref, j)
    update_states(range(tb // 2, tb))
    y_ref[...] = _final_norm(x1, g2, mlp, g_final_ref)


def _sample_layer(x2d, seq, mod, lower_bounds, w_in, w_conv, g_onorm, w_out, w_up, w_down,
                  g_final, rec_in, conv_in):
    n_rows, d = x2d.shape
    bsz = n_rows // seq
    d_rec = lower_bounds.shape[1]
    dk = d_rec // H_REC
    d_conv = w_conv.shape[1]
    tb = SAMPLE_TILE_B
    rows = tb * seq
    assert bsz % tb == 0 and seq == SUBLANES and seq >= CONV_W
    return pl.pallas_call(
        functools.partial(_sample_kernel, seq=seq),
        grid=(bsz // tb,),
        in_specs=[
            pl.BlockSpec((rows, d), lambda i: (i, 0)),
            pl.BlockSpec((tb, N_MOD * d), lambda i: (i, 0)),
            _const_spec(lower_bounds.shape),
            _const_spec(w_in.shape),
            _const_spec(w_conv.shape),
            _const_spec(g_onorm.shape),
            _const_spec(w_out.shape),
            _const_spec(w_up.shape),
            _const_spec(w_down.shape),
            _const_spec(g_final.shape),
            pl.BlockSpec((tb, H_REC, dk, dk), lambda i: (i, 0, 0, 0)),
            pl.BlockSpec((tb, CONV_W - 1, d_conv), lambda i: (i, 0, 0)),
        ],
        out_specs=[
            pl.BlockSpec((rows, d), lambda i: (i, 0)),
            pl.BlockSpec((tb, H_REC, dk, dk), lambda i: (i, 0, 0, 0)),
            pl.BlockSpec((tb, CONV_W - 1, d_conv), lambda i: (i, 0, 0)),
        ],
        out_shape=[
            jax.ShapeDtypeStruct((n_rows, d), F32),
            jax.ShapeDtypeStruct((bsz, H_REC, dk, dk), F32),
            jax.ShapeDtypeStruct((bsz, CONV_W - 1, d_conv), F32),
        ],
        scratch_shapes=[
            pltpu.VMEM((rows, N_MOD * d), F32),
            pltpu.VMEM((CONV_W - 1, rows, d_conv), F32),
            pltpu.VMEM((SUBLANES + rows, d_conv), F32),
            pltpu.VMEM((rows, d_rec + d_conv), BF16),
        ],
        compiler_params=pltpu.CompilerParams(
            dimension_semantics=("arbitrary",), vmem_limit_bytes=VMEM_LIMIT_BYTES),
        name="sample_layer",
    )(x2d, mod, lower_bounds, w_in, w_conv, g_onorm, w_out, w_up, w_down, g_final, rec_in, conv_in)


def kernel(x_prompt, x_sample, state_rec, state_conv, c_prompt, c_sample, lower_bounds, w_ada,
           b_ada, w_in, w_conv, g_onorm, w_out, w_up, w_down, g_final):
    depth = w_in.shape[0]
    assert depth == 1, "single-layer trunk"
    bsz_s, seq_s, d = x_sample.shape

    mod_s, mod_p = _modulation(c_sample, c_prompt, w_ada[0], b_ada)

    g_final = g_final.reshape(1, d)
    y_p, rec_p, conv_p, w_in_b, w_out_b, w_up_b, w_down_b = _prompt_layer(
        x_prompt, mod_p, lower_bounds, w_in[0], w_conv[0], g_onorm, w_out[0], w_up[0], w_down[0],
        g_final)
    y_s, rec_s, conv_s = _sample_layer(
        x_sample.reshape(bsz_s * seq_s, d), seq_s, mod_s, lower_bounds, w_in_b, w_conv[0], g_onorm,
        w_out_b, w_up_b, w_down_b, g_final, state_rec[0], state_conv[0])
    return (y_p, y_s.reshape(bsz_s, seq_s, d), rec_p[None], conv_p[None], rec_s[None],
            conv_s[None])
```

```python
import functools

import jax
import jax.numpy as jnp
from jax import lax
from jax.experimental import pallas as pl
from jax.experimental.pallas import tpu as pltpu

F32 = jnp.float32
BF16 = jnp.bfloat16

EPS = 1e-6
H_REC = 4
CONV_W = 3
N_MOD = 6
N_PROJ_GROUPS = 7
CHUNK = 64

SUBLANES = 8
VMEM_LIMIT_BYTES = 56 * 1024 * 1024

PROMPT_TILE = 512
SAMPLE_TILE_B = 16
MOD_TILE_K = 256
MLP_BLOCKS = 4
WEIGHT_STAGE_SHAPE = (512, 512)
WEIGHT_STAGE_SLOTS = 4


def _const_spec(shape):
    zeros = (0,) * len(shape)
    return pl.BlockSpec(shape, lambda *_: zeros, pipeline_mode=pl.Buffered(1))


def _rms(x):
    return x * lax.rsqrt(jnp.mean(x * x, axis=-1, keepdims=True) + EPS)


def _silu(x):
    return x * jax.nn.sigmoid(x)


def _bdot(a, b):
    return jnp.dot(a, b, preferred_element_type=F32)


def _dot_nt(a, b):
    return lax.dot_general(a, b, (((1,), (1,)), ((), ())), preferred_element_type=F32)


def _dot_tn(a, b):
    return lax.dot_general(a, b, (((0,), (0,)), ((), ())), preferred_element_type=F32)


def _group_pos(shape, group):
    return lax.broadcasted_iota(jnp.int32, shape, 0) % group


def _group_cumsum(x, group):
    pos = _group_pos(x.shape, group)
    step = 1
    while step < group:
        x = x + jnp.where(pos >= step, pltpu.roll(x, step, axis=0), 0.0)
        step *= 2
    return x


def _group_last(x, group):
    rows = x.shape[0]
    pos = _group_pos(x.shape, group)
    x = jnp.where(pos == group - 1, x, 0.0)
    step = 1
    while step < group:
        x = x + jnp.where(pos + step < group, pltpu.roll(x, rows - step, axis=0), 0.0)
        step *= 2
    return x


def _layer_lower_bound(lb_ref, layer):
    lb = lb_ref[...]
    e = jnp.exp(lb - jnp.max(lb, axis=0, keepdims=True))
    sm = e / jnp.sum(e, axis=0, keepdims=True)
    return jnp.sum(sm[: layer + 1], axis=0, keepdims=True)


def _gates(fz, lb):
    sig = jax.nn.sigmoid(fz)
    logf = jnp.log(lb + (1.0 - lb) * sig)
    k = (1.0 - lb) * (1.0 - sig)
    return logf, k


def _decay_terms(decay, row):
    hi = decay.astype(BF16).astype(F32)
    rest = decay - hi
    mid = rest.astype(BF16).astype(F32)
    lo = rest - mid
    return jnp.where(row == 0, hi, jnp.where(row == 1, mid, jnp.where(row == 2, lo, 0.0)))


def _decay_selector(n_rows, width):
    sub = lax.broadcasted_iota(jnp.int32, (n_rows, width), 0)
    return jnp.concatenate(
        [jnp.zeros((n_rows, width), F32), jnp.where(sub < 3, 1.0, 0.0)], axis=1)


def _head_out(o, g, g_onorm):
    dv = o.shape[-1] // H_REC
    pieces = []
    for h in range(H_REC):
        sl = slice(h * dv, (h + 1) * dv)
        pieces.append(_rms(o[:, sl]))
    return (jnp.concatenate(pieces, axis=-1) * g_onorm) * _silu(g)


def _modulate(x, shift, scale):
    return (_rms(x) * (1.0 + scale) + shift).astype(BF16)


def _mlp_cols(w_up_ref, j):
    blk = w_up_ref.shape[1] // MLP_BLOCKS
    return slice(j * blk, (j + 1) * blk)


def _mlp_up(h2, w_up_ref, j):
    up = jnp.maximum(_bdot(h2, w_up_ref[:, _mlp_cols(w_up_ref, j)]), 0.0)
    return (up * up).astype(BF16)


def _mlp_down(up, w_up_ref, w_down_ref, j):
    return _bdot(up, w_down_ref[_mlp_cols(w_up_ref, j), :])


def _final_norm(x, g2, mlp, g_final_ref):
    return _rms(x + g2 * mlp) * g_final_ref[...]


def _mod_kernel(c_s_ref, c_p_ref, w_ref, b_ref, o_s_ref, o_p_ref):
    k = pl.program_id(0)
    n_s = c_s_ref.shape[0]
    kb = w_ref.shape[0]
    cols = pl.ds(pl.multiple_of(k * kb, kb), kb)
    c = jnp.concatenate([c_s_ref[:, cols], c_p_ref[:, cols]], axis=0)
    part = _bdot(_silu(c).astype(BF16), w_ref[...].astype(BF16))

    @pl.when(k == 0)
    def _():
        o_s_ref[...] = part[:n_s] + b_ref[...]
        o_p_ref[...] = part[n_s:] + b_ref[...]

    @pl.when(k > 0)
    def _():
        o_s_ref[...] += part[:n_s]
        o_p_ref[...] += part[n_s:]


def _modulation(c_s, c_p, w_ada, b_ada):
    (n_s, d), n_p = c_s.shape, c_p.shape[0]
    n = w_ada.shape[1]
    assert n_s % SUBLANES == 0 and n_p % SUBLANES == 0 and d % MOD_TILE_K == 0
    return pl.pallas_call(
        _mod_kernel,
        grid=(d // MOD_TILE_K,),
        in_specs=[
            pl.BlockSpec((n_s, d), lambda k: (0, 0)),
            pl.BlockSpec((n_p, d), lambda k: (0, 0)),
            pl.BlockSpec((MOD_TILE_K, n), lambda k: (k, 0)),
            pl.BlockSpec((1, n), lambda k: (0, 0)),
        ],
        out_specs=[
            pl.BlockSpec((n_s, n), lambda k: (0, 0)),
            pl.BlockSpec((n_p, n), lambda k: (0, 0)),
        ],
        out_shape=[
            jax.ShapeDtypeStruct((n_s, n), F32),
            jax.ShapeDtypeStruct((n_p, n), F32),
        ],
        compiler_params=pltpu.CompilerParams(
            dimension_semantics=("arbitrary",), vmem_limit_bytes=VMEM_LIMIT_BYTES),
        name="adaln_modulation",
    )(c_s, c_p, w_ada, b_ada)


def _split_proj(proj):
    width = proj.shape[1] // N_PROJ_GROUPS
    return tuple(proj[:, p * width:(p + 1) * width] for p in range(N_PROJ_GROUPS))


def _weight_blocks(w_hbm, w_vmem):
    rows, cols = w_hbm.shape
    br, bc = WEIGHT_STAGE_SHAPE
    assert rows % br == 0 and cols % bc == 0
    return [(w_hbm.at[r:r + br, c:c + bc], w_vmem.at[r:r + br, c:c + bc])
            for r in range(0, rows, br) for c in range(0, cols, bc)]


def _stage_copy(src, stage_ref, sem_ref, i):
    slot = i % WEIGHT_STAGE_SLOTS
    return pltpu.make_async_copy(src, stage_ref.at[slot], sem_ref.at[slot])


def _load_weights_as_bf16(w_hbm_refs, w_vmem_refs, stage_ref, sem_ref):
    blocks = [blk for w_hbm, w_vmem in zip(w_hbm_refs, w_vmem_refs)
              for blk in _weight_blocks(w_hbm, w_vmem)]
    ahead = WEIGHT_STAGE_SLOTS - 1
    for i in range(min(ahead, len(blocks))):
        _stage_copy(blocks[i][0], stage_ref, sem_ref, i).start()
    for i, (src, dst) in enumerate(blocks):
        if i + ahead < len(blocks):
            _stage_copy(blocks[i + ahead][0], stage_ref, sem_ref, i + ahead).start()
        _stage_copy(src, stage_ref, sem_ref, i).wait()
        dst[...] = stage_ref[i % WEIGHT_STAGE_SLOTS].astype(BF16)


def _alternate(*stages):
    stages = list(stages)
    while stages:
        for stage in list(stages):
            try:
                next(stage)
            except StopIteration:
                stages.remove(stage)


def _mlp_stage(x1_ref, h2_ref, g2, w_up_ref, w_down_ref, g_final_ref, y_ref):
    h2 = h2_ref[...]
    ups = [_mlp_up(h2, w_up_ref, 0)]
    yield
    mlp = None
    for j in range(MLP_BLOCKS):
        if j + 1 < MLP_BLOCKS:
            ups.append(_mlp_up(h2, w_up_ref, j + 1))
        part = _mlp_down(ups[j], w_up_ref, w_down_ref, j)
        mlp = part if mlp is None else mlp + part
        if j + 1 < MLP_BLOCKS:
            yield
    y_ref[...] = _final_norm(x1_ref[...], g2, mlp, g_final_ref)
    yield


def _prompt_mix_stage(x_ref, mod, lb_ref, w_in_ref, w_conv_ref, g_onorm_ref, w_out_ref,
                      st_ref, ubuf_ref, mix_ref, x1_ref, h2_ref, rec_ref, conv_ref, is_last):
    tl, d = x_ref.shape
    d_rec = lb_ref.shape[1]
    dk = d_rec // H_REC
    n_chunks = tl // CHUNK
    heads = [slice(hd * dk, (hd + 1) * dk) for hd in range(H_REC)]
    sh1, sc1, g1, sh2, sc2 = (mod[:, i * d:(i + 1) * d] for i in range(N_MOD - 1))

    x = x_ref[...]
    proj = _bdot(_modulate(x, sh1, sc1), w_in_ref[...])
    q, fz, iv, g, gb, gc, hv = _split_proj(proj)
    d_conv = gb.shape[1]
    yield

    lb = _layer_lower_bound(lb_ref, 0)
    row = lax.broadcasted_iota(jnp.int32, (CHUNK, CHUNK), 0)
    col = lax.broadcasted_iota(jnp.int32, (CHUNK, CHUNK), 1)
    causal = row >= col
    logf, k_all = _gates(fz, lb)
    cum_all = _group_cumsum(logf, CHUNK)
    q_dec, decay, scores, upd = [], [], [], []
    for c in range(n_chunks):
        rows = slice(c * CHUNK, (c + 1) * CHUNK)
        cum = cum_all[rows]
        last = cum[CHUNK - 1:CHUNK, :]
        q_dec.append((q[rows] * jnp.exp(cum)).astype(BF16))
        k_dec = (k_all[rows] * jnp.exp(-cum)).astype(BF16)
        k_end = (k_all[rows] * jnp.exp(last - cum)).astype(BF16)
        decay.append(jnp.exp(last))
        v = iv[rows].astype(BF16)
        scores.append([jnp.where(causal, _dot_nt(q_dec[c][:, sl], k_dec[:, sl]), 0.0).astype(BF16)
                       for sl in heads])
        upd.append([_dot_tn(v[:, sl], k_end[:, sl]) for sl in heads])
    yield

    st = [st_ref[hd] for hd in range(H_REC)]
    o_chunks = []
    for c in range(n_chunks):
        v = iv[c * CHUNK:(c + 1) * CHUNK].astype(BF16)
        o_heads = []
        for hd, sl in enumerate(heads):
            o_heads.append(_bdot(scores[c][hd], v[:, sl])
                           + _dot_nt(q_dec[c][:, sl], st[hd].astype(BF16)))
            st[hd] = st[hd] * decay[c][:, sl] + upd[c][hd]
        o_chunks.append(jnp.concatenate(o_heads, axis=-1))
    for hd in range(H_REC):
        st_ref[hd] = st[hd]
    yield

    o_rec = _head_out(jnp.concatenate(o_chunks, axis=0), g, g_onorm_ref[...])
    mix_ref[:, 0:d_rec] = o_rec.astype(BF16)

    u = gc * hv
    ubuf_ref[SUBLANES:SUBLANES + tl, :] = u
    w_conv = w_conv_ref[...]
    y_conv = (w_conv[0:1] * ubuf_ref[SUBLANES - 2:SUBLANES - 2 + tl, :]
              + w_conv[1:2] * ubuf_ref[SUBLANES - 1:SUBLANES - 1 + tl, :]
              + w_conv[2:3] * u)
    mix_ref[:, d_rec:d_rec + d_conv] = (gb * y_conv).astype(BF16)
    tail = u[tl - (CONV_W - 1):tl]
    ubuf_ref[SUBLANES - (CONV_W - 1):SUBLANES, :] = tail
    yield

    x1 = x + g1 * _bdot(mix_ref[...], w_out_ref[...])
    x1_ref[...] = x1
    h2_ref[...] = _modulate(x1, sh2, sc2)

    @pl.when(is_last)
    def _():
        for hd in range(H_REC):
            rec_ref[hd] = st_ref[hd].T
        conv_ref[...] = tail


def _prompt_kernel(x_ref, mod_ref, lb_ref, w_in_hbm, w_conv_ref, g_onorm_ref, w_out_hbm,
                   w_up_hbm, w_down_hbm, g_final_ref,
                   y_ref, rec_ref, conv_ref, w_in_out, w_out_out, w_up_out, w_down_out,
                   st_ref, ubuf_ref, mix_ref, x1_ref, h2_ref,
                   w_in_ref, w_out_ref, w_up_ref, w_down_ref, stage_ref, stage_sem, out_sem,
                   *, n_tiles, n_steps):
    s = pl.program_id(0)
    tile = jnp.minimum(s, n_steps - 1)
    l = lax.rem(tile, n_tiles)
    d = x_ref.shape[1]

    w_vmem = (w_in_ref, w_out_ref, w_up_ref, w_down_ref)
    w_outs = (w_in_out, w_out_out, w_up_out, w_down_out)

    def bf16_export(i):
        return pltpu.make_async_copy(w_vmem[i], w_outs[i], out_sem.at[i])

    @pl.when(s == 0)
    def _():
        _load_weights_as_bf16((w_in_hbm, w_out_hbm, w_up_hbm, w_down_hbm), w_vmem,
                              stage_ref, stage_sem)
        for i in range(len(w_vmem)):
            bf16_export(i).start()

    @pl.when(s == n_steps)
    def _():
        for i in range(len(w_vmem)):
            bf16_export(i).wait()

    @pl.when(l == 0)
    def _():
        st_ref[...] = jnp.zeros_like(st_ref)
        ubuf_ref[0:SUBLANES, :] = jnp.zeros((SUBLANES, ubuf_ref.shape[1]), F32)

    def mix_stage():
        mod = mod_ref[pl.ds(lax.div(tile, n_tiles), 1), :]
        return _prompt_mix_stage(x_ref, mod, lb_ref, w_in_ref, w_conv_ref, g_onorm_ref,
                                 w_out_ref, st_ref, ubuf_ref, mix_ref, x1_ref, h2_ref,
                                 rec_ref, conv_ref, l == n_tiles - 1)

    def mlp_stage():
        mod_prev = mod_ref[pl.ds(lax.div(s - 1, n_tiles), 1), :]
        return _mlp_stage(x1_ref, h2_ref, mod_prev[:, (N_MOD - 1) * d:], w_up_ref, w_down_ref,
                          g_final_ref, y_ref)

    @pl.when(s == 0)
    def _():
        _alternate(mix_stage())

    @pl.when((s > 0) & (s < n_steps))
    def _():
        _alternate(mlp_stage(), mix_stage())

    @pl.when(s == n_steps)
    def _():
        _alternate(mlp_stage())


def _prompt_layer(x, mod_p, lower_bounds, w_in, w_conv, g_onorm, w_out, w_up, w_down, g_final):
    bsz, seq, d = x.shape
    d_rec = lower_bounds.shape[1]
    dk = d_rec // H_REC
    d_conv = w_conv.shape[1]
    tl = PROMPT_TILE
    n_tiles = seq // tl
    n_steps = bsz * n_tiles
    assert seq % tl == 0 and tl % CHUNK == 0
    mats = (w_in, w_out, w_up, w_down)
    hbm = pl.BlockSpec(memory_space=pl.ANY)

    def mix_tile(s):
        t = jnp.minimum(s, n_steps - 1)
        return lax.div(t, n_tiles), lax.rem(t, n_tiles)

    def mlp_tile(s):
        t = jnp.maximum(s - 1, 0)
        return lax.div(t, n_tiles), lax.rem(t, n_tiles)

    return pl.pallas_call(
        functools.partial(_prompt_kernel, n_tiles=n_tiles, n_steps=n_steps),
        grid=(n_steps + 1,),
        in_specs=[
            pl.BlockSpec((None, tl, d), lambda s: (*mix_tile(s), 0)),
            _const_spec(mod_p.shape),
            _const_spec(lower_bounds.shape),
            hbm,
            _const_spec(w_conv.shape),
            _const_spec(g_onorm.shape),
            hbm,
            hbm,
            hbm,
            _const_spec(g_final.shape),
        ],
        out_specs=[
            pl.BlockSpec((None, tl, d), lambda s: (*mlp_tile(s), 0)),
            pl.BlockSpec((None, H_REC, dk, dk), lambda s: (mix_tile(s)[0], 0, 0, 0)),
            pl.BlockSpec((None, CONV_W - 1, d_conv), lambda s: (mix_tile(s)[0], 0, 0)),
        ] + [hbm] * len(mats),
        out_shape=[
            jax.ShapeDtypeStruct((bsz, seq, d), F32),
            jax.ShapeDtypeStruct((bsz, H_REC, dk, dk), F32),
            jax.ShapeDtypeStruct((bsz, CONV_W - 1, d_conv), F32),
        ] + [jax.ShapeDtypeStruct(w.shape, BF16) for w in mats],
        scratch_shapes=[
            pltpu.VMEM((H_REC, dk, dk), F32),
            pltpu.VMEM((SUBLANES + tl, d_conv), F32),
            pltpu.VMEM((tl, d_rec + d_conv), BF16),
            pltpu.VMEM((tl, d), F32),
            pltpu.VMEM((tl, d), BF16),
        ] + [pltpu.VMEM(w.shape, BF16) for w in mats] + [
            pltpu.VMEM((WEIGHT_STAGE_SLOTS, *WEIGHT_STAGE_SHAPE), F32),
            pltpu.SemaphoreType.DMA((WEIGHT_STAGE_SLOTS,)),
            pltpu.SemaphoreType.DMA((len(mats),)),
        ],
        compiler_params=pltpu.CompilerParams(
            dimension_semantics=("arbitrary",), vmem_limit_bytes=VMEM_LIMIT_BYTES),
        name="prompt_layer",
    )(x, mod_p, lower_bounds, w_in, w_conv, g_onorm, w_out, w_up, w_down, g_final)


def _sample_kernel(x_ref, mod_ref, lb_ref, w_in_ref, w_conv_ref, g_onorm_ref, w_out_ref,
                   w_up_ref, w_down_ref, g_final_ref, rec_in_ref, conv_in_ref,
                   y_ref, rec_ref, conv_ref,
                   modx_ref, cbx_ref, ubuf_ref, mix_ref, *, seq):
    rows, d = x_ref.shape
    tb = rows // seq
    d_rec = lb_ref.shape[1]
    dk = d_rec // H_REC
    d_conv = conv_in_ref.shape[2]

    for b in range(tb):
        r = slice(b * seq, (b + 1) * seq)
        modx_ref[r, :] = jnp.broadcast_to(mod_ref[b:b + 1, :], (seq, mod_ref.shape[1]))
        for j in range(CONV_W - 1):
            cbx_ref[j, r, :] = jnp.broadcast_to(conv_in_ref[b, j:j + 1, :], (seq, d_conv))

    sh1, sc1, g1, sh2, sc2, g2 = (modx_ref[:, i * d:(i + 1) * d] for i in range(N_MOD))

    x = x_ref[...]
    proj = _bdot(_modulate(x, sh1, sc1), w_in_ref[...])
    q, fz, iv, g, gb, gc, hv = _split_proj(proj)

    lb = _layer_lower_bound(lb_ref, 0)
    row = lax.broadcasted_iota(jnp.int32, (rows, rows), 0)
    col = lax.broadcasted_iota(jnp.int32, (rows, rows), 1)
    causal = ((row // seq) == (col // seq)) & (row >= col)
    logf, k = _gates(fz, lb)
    cum = _group_cumsum(logf, seq)
    last = _group_last(cum, seq)
    q_dec = (q * jnp.exp(cum)).astype(BF16)
    k_dec = (k * jnp.exp(-cum)).astype(BF16)
    k_end = k * jnp.exp(last - cum)
    decay = jnp.exp(last)
    vb = iv.astype(BF16)

    dec3 = _decay_terms(decay, _group_pos((rows, d_rec), seq))
    zeros_blk = jnp.zeros((seq, dk), F32)
    rhs_bottom = _decay_selector(seq, dk)

    heads = [slice(hd * dk, (hd + 1) * dk) for hd in range(H_REC)]

    def update_states(elements):
        for b in elements:
            r = slice(b * seq, (b + 1) * seq)
            for hd, sl in enumerate(heads):
                lhs = jnp.concatenate([k_end[r, sl], dec3[r, sl]], axis=0)
                rhs = jnp.concatenate(
                    [jnp.concatenate([iv[r, sl], zeros_blk], axis=1), rhs_bottom], axis=0)
                both = _dot_tn(lhs.astype(BF16), rhs.astype(BF16))
                rec_ref[b, hd] = both[:, dk:] * rec_in_ref[b, hd] + both[:, :dk]

    scores = [jnp.where(causal, _dot_nt(q_dec[:, sl], k_dec[:, sl]), 0.0).astype(BF16)
              for sl in heads]
    o_inter = jnp.concatenate(
        [jnp.concatenate([_bdot(q_dec[b * seq:(b + 1) * seq, sl], rec_in_ref[b, hd].astype(BF16))
                          for hd, sl in enumerate(heads)], axis=-1)
         for b in range(tb)], axis=0)
    o_intra = jnp.concatenate([_bdot(scores[hd], vb[:, sl]) for hd, sl in enumerate(heads)],
                              axis=-1)
    o_rec = _head_out(o_intra + o_inter, g, g_onorm_ref[...])
    mix_ref[:, 0:d_rec] = o_rec.astype(BF16)

    u = gc * hv
    ubuf_ref[SUBLANES:SUBLANES + rows, :] = u
    tok = _group_pos((rows, d_conv), seq)
    u_m1 = jnp.where(tok >= 1, ubuf_ref[SUBLANES - 1:SUBLANES - 1 + rows, :], cbx_ref[1])
    u_m2 = jnp.where(tok >= 2, ubuf_ref[SUBLANES - 2:SUBLANES - 2 + rows, :],
                     jnp.where(tok == 1, cbx_ref[1], cbx_ref[0]))
    w_conv = w_conv_ref[...]
    y_conv = w_conv[0:1] * u_m2 + w_conv[1:2] * u_m1 + w_conv[2:3] * u
    mix_ref[:, d_rec:d_rec + d_conv] = (gb * y_conv).astype(BF16)
    for b in range(tb):
        conv_ref[b] = u[(b + 1) * seq - (CONV_W - 1):(b + 1) * seq]

    x1 = x + g1 * _bdot(mix_ref[...], w_out_ref[...])
    update_states(range(tb // 2))
    h2 = _modulate(x1, sh2, sc2)
    mlp = _mlp_down(_mlp_up(h2, w_up_ref, 0), w_up_ref, w_down_ref, 0)
    for j in range(1, MLP_BLOCKS):
        mlp += _mlp_down(_mlp_up(h2, w_up_ref, j), w_up_ref, w_down_ref, j)
    update_states(range(tb // 2, tb))
    y_ref[...] = _final_norm(x1, g2, mlp, g_final_ref)


def _sample_layer(x2d, seq, mod, lower_bounds, w_in, w_conv, g_onorm, w_out, w_up, w_down,
                  g_final, rec_in, conv_in):
    n_rows, d = x2d.shape
    bsz = n_rows // seq
    d_rec = lower_bounds.shape[1]
    dk = d_rec // H_REC
    d_conv = w_conv.shape[1]
    tb = SAMPLE_TILE_B
    rows = tb * seq
    assert bsz % tb == 0 and seq == SUBLANES and seq >= CONV_W
    return pl.pallas_call(
        functools.partial(_sample_kernel, seq=seq),
        grid=(bsz // tb,),
        in_specs=[
            pl.BlockSpec((rows, d), lambda i: (i, 0)),
            pl.BlockSpec((tb, N_MOD * d), lambda i: (i, 0)),
            _const_spec(lower_bounds.shape),
            _const_spec(w_in.shape),
            _const_spec(w_conv.shape),
            _const_spec(g_onorm.shape),
            _const_spec(w_out.shape),
            _const_spec(w_up.shape),
            _const_spec(w_down.shape),
            _const_spec(g_final.shape),
            pl.BlockSpec((tb, H_REC, dk, dk), lambda i: (i, 0, 0, 0)),
            pl.BlockSpec((tb, CONV_W - 1, d_conv), lambda i: (i, 0, 0)),
        ],
        out_specs=[
            pl.BlockSpec((rows, d), lambda i: (i, 0)),
            pl.BlockSpec((tb, H_REC, dk, dk), lambda i: (i, 0, 0, 0)),
            pl.BlockSpec((tb, CONV_W - 1, d_conv), lambda i: (i, 0, 0)),
        ],
        out_shape=[
            jax.ShapeDtypeStruct((n_rows, d), F32),
            jax.ShapeDtypeStruct((bsz, H_REC, dk, dk), F32),
            jax.ShapeDtypeStruct((bsz, CONV_W - 1, d_conv), F32),
        ],
        scratch_shapes=[
            pltpu.VMEM((rows, N_MOD * d), F32),
            pltpu.VMEM((CONV_W - 1, rows, d_conv), F32),
            pltpu.VMEM((SUBLANES + rows, d_conv), F32),
            pltpu.VMEM((rows, d_rec + d_conv), BF16),
        ],
        compiler_params=pltpu.CompilerParams(
            dimension_semantics=("arbitrary",), vmem_limit_bytes=VMEM_LIMIT_BYTES),
        name="sample_layer",
    )(x2d, mod, lower_bounds, w_in, w_conv, g_onorm, w_out, w_up, w_down, g_final, rec_in, conv_in)


def kernel(x_prompt, x_sample, state_rec, state_conv, c_prompt, c_sample, lower_bounds, w_ada,
           b_ada, w_in, w_conv, g_onorm, w_out, w_up, w_down, g_final):
    depth = w_in.shape[0]
    assert depth == 1, "single-layer trunk"
    bsz_s, seq_s, d = x_sample.shape

    mod_s, mod_p = _modulation(c_sample, c_prompt, w_ada[0], b_ada)

    g_final = g_final.reshape(1, d)
    y_p, rec_p, conv_p, w_in_b, w_out_b, w_up_b, w_down_b = _prompt_layer(
        x_prompt, mod_p, lower_bounds, w_in[0], w_conv[0], g_onorm, w_out[0], w_up[0], w_down[0],
        g_final)
    y_s, rec_s, conv_s = _sample_layer(
        x_sample.reshape(bsz_s * seq_s, d), seq_s, mod_s, lower_bounds, w_in_b, w_conv[0], g_onorm,
        w_out_b, w_up_b, w_down_b, g_final, state_rec[0], state_conv[0])
    return (y_p, y_s.reshape(bsz_s, seq_s, d), rec_p[None], conv_p[None], rec_s[None],
            conv_s[None])
```

```python
import functools

import jax
import jax.numpy as jnp
from jax import lax
from jax.experimental import pallas as pl
from jax.experimental.pallas import tpu as pltpu

F32 = jnp.float32
BF16 = jnp.bfloat16

EPS = 1e-6
H_REC = 4
CONV_W = 3
N_MOD = 6
N_PROJ_GROUPS = 7
CHUNK = 64

SUBLANES = 8
VMEM_LIMIT_BYTES = 56 * 1024 * 1024

PROMPT_TILE = 512
SAMPLE_TILE_B = 16
MOD_TILE_K = 256
MLP_BLOCKS = 4
WEIGHT_STAGE_SHAPE = (512, 512)
WEIGHT_STAGE_SLOTS = 4


def _const_spec(shape):
    zeros = (0,) * len(shape)
    return pl.BlockSpec(shape, lambda *_: zeros, pipeline_mode=pl.Buffered(1))


def _rms(x):
    return x * lax.rsqrt(jnp.mean(x * x, axis=-1, keepdims=True) + EPS)


def _silu(x):
    return x * jax.nn.sigmoid(x)


def _bdot(a, b):
    return jnp.dot(a, b, preferred_element_type=F32)


def _dot_nt(a, b):
    return lax.dot_general(a, b, (((1,), (1,)), ((), ())), preferred_element_type=F32)


def _dot_tn(a, b):
    return lax.dot_general(a, b, (((0,), (0,)), ((), ())), preferred_element_type=F32)


def _group_pos(shape, group):
    return lax.broadcasted_iota(jnp.int32, shape, 0) % group


def _group_cumsum(x, group):
    pos = _group_pos(x.shape, group)
    step = 1
    while step < group:
        x = x + jnp.where(pos >= step, pltpu.roll(x, step, axis=0), 0.0)
        step *= 2
    return x


def _group_last(x, group):
    rows = x.shape[0]
    pos = _group_pos(x.shape, group)
    x = jnp.where(pos == group - 1, x, 0.0)
    step = 1
    while step < group:
        x = x + jnp.where(pos + step < group, pltpu.roll(x, rows - step, axis=0), 0.0)
        step *= 2
    return x


def _layer_lower_bound(lb_ref, layer):
    lb = lb_ref[...]
    e = jnp.exp(lb - jnp.max(lb, axis=0, keepdims=True))
    sm = e / jnp.sum(e, axis=0, keepdims=True)
    return jnp.sum(sm[: layer + 1], axis=0, keepdims=True)


def _gates(fz, lb):
    sig = jax.nn.sigmoid(fz)
    logf = jnp.log(lb + (1.0 - lb) * sig)
    k = (1.0 - lb) * (1.0 - sig)
    return logf, k


def _decay_terms(decay, row):
    hi = decay.astype(BF16).astype(F32)
    rest = decay - hi
    mid = rest.astype(BF16).astype(F32)
    lo = rest - mid
    return jnp.where(row == 0, hi, jnp.where(row == 1, mid, jnp.where(row == 2, lo, 0.0)))


def _decay_selector(n_rows, width):
    sub = lax.broadcasted_iota(jnp.int32, (n_rows, width), 0)
    return jnp.concatenate(
        [jnp.zeros((n_rows, width), F32), jnp.where(sub < 3, 1.0, 0.0)], axis=1)


def _head_out(o, g, g_onorm):
    dv = o.shape[-1] // H_REC
    pieces = []
    for h in range(H_REC):
        sl = slice(h * dv, (h + 1) * dv)
        pieces.append(_rms(o[:, sl]))
    return (jnp.concatenate(pieces, axis=-1) * g_onorm) * _silu(g)


def _modulate(x, shift, scale):
    return (_rms(x) * (1.0 + scale) + shift).astype(BF16)


def _mlp_cols(w_up_ref, j):
    blk = w_up_ref.shape[1] // MLP_BLOCKS
    return slice(j * blk, (j + 1) * blk)


def _mlp_up(h2, w_up_ref, j):
    up = jnp.maximum(_bdot(h2, w_up_ref[:, _mlp_cols(w_up_ref, j)]), 0.0)
    return (up * up).astype(BF16)


def _mlp_down(up, w_up_ref, w_down_ref, j):
    return _bdot(up, w_down_ref[_mlp_cols(w_up_ref, j), :])


def _final_norm(x, g2, mlp, g_final_ref):
    return _rms(x + g2 * mlp) * g_final_ref[...]


def _mod_kernel(c_s_ref, c_p_ref, w_ref, b_ref, o_s_ref, o_p_ref):
    k = pl.program_id(0)
    n_s = c_s_ref.shape[0]
    kb = w_ref.shape[0]
    cols = pl.ds(pl.multiple_of(k * kb, kb), kb)
    c = jnp.concatenate([c_s_ref[:, cols], c_p_ref[:, cols]], axis=0)
    part = _bdot(_silu(c).astype(BF16), w_ref[...].astype(BF16))

    @pl.when(k == 0)
    def _():
        o_s_ref[...] = part[:n_s] + b_ref[...]
        o_p_ref[...] = part[n_s:] + b_ref[...]

    @pl.when(k > 0)
    def _():
        o_s_ref[...] += part[:n_s]
        o_p_ref[...] += part[n_s:]


def _modulation(c_s, c_p, w_ada, b_ada):
    (n_s, d), n_p = c_s.shape, c_p.shape[0]
    n = w_ada.shape[1]
    assert n_s % SUBLANES == 0 and n_p % SUBLANES == 0 and d % MOD_TILE_K == 0
    return pl.pallas_call(
        _mod_kernel,
        grid=(d // MOD_TILE_K,),
        in_specs=[
            pl.BlockSpec((n_s, d), lambda k: (0, 0)),
            pl.BlockSpec((n_p, d), lambda k: (0, 0)),
            pl.BlockSpec((MOD_TILE_K, n), lambda k: (k, 0)),
            pl.BlockSpec((1, n), lambda k: (0, 0)),
        ],
        out_specs=[
            pl.BlockSpec((n_s, n), lambda k: (0, 0)),
            pl.BlockSpec((n_p, n), lambda k: (0, 0)),
        ],
        out_shape=[
            jax.ShapeDtypeStruct((n_s, n), F32),
            jax.ShapeDtypeStruct((n_p, n), F32),
        ],
        compiler_params=pltpu.CompilerParams(
            dimension_semantics=("arbitrary",), vmem_limit_bytes=VMEM_LIMIT_BYTES),
        name="adaln_modulation",
    )(c_s, c_p, w_ada, b_ada)


def _split_proj(proj):
    width = proj.shape[1] // N_PROJ_GROUPS
    return tuple(proj[:, p * width:(p + 1) * width] for p in range(N_PROJ_GROUPS))


def _weight_blocks(w_hbm, w_vmem):
    rows, cols = w_hbm.shape
    br, bc = WEIGHT_STAGE_SHAPE
    assert rows % br == 0 and cols % bc == 0
    return [(w_hbm.at[r:r + br, c:c + bc], w_vmem.at[r:r + br, c:c + bc])
            for r in range(0, rows, br) for c in range(0, cols, bc)]


def _stage_copy(src, stage_ref, sem_ref, i):
    slot = i % WEIGHT_STAGE_SLOTS
    return pltpu.make_async_copy(src, stage_ref.at[slot], sem_ref.at[slot])


def _load_weights_as_bf16(w_hbm_refs, w_vmem_refs, stage_ref, sem_ref):
    blocks = [blk for w_hbm, w_vmem in zip(w_hbm_refs, w_vmem_refs)
              for blk in _weight_blocks(w_hbm, w_vmem)]
    ahead = WEIGHT_STAGE_SLOTS - 1
    for i in range(min(ahead, len(blocks))):
        _stage_copy(blocks[i][0], stage_ref, sem_ref, i).start()
    for i, (src, dst) in enumerate(blocks):
        if i + ahead < len(blocks):
            _stage_copy(blocks[i + ahead][0], stage_ref, sem_ref, i + ahead).start()
        _stage_copy(src, stage_ref, sem_ref, i).wait()
        dst[...] = stage_ref[i % WEIGHT_STAGE_SLOTS].astype(BF16)


def _alternate(*stages):
    stages = list(stages)
    while stages:
        for stage in list(stages):
            try:
                next(stage)
            except StopIteration:
                stages.remove(stage)


def _mlp_stage(x1_ref, h2_ref, g2, w_up_ref, w_down_ref, g_final_ref, y_ref):
    h2 = h2_ref[...]
    ups = [_mlp_up(h2, w_up_ref, 0)]
    yield
    mlp = None
    for j in range(MLP_BLOCKS):
        if j + 1 < MLP_BLOCKS:
            ups.append(_mlp_up(h2, w_up_ref, j + 1))
        part = _mlp_down(ups[j], w_up_ref, w_down_ref, j)
        mlp = part if mlp is None else mlp + part
        if j + 1 < MLP_BLOCKS:
            yield
    y_ref[...] = _final_norm(x1_ref[...], g2, mlp, g_final_ref)
    yield


def _prompt_mix_stage(x_ref, mod, lb_ref, w_in_ref, w_conv_ref, g_onorm_ref, w_out_ref,
                      st_ref, ubuf_ref, mix_ref, x1_ref, h2_ref, rec_ref, conv_ref, is_last):
    tl, d = x_ref.shape
    d_rec = lb_ref.shape[1]
    dk = d_rec // H_REC
    n_chunks = tl // CHUNK
    heads = [slice(hd * dk, (hd + 1) * dk) for hd in range(H_REC)]
    sh1, sc1, g1, sh2, sc2 = (mod[:, i * d:(i + 1) * d] for i in range(N_MOD - 1))

    x = x_ref[...]
    proj = _bdot(_modulate(x, sh1, sc1), w_in_ref[...])
    q, fz, iv, g, gb, gc, hv = _split_proj(proj)
    d_conv = gb.shape[1]
    yield

    lb = _layer_lower_bound(lb_ref, 0)
    row = lax.broadcasted_iota(jnp.int32, (CHUNK, CHUNK), 0)
    col = lax.broadcasted_iota(jnp.int32, (CHUNK, CHUNK), 1)
    causal = row >= col
    logf, k_all = _gates(fz, lb)
    cum_all = _group_cumsum(logf, CHUNK)
    q_dec, decay, scores, upd = [], [], [], []
    for c in range(n_chunks):
        rows = slice(c * CHUNK, (c + 1) * CHUNK)
        cum = cum_all[rows]
        last = cum[CHUNK - 1:CHUNK, :]
        q_dec.append((q[rows] * jnp.exp(cum)).astype(BF16))
        k_dec = (k_all[rows] * jnp.exp(-cum)).astype(BF16)
        k_end = (k_all[rows] * jnp.exp(last - cum)).astype(BF16)
        dec = jnp.exp(last)
        decay.append([jnp.broadcast_to(dec[:, sl], (dk, dk)).T for sl in heads])
        v = iv[rows].astype(BF16)
        scores.append([jnp.where(causal, _dot_nt(q_dec[c][:, sl], k_dec[:, sl]), 0.0).astype(BF16)
                       for sl in heads])
        upd.append([_dot_tn(k_end[:, sl], v[:, sl]) for sl in heads])
    yield

    st = [st_ref[hd] for hd in range(H_REC)]
    o_chunks = []
    for c in range(n_chunks):
        v = iv[c * CHUNK:(c + 1) * CHUNK].astype(BF16)
        o_heads = []
        for hd, sl in enumerate(heads):
            o_heads.append(_bdot(scores[c][hd], v[:, sl])
                           + _bdot(q_dec[c][:, sl], st[hd].astype(BF16)))
            st[hd] = st[hd] * decay[c][hd] + upd[c][hd]
        o_chunks.append(jnp.concatenate(o_heads, axis=-1))
    for hd in range(H_REC):
        st_ref[hd] = st[hd]
    yield

    o_rec = _head_out(jnp.concatenate(o_chunks, axis=0), g, g_onorm_ref[...])
    mix_ref[:, 0:d_rec] = o_rec.astype(BF16)

    u = gc * hv
    ubuf_ref[SUBLANES:SUBLANES + tl, :] = u
    w_conv = w_conv_ref[...]
    y_conv = (w_conv[0:1] * ubuf_ref[SUBLANES - 2:SUBLANES - 2 + tl, :]
              + w_conv[1:2] * ubuf_ref[SUBLANES - 1:SUBLANES - 1 + tl, :]
              + w_conv[2:3] * u)
    mix_ref[:, d_rec:d_rec + d_conv] = (gb * y_conv).astype(BF16)
    tail = u[tl - (CONV_W - 1):tl]
    ubuf_ref[SUBLANES - (CONV_W - 1):SUBLANES, :] = tail
    yield

    x1 = x + g1 * _bdot(mix_ref[...], w_out_ref[...])
    x1_ref[...] = x1
    h2_ref[...] = _modulate(x1, sh2, sc2)

    @pl.when(is_last)
    def _():
        for hd in range(H_REC):
            rec_ref[hd] = st_ref[hd]
        conv_ref[...] = tail


def _prompt_kernel(x_ref, mod_ref, lb_ref, w_in_hbm, w_conv_ref, g_onorm_ref, w_out_hbm,
                   w_up_hbm, w_down_hbm, g_final_ref,
                   y_ref, rec_ref, conv_ref, w_in_out, w_out_out, w_up_out, w_down_out,
                   st_ref, ubuf_ref, mix_ref, x1_ref, h2_ref,
                   w_in_ref, w_out_ref, w_up_ref, w_down_ref, stage_ref, stage_sem, out_sem,
                   *, n_tiles, n_steps):
    s = pl.program_id(0)
    tile = jnp.minimum(s, n_steps - 1)
    l = lax.rem(tile, n_tiles)
    d = x_ref.shape[1]

    w_vmem = (w_in_ref, w_out_ref, w_up_ref, w_down_ref)
    w_outs = (w_in_out, w_out_out, w_up_out, w_down_out)

    def bf16_export(i):
        return pltpu.make_async_copy(w_vmem[i], w_outs[i], out_sem.at[i])

    @pl.when(s == 0)
    def _():
        _load_weights_as_bf16((w_in_hbm, w_out_hbm, w_up_hbm, w_down_hbm), w_vmem,
                              stage_ref, stage_sem)
        for i in range(len(w_vmem)):
            bf16_export(i).start()

    @pl.when(s == n_steps)
    def _():
        for i in range(len(w_vmem)):
            bf16_export(i).wait()

    @pl.when(l == 0)
    def _():
        st_ref[...] = jnp.zeros_like(st_ref)
        ubuf_ref[0:SUBLANES, :] = jnp.zeros((SUBLANES, ubuf_ref.shape[1]), F32)

    def mix_stage():
        mod = mod_ref[pl.ds(lax.div(tile, n_tiles), 1), :]
        return _prompt_mix_stage(x_ref, mod, lb_ref, w_in_ref, w_conv_ref, g_onorm_ref,
                                 w_out_ref, st_ref, ubuf_ref, mix_ref, x1_ref, h2_ref,
                                 rec_ref, conv_ref, l == n_tiles - 1)

    def mlp_stage():
        mod_prev = mod_ref[pl.ds(lax.div(s - 1, n_tiles), 1), :]
        return _mlp_stage(x1_ref, h2_ref, mod_prev[:, (N_MOD - 1) * d:], w_up_ref, w_down_ref,
                          g_final_ref, y_ref)

    @pl.when(s == 0)
    def _():
        _alternate(mix_stage())

    @pl.when((s > 0) & (s < n_steps))
    def _():
        _alternate(mlp_stage(), mix_stage())

    @pl.when(s == n_steps)
    def _():
        _alternate(mlp_stage())


def _prompt_layer(x, mod_p, lower_bounds, w_in, w_conv, g_onorm, w_out, w_up, w_down, g_final):
    bsz, seq, d = x.shape
    d_rec = lower_bounds.shape[1]
    dk = d_rec // H_REC
    d_conv = w_conv.shape[1]
    tl = PROMPT_TILE
    n_tiles = seq // tl
    n_steps = bsz * n_tiles
    assert seq % tl == 0 and tl % CHUNK == 0
    mats = (w_in, w_out, w_up, w_down)
    hbm = pl.BlockSpec(memory_space=pl.ANY)

    def mix_tile(s):
        t = jnp.minimum(s, n_steps - 1)
        return lax.div(t, n_tiles), lax.rem(t, n_tiles)

    def mlp_tile(s):
        t = jnp.maximum(s - 1, 0)
        return lax.div(t, n_tiles), lax.rem(t, n_tiles)

    return pl.pallas_call(
        functools.partial(_prompt_kernel, n_tiles=n_tiles, n_steps=n_steps),
        grid=(n_steps + 1,),
        in_specs=[
            pl.BlockSpec((None, tl, d), lambda s: (*mix_tile(s), 0)),
            _const_spec(mod_p.shape),
            _const_spec(lower_bounds.shape),
            hbm,
            _const_spec(w_conv.shape),
            _const_spec(g_onorm.shape),
            hbm,
            hbm,
            hbm,
            _const_spec(g_final.shape),
        ],
        out_specs=[
            pl.BlockSpec((None, tl, d), lambda s: (*mlp_tile(s), 0)),
            pl.BlockSpec((None, H_REC, dk, dk), lambda s: (mix_tile(s)[0], 0, 0, 0)),
            pl.BlockSpec((None, CONV_W - 1, d_conv), lambda s: (mix_tile(s)[0], 0, 0)),
        ] + [hbm] * len(mats),
        out_shape=[
            jax.ShapeDtypeStruct((bsz, seq, d), F32),
            jax.ShapeDtypeStruct((bsz, H_REC, dk, dk), F32),
            jax.ShapeDtypeStruct((bsz, CONV_W - 1, d_conv), F32),
        ] + [jax.ShapeDtypeStruct(w.shape, BF16) for w in mats],
        scratch_shapes=[
            pltpu.VMEM((H_REC, dk, dk), F32),
            pltpu.VMEM((SUBLANES + tl, d_conv), F32),
            pltpu.VMEM((tl, d_rec + d_conv), BF16),
            pltpu.VMEM((tl, d), F32),
            pltpu.VMEM((tl, d), BF16),
        ] + [pltpu.VMEM(w.shape, BF16) for w in mats] + [
            pltpu.VMEM((WEIGHT_STAGE_SLOTS, *WEIGHT_STAGE_SHAPE), F32),
            pltpu.SemaphoreType.DMA((WEIGHT_STAGE_SLOTS,)),
            pltpu.SemaphoreType.DMA((len(mats),)),
        ],
        compiler_params=pltpu.CompilerParams(
            dimension_semantics=("arbitrary",), vmem_limit_bytes=VMEM_LIMIT_BYTES),
        name="prompt_layer",
    )(x, mod_p, lower_bounds, w_in, w_conv, g_onorm, w_out, w_up, w_down, g_final)


def _sample_kernel(x_ref, mod_ref, lb_ref, w_in_ref, w_conv_ref, g_onorm_ref, w_out_ref,
                   w_up_ref, w_down_ref, g_final_ref, rec_in_ref, conv_in_ref,
                   y_ref, rec_ref, conv_ref,
                   modx_ref, cbx_ref, ubuf_ref, mix_ref, *, seq):
    rows, d = x_ref.shape
    tb = rows // seq
    d_rec = lb_ref.shape[1]
    dk = d_rec // H_REC
    d_conv = conv_in_ref.shape[2]

    for b in range(tb):
        r = slice(b * seq, (b + 1) * seq)
        modx_ref[r, :] = jnp.broadcast_to(mod_ref[b:b + 1, :], (seq, mod_ref.shape[1]))
        for j in range(CONV_W - 1):
            cbx_ref[j, r, :] = jnp.broadcast_to(conv_in_ref[b, j:j + 1, :], (seq, d_conv))

    sh1, sc1, g1, sh2, sc2, g2 = (modx_ref[:, i * d:(i + 1) * d] for i in range(N_MOD))

    x = x_ref[...]
    proj = _bdot(_modulate(x, sh1, sc1), w_in_ref[...])
    q, fz, iv, g, gb, gc, hv = _split_proj(proj)

    lb = _layer_lower_bound(lb_ref, 0)
    row = lax.broadcasted_iota(jnp.int32, (rows, rows), 0)
    col = lax.broadcasted_iota(jnp.int32, (rows, rows), 1)
    causal = ((row // seq) == (col // seq)) & (row >= col)
    logf, k = _gates(fz, lb)
    cum = _group_cumsum(logf, seq)
    last = _group_last(cum, seq)
    q_dec = (q * jnp.exp(cum)).astype(BF16)
    k_dec = (k * jnp.exp(-cum)).astype(BF16)
    k_end = k * jnp.exp(last - cum)
    decay = jnp.exp(last)
    vb = iv.astype(BF16)

    dec3 = _decay_terms(decay, _group_pos((rows, d_rec), seq))
    zeros_blk = jnp.zeros((seq, dk), F32)
    rhs_bottom = _decay_selector(seq, dk)

    heads = [slice(hd * dk, (hd + 1) * dk) for hd in range(H_REC)]

    def update_states(elements):
        for b in elements:
            r = slice(b * seq, (b + 1) * seq)
            for hd, sl in enumerate(heads):
                lhs = jnp.concatenate([k_end[r, sl], dec3[r, sl]], axis=0)
                rhs = jnp.concatenate(
                    [jnp.concatenate([iv[r, sl], zeros_blk], axis=1), rhs_bottom], axis=0)
                both = _dot_tn(lhs.astype(BF16), rhs.astype(BF16))
                rec_ref[b, hd] = both[:, dk:] * rec_in_ref[b, hd] + both[:, :dk]

    scores = [jnp.where(causal, _dot_nt(q_dec[:, sl], k_dec[:, sl]), 0.0).astype(BF16)
              for sl in heads]
    o_inter = jnp.concatenate(
        [jnp.concatenate([_bdot(q_dec[b * seq:(b + 1) * seq, sl], rec_in_ref[b, hd].astype(BF16))
                          for hd, sl in enumerate(heads)], axis=-1)
         for b in range(tb)], axis=0)
    o_intra = jnp.concatenate([_bdot(scores[hd], vb[:, sl]) for hd, sl in enumerate(heads)],
                              axis=-1)
    o_rec = _head_out(o_intra + o_inter, g, g_onorm_ref[...])
    mix_ref[:, 0:d_rec] = o_rec.astype(BF16)

    u = gc * hv
    ubuf_ref[SUBLANES:SUBLANES + rows, :] = u
    tok = _group_pos((rows, d_conv), seq)
    u_m1 = jnp.where(tok >= 1, ubuf_ref[SUBLANES - 1:SUBLANES - 1 + rows, :], cbx_ref[1])
    u_m2 = jnp.where(tok >= 2, ubuf_ref[SUBLANES - 2:SUBLANES - 2 + rows, :],
                     jnp.where(tok == 1, cbx_ref[1], cbx_ref[0]))
    w_conv = w_conv_ref[...]
    y_conv = w_conv[0:1] * u_m2 + w_conv[1:2] * u_m1 + w_conv[2:3] * u
    mix_ref[:, d_rec:d_rec + d_conv] = (gb * y_conv).astype(BF16)
    for b in range(tb):
        conv_ref[b] = u[(b + 1) * seq - (CONV_W - 1):(b + 1) * seq]

    x1 = x + g1 * _bdot(mix_ref[...], w_out_ref[...])
    update_states(range(tb // 2))
    h2 = _modulate(x1, sh2, sc2)
    mlp = _mlp_down(_mlp_up(h2, w_up_ref, 0), w_up_ref, w_down_ref, 0)
    for j in range(1, MLP_BLOCKS):
        mlp += _mlp_down(_mlp_up(h2, w_up_ref, j), w_up_ref, w_down_ref, j)
    update_states(range(tb // 2, tb))
    y_ref[...] = _final_norm(x1, g2, mlp, g_final_ref)


def _sample_layer(x2d, seq, mod, lower_bounds, w_in, w_conv, g_onorm, w_out, w_up, w_down,
                  g_final, rec_in, conv_in):
    n_rows, d = x2d.shape
    bsz = n_rows // seq
    d_rec = lower_bounds.shape[1]
    dk = d_rec // H_REC
    d_conv = w_conv.shape[1]
    tb = SAMPLE_TILE_B
    rows = tb * seq
    assert bsz % tb == 0 and seq == SUBLANES and seq >= CONV_W
    return pl.pallas_call(
        functools.partial(_sample_kernel, seq=seq),
        grid=(bsz // tb,),
        in_specs=[
            pl.BlockSpec((rows, d), lambda i: (i, 0)),
            pl.BlockSpec((tb, N_MOD * d), lambda i: (i, 0)),
            _const_spec(lower_bounds.shape),
            _const_spec(w_in.shape),
            _const_spec(w_conv.shape),
            _const_spec(g_onorm.shape),
            _const_spec(w_out.shape),
            _const_spec(w_up.shape),
            _const_spec(w_down.shape),
            _const_spec(g_final.shape),
            pl.BlockSpec((tb, H_REC, dk, dk), lambda i: (i, 0, 0, 0)),
            pl.BlockSpec((tb, CONV_W - 1, d_conv), lambda i: (i, 0, 0)),
        ],
        out_specs=[
            pl.BlockSpec((rows, d), lambda i: (i, 0)),
            pl.BlockSpec((tb, H_REC, dk, dk), lambda i: (i, 0, 0, 0)),
            pl.BlockSpec((tb, CONV_W - 1, d_conv), lambda i: (i, 0, 0)),
        ],
        out_shape=[
            jax.ShapeDtypeStruct((n_rows, d), F32),
            jax.ShapeDtypeStruct((bsz, H_REC, dk, dk), F32),
            jax.ShapeDtypeStruct((bsz, CONV_W - 1, d_conv), F32),
        ],
        scratch_shapes=[
            pltpu.VMEM((rows, N_MOD * d), F32),
            pltpu.VMEM((CONV_W - 1, rows, d_conv), F32),
            pltpu.VMEM((SUBLANES + rows, d_conv), F32),
            pltpu.VMEM((rows, d_rec + d_conv), BF16),
        ],
        compiler_params=pltpu.CompilerParams(
            dimension_semantics=("arbitrary",), vmem_limit_bytes=VMEM_LIMIT_BYTES),
        name="sample_layer",
    )(x2d, mod, lower_bounds, w_in, w_conv, g_onorm, w_out, w_up, w_down, g_final, rec_in, conv_in)


def kernel(x_prompt, x_sample, state_rec, state_conv, c_prompt, c_sample, lower_bounds, w_ada,
           b_ada, w_in, w_conv, g_onorm, w_out, w_up, w_down, g_final):
    depth = w_in.shape[0]
    assert depth == 1, "single-layer trunk"
    bsz_s, seq_s, d = x_sample.shape

    mod_s, mod_p = _modulation(c_sample, c_prompt, w_ada[0], b_ada)

    g_final = g_final.reshape(1, d)
    y_p, rec_p, conv_p, w_in_b, w_out_b, w_up_b, w_down_b = _prompt_layer(
        x_prompt, mod_p, lower_bounds, w_in[0], w_conv[0], g_onorm, w_out[0], w_up[0], w_down[0],
        g_final)
    y_s, rec_s, conv_s = _sample_layer(
        x_sample.reshape(bsz_s * seq_s, d), seq_s, mod_s, lower_bounds, w_in_b, w_conv[0], g_onorm,
        w_out_b, w_up_b, w_down_b, g_final, state_rec[0], state_conv[0])
    return (y_p, y_s.reshape(bsz_s, seq_s, d), rec_p[None], conv_p[None], rec_s[None],
            conv_s[None])
```

```python
import functools

import jax
import jax.numpy as jnp
from jax import lax
from jax.experimental import pallas as pl
from jax.experimental.pallas import tpu as pltpu

F32 = jnp.float32
BF16 = jnp.bfloat16

EPS = 1e-6
H_REC = 4
CONV_W = 3
N_MOD = 6
N_PROJ_GROUPS = 7
CHUNK = 64

SUBLANES = 8
VMEM_LIMIT_BYTES = 56 * 1024 * 1024

PROMPT_TILE = 512
SAMPLE_TILE_B = 16
MOD_TILE_K = 256
MLP_BLOCKS = 4
WEIGHT_STAGE_SHAPE = (512, 512)
WEIGHT_STAGE_SLOTS = 4


def _const_spec(shape):
    zeros = (0,) * len(shape)
    return pl.BlockSpec(shape, lambda *_: zeros, pipeline_mode=pl.Buffered(1))


def _rms(x):
    return x * lax.rsqrt(jnp.mean(x * x, axis=-1, keepdims=True) + EPS)


def _silu(x):
    return x * jax.nn.sigmoid(x)


def _bdot(a, b):
    return jnp.dot(a, b, preferred_element_type=F32)


def _dot_nt(a, b):
    return lax.dot_general(a, b, (((1,), (1,)), ((), ())), preferred_element_type=F32)


def _dot_tn(a, b):
    return lax.dot_general(a, b, (((0,), (0,)), ((), ())), preferred_element_type=F32)


def _group_pos(shape, group):
    return lax.broadcasted_iota(jnp.int32, shape, 0) % group


def _group_cumsum(x, group):
    pos = _group_pos(x.shape, group)
    step = 1
    while step < group:
        x = x + jnp.where(pos >= step, pltpu.roll(x, step, axis=0), 0.0)
        step *= 2
    return x


def _group_last(x, group):
    rows = x.shape[0]
    pos = _group_pos(x.shape, group)
    x = jnp.where(pos == group - 1, x, 0.0)
    step = 1
    while step < group:
        x = x + jnp.where(pos + step < group, pltpu.roll(x, rows - step, axis=0), 0.0)
        step *= 2
    return x


def _layer_lower_bound(lb_ref, layer):
    lb = lb_ref[...]
    e = jnp.exp(lb - jnp.max(lb, axis=0, keepdims=True))
    sm = e / jnp.sum(e, axis=0, keepdims=True)
    return jnp.sum(sm[: layer + 1], axis=0, keepdims=True)


def _gates(fz, lb):
    sig = jax.nn.sigmoid(fz)
    logf = jnp.log(lb + (1.0 - lb) * sig)
    k = (1.0 - lb) * (1.0 - sig)
    return logf, k


def _decay_terms(decay, row):
    hi = decay.astype(BF16).astype(F32)
    rest = decay - hi
    mid = rest.astype(BF16).astype(F32)
    lo = rest - mid
    return jnp.where(row == 0, hi, jnp.where(row == 1, mid, jnp.where(row == 2, lo, 0.0)))


def _decay_selector(n_rows, width):
    sub = lax.broadcasted_iota(jnp.int32, (n_rows, width), 0)
    return jnp.concatenate(
        [jnp.zeros((n_rows, width), F32), jnp.where(sub < 3, 1.0, 0.0)], axis=1)


def _head_out(o, g, g_onorm):
    dv = o.shape[-1] // H_REC
    pieces = []
    for h in range(H_REC):
        sl = slice(h * dv, (h + 1) * dv)
        pieces.append(_rms(o[:, sl]))
    return (jnp.concatenate(pieces, axis=-1) * g_onorm) * _silu(g)


def _modulate(x, shift, scale):
    return (_rms(x) * (1.0 + scale) + shift).astype(BF16)


def _mlp_cols(w_up_ref, j):
    blk = w_up_ref.shape[1] // MLP_BLOCKS
    return slice(j * blk, (j + 1) * blk)


def _mlp_up(h2, w_up_ref, j):
    up = jnp.maximum(_bdot(h2, w_up_ref[:, _mlp_cols(w_up_ref, j)]), 0.0)
    return (up * up).astype(BF16)


def _mlp_down(up, w_up_ref, w_down_ref, j):
    return _bdot(up, w_down_ref[_mlp_cols(w_up_ref, j), :])


def _final_norm(x, g2, mlp, g_final_ref):
    return _rms(x + g2 * mlp) * g_final_ref[...]


def _mod_kernel(c_s_ref, c_p_ref, w_ref, b_ref, o_s_ref, o_p_ref):
    k = pl.program_id(0)
    n_s = c_s_ref.shape[0]
    kb = w_ref.shape[0]
    cols = pl.ds(pl.multiple_of(k * kb, kb), kb)
    c = jnp.concatenate([c_s_ref[:, cols], c_p_ref[:, cols]], axis=0)
    part = _bdot(_silu(c).astype(BF16), w_ref[...].astype(BF16))

    @pl.when(k == 0)
    def _():
        o_s_ref[...] = part[:n_s] + b_ref[...]
        o_p_ref[...] = part[n_s:] + b_ref[...]

    @pl.when(k > 0)
    def _():
        o_s_ref[...] += part[:n_s]
        o_p_ref[...] += part[n_s:]


def _modulation(c_s, c_p, w_ada, b_ada):
    (n_s, d), n_p = c_s.shape, c_p.shape[0]
    n = w_ada.shape[1]
    assert n_s % SUBLANES == 0 and n_p % SUBLANES == 0 and d % MOD_TILE_K == 0
    return pl.pallas_call(
        _mod_kernel,
        grid=(d // MOD_TILE_K,),
        in_specs=[
            pl.BlockSpec((n_s, d), lambda k: (0, 0)),
            pl.BlockSpec((n_p, d), lambda k: (0, 0)),
            pl.BlockSpec((MOD_TILE_K, n), lambda k: (k, 0)),
            pl.BlockSpec((1, n), lambda k: (0, 0)),
        ],
        out_specs=[
            pl.BlockSpec((n_s, n), lambda k: (0, 0)),
            pl.BlockSpec((n_p, n), lambda k: (0, 0)),
        ],
        out_shape=[
            jax.ShapeDtypeStruct((n_s, n), F32),
            jax.ShapeDtypeStruct((n_p, n), F32),
        ],
        compiler_params=pltpu.CompilerParams(
            dimension_semantics=("arbitrary",), vmem_limit_bytes=VMEM_LIMIT_BYTES),
        name="adaln_modulation",
    )(c_s, c_p, w_ada, b_ada)


def _split_proj(proj):
    width = proj.shape[1] // N_PROJ_GROUPS
    return tuple(proj[:, p * width:(p + 1) * width] for p in range(N_PROJ_GROUPS))


def _weight_blocks(w_hbm, w_vmem):
    rows, cols = w_hbm.shape
    br, bc = WEIGHT_STAGE_SHAPE
    assert rows % br == 0 and cols % bc == 0
    return [(w_hbm.at[r:r + br, c:c + bc], w_vmem.at[r:r + br, c:c + bc])
            for r in range(0, rows, br) for c in range(0, cols, bc)]


def _stage_copy(src, stage_ref, sem_ref, i):
    slot = i % WEIGHT_STAGE_SLOTS
    return pltpu.make_async_copy(src, stage_ref.at[slot], sem_ref.at[slot])


def _load_weights_as_bf16(w_hbm_refs, w_vmem_refs, stage_ref, sem_ref):
    blocks = [blk for w_hbm, w_vmem in zip(w_hbm_refs, w_vmem_refs)
              for blk in _weight_blocks(w_hbm, w_vmem)]
    ahead = WEIGHT_STAGE_SLOTS - 1
    for i in range(min(ahead, len(blocks))):
        _stage_copy(blocks[i][0], stage_ref, sem_ref, i).start()
    for i, (src, dst) in enumerate(blocks):
        if i + ahead < len(blocks):
            _stage_copy(blocks[i + ahead][0], stage_ref, sem_ref, i + ahead).start()
        _stage_copy(src, stage_ref, sem_ref, i).wait()
        dst[...] = stage_ref[i % WEIGHT_STAGE_SLOTS].astype(BF16)


def _alternate(*stages):
    stages = list(stages)
    while stages:
        for stage in list(stages):
            try:
                next(stage)
            except StopIteration:
                stages.remove(stage)


def _mlp_stage(x1_ref, mod, w_up_ref, w_down_ref, g_final_ref, y_ref):
    d = x1_ref.shape[1]
    sh2, sc2, g2 = (mod[:, i * d:(i + 1) * d] for i in range(3, N_MOD))
    h2 = _modulate(x1_ref[...], sh2, sc2)
    ups = [_mlp_up(h2, w_up_ref, 0)]
    yield
    mlp = None
    for j in range(MLP_BLOCKS):
        if j + 1 < MLP_BLOCKS:
            ups.append(_mlp_up(h2, w_up_ref, j + 1))
        part = _mlp_down(ups[j], w_up_ref, w_down_ref, j)
        mlp = part if mlp is None else mlp + part
        if j + 2 < MLP_BLOCKS:
            yield
    y_ref[...] = _final_norm(x1_ref[...], g2, mlp, g_final_ref)
    yield


def _prompt_mix_stage(x_ref, mod, lb_ref, w_in_ref, w_conv_ref, g_onorm_ref, w_out_ref,
                      st_ref, ubuf_ref, mix_ref, x1_ref, rec_ref, conv_ref, is_last):
    tl, d = x_ref.shape
    d_rec = lb_ref.shape[1]
    dk = d_rec // H_REC
    n_chunks = tl // CHUNK
    heads = [slice(hd * dk, (hd + 1) * dk) for hd in range(H_REC)]
    sh1, sc1, g1 = (mod[:, i * d:(i + 1) * d] for i in range(3))

    x = x_ref[...]
    proj = _bdot(_modulate(x, sh1, sc1), w_in_ref[...])
    q, fz, iv, g, gb, gc, hv = _split_proj(proj)
    d_conv = gb.shape[1]
    yield

    lb = _layer_lower_bound(lb_ref, 0)
    row = lax.broadcasted_iota(jnp.int32, (CHUNK, CHUNK), 0)
    col = lax.broadcasted_iota(jnp.int32, (CHUNK, CHUNK), 1)
    causal = row >= col
    logf, k_all = _gates(fz, lb)
    cum_all = _group_cumsum(logf, CHUNK)
    q_dec, decay, scores, upd = [], [], [], []
    for c in range(n_chunks):
        rows = slice(c * CHUNK, (c + 1) * CHUNK)
        cum = cum_all[rows]
        last = cum[CHUNK - 1:CHUNK, :]
        q_dec.append((q[rows] * jnp.exp(cum)).astype(BF16))
        k_dec = (k_all[rows] * jnp.exp(-cum)).astype(BF16)
        k_end = (k_all[rows] * jnp.exp(last - cum)).astype(BF16)
        dec = jnp.exp(last)
        decay.append([jnp.broadcast_to(dec[:, sl], (dk, dk)).T for sl in heads])
        v = iv[rows].astype(BF16)
        scores.append([jnp.where(causal, _dot_nt(q_dec[c][:, sl], k_dec[:, sl]), 0.0).astype(BF16)
                       for sl in heads])
        upd.append([_dot_tn(k_end[:, sl], v[:, sl]) for sl in heads])
    yield

    st = [st_ref[hd] for hd in range(H_REC)]
    o_chunks = []
    for c in range(n_chunks):
        v = iv[c * CHUNK:(c + 1) * CHUNK].astype(BF16)
        o_heads = []
        for hd, sl in enumerate(heads):
            o_heads.append(_bdot(scores[c][hd], v[:, sl])
                           + _bdot(q_dec[c][:, sl], st[hd].astype(BF16)))
            st[hd] = st[hd] * decay[c][hd] + upd[c][hd]
        o_chunks.append(jnp.concatenate(o_heads, axis=-1))
    for hd in range(H_REC):
        st_ref[hd] = st[hd]
    yield

    o_rec = _head_out(jnp.concatenate(o_chunks, axis=0), g, g_onorm_ref[...])
    mix_ref[:, 0:d_rec] = o_rec.astype(BF16)

    u = gc * hv
    ubuf_ref[SUBLANES:SUBLANES + tl, :] = u
    w_conv = w_conv_ref[...]
    y_conv = (w_conv[0:1] * ubuf_ref[SUBLANES - 2:SUBLANES - 2 + tl, :]
              + w_conv[1:2] * ubuf_ref[SUBLANES - 1:SUBLANES - 1 + tl, :]
              + w_conv[2:3] * u)
    mix_ref[:, d_rec:d_rec + d_conv] = (gb * y_conv).astype(BF16)
    tail = u[tl - (CONV_W - 1):tl]
    ubuf_ref[SUBLANES - (CONV_W - 1):SUBLANES, :] = tail
    yield

    x1 = x + g1 * _bdot(mix_ref[...], w_out_ref[...])
    x1_ref[...] = x1

    @pl.when(is_last)
    def _():
        for hd in range(H_REC):
            rec_ref[hd] = st_ref[hd]
        conv_ref[...] = tail


def _prompt_kernel(x_ref, mod_ref, lb_ref, w_in_hbm, w_conv_ref, g_onorm_ref, w_out_hbm,
                   w_up_hbm, w_down_hbm, g_final_ref,
                   y_ref, rec_ref, conv_ref, w_in_out, w_out_out, w_up_out, w_down_out,
                   st_ref, ubuf_ref, mix_ref, x1_ref,
                   w_in_ref, w_out_ref, w_up_ref, w_down_ref, stage_ref, stage_sem, out_sem,
                   *, n_tiles, n_steps):
    s = pl.program_id(0)
    tile = jnp.minimum(s, n_steps - 1)
    l = lax.rem(tile, n_tiles)
    d = x_ref.shape[1]

    w_vmem = (w_in_ref, w_out_ref, w_up_ref, w_down_ref)
    w_outs = (w_in_out, w_out_out, w_up_out, w_down_out)

    def bf16_export(i):
        return pltpu.make_async_copy(w_vmem[i], w_outs[i], out_sem.at[i])

    @pl.when(s == 0)
    def _():
        _load_weights_as_bf16((w_in_hbm, w_out_hbm, w_up_hbm, w_down_hbm), w_vmem,
                              stage_ref, stage_sem)
        for i in range(len(w_vmem)):
            bf16_export(i).start()

    @pl.when(s == n_steps)
    def _():
        for i in range(len(w_vmem)):
            bf16_export(i).wait()

    @pl.when(l == 0)
    def _():
        st_ref[...] = jnp.zeros_like(st_ref)
        ubuf_ref[0:SUBLANES, :] = jnp.zeros((SUBLANES, ubuf_ref.shape[1]), F32)

    def mix_stage():
        mod = mod_ref[pl.ds(lax.div(tile, n_tiles), 1), :]
        return _prompt_mix_stage(x_ref, mod, lb_ref, w_in_ref, w_conv_ref, g_onorm_ref,
                                 w_out_ref, st_ref, ubuf_ref, mix_ref, x1_ref,
                                 rec_ref, conv_ref, l == n_tiles - 1)

    def mlp_stage():
        mod_prev = mod_ref[pl.ds(lax.div(s - 1, n_tiles), 1), :]
        return _mlp_stage(x1_ref, mod_prev, w_up_ref, w_down_ref, g_final_ref, y_ref)

    @pl.when(s == 0)
    def _():
        _alternate(mix_stage())

    @pl.when((s > 0) & (s < n_steps))
    def _():
        _alternate(mix_stage(), mlp_stage())

    @pl.when(s == n_steps)
    def _():
        _alternate(mlp_stage())


def _prompt_layer(x, mod_p, lower_bounds, w_in, w_conv, g_onorm, w_out, w_up, w_down, g_final):
    bsz, seq, d = x.shape
    d_rec = lower_bounds.shape[1]
    dk = d_rec // H_REC
    d_conv = w_conv.shape[1]
    tl = PROMPT_TILE
    n_tiles = seq // tl
    n_steps = bsz * n_tiles
    assert seq % tl == 0 and tl % CHUNK == 0
    mats = (w_in, w_out, w_up, w_down)
    hbm = pl.BlockSpec(memory_space=pl.ANY)

    def mix_tile(s):
        t = jnp.minimum(s, n_steps - 1)
        return lax.div(t, n_tiles), lax.rem(t, n_tiles)

    def mlp_tile(s):
        t = jnp.maximum(s - 1, 0)
        return lax.div(t, n_tiles), lax.rem(t, n_tiles)

    return pl.pallas_call(
        functools.partial(_prompt_kernel, n_tiles=n_tiles, n_steps=n_steps),
        grid=(n_steps + 1,),
        in_specs=[
            pl.BlockSpec((None, tl, d), lambda s: (*mix_tile(s), 0)),
            _const_spec(mod_p.shape),
            _const_spec(lower_bounds.shape),
            hbm,
            _const_spec(w_conv.shape),
            _const_spec(g_onorm.shape),
            hbm,
            hbm,
            hbm,
            _const_spec(g_final.shape),
        ],
        out_specs=[
            pl.BlockSpec((None, tl, d), lambda s: (*mlp_tile(s), 0)),
            pl.BlockSpec((None, H_REC, dk, dk), lambda s: (mix_tile(s)[0], 0, 0, 0)),
            pl.BlockSpec((None, CONV_W - 1, d_conv), lambda s: (mix_tile(s)[0], 0, 0)),
        ] + [hbm] * len(mats),
        out_shape=[
            jax.ShapeDtypeStruct((bsz, seq, d), F32),
            jax.ShapeDtypeStruct((bsz, H_REC, dk, dk), F32),
            jax.ShapeDtypeStruct((bsz, CONV_W - 1, d_conv), F32),
        ] + [jax.ShapeDtypeStruct(w.shape, BF16) for w in mats],
        scratch_shapes=[
            pltpu.VMEM((H_REC, dk, dk), F32),
            pltpu.VMEM((SUBLANES + tl, d_conv), F32),
            pltpu.VMEM((tl, d_rec + d_conv), BF16),
            pltpu.VMEM((tl, d), F32),
        ] + [pltpu.VMEM(w.shape, BF16) for w in mats] + [
            pltpu.VMEM((WEIGHT_STAGE_SLOTS, *WEIGHT_STAGE_SHAPE), F32),
            pltpu.SemaphoreType.DMA((WEIGHT_STAGE_SLOTS,)),
            pltpu.SemaphoreType.DMA((len(mats),)),
        ],
        compiler_params=pltpu.CompilerParams(
            dimension_semantics=("arbitrary",), vmem_limit_bytes=VMEM_LIMIT_BYTES),
        name="prompt_layer",
    )(x, mod_p, lower_bounds, w_in, w_conv, g_onorm, w_out, w_up, w_down, g_final)


def _sample_kernel(x_ref, mod_ref, lb_ref, w_in_ref, w_conv_ref, g_onorm_ref, w_out_ref,
                   w_up_ref, w_down_ref, g_final_ref, rec_in_ref, conv_in_ref,
                   y_ref, rec_ref, conv_ref,
                   modx_ref, cbx_ref, ubuf_ref, mix_ref, *, seq):
    rows, d = x_ref.shape
    tb = rows // seq
    d_rec = lb_ref.shape[1]
    dk = d_rec // H_REC
    d_conv = conv_in_ref.shape[2]

    for b in range(tb):
        r = slice(b * seq, (b + 1) * seq)
        modx_ref[r, :] = jnp.broadcast_to(mod_ref[b:b + 1, :], (seq, mod_ref.shape[1]))
        for j in range(CONV_W - 1):
            cbx_ref[j, r, :] = jnp.broadcast_to(conv_in_ref[b, j:j + 1, :], (seq, d_conv))

    sh1, sc1, g1, sh2, sc2, g2 = (modx_ref[:, i * d:(i + 1) * d] for i in range(N_MOD))

    x = x_ref[...]
    proj = _bdot(_modulate(x, sh1, sc1), w_in_ref[...])
    q, fz, iv, g, gb, gc, hv = _split_proj(proj)

    lb = _layer_lower_bound(lb_ref, 0)
    row = lax.broadcasted_iota(jnp.int32, (rows, rows), 0)
    col = lax.broadcasted_iota(jnp.int32, (rows, rows), 1)
    causal = ((row // seq) == (col // seq)) & (row >= col)
    logf, k = _gates(fz, lb)
    cum = _group_cumsum(logf, seq)
    last = _group_last(cum, seq)
    q_dec = (q * jnp.exp(cum)).astype(BF16)
    k_dec = (k * jnp.exp(-cum)).astype(BF16)
    k_end = k * jnp.exp(last - cum)
    decay = jnp.exp(last)
    vb = iv.astype(BF16)

    dec3 = _decay_terms(decay, _group_pos((rows, d_rec), seq))
    zeros_blk = jnp.zeros((seq, dk), F32)
    rhs_bottom = _decay_selector(seq, dk)

    heads = [slice(hd * dk, (hd + 1) * dk) for hd in range(H_REC)]

    def update_states(elements):
        for b in elements:
            r = slice(b * seq, (b + 1) * seq)
            for hd, sl in enumerate(heads):
                lhs = jnp.concatenate([k_end[r, sl], dec3[r, sl]], axis=0)
                rhs = jnp.concatenate(
                    [jnp.concatenate([iv[r, sl], zeros_blk], axis=1), rhs_bottom], axis=0)
                both = _dot_tn(lhs.astype(BF16), rhs.astype(BF16))
                rec_ref[b, hd] = both[:, dk:] * rec_in_ref[b, hd] + both[:, :dk]

    scores = [jnp.where(causal, _dot_nt(q_dec[:, sl], k_dec[:, sl]), 0.0).astype(BF16)
              for sl in heads]
    o_inter = jnp.concatenate(
        [jnp.concatenate([_bdot(q_dec[b * seq:(b + 1) * seq, sl], rec_in_ref[b, hd].astype(BF16))
                          for hd, sl in enumerate(heads)], axis=-1)
         for b in range(tb)], axis=0)
    o_intra = jnp.concatenate([_bdot(scores[hd], vb[:, sl]) for hd, sl in enumerate(heads)],
                              axis=-1)
    o_rec = _head_out(o_intra + o_inter, g, g_onorm_ref[...])
    mix_ref[:, 0:d_rec] = o_rec.astype(BF16)

    u = gc * hv
    ubuf_ref[SUBLANES:SUBLANES + rows, :] = u
    tok = _group_pos((rows, d_conv), seq)
    u_m1 = jnp.where(tok >= 1, ubuf_ref[SUBLANES - 1:SUBLANES - 1 + rows, :], cbx_ref[1])
    u_m2 = jnp.where(tok >= 2, ubuf_ref[SUBLANES - 2:SUBLANES - 2 + rows, :],
                     jnp.where(tok == 1, cbx_ref[1], cbx_ref[0]))
    w_conv = w_conv_ref[...]
    y_conv = w_conv[0:1] * u_m2 + w_conv[1:2] * u_m1 + w_conv[2:3] * u
    mix_ref[:, d_rec:d_rec + d_conv] = (gb * y_conv).astype(BF16)
    for b in range(tb):
        conv_ref[b] = u[(b + 1) * seq - (CONV_W - 1):(b + 1) * seq]

    x1 = x + g1 * _bdot(mix_ref[...], w_out_ref[...])
    update_states(range(tb // 2))
    h2 = _modulate(x1, sh2, sc2)
    mlp = _mlp_down(_mlp_up(h2, w_up_ref, 0), w_up_ref, w_down_ref, 0)
    for j in range(1, MLP_BLOCKS):
        mlp += _mlp_down(_mlp_up(h2, w_up_ref, j), w_up_ref, w_down_ref, j)
    update_states(range(tb // 2, tb))
    y_ref[...] = _final_norm(x1, g2, mlp, g_final_ref)


def _sample_layer(x2d, seq, mod, lower_bounds, w_in, w_conv, g_onorm, w_out, w_up, w_down,
                  g_final, rec_in, conv_in):
    n_rows, d = x2d.shape
    bsz = n_rows // seq
    d_rec = lower_bounds.shape[1]
    dk = d_rec // H_REC
    d_conv = w_conv.shape[1]
    tb = SAMPLE_TILE_B
    rows = tb * seq
    assert bsz % tb == 0 and seq == SUBLANES and seq >= CONV_W
    return pl.pallas_call(
        functools.partial(_sample_kernel, seq=seq),
        grid=(bsz // tb,),
        in_specs=[
            pl.BlockSpec((rows, d), lambda i: (i, 0)),
            pl.BlockSpec((tb, N_MOD * d), lambda i: (i, 0)),
            _const_spec(lower_bounds.shape),
            _const_spec(w_in.shape),
            _const_spec(w_conv.shape),
            _const_spec(g_onorm.shape),
            _const_spec(w_out.shape),
            _const_spec(w_up.shape),
            _const_spec(w_down.shape),
            _const_spec(g_final.shape),
            pl.BlockSpec((tb, H_REC, dk, dk), lambda i: (i, 0, 0, 0)),
            pl.BlockSpec((tb, CONV_W - 1, d_conv), lambda i: (i, 0, 0)),
        ],
        out_specs=[
            pl.BlockSpec((rows, d), lambda i: (i, 0)),
            pl.BlockSpec((tb, H_REC, dk, dk), lambda i: (i, 0, 0, 0)),
            pl.BlockSpec((tb, CONV_W - 1, d_conv), lambda i: (i, 0, 0)),
        ],
        out_shape=[
            jax.ShapeDtypeStruct((n_rows, d), F32),
            jax.ShapeDtypeStruct((bsz, H_REC, dk, dk), F32),
            jax.ShapeDtypeStruct((bsz, CONV_W - 1, d_conv), F32),
        ],
        scratch_shapes=[
            pltpu.VMEM((rows, N_MOD * d), F32),
            pltpu.VMEM((CONV_W - 1, rows, d_conv), F32),
            pltpu.VMEM((SUBLANES + rows, d_conv), F32),
            pltpu.VMEM((rows, d_rec + d_conv), BF16),
        ],
        compiler_params=pltpu.CompilerParams(
            dimension_semantics=("arbitrary",), vmem_limit_bytes=VMEM_LIMIT_BYTES),
        name="sample_layer",
    )(x2d, mod, lower_bounds, w_in, w_conv, g_onorm, w_out, w_up, w_down, g_final, rec_in, conv_in)


def kernel(x_prompt, x_sample, state_rec, state_conv, c_prompt, c_sample, lower_bounds, w_ada,
           b_ada, w_in, w_conv, g_onorm, w_out, w_up, w_down, g_final):
    depth = w_in.shape[0]
    assert depth == 1, "single-layer trunk"
    bsz_s, seq_s, d = x_sample.shape

    mod_s, mod_p = _modulation(c_sample, c_prompt, w_ada[0], b_ada)

    g_final = g_final.reshape(1, d)
    y_p, rec_p, conv_p, w_in_b, w_out_b, w_up_b, w_down_b = _prompt_layer(
        x_prompt, mod_p, lower_bounds, w_in[0], w_conv[0], g_onorm, w_out[0], w_up[0], w_down[0],
        g_final)
    y_s, rec_s, conv_s = _sample_layer(
        x_sample.reshape(bsz_s * seq_s, d), seq_s, mod_s, lower_bounds, w_in_b, w_conv[0], g_onorm,
        w_out_b, w_up_b, w_down_b, g_final, state_rec[0], state_conv[0])
    return (y_p, y_s.reshape(bsz_s, seq_s, d), rec_p[None], conv_p[None], rec_s[None],
            conv_s[None])
```

```python
import functools

import jax
import jax.numpy as jnp
from jax import lax
from jax.experimental import pallas as pl
from jax.experimental.pallas import tpu as pltpu

F32 = jnp.float32
BF16 = jnp.bfloat16

EPS = 1e-6
H_REC = 4
CONV_W = 3
N_MOD = 6
N_PROJ_GROUPS = 7
CHUNK = 64

SUBLANES = 8
VMEM_LIMIT_BYTES = 56 * 1024 * 1024

PROMPT_TILE = 512
SAMPLE_TILE_B = 16
MOD_TILE_K = 256
MLP_BLOCKS = 4
WEIGHT_STAGE_SHAPE = (512, 512)
WEIGHT_STAGE_SLOTS = 4


def _const_spec(shape):
    zeros = (0,) * len(shape)
    return pl.BlockSpec(shape, lambda *_: zeros, pipeline_mode=pl.Buffered(1))


def _rms(x):
    return x * lax.rsqrt(jnp.mean(x * x, axis=-1, keepdims=True) + EPS)


def _silu(x):
    return x * jax.nn.sigmoid(x)


def _bdot(a, b):
    return jnp.dot(a, b, preferred_element_type=F32)


def _dot_nt(a, b):
    return lax.dot_general(a, b, (((1,), (1,)), ((), ())), preferred_element_type=F32)


def _dot_tn(a, b):
    return lax.dot_general(a, b, (((0,), (0,)), ((), ())), preferred_element_type=F32)


def _group_pos(shape, group):
    return lax.broadcasted_iota(jnp.int32, shape, 0) % group


def _tile_cumsum(x):
    rows, n = x.shape
    x = x.reshape(rows // SUBLANES, SUBLANES, n)
    pos = lax.broadcasted_iota(jnp.int32, x.shape, 1)
    step = 1
    while step < SUBLANES:
        x = x + jnp.where(pos >= step, pltpu.roll(x, step, axis=1), 0.0)
        step *= 2
    return x.reshape(rows, n)


def _tile_last(x):
    rows, n = x.shape
    x = x.reshape(rows // SUBLANES, SUBLANES, n)
    return jnp.broadcast_to(x[:, SUBLANES - 1:, :], x.shape).reshape(rows, n)


def _chain_tiles(y):
    tiles, carry = [], None
    for r in range(0, y.shape[0], SUBLANES):
        t = y[r:r + SUBLANES]
        if carry is not None:
            t = t + carry
        tiles.append(t)
        carry = t[SUBLANES - 1:, :]
    return jnp.concatenate(tiles, axis=0)


def _layer_lower_bound(lb_ref, layer):
    lb = lb_ref[...]
    e = jnp.exp(lb - jnp.max(lb, axis=0, keepdims=True))
    sm = e / jnp.sum(e, axis=0, keepdims=True)
    return jnp.sum(sm[: layer + 1], axis=0, keepdims=True)


def _gates(fz, lb):
    sig = jax.nn.sigmoid(fz)
    logf = jnp.log(lb + (1.0 - lb) * sig)
    k = (1.0 - lb) * (1.0 - sig)
    return logf, k


def _decay_terms(decay, row):
    hi = decay.astype(BF16).astype(F32)
    rest = decay - hi
    mid = rest.astype(BF16).astype(F32)
    lo = rest - mid
    return jnp.where(row == 0, hi, jnp.where(row == 1, mid, jnp.where(row == 2, lo, 0.0)))


def _decay_selector(n_rows, width):
    sub = lax.broadcasted_iota(jnp.int32, (n_rows, width), 0)
    return jnp.concatenate(
        [jnp.zeros((n_rows, width), F32), jnp.where(sub < 3, 1.0, 0.0)], axis=1)


def _head_out(o, g, g_onorm):
    dv = o.shape[-1] // H_REC
    pieces = []
    for h in range(H_REC):
        sl = slice(h * dv, (h + 1) * dv)
        pieces.append(_rms(o[:, sl]))
    return (jnp.concatenate(pieces, axis=-1) * g_onorm) * _silu(g)


def _modulate(x, shift, scale):
    return (_rms(x) * (1.0 + scale) + shift).astype(BF16)


def _mlp_cols(w_up_ref, j):
    blk = w_up_ref.shape[1] // MLP_BLOCKS
    return slice(j * blk, (j + 1) * blk)


def _mlp_up(h2, w_up_ref, j):
    up = jnp.maximum(_bdot(h2, w_up_ref[:, _mlp_cols(w_up_ref, j)]), 0.0)
    return (up * up).astype(BF16)


def _mlp_down(up, w_up_ref, w_down_ref, j):
    return _bdot(up, w_down_ref[_mlp_cols(w_up_ref, j), :])


def _final_norm(x, g2, mlp, g_final_ref):
    return _rms(x + g2 * mlp) * g_final_ref[...]


def _mod_kernel(c_s_ref, c_p_ref, w_ref, b_ref, o_s_ref, o_p_ref):
    k = pl.program_id(0)
    n_s = c_s_ref.shape[0]
    kb = w_ref.shape[0]
    cols = pl.ds(pl.multiple_of(k * kb, kb), kb)
    c = jnp.concatenate([c_s_ref[:, cols], c_p_ref[:, cols]], axis=0)
    part = _bdot(_silu(c).astype(BF16), w_ref[...].astype(BF16))

    @pl.when(k == 0)
    def _():
        o_s_ref[...] = part[:n_s] + b_ref[...]
        o_p_ref[...] = part[n_s:] + b_ref[...]

    @pl.when(k > 0)
    def _():
        o_s_ref[...] += part[:n_s]
        o_p_ref[...] += part[n_s:]


def _modulation(c_s, c_p, w_ada, b_ada):
    (n_s, d), n_p = c_s.shape, c_p.shape[0]
    n = w_ada.shape[1]
    assert n_s % SUBLANES == 0 and n_p % SUBLANES == 0 and d % MOD_TILE_K == 0
    return pl.pallas_call(
        _mod_kernel,
        grid=(d // MOD_TILE_K,),
        in_specs=[
            pl.BlockSpec((n_s, d), lambda k: (0, 0)),
            pl.BlockSpec((n_p, d), lambda k: (0, 0)),
            pl.BlockSpec((MOD_TILE_K, n), lambda k: (k, 0)),
            pl.BlockSpec((1, n), lambda k: (0, 0)),
        ],
        out_specs=[
            pl.BlockSpec((n_s, n), lambda k: (0, 0)),
            pl.BlockSpec((n_p, n), lambda k: (0, 0)),
        ],
        out_shape=[
            jax.ShapeDtypeStruct((n_s, n), F32),
            jax.ShapeDtypeStruct((n_p, n), F32),
        ],
        compiler_params=pltpu.CompilerParams(
            dimension_semantics=("arbitrary",), vmem_limit_bytes=VMEM_LIMIT_BYTES),
        name="adaln_modulation",
    )(c_s, c_p, w_ada, b_ada)


def _split_proj(proj):
    width = proj.shape[1] // N_PROJ_GROUPS
    return tuple(proj[:, p * width:(p + 1) * width] for p in range(N_PROJ_GROUPS))


def _weight_blocks(w_hbm, w_vmem):
    rows, cols = w_hbm.shape
    br, bc = WEIGHT_STAGE_SHAPE
    assert rows % br == 0 and cols % bc == 0
    return [(w_hbm.at[r:r + br, c:c + bc], w_vmem.at[r:r + br, c:c + bc])
            for r in range(0, rows, br) for c in range(0, cols, bc)]


def _stage_copy(src, stage_ref, sem_ref, i):
    slot = i % WEIGHT_STAGE_SLOTS
    return pltpu.make_async_copy(src, stage_ref.at[slot], sem_ref.at[slot])


def _load_weights_as_bf16(w_hbm_refs, w_vmem_refs, stage_ref, sem_ref):
    blocks = [blk for w_hbm, w_vmem in zip(w_hbm_refs, w_vmem_refs)
              for blk in _weight_blocks(w_hbm, w_vmem)]
    ahead = WEIGHT_STAGE_SLOTS - 1
    for i in range(min(ahead, len(blocks))):
        _stage_copy(blocks[i][0], stage_ref, sem_ref, i).start()
    for i, (src, dst) in enumerate(blocks):
        if i + ahead < len(blocks):
            _stage_copy(blocks[i + ahead][0], stage_ref, sem_ref, i + ahead).start()
        _stage_copy(src, stage_ref, sem_ref, i).wait()
        dst[...] = stage_ref[i % WEIGHT_STAGE_SLOTS].astype(BF16)


def _alternate(*stages):
    stages = list(stages)
    while stages:
        for stage in list(stages):
            try:
                next(stage)
            except StopIteration:
                stages.remove(stage)


def _mlp_stage(x1_ref, mod, w_up_ref, w_down_ref, g_final_ref, y_ref):
    d = x1_ref.shape[1]
    sh2, sc2, g2 = (mod[:, i * d:(i + 1) * d] for i in range(3, N_MOD))
    h2 = _modulate(x1_ref[...], sh2, sc2)
    ups = [_mlp_up(h2, w_up_ref, 0)]
    yield
    mlp = None
    for j in range(MLP_BLOCKS):
        if j + 1 < MLP_BLOCKS:
            ups.append(_mlp_up(h2, w_up_ref, j + 1))
        part = _mlp_down(ups[j], w_up_ref, w_down_ref, j)
        mlp = part if mlp is None else mlp + part
        if j + 2 < MLP_BLOCKS:
            yield
    y_ref[...] = _final_norm(x1_ref[...], g2, mlp, g_final_ref)
    yield


def _prompt_mix_stage(x_ref, mod, lb_ref, w_in_ref, w_conv_ref, g_onorm_ref, w_out_ref,
                      st_ref, ubuf_ref, mix_ref, x1_ref, rec_ref, conv_ref, is_last):
    tl, d = x_ref.shape
    d_rec = lb_ref.shape[1]
    dk = d_rec // H_REC
    n_chunks = tl // CHUNK
    heads = [slice(hd * dk, (hd + 1) * dk) for hd in range(H_REC)]
    sh1, sc1, g1 = (mod[:, i * d:(i + 1) * d] for i in range(3))

    x = x_ref[...]
    proj = _bdot(_modulate(x, sh1, sc1), w_in_ref[...])
    q, fz, iv, g, gb, gc, hv = _split_proj(proj)
    d_conv = gb.shape[1]
    yield

    lb = _layer_lower_bound(lb_ref, 0)
    row = lax.broadcasted_iota(jnp.int32, (CHUNK, CHUNK), 0)
    col = lax.broadcasted_iota(jnp.int32, (CHUNK, CHUNK), 1)
    causal = row >= col
    logf, k_all = _gates(fz, lb)
    cum_tiles = _tile_cumsum(logf)
    q_dec, decay, scores, upd = [], [], [], []
    for c in range(n_chunks):
        rows = slice(c * CHUNK, (c + 1) * CHUNK)
        cum = _chain_tiles(cum_tiles[rows])
        last = cum[CHUNK - 1:CHUNK, :]
        q_dec.append((q[rows] * jnp.exp(cum)).astype(BF16))
        k_dec = (k_all[rows] * jnp.exp(-cum)).astype(BF16)
        k_end = (k_all[rows] * jnp.exp(last - cum)).astype(BF16)
        dec = jnp.exp(last)
        decay.append([jnp.broadcast_to(dec[:, sl], (dk, dk)).T for sl in heads])
        v = iv[rows].astype(BF16)
        scores.append([jnp.where(causal, _dot_nt(q_dec[c][:, sl], k_dec[:, sl]), 0.0).astype(BF16)
                       for sl in heads])
        upd.append([_dot_tn(k_end[:, sl], v[:, sl]) for sl in heads])
    yield

    st = [st_ref[hd] for hd in range(H_REC)]
    o_chunks = []
    for c in range(n_chunks):
        v = iv[c * CHUNK:(c + 1) * CHUNK].astype(BF16)
        o_heads = []
        for hd, sl in enumerate(heads):
            o_heads.append(_bdot(scores[c][hd], v[:, sl])
                           + _bdot(q_dec[c][:, sl], st[hd].astype(BF16)))
            st[hd] = st[hd] * decay[c][hd] + upd[c][hd]
        o_chunks.append(jnp.concatenate(o_heads, axis=-1))
    for hd in range(H_REC):
        st_ref[hd] = st[hd]
    yield

    o_rec = _head_out(jnp.concatenate(o_chunks, axis=0), g, g_onorm_ref[...])
    mix_ref[:, 0:d_rec] = o_rec.astype(BF16)

    u = gc * hv
    ubuf_ref[SUBLANES:SUBLANES + tl, :] = u
    w_conv = w_conv_ref[...]
    y_conv = (w_conv[0:1] * ubuf_ref[SUBLANES - 2:SUBLANES - 2 + tl, :]
              + w_conv[1:2] * ubuf_ref[SUBLANES - 1:SUBLANES - 1 + tl, :]
              + w_conv[2:3] * u)
    mix_ref[:, d_rec:d_rec + d_conv] = (gb * y_conv).astype(BF16)
    tail = u[tl - (CONV_W - 1):tl]
    ubuf_ref[SUBLANES - (CONV_W - 1):SUBLANES, :] = tail
    yield

    x1 = x + g1 * _bdot(mix_ref[...], w_out_ref[...])
    x1_ref[...] = x1

    @pl.when(is_last)
    def _():
        for hd in range(H_REC):
            rec_ref[hd] = st_ref[hd]
        conv_ref[...] = tail


def _prompt_kernel(x_ref, mod_ref, lb_ref, w_in_hbm, w_conv_ref, g_onorm_ref, w_out_hbm,
                   w_up_hbm, w_down_hbm, g_final_ref,
                   y_ref, rec_ref, conv_ref, w_in_out, w_out_out, w_up_out, w_down_out,
                   st_ref, ubuf_ref, mix_ref, x1_ref,
                   w_in_ref, w_out_ref, w_up_ref, w_down_ref, stage_ref, stage_sem, out_sem,
                   *, n_tiles, n_steps):
    s = pl.program_id(0)
    tile = jnp.minimum(s, n_steps - 1)
    l = lax.rem(tile, n_tiles)
    d = x_ref.shape[1]

    w_vmem = (w_in_ref, w_out_ref, w_up_ref, w_down_ref)
    w_outs = (w_in_out, w_out_out, w_up_out, w_down_out)

    def bf16_export(i):
        return pltpu.make_async_copy(w_vmem[i], w_outs[i], out_sem.at[i])

    @pl.when(s == 0)
    def _():
        _load_weights_as_bf16((w_in_hbm, w_out_hbm, w_up_hbm, w_down_hbm), w_vmem,
                              stage_ref, stage_sem)
        for i in range(len(w_vmem)):
            bf16_export(i).start()

    @pl.when(s == n_steps)
    def _():
        for i in range(len(w_vmem)):
            bf16_export(i).wait()

    @pl.when(l == 0)
    def _():
        st_ref[...] = jnp.zeros_like(st_ref)
        ubuf_ref[0:SUBLANES, :] = jnp.zeros((SUBLANES, ubuf_ref.shape[1]), F32)

    def mix_stage():
        mod = mod_ref[pl.ds(lax.div(tile, n_tiles), 1), :]
        return _prompt_mix_stage(x_ref, mod, lb_ref, w_in_ref, w_conv_ref, g_onorm_ref,
                                 w_out_ref, st_ref, ubuf_ref, mix_ref, x1_ref,
                                 rec_ref, conv_ref, l == n_tiles - 1)

    def mlp_stage():
        mod_prev = mod_ref[pl.ds(lax.div(s - 1, n_tiles), 1), :]
        return _mlp_stage(x1_ref, mod_prev, w_up_ref, w_down_ref, g_final_ref, y_ref)

    @pl.when(s == 0)
    def _():
        _alternate(mix_stage())

    @pl.when((s > 0) & (s < n_steps))
    def _():
        _alternate(mix_stage(), mlp_stage())

    @pl.when(s == n_steps)
    def _():
        _alternate(mlp_stage())


def _prompt_layer(x, mod_p, lower_bounds, w_in, w_conv, g_onorm, w_out, w_up, w_down, g_final):
    bsz, seq, d = x.shape
    d_rec = lower_bounds.shape[1]
    dk = d_rec // H_REC
    d_conv = w_conv.shape[1]
    tl = PROMPT_TILE
    n_tiles = seq // tl
    n_steps = bsz * n_tiles
    assert seq % tl == 0 and tl % CHUNK == 0
    mats = (w_in, w_out, w_up, w_down)
    hbm = pl.BlockSpec(memory_space=pl.ANY)

    def mix_tile(s):
        t = jnp.minimum(s, n_steps - 1)
        return lax.div(t, n_tiles), lax.rem(t, n_tiles)

    def mlp_tile(s):
        t = jnp.maximum(s - 1, 0)
        return lax.div(t, n_tiles), lax.rem(t, n_tiles)

    return pl.pallas_call(
        functools.partial(_prompt_kernel, n_tiles=n_tiles, n_steps=n_steps),
        grid=(n_steps + 1,),
        in_specs=[
            pl.BlockSpec((None, tl, d), lambda s: (*mix_tile(s), 0)),
            _const_spec(mod_p.shape),
            _const_spec(lower_bounds.shape),
            hbm,
            _const_spec(w_conv.shape),
            _const_spec(g_onorm.shape),
            hbm,
            hbm,
            hbm,
            _const_spec(g_final.shape),
        ],
        out_specs=[
            pl.BlockSpec((None, tl, d), lambda s: (*mlp_tile(s), 0)),
            pl.BlockSpec((None, H_REC, dk, dk), lambda s: (mix_tile(s)[0], 0, 0, 0)),
            pl.BlockSpec((None, CONV_W - 1, d_conv), lambda s: (mix_tile(s)[0], 0, 0)),
        ] + [hbm] * len(mats),
        out_shape=[
            jax.ShapeDtypeStruct((bsz, seq, d), F32),
            jax.ShapeDtypeStruct((bsz, H_REC, dk, dk), F32),
            jax.ShapeDtypeStruct((bsz, CONV_W - 1, d_conv), F32),
        ] + [jax.ShapeDtypeStruct(w.shape, BF16) for w in mats],
        scratch_shapes=[
            pltpu.VMEM((H_REC, dk, dk), F32),
            pltpu.VMEM((SUBLANES + tl, d_conv), F32),
            pltpu.VMEM((tl, d_rec + d_conv), BF16),
            pltpu.VMEM((tl, d), F32),
        ] + [pltpu.VMEM(w.shape, BF16) for w in mats] + [
            pltpu.VMEM((WEIGHT_STAGE_SLOTS, *WEIGHT_STAGE_SHAPE), F32),
            pltpu.SemaphoreType.DMA((WEIGHT_STAGE_SLOTS,)),
            pltpu.SemaphoreType.DMA((len(mats),)),
        ],
        compiler_params=pltpu.CompilerParams(
            dimension_semantics=("arbitrary",), vmem_limit_bytes=VMEM_LIMIT_BYTES),
        name="prompt_layer",
    )(x, mod_p, lower_bounds, w_in, w_conv, g_onorm, w_out, w_up, w_down, g_final)


def _sample_kernel(x_ref, mod_ref, lb_ref, w_in_ref, w_conv_ref, g_onorm_ref, w_out_ref,
                   w_up_ref, w_down_ref, g_final_ref, rec_in_ref, conv_in_ref,
                   y_ref, rec_ref, conv_ref,
                   modx_ref, cbx_ref, ubuf_ref, mix_ref, *, seq):
    rows, d = x_ref.shape
    tb = rows // seq
    d_rec = lb_ref.shape[1]
    dk = d_rec // H_REC
    d_conv = conv_in_ref.shape[2]

    for b in range(tb):
        r = slice(b * seq, (b + 1) * seq)
        modx_ref[r, :] = jnp.broadcast_to(mod_ref[b:b + 1, :], (seq, mod_ref.shape[1]))
        for j in range(CONV_W - 1):
            cbx_ref[j, r, :] = jnp.broadcast_to(conv_in_ref[b, j:j + 1, :], (seq, d_conv))

    sh1, sc1, g1, sh2, sc2, g2 = (modx_ref[:, i * d:(i + 1) * d] for i in range(N_MOD))

    x = x_ref[...]
    proj = _bdot(_modulate(x, sh1, sc1), w_in_ref[...])
    q, fz, iv, g, gb, gc, hv = _split_proj(proj)

    lb = _layer_lower_bound(lb_ref, 0)
    row = lax.broadcasted_iota(jnp.int32, (rows, rows), 0)
    col = lax.broadcasted_iota(jnp.int32, (rows, rows), 1)
    causal = ((row // seq) == (col // seq)) & (row >= col)
    logf, k = _gates(fz, lb)
    cum = _tile_cumsum(logf)
    last = _tile_last(cum)
    q_dec = (q * jnp.exp(cum)).astype(BF16)
    k_dec = (k * jnp.exp(-cum)).astype(BF16)
    k_end = k * jnp.exp(last - cum)
    decay = jnp.exp(last)
    vb = iv.astype(BF16)

    dec3 = _decay_terms(decay, _group_pos((rows, d_rec), seq))
    zeros_blk = jnp.zeros((seq, dk), F32)
    rhs_bottom = _decay_selector(seq, dk)

    heads = [slice(hd * dk, (hd + 1) * dk) for hd in range(H_REC)]

    def update_states(elements):
        for b in elements:
            r = slice(b * seq, (b + 1) * seq)
            for hd, sl in enumerate(heads):
                lhs = jnp.concatenate([k_end[r, sl], dec3[r, sl]], axis=0)
                rhs = jnp.concatenate(
                    [jnp.concatenate([iv[r, sl], zeros_blk], axis=1), rhs_bottom], axis=0)
                both = _dot_tn(lhs.astype(BF16), rhs.astype(BF16))
                rec_ref[b, hd] = both[:, dk:] * rec_in_ref[b, hd] + both[:, :dk]

    scores = [jnp.where(causal, _dot_nt(q_dec[:, sl], k_dec[:, sl]), 0.0).astype(BF16)
              for sl in heads]
    o_inter = jnp.concatenate(
        [jnp.concatenate([_bdot(q_dec[b * seq:(b + 1) * seq, sl], rec_in_ref[b, hd].astype(BF16))
                          for hd, sl in enumerate(heads)], axis=-1)
         for b in range(tb)], axis=0)
    o_intra = jnp.concatenate([_bdot(scores[hd], vb[:, sl]) for hd, sl in enumerate(heads)],
                              axis=-1)
    o_rec = _head_out(o_intra + o_inter, g, g_onorm_ref[...])
    mix_ref[:, 0:d_rec] = o_rec.astype(BF16)

    u = gc * hv
    ubuf_ref[SUBLANES:SUBLANES + rows, :] = u
    tok = _group_pos((rows, d_conv), seq)
    u_m1 = jnp.where(tok >= 1, ubuf_ref[SUBLANES - 1:SUBLANES - 1 + rows, :], cbx_ref[1])
    u_m2 = jnp.where(tok >= 2, ubuf_ref[SUBLANES - 2:SUBLANES - 2 + rows, :],
                     jnp.where(tok == 1, cbx_ref[1], cbx_ref[0]))
    w_conv = w_conv_ref[...]
    y_conv = w_conv[0:1] * u_m2 + w_conv[1:2] * u_m1 + w_conv[2:3] * u
    mix_ref[:, d_rec:d_rec + d_conv] = (gb * y_conv).astype(BF16)
    for b in range(tb):
        conv_ref[b] = u[(b + 1) * seq - (CONV_W - 1):(b + 1) * seq]

    x1 = x + g1 * _bdot(mix_ref[...], w_out_ref[...])
    update_states(range(tb // 2))
    h2 = _modulate(x1, sh2, sc2)
    mlp = _mlp_down(_mlp_up(h2, w_up_ref, 0), w_up_ref, w_down_ref, 0)
    for j in range(1, MLP_BLOCKS):
        mlp += _mlp_down(_mlp_up(h2, w_up_ref, j), w_up_ref, w_down_ref, j)
    update_states(range(tb // 2, tb))
    y_ref[...] = _final_norm(x1, g2, mlp, g_final_ref)


def _sample_layer(x2d, seq, mod, lower_bounds, w_in, w_conv, g_onorm, w_out, w_up, w_down,
                  g_final, rec_in, conv_in):
    n_rows, d = x2d.shape
    bsz = n_rows // seq
    d_rec = lower_bounds.shape[1]
    dk = d_rec // H_REC
    d_conv = w_conv.shape[1]
    tb = SAMPLE_TILE_B
    rows = tb * seq
    assert bsz % tb == 0 and seq == SUBLANES and seq >= CONV_W
    return pl.pallas_call(
        functools.partial(_sample_kernel, seq=seq),
        grid=(bsz // tb,),
        in_specs=[
            pl.BlockSpec((rows, d), lambda i: (i, 0)),
            pl.BlockSpec((tb, N_MOD * d), lambda i: (i, 0)),
            _const_spec(lower_bounds.shape),
            _const_spec(w_in.shape),
            _const_spec(w_conv.shape),
            _const_spec(g_onorm.shape),
            _const_spec(w_out.shape),
            _const_spec(w_up.shape),
            _const_spec(w_down.shape),
            _const_spec(g_final.shape),
            pl.BlockSpec((tb, H_REC, dk, dk), lambda i: (i, 0, 0, 0)),
            pl.BlockSpec((tb, CONV_W - 1, d_conv), lambda i: (i, 0, 0)),
        ],
        out_specs=[
            pl.BlockSpec((rows, d), lambda i: (i, 0)),
            pl.BlockSpec((tb, H_REC, dk, dk), lambda i: (i, 0, 0, 0)),
            pl.BlockSpec((tb, CONV_W - 1, d_conv), lambda i: (i, 0, 0)),
        ],
        out_shape=[
            jax.ShapeDtypeStruct((n_rows, d), F32),
            jax.ShapeDtypeStruct((bsz, H_REC, dk, dk), F32),
            jax.ShapeDtypeStruct((bsz, CONV_W - 1, d_conv), F32),
        ],
        scratch_shapes=[
            pltpu.VMEM((rows, N_MOD * d), F32),
            pltpu.VMEM((CONV_W - 1, rows, d_conv), F32),
            pltpu.VMEM((SUBLANES + rows, d_conv), F32),
            pltpu.VMEM((rows, d_rec + d_conv), BF16),
        ],
        compiler_params=pltpu.CompilerParams(
            dimension_semantics=("arbitrary",), vmem_limit_bytes=VMEM_LIMIT_BYTES),
        name="sample_layer",
    )(x2d, mod, lower_bounds, w_in, w_conv, g_onorm, w_out, w_up, w_down, g_final, rec_in, conv_in)


def kernel(x_prompt, x_sample, state_rec, state_conv, c_prompt, c_sample, lower_bounds, w_ada,
           b_ada, w_in, w_conv, g_onorm, w_out, w_up, w_down, g_final):
    depth = w_in.shape[0]
    assert depth == 1, "single-layer trunk"
    bsz_s, seq_s, d = x_sample.shape

    mod_s, mod_p = _modulation(c_sample, c_prompt, w_ada[0], b_ada)

    g_final = g_final.reshape(1, d)
    y_p, rec_p, conv_p, w_in_b, w_out_b, w_up_b, w_down_b = _prompt_layer(
        x_prompt, mod_p, lower_bounds, w_in[0], w_conv[0], g_onorm, w_out[0], w_up[0], w_down[0],
        g_final)
    y_s, rec_s, conv_s = _sample_layer(
        x_sample.reshape(bsz_s * seq_s, d), seq_s, mod_s, lower_bounds, w_in_b, w_conv[0], g_onorm,
        w_out_b, w_up_b, w_down_b, g_final, state_rec[0], state_conv[0])
    return (y_p, y_s.reshape(bsz_s, seq_s, d), rec_p[None], conv_p[None], rec_s[None],
            conv_s[None])
```

```python
import functools

import jax
import jax.numpy as jnp
from jax import lax
from jax.experimental import pallas as pl
from jax.experimental.pallas import tpu as pltpu

F32 = jnp.float32
BF16 = jnp.bfloat16

EPS = 1e-6
LOG2_E = 1.4426950408889634
H_REC = 4
CONV_W = 3
N_MOD = 6
N_PROJ_GROUPS = 7
CHUNK = 64

SUBLANES = 8
VMEM_LIMIT_BYTES = 56 * 1024 * 1024

PROMPT_TILE = 512
SAMPLE_TILE_B = 16
MOD_TILE_K = 256
MLP_BLOCKS = 4
WEIGHT_STAGE_SHAPE = (512, 512)
WEIGHT_STAGE_SLOTS = 4


def _const_spec(shape):
    zeros = (0,) * len(shape)
    return pl.BlockSpec(shape, lambda *_: zeros, pipeline_mode=pl.Buffered(1))


def _rms(x):
    return x * lax.rsqrt(jnp.mean(x * x, axis=-1, keepdims=True) + EPS)


def _silu(x):
    return x * jax.nn.sigmoid(x)


def _bdot(a, b):
    return jnp.dot(a, b, preferred_element_type=F32)


def _dot_nt(a, b):
    return lax.dot_general(a, b, (((1,), (1,)), ((), ())), preferred_element_type=F32)


def _dot_tn(a, b):
    return lax.dot_general(a, b, (((0,), (0,)), ((), ())), preferred_element_type=F32)


def _group_pos(shape, group):
    return lax.broadcasted_iota(jnp.int32, shape, 0) % group


def _tile_cumsum(x):
    rows, n = x.shape
    x = x.reshape(rows // SUBLANES, SUBLANES, n)
    pos = lax.broadcasted_iota(jnp.int32, x.shape, 1)
    step = 1
    while step < SUBLANES:
        x = x + jnp.where(pos >= step, pltpu.roll(x, step, axis=1), 0.0)
        step *= 2
    return x.reshape(rows, n)


def _tile_last(x):
    rows, n = x.shape
    x = x.reshape(rows // SUBLANES, SUBLANES, n)
    return jnp.broadcast_to(x[:, SUBLANES - 1:, :], x.shape).reshape(rows, n)


def _chain_tiles(y):
    tiles, carry = [], None
    for r in range(0, y.shape[0], SUBLANES):
        t = y[r:r + SUBLANES]
        if carry is not None:
            t = t + carry
        tiles.append(t)
        carry = t[SUBLANES - 1:, :]
    return jnp.concatenate(tiles, axis=0)


def _layer_lower_bound(lb_ref, layer):
    lb = lb_ref[...]
    e = jnp.exp(lb - jnp.max(lb, axis=0, keepdims=True))
    sm = e / jnp.sum(e, axis=0, keepdims=True)
    return jnp.sum(sm[: layer + 1], axis=0, keepdims=True)


def _gates(fz, lb):
    sig = jax.nn.sigmoid(fz)
    log2f = jnp.log(lb + (1.0 - lb) * sig) * LOG2_E
    k = (1.0 - lb) * (1.0 - sig)
    return log2f, k


def _decay_terms(decay, row):
    hi = decay.astype(BF16).astype(F32)
    rest = decay - hi
    mid = rest.astype(BF16).astype(F32)
    lo = rest - mid
    return jnp.where(row == 0, hi, jnp.where(row == 1, mid, jnp.where(row == 2, lo, 0.0)))


def _decay_selector(n_rows, width):
    sub = lax.broadcasted_iota(jnp.int32, (n_rows, width), 0)
    return jnp.concatenate(
        [jnp.zeros((n_rows, width), F32), jnp.where(sub < 3, 1.0, 0.0)], axis=1)


def _head_out(o, g, g_onorm):
    dv = o.shape[-1] // H_REC
    pieces = []
    for h in range(H_REC):
        sl = slice(h * dv, (h + 1) * dv)
        pieces.append(_rms(o[:, sl]))
    return (jnp.concatenate(pieces, axis=-1) * g_onorm) * _silu(g)


def _modulate(x, shift, scale):
    return _rms(x).astype(BF16) * (1.0 + scale).astype(BF16) + shift.astype(BF16)


def _mlp_cols(w_up_ref, j):
    blk = w_up_ref.shape[1] // MLP_BLOCKS
    return slice(j * blk, (j + 1) * blk)


def _mlp_up(h2, w_up_ref, j):
    up = jnp.maximum(_bdot(h2, w_up_ref[:, _mlp_cols(w_up_ref, j)]).astype(BF16), 0.0)
    return up * up


def _mlp_down(up, w_up_ref, w_down_ref, j):
    return _bdot(up, w_down_ref[_mlp_cols(w_up_ref, j), :])


def _final_norm(x, g2, mlp, g_final_ref):
    return _rms(x + g2 * mlp) * g_final_ref[...]


def _mod_kernel(c_s_ref, c_p_ref, w_ref, b_ref, o_s_ref, o_p_ref):
    k = pl.program_id(0)
    n_s = c_s_ref.shape[0]
    kb = w_ref.shape[0]
    cols = pl.ds(pl.multiple_of(k * kb, kb), kb)
    c = jnp.concatenate([c_s_ref[:, cols], c_p_ref[:, cols]], axis=0)
    part = _bdot(_silu(c).astype(BF16), w_ref[...].astype(BF16))

    @pl.when(k == 0)
    def _():
        o_s_ref[...] = part[:n_s] + b_ref[...]
        o_p_ref[...] = part[n_s:] + b_ref[...]

    @pl.when(k > 0)
    def _():
        o_s_ref[...] += part[:n_s]
        o_p_ref[...] += part[n_s:]


def _modulation(c_s, c_p, w_ada, b_ada):
    (n_s, d), n_p = c_s.shape, c_p.shape[0]
    n = w_ada.shape[1]
    assert n_s % SUBLANES == 0 and n_p % SUBLANES == 0 and d % MOD_TILE_K == 0
    return pl.pallas_call(
        _mod_kernel,
        grid=(d // MOD_TILE_K,),
        in_specs=[
            pl.BlockSpec((n_s, d), lambda k: (0, 0)),
            pl.BlockSpec((n_p, d), lambda k: (0, 0)),
            pl.BlockSpec((MOD_TILE_K, n), lambda k: (k, 0)),
            pl.BlockSpec((1, n), lambda k: (0, 0)),
        ],
        out_specs=[
            pl.BlockSpec((n_s, n), lambda k: (0, 0)),
            pl.BlockSpec((n_p, n), lambda k: (0, 0)),
        ],
        out_shape=[
            jax.ShapeDtypeStruct((n_s, n), F32),
            jax.ShapeDtypeStruct((n_p, n), F32),
        ],
        compiler_params=pltpu.CompilerParams(
            dimension_semantics=("arbitrary",), vmem_limit_bytes=VMEM_LIMIT_BYTES),
        name="adaln_modulation",
    )(c_s, c_p, w_ada, b_ada)


def _split_proj(proj):
    width = proj.shape[1] // N_PROJ_GROUPS
    return tuple(proj[:, p * width:(p + 1) * width] for p in range(N_PROJ_GROUPS))


def _weight_blocks(w_hbm, w_vmem):
    rows, cols = w_hbm.shape
    br, bc = WEIGHT_STAGE_SHAPE
    assert rows % br == 0 and cols % bc == 0
    return [(w_hbm.at[r:r + br, c:c + bc], w_vmem.at[r:r + br, c:c + bc])
            for r in range(0, rows, br) for c in range(0, cols, bc)]


def _stage_copy(src, stage_ref, sem_ref, i):
    slot = i % WEIGHT_STAGE_SLOTS
    return pltpu.make_async_copy(src, stage_ref.at[slot], sem_ref.at[slot])


def _load_weights_as_bf16(w_hbm_refs, w_vmem_refs, stage_ref, sem_ref):
    blocks = [blk for w_hbm, w_vmem in zip(w_hbm_refs, w_vmem_refs)
              for blk in _weight_blocks(w_hbm, w_vmem)]
    ahead = WEIGHT_STAGE_SLOTS - 1
    for i in range(min(ahead, len(blocks))):
        _stage_copy(blocks[i][0], stage_ref, sem_ref, i).start()
    for i, (src, dst) in enumerate(blocks):
        if i + ahead < len(blocks):
            _stage_copy(blocks[i + ahead][0], stage_ref, sem_ref, i + ahead).start()
        _stage_copy(src, stage_ref, sem_ref, i).wait()
        dst[...] = stage_ref[i % WEIGHT_STAGE_SLOTS].astype(BF16)


def _alternate(*stages):
    stages = list(stages)
    while stages:
        for stage in list(stages):
            try:
                next(stage)
            except StopIteration:
                stages.remove(stage)


def _mlp_stage(x1_ref, mod, w_up_ref, w_down_ref, g_final_ref, y_ref):
    d = x1_ref.shape[1]
    sh2, sc2, g2 = (mod[:, i * d:(i + 1) * d] for i in range(3, N_MOD))
    h2 = _modulate(x1_ref[...], sh2, sc2)
    ups = [_mlp_up(h2, w_up_ref, 0)]
    yield
    mlp = None
    for j in range(MLP_BLOCKS):
        if j + 1 < MLP_BLOCKS:
            ups.append(_mlp_up(h2, w_up_ref, j + 1))
        part = _mlp_down(ups[j], w_up_ref, w_down_ref, j)
        mlp = part if mlp is None else mlp + part
        if j + 2 < MLP_BLOCKS:
            yield
    y_ref[...] = _final_norm(x1_ref[...], g2, mlp, g_final_ref)
    yield


def _prompt_mix_stage(x_ref, mod, lb_ref, w_in_ref, w_conv_ref, g_onorm_ref, w_out_ref,
                      st_ref, ubuf_ref, mix_ref, x1_ref, rec_ref, conv_ref, is_last):
    tl, d = x_ref.shape
    d_rec = lb_ref.shape[1]
    dk = d_rec // H_REC
    n_chunks = tl // CHUNK
    heads = [slice(hd * dk, (hd + 1) * dk) for hd in range(H_REC)]
    sh1, sc1, g1 = (mod[:, i * d:(i + 1) * d] for i in range(3))

    x = x_ref[...]
    proj = _bdot(_modulate(x, sh1, sc1), w_in_ref[...])
    q, fz, iv, g, gb, gc, hv = _split_proj(proj)
    d_conv = gb.shape[1]
    yield

    lb = _layer_lower_bound(lb_ref, 0)
    row = lax.broadcasted_iota(jnp.int32, (CHUNK, CHUNK), 0)
    col = lax.broadcasted_iota(jnp.int32, (CHUNK, CHUNK), 1)
    causal = row >= col
    log2f, k_all = _gates(fz, lb)
    cum_tiles = _tile_cumsum(log2f)
    q_dec, decay, scores, upd = [], [], [], []
    for c in range(n_chunks):
        rows = slice(c * CHUNK, (c + 1) * CHUNK)
        cum = _chain_tiles(cum_tiles[rows])
        last = cum[CHUNK - 1:CHUNK, :]
        q_dec.append((q[rows] * jnp.exp2(cum)).astype(BF16))
        k_dec = (k_all[rows] * jnp.exp2(-cum)).astype(BF16)
        k_end = (k_all[rows] * jnp.exp2(last - cum)).astype(BF16)
        dec = jnp.exp2(last)
        decay.append([jnp.broadcast_to(dec[:, sl], (dk, dk)).T for sl in heads])
        v = iv[rows].astype(BF16)
        scores.append([jnp.where(causal, _dot_nt(q_dec[c][:, sl], k_dec[:, sl]), 0.0).astype(BF16)
                       for sl in heads])
        upd.append([_dot_tn(k_end[:, sl], v[:, sl]) for sl in heads])
    yield

    st = [st_ref[hd] for hd in range(H_REC)]
    o_chunks = []
    for c in range(n_chunks):
        v = iv[c * CHUNK:(c + 1) * CHUNK].astype(BF16)
        o_heads = []
        for hd, sl in enumerate(heads):
            o_heads.append(_bdot(scores[c][hd], v[:, sl])
                           + _bdot(q_dec[c][:, sl], st[hd].astype(BF16)))
            st[hd] = st[hd] * decay[c][hd] + upd[c][hd]
        o_chunks.append(jnp.concatenate(o_heads, axis=-1))
    for hd in range(H_REC):
        st_ref[hd] = st[hd]
    yield

    o_rec = _head_out(jnp.concatenate(o_chunks, axis=0), g, g_onorm_ref[...])
    mix_ref[:, 0:d_rec] = o_rec.astype(BF16)

    u = gc * hv
    ubuf_ref[SUBLANES:SUBLANES + tl, :] = u
    w_conv = w_conv_ref[...]
    y_conv = (w_conv[0:1] * ubuf_ref[SUBLANES - 2:SUBLANES - 2 + tl, :]
              + w_conv[1:2] * ubuf_ref[SUBLANES - 1:SUBLANES - 1 + tl, :]
              + w_conv[2:3] * u)
    mix_ref[:, d_rec:d_rec + d_conv] = (gb * y_conv).astype(BF16)
    tail = u[tl - (CONV_W - 1):tl]
    ubuf_ref[SUBLANES - (CONV_W - 1):SUBLANES, :] = tail
    yield

    x1 = x + g1 * _bdot(mix_ref[...], w_out_ref[...])
    x1_ref[...] = x1

    @pl.when(is_last)
    def _():
        for hd in range(H_REC):
            rec_ref[hd] = st_ref[hd]
        conv_ref[...] = tail


def _prompt_kernel(x_ref, mod_ref, lb_ref, w_in_hbm, w_conv_ref, g_onorm_ref, w_out_hbm,
                   w_up_hbm, w_down_hbm, g_final_ref,
                   y_ref, rec_ref, conv_ref, w_in_out, w_out_out, w_up_out, w_down_out,
                   st_ref, ubuf_ref, mix_ref, x1_ref,
                   w_in_ref, w_out_ref, w_up_ref, w_down_ref, stage_ref, stage_sem, out_sem,
                   *, n_tiles, n_steps):
    s = pl.program_id(0)
    tile = jnp.minimum(s, n_steps - 1)
    l = lax.rem(tile, n_tiles)
    d = x_ref.shape[1]

    w_vmem = (w_in_ref, w_out_ref, w_up_ref, w_down_ref)
    w_outs = (w_in_out, w_out_out, w_up_out, w_down_out)

    def bf16_export(i):
        return pltpu.make_async_copy(w_vmem[i], w_outs[i], out_sem.at[i])

    @pl.when(s == 0)
    def _():
        _load_weights_as_bf16((w_in_hbm, w_out_hbm, w_up_hbm, w_down_hbm), w_vmem,
                              stage_ref, stage_sem)
        for i in range(len(w_vmem)):
            bf16_export(i).start()

    @pl.when(s == n_steps)
    def _():
        for i in range(len(w_vmem)):
            bf16_export(i).wait()

    @pl.when(l == 0)
    def _():
        st_ref[...] = jnp.zeros_like(st_ref)
        ubuf_ref[0:SUBLANES, :] = jnp.zeros((SUBLANES, ubuf_ref.shape[1]), F32)

    def mix_stage():
        mod = mod_ref[pl.ds(lax.div(tile, n_tiles), 1), :]
        return _prompt_mix_stage(x_ref, mod, lb_ref, w_in_ref, w_conv_ref, g_onorm_ref,
                                 w_out_ref, st_ref, ubuf_ref, mix_ref, x1_ref,
                                 rec_ref, conv_ref, l == n_tiles - 1)

    def mlp_stage():
        mod_prev = mod_ref[pl.ds(lax.div(s - 1, n_tiles), 1), :]
        return _mlp_stage(x1_ref, mod_prev, w_up_ref, w_down_ref, g_final_ref, y_ref)

    @pl.when(s == 0)
    def _():
        _alternate(mix_stage())

    @pl.when((s > 0) & (s < n_steps))
    def _():
        _alternate(mix_stage(), mlp_stage())

    @pl.when(s == n_steps)
    def _():
        _alternate(mlp_stage())


def _prompt_layer(x, mod_p, lower_bounds, w_in, w_conv, g_onorm, w_out, w_up, w_down, g_final):
    bsz, seq, d = x.shape
    d_rec = lower_bounds.shape[1]
    dk = d_rec // H_REC
    d_conv = w_conv.shape[1]
    tl = PROMPT_TILE
    n_tiles = seq // tl
    n_steps = bsz * n_tiles
    assert seq % tl == 0 and tl % CHUNK == 0
    mats = (w_in, w_out, w_up, w_down)
    hbm = pl.BlockSpec(memory_space=pl.ANY)

    def mix_tile(s):
        t = jnp.minimum(s, n_steps - 1)
        return lax.div(t, n_tiles), lax.rem(t, n_tiles)

    def mlp_tile(s):
        t = jnp.maximum(s - 1, 0)
        return lax.div(t, n_tiles), lax.rem(t, n_tiles)

    return pl.pallas_call(
        functools.partial(_prompt_kernel, n_tiles=n_tiles, n_steps=n_steps),
        grid=(n_steps + 1,),
        in_specs=[
            pl.BlockSpec((None, tl, d), lambda s: (*mix_tile(s), 0)),
            _const_spec(mod_p.shape),
            _const_spec(lower_bounds.shape),
            hbm,
            _const_spec(w_conv.shape),
            _const_spec(g_onorm.shape),
            hbm,
            hbm,
            hbm,
            _const_spec(g_final.shape),
        ],
        out_specs=[
            pl.BlockSpec((None, tl, d), lambda s: (*mlp_tile(s), 0)),
            pl.BlockSpec((None, H_REC, dk, dk), lambda s: (mix_tile(s)[0], 0, 0, 0)),
            pl.BlockSpec((None, CONV_W - 1, d_conv), lambda s: (mix_tile(s)[0], 0, 0)),
        ] + [hbm] * len(mats),
        out_shape=[
            jax.ShapeDtypeStruct((bsz, seq, d), F32),
            jax.ShapeDtypeStruct((bsz, H_REC, dk, dk), F32),
            jax.ShapeDtypeStruct((bsz, CONV_W - 1, d_conv), F32),
        ] + [jax.ShapeDtypeStruct(w.shape, BF16) for w in mats],
        scratch_shapes=[
            pltpu.VMEM((H_REC, dk, dk), F32),
            pltpu.VMEM((SUBLANES + tl, d_conv), F32),
            pltpu.VMEM((tl, d_rec + d_conv), BF16),
            pltpu.VMEM((tl, d), F32),
        ] + [pltpu.VMEM(w.shape, BF16) for w in mats] + [
            pltpu.VMEM((WEIGHT_STAGE_SLOTS, *WEIGHT_STAGE_SHAPE), F32),
            pltpu.SemaphoreType.DMA((WEIGHT_STAGE_SLOTS,)),
            pltpu.SemaphoreType.DMA((len(mats),)),
        ],
        compiler_params=pltpu.CompilerParams(
            dimension_semantics=("arbitrary",), vmem_limit_bytes=VMEM_LIMIT_BYTES),
        name="prompt_layer",
    )(x, mod_p, lower_bounds, w_in, w_conv, g_onorm, w_out, w_up, w_down, g_final)


def _sample_kernel(x_ref, mod_ref, lb_ref, w_in_ref, w_conv_ref, g_onorm_ref, w_out_ref,
                   w_up_ref, w_down_ref, g_final_ref, rec_in_ref, conv_in_ref,
                   y_ref, rec_ref, conv_ref,
                   modx_ref, cbx_ref, ubuf_ref, mix_ref, *, seq):
    rows, d = x_ref.shape
    tb = rows // seq
    d_rec = lb_ref.shape[1]
    dk = d_rec // H_REC
    d_conv = conv_in_ref.shape[2]

    for b in range(tb):
        r = slice(b * seq, (b + 1) * seq)
        modx_ref[r, :] = jnp.broadcast_to(mod_ref[b:b + 1, :], (seq, mod_ref.shape[1]))
        for j in range(CONV_W - 1):
            cbx_ref[j, r, :] = jnp.broadcast_to(conv_in_ref[b, j:j + 1, :], (seq, d_conv))

    sh1, sc1, g1, sh2, sc2, g2 = (modx_ref[:, i * d:(i + 1) * d] for i in range(N_MOD))

    x = x_ref[...]
    proj = _bdot(_modulate(x, sh1, sc1), w_in_ref[...])
    q, fz, iv, g, gb, gc, hv = _split_proj(proj)

    lb = _layer_lower_bound(lb_ref, 0)
    row = lax.broadcasted_iota(jnp.int32, (rows, rows), 0)
    col = lax.broadcasted_iota(jnp.int32, (rows, rows), 1)
    causal = ((row // seq) == (col // seq)) & (row >= col)
    log2f, k = _gates(fz, lb)
    cum = _tile_cumsum(log2f)
    last = _tile_last(cum)
    q_dec = (q * jnp.exp2(cum)).astype(BF16)
    k_dec = (k * jnp.exp2(-cum)).astype(BF16)
    k_end = k * jnp.exp2(last - cum)
    decay = jnp.exp2(last)
    vb = iv.astype(BF16)

    dec3 = _decay_terms(decay, _group_pos((rows, d_rec), seq))
    zeros_blk = jnp.zeros((seq, dk), F32)
    rhs_bottom = _decay_selector(seq, dk)

    heads = [slice(hd * dk, (hd + 1) * dk) for hd in range(H_REC)]

    def update_states(elements):
        for b in elements:
            r = slice(b * seq, (b + 1) * seq)
            for hd, sl in enumerate(heads):
                lhs = jnp.concatenate([k_end[r, sl], dec3[r, sl]], axis=0)
                rhs = jnp.concatenate(
                    [jnp.concatenate([iv[r, sl], zeros_blk], axis=1), rhs_bottom], axis=0)
                both = _dot_tn(lhs.astype(BF16), rhs.astype(BF16))
                rec_ref[b, hd] = both[:, dk:] * rec_in_ref[b, hd] + both[:, :dk]

    scores = [jnp.where(causal, _dot_nt(q_dec[:, sl], k_dec[:, sl]), 0.0).astype(BF16)
              for sl in heads]
    o_inter = jnp.concatenate(
        [jnp.concatenate([_bdot(q_dec[b * seq:(b + 1) * seq, sl], rec_in_ref[b, hd].astype(BF16))
                          for hd, sl in enumerate(heads)], axis=-1)
         for b in range(tb)], axis=0)
    o_intra = jnp.concatenate([_bdot(scores[hd], vb[:, sl]) for hd, sl in enumerate(heads)],
                              axis=-1)
    o_rec = _head_out(o_intra + o_inter, g, g_onorm_ref[...])
    mix_ref[:, 0:d_rec] = o_rec.astype(BF16)

    u = gc * hv
    ubuf_ref[SUBLANES:SUBLANES + rows, :] = u
    tok = _group_pos((rows, d_conv), seq)
    u_m1 = jnp.where(tok >= 1, ubuf_ref[SUBLANES - 1:SUBLANES - 1 + rows, :], cbx_ref[1])
    u_m2 = jnp.where(tok >= 2, ubuf_ref[SUBLANES - 2:SUBLANES - 2 + rows, :],
                     jnp.where(tok == 1, cbx_ref[1], cbx_ref[0]))
    w_conv = w_conv_ref[...]
    y_conv = w_conv[0:1] * u_m2 + w_conv[1:2] * u_m1 + w_conv[2:3] * u
    mix_ref[:, d_rec:d_rec + d_conv] = (gb * y_conv).astype(BF16)
    for b in range(tb):
        conv_ref[b] = u[(b + 1) * seq - (CONV_W - 1):(b + 1) * seq]

    x1 = x + g1 * _bdot(mix_ref[...], w_out_ref[...])
    update_states(range(tb // 2))
    h2 = _modulate(x1, sh2, sc2)
    mlp = _mlp_down(_mlp_up(h2, w_up_ref, 0), w_up_ref, w_down_ref, 0)
    for j in range(1, MLP_BLOCKS):
        mlp += _mlp_down(_mlp_up(h2, w_up_ref, j), w_up_ref, w_down_ref, j)
    update_states(range(tb // 2, tb))
    y_ref[...] = _final_norm(x1, g2, mlp, g_final_ref)


def _sample_layer(x2d, seq, mod, lower_bounds, w_in, w_conv, g_onorm, w_out, w_up, w_down,
                  g_final, rec_in, conv_in):
    n_rows, d = x2d.shape
    bsz = n_rows // seq
    d_rec = lower_bounds.shape[1]
    dk = d_rec // H_REC
    d_conv = w_conv.shape[1]
    tb = SAMPLE_TILE_B
    rows = tb * seq
    assert bsz % tb == 0 and seq == SUBLANES and seq >= CONV_W
    return pl.pallas_call(
        functools.partial(_sample_kernel, seq=seq),
        grid=(bsz // tb,),
        in_specs=[
            pl.BlockSpec((rows, d), lambda i: (i, 0)),
            pl.BlockSpec((tb, N_MOD * d), lambda i: (i, 0)),
            _const_spec(lower_bounds.shape),
            _const_spec(w_in.shape),
            _const_spec(w_conv.shape),
            _const_spec(g_onorm.shape),
            _const_spec(w_out.shape),
            _const_spec(w_up.shape),
            _const_spec(w_down.shape),
            _const_spec(g_final.shape),
            pl.BlockSpec((tb, H_REC, dk, dk), lambda i: (i, 0, 0, 0)),
            pl.BlockSpec((tb, CONV_W - 1, d_conv), lambda i: (i, 0, 0)),
        ],
        out_specs=[
            pl.BlockSpec((rows, d), lambda i: (i, 0)),
            pl.BlockSpec((tb, H_REC, dk, dk), lambda i: (i, 0, 0, 0)),
            pl.BlockSpec((tb, CONV_W - 1, d_conv), lambda i: (i, 0, 0)),
        ],
        out_shape=[
            jax.ShapeDtypeStruct((n_rows, d), F32),
            jax.ShapeDtypeStruct((bsz, H_REC, dk, dk), F32),
            jax.ShapeDtypeStruct((bsz, CONV_W - 1, d_conv), F32),
        ],
        scratch_shapes=[
            pltpu.VMEM((rows, N_MOD * d), F32),
            pltpu.VMEM((CONV_W - 1, rows, d_conv), F32),
            pltpu.VMEM((SUBLANES + rows, d_conv), F32),
            pltpu.VMEM((rows, d_rec + d_conv), BF16),
        ],
        compiler_params=pltpu.CompilerParams(
            dimension_semantics=("arbitrary",), vmem_limit_bytes=VMEM_LIMIT_BYTES),
        name="sample_layer",
    )(x2d, mod, lower_bounds, w_in, w_conv, g_onorm, w_out, w_up, w_down, g_final, rec_in, conv_in)


def kernel(x_prompt, x_sample, state_rec, state_conv, c_prompt, c_sample, lower_bounds, w_ada,
           b_ada, w_in, w_conv, g_onorm, w_out, w_up, w_down, g_final):
    depth = w_in.shape[0]
    assert depth == 1, "single-layer trunk"
    bsz_s, seq_s, d = x_sample.shape

    mod_s, mod_p = _modulation(c_sample, c_prompt, w_ada[0], b_ada)

    g_final = g_final.reshape(1, d)
    y_p, rec_p, conv_p, w_in_b, w_out_b, w_up_b, w_down_b = _prompt_layer(
        x_prompt, mod_p, lower_bounds, w_in[0], w_conv[0], g_onorm, w_out[0], w_up[0], w_down[0],
        g_final)
    y_s, rec_s, conv_s = _sample_layer(
        x_sample.reshape(bsz_s * seq_s, d), seq_s, mod_s, lower_bounds, w_in_b, w_conv[0], g_onorm,
        w_out_b, w_up_b, w_down_b, g_final, state_rec[0], state_conv[0])
    return (y_p, y_s.reshape(bsz_s, seq_s, d), rec_p[None], conv_p[None], rec_s[None],
            conv_s[None])
```

```python
import functools

import jax
import jax.numpy as jnp
from jax import lax
from jax.experimental import pallas as pl
from jax.experimental.pallas import tpu as pltpu

F32 = jnp.float32
BF16 = jnp.bfloat16

EPS = 1e-6
H_REC = 4
CONV_W = 3
N_MOD = 6
N_PROJ_GROUPS = 7
CHUNK = 64

SUBLANES = 8
VMEM_LIMIT_BYTES = 56 * 1024 * 1024

PROMPT_TILE = 512
SAMPLE_TILE_B = 16
MOD_TILE_K = 256
MLP_BLOCKS = 4
WEIGHT_STAGE_SHAPE = (512, 512)
WEIGHT_STAGE_SLOTS = 4


def _const_spec(shape):
    zeros = (0,) * len(shape)
    return pl.BlockSpec(shape, lambda *_: zeros, pipeline_mode=pl.Buffered(1))


def _rms(x):
    return x * lax.rsqrt(jnp.mean(x * x, axis=-1, keepdims=True) + EPS)


def _silu(x):
    return x * jax.nn.sigmoid(x)


def _bdot(a, b):
    return jnp.dot(a, b, preferred_element_type=F32)


def _dot_nt(a, b):
    return lax.dot_general(a, b, (((1,), (1,)), ((), ())), preferred_element_type=F32)


def _dot_tn(a, b):
    return lax.dot_general(a, b, (((0,), (0,)), ((), ())), preferred_element_type=F32)


def _group_pos(shape, group):
    return lax.broadcasted_iota(jnp.int32, shape, 0) % group


def _tile_cumprod(x):
    rows, n = x.shape
    x = x.reshape(rows // SUBLANES, SUBLANES, n)
    pos = lax.broadcasted_iota(jnp.int32, x.shape, 1)
    step = 1
    while step < SUBLANES:
        x = x * jnp.where(pos >= step, pltpu.roll(x, step, axis=1), 1.0)
        step *= 2
    return x.reshape(rows, n)


def _tile_last(x):
    rows, n = x.shape
    x = x.reshape(rows // SUBLANES, SUBLANES, n)
    return jnp.broadcast_to(x[:, SUBLANES - 1:, :], x.shape).reshape(rows, n)


def _chain_tiles(y):
    tiles, carry = [], None
    for r in range(0, y.shape[0], SUBLANES):
        t = y[r:r + SUBLANES]
        if carry is not None:
            t = t * carry
        tiles.append(t)
        carry = t[SUBLANES - 1:, :]
    return jnp.concatenate(tiles, axis=0)


def _layer_lower_bound(lb_ref, layer):
    lb = lb_ref[...]
    e = jnp.exp(lb - jnp.max(lb, axis=0, keepdims=True))
    sm = e / jnp.sum(e, axis=0, keepdims=True)
    return jnp.sum(sm[: layer + 1], axis=0, keepdims=True)


def _gates(fz, lb):
    sig = jax.nn.sigmoid(fz)
    f = lb + (1.0 - lb) * sig
    k = (1.0 - lb) * (1.0 - sig)
    return f, k


def _decay_terms(decay, row):
    hi = decay.astype(BF16).astype(F32)
    rest = decay - hi
    mid = rest.astype(BF16).astype(F32)
    lo = rest - mid
    return jnp.where(row == 0, hi, jnp.where(row == 1, mid, jnp.where(row == 2, lo, 0.0)))


def _decay_selector(n_rows, width):
    sub = lax.broadcasted_iota(jnp.int32, (n_rows, width), 0)
    return jnp.concatenate(
        [jnp.zeros((n_rows, width), F32), jnp.where(sub < 3, 1.0, 0.0)], axis=1)


def _head_out(o, g, g_onorm):
    dv = o.shape[-1] // H_REC
    pieces = []
    for h in range(H_REC):
        sl = slice(h * dv, (h + 1) * dv)
        pieces.append(_rms(o[:, sl]))
    return (jnp.concatenate(pieces, axis=-1) * g_onorm) * _silu(g)


def _modulate(x, shift, scale):
    return (_rms(x) * (1.0 + scale) + shift).astype(BF16)


def _mlp_cols(w_up_ref, j):
    blk = w_up_ref.shape[1] // MLP_BLOCKS
    return slice(j * blk, (j + 1) * blk)


def _mlp_up(h2, w_up_ref, j):
    up = jnp.maximum(_bdot(h2, w_up_ref[:, _mlp_cols(w_up_ref, j)]), 0.0)
    return (up * up).astype(BF16)


def _mlp_down(up, w_up_ref, w_down_ref, j):
    return _bdot(up, w_down_ref[_mlp_cols(w_up_ref, j), :])


def _final_norm(x, g2, mlp, g_final_ref):
    return _rms(x + g2 * mlp) * g_final_ref[...]


def _mod_kernel(c_s_ref, c_p_ref, w_ref, b_ref, o_s_ref, o_p_ref):
    k = pl.program_id(0)
    n_s = c_s_ref.shape[0]
    kb = w_ref.shape[0]
    cols = pl.ds(pl.multiple_of(k * kb, kb), kb)
    c = jnp.concatenate([c_s_ref[:, cols], c_p_ref[:, cols]], axis=0)
    part = _bdot(_silu(c).astype(BF16), w_ref[...].astype(BF16))

    @pl.when(k == 0)
    def _():
        o_s_ref[...] = part[:n_s] + b_ref[...]
        o_p_ref[...] = part[n_s:] + b_ref[...]

    @pl.when(k > 0)
    def _():
        o_s_ref[...] += part[:n_s]
        o_p_ref[...] += part[n_s:]


def _modulation(c_s, c_p, w_ada, b_ada):
    (n_s, d), n_p = c_s.shape, c_p.shape[0]
    n = w_ada.shape[1]
    assert n_s % SUBLANES == 0 and n_p % SUBLANES == 0 and d % MOD_TILE_K == 0
    return pl.pallas_call(
        _mod_kernel,
        grid=(d // MOD_TILE_K,),
        in_specs=[
            pl.BlockSpec((n_s, d), lambda k: (0, 0)),
            pl.BlockSpec((n_p, d), lambda k: (0, 0)),
            pl.BlockSpec((MOD_TILE_K, n), lambda k: (k, 0)),
            pl.BlockSpec((1, n), lambda k: (0, 0)),
        ],
        out_specs=[
            pl.BlockSpec((n_s, n), lambda k: (0, 0)),
            pl.BlockSpec((n_p, n), lambda k: (0, 0)),
        ],
        out_shape=[
            jax.ShapeDtypeStruct((n_s, n), F32),
            jax.ShapeDtypeStruct((n_p, n), F32),
        ],
        compiler_params=pltpu.CompilerParams(
            dimension_semantics=("arbitrary",), vmem_limit_bytes=VMEM_LIMIT_BYTES),
        name="adaln_modulation",
    )(c_s, c_p, w_ada, b_ada)


def _split_proj(proj):
    width = proj.shape[1] // N_PROJ_GROUPS
    return tuple(proj[:, p * width:(p + 1) * width] for p in range(N_PROJ_GROUPS))


def _weight_blocks(w_hbm, w_vmem):
    rows, cols = w_hbm.shape
    br, bc = WEIGHT_STAGE_SHAPE
    assert rows % br == 0 and cols % bc == 0
    return [(w_hbm.at[r:r + br, c:c + bc], w_vmem.at[r:r + br, c:c + bc])
            for r in range(0, rows, br) for c in range(0, cols, bc)]


def _stage_copy(src, stage_ref, sem_ref, i):
    slot = i % WEIGHT_STAGE_SLOTS
    return pltpu.make_async_copy(src, stage_ref.at[slot], sem_ref.at[slot])


def _load_weights_as_bf16(w_hbm_refs, w_vmem_refs, stage_ref, sem_ref):
    blocks = [blk for w_hbm, w_vmem in zip(w_hbm_refs, w_vmem_refs)
              for blk in _weight_blocks(w_hbm, w_vmem)]
    ahead = WEIGHT_STAGE_SLOTS - 1
    for i in range(min(ahead, len(blocks))):
        _stage_copy(blocks[i][0], stage_ref, sem_ref, i).start()
    for i, (src, dst) in enumerate(blocks):
        if i + ahead < len(blocks):
            _stage_copy(blocks[i + ahead][0], stage_ref, sem_ref, i + ahead).start()
        _stage_copy(src, stage_ref, sem_ref, i).wait()
        dst[...] = stage_ref[i % WEIGHT_STAGE_SLOTS].astype(BF16)


def _alternate(*stages):
    stages = list(stages)
    while stages:
        for stage in list(stages):
            try:
                next(stage)
            except StopIteration:
                stages.remove(stage)


def _mlp_stage(x1_ref, mod, w_up_ref, w_down_ref, g_final_ref, y_ref):
    d = x1_ref.shape[1]
    sh2, sc2, g2 = (mod[:, i * d:(i + 1) * d] for i in range(3, N_MOD))
    h2 = _modulate(x1_ref[...], sh2, sc2)
    ups = [_mlp_up(h2, w_up_ref, 0)]
    yield
    mlp = None
    for j in range(MLP_BLOCKS):
        if j + 1 < MLP_BLOCKS:
            ups.append(_mlp_up(h2, w_up_ref, j + 1))
        part = _mlp_down(ups[j], w_up_ref, w_down_ref, j)
        mlp = part if mlp is None else mlp + part
        if j + 2 < MLP_BLOCKS:
            yield
    y_ref[...] = _final_norm(x1_ref[...], g2, mlp, g_final_ref)
    yield


def _prompt_mix_stage(x_ref, mod, lb_ref, w_in_ref, w_conv_ref, g_onorm_ref, w_out_ref,
                      st_ref, ubuf_ref, mix_ref, x1_ref, rec_ref, conv_ref, is_last):
    tl, d = x_ref.shape
    d_rec = lb_ref.shape[1]
    dk = d_rec // H_REC
    n_chunks = tl // CHUNK
    heads = [slice(hd * dk, (hd + 1) * dk) for hd in range(H_REC)]
    sh1, sc1, g1 = (mod[:, i * d:(i + 1) * d] for i in range(3))

    x = x_ref[...]
    proj = _bdot(_modulate(x, sh1, sc1), w_in_ref[...])
    q, fz, iv, g, gb, gc, hv = _split_proj(proj)
    d_conv = gb.shape[1]
    yield

    lb = _layer_lower_bound(lb_ref, 0)
    row = lax.broadcasted_iota(jnp.int32, (CHUNK, CHUNK), 0)
    col = lax.broadcasted_iota(jnp.int32, (CHUNK, CHUNK), 1)
    causal = row >= col
    f_all, k_all = _gates(fz, lb)
    prod_tiles = _tile_cumprod(f_all)
    q_dec, decay, scores, upd = [], [], [], []
    for c in range(n_chunks):
        rows = slice(c * CHUNK, (c + 1) * CHUNK)
        prod = _chain_tiles(prod_tiles[rows])
        dec = prod[CHUNK - 1:CHUNK, :]
        q_dec.append((q[rows] * prod).astype(BF16))
        k_inv = k_all[rows] / prod
        k_dec = k_inv.astype(BF16)
        k_end = (k_inv * dec).astype(BF16)
        decay.append([jnp.broadcast_to(dec[:, sl], (dk, dk)).T for sl in heads])
        v = iv[rows].astype(BF16)
        scores.append([jnp.where(causal, _dot_nt(q_dec[c][:, sl], k_dec[:, sl]), 0.0).astype(BF16)
                       for sl in heads])
        upd.append([_dot_tn(k_end[:, sl], v[:, sl]) for sl in heads])
    yield

    st = [st_ref[hd] for hd in range(H_REC)]
    o_chunks = []
    for c in range(n_chunks):
        v = iv[c * CHUNK:(c + 1) * CHUNK].astype(BF16)
        o_heads = []
        for hd, sl in enumerate(heads):
            o_heads.append(_bdot(scores[c][hd], v[:, sl])
                           + _bdot(q_dec[c][:, sl], st[hd].astype(BF16)))
            st[hd] = st[hd] * decay[c][hd] + upd[c][hd]
        o_chunks.append(jnp.concatenate(o_heads, axis=-1))
    for hd in range(H_REC):
        st_ref[hd] = st[hd]
    yield

    o_rec = _head_out(jnp.concatenate(o_chunks, axis=0), g, g_onorm_ref[...])
    mix_ref[:, 0:d_rec] = o_rec.astype(BF16)

    u = gc * hv
    ubuf_ref[SUBLANES:SUBLANES + tl, :] = u
    w_conv = w_conv_ref[...]
    y_conv = (w_conv[0:1] * ubuf_ref[SUBLANES - 2:SUBLANES - 2 + tl, :]
              + w_conv[1:2] * ubuf_ref[SUBLANES - 1:SUBLANES - 1 + tl, :]
              + w_conv[2:3] * u)
    mix_ref[:, d_rec:d_rec + d_conv] = (gb * y_conv).astype(BF16)
    tail = u[tl - (CONV_W - 1):tl]
    ubuf_ref[SUBLANES - (CONV_W - 1):SUBLANES, :] = tail
    yield

    x1 = x + g1 * _bdot(mix_ref[...], w_out_ref[...])
    x1_ref[...] = x1

    @pl.when(is_last)
    def _():
        for hd in range(H_REC):
            rec_ref[hd] = st_ref[hd]
        conv_ref[...] = tail


def _prompt_kernel(x_ref, mod_ref, lb_ref, w_in_hbm, w_conv_ref, g_onorm_ref, w_out_hbm,
                   w_up_hbm, w_down_hbm, g_final_ref,
                   y_ref, rec_ref, conv_ref, w_in_out, w_out_out, w_up_out, w_down_out,
                   st_ref, ubuf_ref, mix_ref, x1_ref,
                   w_in_ref, w_out_ref, w_up_ref, w_down_ref, stage_ref, stage_sem, out_sem,
                   *, n_tiles, n_steps):
    s = pl.program_id(0)
    tile = jnp.minimum(s, n_steps - 1)
    l = lax.rem(tile, n_tiles)
    d = x_ref.shape[1]

    w_vmem = (w_in_ref, w_out_ref, w_up_ref, w_down_ref)
    w_outs = (w_in_out, w_out_out, w_up_out, w_down_out)

    def bf16_export(i):
        return pltpu.make_async_copy(w_vmem[i], w_outs[i], out_sem.at[i])

    @pl.when(s == 0)
    def _():
        _load_weights_as_bf16((w_in_hbm, w_out_hbm, w_up_hbm, w_down_hbm), w_vmem,
                              stage_ref, stage_sem)
        for i in range(len(w_vmem)):
            bf16_export(i).start()

    @pl.when(s == n_steps)
    def _():
        for i in range(len(w_vmem)):
            bf16_export(i).wait()

    @pl.when(l == 0)
    def _():
        st_ref[...] = jnp.zeros_like(st_ref)
        ubuf_ref[0:SUBLANES, :] = jnp.zeros((SUBLANES, ubuf_ref.shape[1]), F32)

    def mix_stage():
        mod = mod_ref[pl.ds(lax.div(tile, n_tiles), 1), :]
        return _prompt_mix_stage(x_ref, mod, lb_ref, w_in_ref, w_conv_ref, g_onorm_ref,
                                 w_out_ref, st_ref, ubuf_ref, mix_ref, x1_ref,
                                 rec_ref, conv_ref, l == n_tiles - 1)

    def mlp_stage():
        mod_prev = mod_ref[pl.ds(lax.div(s - 1, n_tiles), 1), :]
        return _mlp_stage(x1_ref, mod_prev, w_up_ref, w_down_ref, g_final_ref, y_ref)

    @pl.when(s == 0)
    def _():
        _alternate(mix_stage())

    @pl.when((s > 0) & (s < n_steps))
    def _():
        _alternate(mix_stage(), mlp_stage())

    @pl.when(s == n_steps)
    def _():
        _alternate(mlp_stage())


def _prompt_layer(x, mod_p, lower_bounds, w_in, w_conv, g_onorm, w_out, w_up, w_down, g_final):
    bsz, seq, d = x.shape
    d_rec = lower_bounds.shape[1]
    dk = d_rec // H_REC
    d_conv = w_conv.shape[1]
    tl = PROMPT_TILE
    n_tiles = seq // tl
    n_steps = bsz * n_tiles
    assert seq % tl == 0 and tl % CHUNK == 0
    mats = (w_in, w_out, w_up, w_down)
    hbm = pl.BlockSpec(memory_space=pl.ANY)

    def mix_tile(s):
        t = jnp.minimum(s, n_steps - 1)
        return lax.div(t, n_tiles), lax.rem(t, n_tiles)

    def mlp_tile(s):
        t = jnp.maximum(s - 1, 0)
        return lax.div(t, n_tiles), lax.rem(t, n_tiles)

    return pl.pallas_call(
        functools.partial(_prompt_kernel, n_tiles=n_tiles, n_steps=n_steps),
        grid=(n_steps + 1,),
        in_specs=[
            pl.BlockSpec((None, tl, d), lambda s: (*mix_tile(s), 0)),
            _const_spec(mod_p.shape),
            _const_spec(lower_bounds.shape),
            hbm,
            _const_spec(w_conv.shape),
            _const_spec(g_onorm.shape),
            hbm,
            hbm,
            hbm,
            _const_spec(g_final.shape),
        ],
        out_specs=[
            pl.BlockSpec((None, tl, d), lambda s: (*mlp_tile(s), 0)),
            pl.BlockSpec((None, H_REC, dk, dk), lambda s: (mix_tile(s)[0], 0, 0, 0)),
            pl.BlockSpec((None, CONV_W - 1, d_conv), lambda s: (mix_tile(s)[0], 0, 0)),
        ] + [hbm] * len(mats),
        out_shape=[
            jax.ShapeDtypeStruct((bsz, seq, d), F32),
            jax.ShapeDtypeStruct((bsz, H_REC, dk, dk), F32),
            jax.ShapeDtypeStruct((bsz, CONV_W - 1, d_conv), F32),
        ] + [jax.ShapeDtypeStruct(w.shape, BF16) for w in mats],
        scratch_shapes=[
            pltpu.VMEM((H_REC, dk, dk), F32),
            pltpu.VMEM((SUBLANES + tl, d_conv), F32),
            pltpu.VMEM((tl, d_rec + d_conv), BF16),
            pltpu.VMEM((tl, d), F32),
        ] + [pltpu.VMEM(w.shape, BF16) for w in mats] + [
            pltpu.VMEM((WEIGHT_STAGE_SLOTS, *WEIGHT_STAGE_SHAPE), F32),
            pltpu.SemaphoreType.DMA((WEIGHT_STAGE_SLOTS,)),
            pltpu.SemaphoreType.DMA((len(mats),)),
        ],
        compiler_params=pltpu.CompilerParams(
            dimension_semantics=("arbitrary",), vmem_limit_bytes=VMEM_LIMIT_BYTES),
        name="prompt_layer",
    )(x, mod_p, lower_bounds, w_in, w_conv, g_onorm, w_out, w_up, w_down, g_final)


def _sample_kernel(x_ref, mod_ref, lb_ref, w_in_ref, w_conv_ref, g_onorm_ref, w_out_ref,
                   w_up_ref, w_down_ref, g_final_ref, rec_in_ref, conv_in_ref,
                   y_ref, rec_ref, conv_ref,
                   modx_ref, cbx_ref, ubuf_ref, mix_ref, *, seq):
    rows, d = x_ref.shape
    tb = rows // seq
    d_rec = lb_ref.shape[1]
    dk = d_rec // H_REC
    d_conv = conv_in_ref.shape[2]

    for b in range(tb):
        r = slice(b * seq, (b + 1) * seq)
        modx_ref[r, :] = jnp.broadcast_to(mod_ref[b:b + 1, :], (seq, mod_ref.shape[1]))
        for j in range(CONV_W - 1):
            cbx_ref[j, r, :] = jnp.broadcast_to(conv_in_ref[b, j:j + 1, :], (seq, d_conv))

    sh1, sc1, g1, sh2, sc2, g2 = (modx_ref[:, i * d:(i + 1) * d] for i in range(N_MOD))

    x = x_ref[...]
    proj = _bdot(_modulate(x, sh1, sc1), w_in_ref[...])
    q, fz, iv, g, gb, gc, hv = _split_proj(proj)

    lb = _layer_lower_bound(lb_ref, 0)
    row = lax.broadcasted_iota(jnp.int32, (rows, rows), 0)
    col = lax.broadcasted_iota(jnp.int32, (rows, rows), 1)
    causal = ((row // seq) == (col // seq)) & (row >= col)
    f, k = _gates(fz, lb)
    prod = _tile_cumprod(f)
    decay = _tile_last(prod)
    q_dec = (q * prod).astype(BF16)
    k_inv = k / prod
    k_dec = k_inv.astype(BF16)
    k_end = k_inv * decay
    vb = iv.astype(BF16)

    dec3 = _decay_terms(decay, _group_pos((rows, d_rec), seq))
    zeros_blk = jnp.zeros((seq, dk), F32)
    rhs_bottom = _decay_selector(seq, dk)

    heads = [slice(hd * dk, (hd + 1) * dk) for hd in range(H_REC)]

    def update_states(elements):
        for b in elements:
            r = slice(b * seq, (b + 1) * seq)
            for hd, sl in enumerate(heads):
                lhs = jnp.concatenate([k_end[r, sl], dec3[r, sl]], axis=0)
                rhs = jnp.concatenate(
                    [jnp.concatenate([iv[r, sl], zeros_blk], axis=1), rhs_bottom], axis=0)
                both = _dot_tn(lhs.astype(BF16), rhs.astype(BF16))
                rec_ref[b, hd] = both[:, dk:] * rec_in_ref[b, hd] + both[:, :dk]

    scores = [jnp.where(causal, _dot_nt(q_dec[:, sl], k_dec[:, sl]), 0.0).astype(BF16)
              for sl in heads]
    o_inter = jnp.concatenate(
        [jnp.concatenate([_bdot(q_dec[b * seq:(b + 1) * seq, sl], rec_in_ref[b, hd].astype(BF16))
                          for hd, sl in enumerate(heads)], axis=-1)
         for b in range(tb)], axis=0)
    o_intra = jnp.concatenate([_bdot(scores[hd], vb[:, sl]) for hd, sl in enumerate(heads)],
                              axis=-1)
    o_rec = _head_out(o_intra + o_inter, g, g_onorm_ref[...])
    mix_ref[:, 0:d_rec] = o_rec.astype(BF16)

    u = gc * hv
    ubuf_ref[SUBLANES:SUBLANES + rows, :] = u
    tok = _group_pos((rows, d_conv), seq)
    u_m1 = jnp.where(tok >= 1, ubuf_ref[SUBLANES - 1:SUBLANES - 1 + rows, :], cbx_ref[1])
    u_m2 = jnp.where(tok >= 2, ubuf_ref[SUBLANES - 2:SUBLANES - 2 + rows, :],
                     jnp.where(tok == 1, cbx_ref[1], cbx_ref[0]))
    w_conv = w_conv_ref[...]
    y_conv = w_conv[0:1] * u_m2 + w_conv[1:2] * u_m1 + w_conv[2:3] * u
    mix_ref[:, d_rec:d_rec + d_conv] = (gb * y_conv).astype(BF16)
    for b in range(tb):
        conv_ref[b] = u[(b + 1) * seq - (CONV_W - 1):(b + 1) * seq]

    x1 = x + g1 * _bdot(mix_ref[...], w_out_ref[...])
    update_states(range(tb // 2))
    h2 = _modulate(x1, sh2, sc2)
    mlp = _mlp_down(_mlp_up(h2, w_up_ref, 0), w_up_ref, w_down_ref, 0)
    for j in range(1, MLP_BLOCKS):
        mlp += _mlp_down(_mlp_up(h2, w_up_ref, j), w_up_ref, w_down_ref, j)
    update_states(range(tb // 2, tb))
    y_ref[...] = _final_norm(x1, g2, mlp, g_final_ref)


def _sample_layer(x2d, seq, mod, lower_bounds, w_in, w_conv, g_onorm, w_out, w_up, w_down,
                  g_final, rec_in, conv_in):
    n_rows, d = x2d.shape
    bsz = n_rows // seq
    d_rec = lower_bounds.shape[1]
    dk = d_rec // H_REC
    d_conv = w_conv.shape[1]
    tb = SAMPLE_TILE_B
    rows = tb * seq
    assert bsz % tb == 0 and seq == SUBLANES and seq >= CONV_W
    return pl.pallas_call(
        functools.partial(_sample_kernel, seq=seq),
        grid=(bsz // tb,),
        in_specs=[
            pl.BlockSpec((rows, d), lambda i: (i, 0)),
            pl.BlockSpec((tb, N_MOD * d), lambda i: (i, 0)),
            _const_spec(lower_bounds.shape),
            _const_spec(w_in.shape),
            _const_spec(w_conv.shape),
            _const_spec(g_onorm.shape),
            _const_spec(w_out.shape),
            _const_spec(w_up.shape),
            _const_spec(w_down.shape),
            _const_spec(g_final.shape),
            pl.BlockSpec((tb, H_REC, dk, dk), lambda i: (i, 0, 0, 0)),
            pl.BlockSpec((tb, CONV_W - 1, d_conv), lambda i: (i, 0, 0)),
        ],
        out_specs=[
            pl.BlockSpec((rows, d), lambda i: (i, 0)),
            pl.BlockSpec((tb, H_REC, dk, dk), lambda i: (i, 0, 0, 0)),
            pl.BlockSpec((tb, CONV_W - 1, d_conv), lambda i: (i, 0, 0)),
        ],
        out_shape=[
            jax.ShapeDtypeStruct((n_rows, d), F32),
            jax.ShapeDtypeStruct((bsz, H_REC, dk, dk), F32),
            jax.ShapeDtypeStruct((bsz, CONV_W - 1, d_conv), F32),
        ],
        scratch_shapes=[
            pltpu.VMEM((rows, N_MOD * d), F32),
            pltpu.VMEM((CONV_W - 1, rows, d_conv), F32),
            pltpu.VMEM((SUBLANES + rows, d_conv), F32),
            pltpu.VMEM((rows, d_rec + d_conv), BF16),
        ],
        compiler_params=pltpu.CompilerParams(
            dimension_semantics=("arbitrary",), vmem_limit_bytes=VMEM_LIMIT_BYTES),
        name="sample_layer",
    )(x2d, mod, lower_bounds, w_in, w_conv, g_onorm, w_out, w_up, w_down, g_final, rec_in, conv_in)


def kernel(x_prompt, x_sample, state_rec, state_conv, c_prompt, c_sample, lower_bounds, w_ada,
           b_ada, w_in, w_conv, g_onorm, w_out, w_up, w_down, g_final):
    depth = w_in.shape[0]
    assert depth == 1, "single-layer trunk"
    bsz_s, seq_s, d = x_sample.shape

    mod_s, mod_p = _modulation(c_sample, c_prompt, w_ada[0], b_ada)

    g_final = g_final.reshape(1, d)
    y_p, rec_p, conv_p, w_in_b, w_out_b, w_up_b, w_down_b = _prompt_layer(
        x_prompt, mod_p, lower_bounds, w_in[0], w_conv[0], g_onorm, w_out[0], w_up[0], w_down[0],
        g_final)
    y_s, rec_s, conv_s = _sample_layer(
        x_sample.reshape(bsz_s * seq_s, d), seq_s, mod_s, lower_bounds, w_in_b, w_conv[0], g_onorm,
        w_out_b, w_up_b, w_down_b, g_final, state_rec[0], state_conv[0])
    return (y_p, y_s.reshape(bsz_s, seq_s, d), rec_p[None], conv_p[None], rec_s[None],
            conv_s[None])
```

```python
import functools

import jax
import jax.numpy as jnp
from jax import lax
from jax.experimental import pallas as pl
from jax.experimental.pallas import tpu as pltpu

F32 = jnp.float32
BF16 = jnp.bfloat16

EPS = 1e-6
H_REC = 4
CONV_W = 3
N_MOD = 6
N_PROJ_GROUPS = 7
CHUNK = 64

SUBLANES = 8
VMEM_LIMIT_BYTES = 56 * 1024 * 1024

PROMPT_TILE = 512
SAMPLE_TILE_B = 16
MOD_TILE_K = 256
MLP_BLOCKS = 4
WEIGHT_STAGE_SHAPE = (512, 512)
WEIGHT_STAGE_SLOTS = 4


def _const_spec(shape):
    zeros = (0,) * len(shape)
    return pl.BlockSpec(shape, lambda *_: zeros, pipeline_mode=pl.Buffered(1))


def _rms(x):
    return x * lax.rsqrt(jnp.mean(x * x, axis=-1, keepdims=True) + EPS)


def _silu(x):
    return x * jax.nn.sigmoid(x)


def _bdot(a, b):
    return jnp.dot(a, b, preferred_element_type=F32)


def _dot_nt(a, b):
    return lax.dot_general(a, b, (((1,), (1,)), ((), ())), preferred_element_type=F32)


def _dot_tn(a, b):
    return lax.dot_general(a, b, (((0,), (0,)), ((), ())), preferred_element_type=F32)


def _group_pos(shape, group):
    return lax.broadcasted_iota(jnp.int32, shape, 0) % group


def _tile_cumprod(x):
    rows, n = x.shape
    x = x.reshape(rows // SUBLANES, SUBLANES, n)
    pos = lax.broadcasted_iota(jnp.int32, x.shape, 1)
    step = 1
    while step < SUBLANES:
        x = x * jnp.where(pos >= step, pltpu.roll(x, step, axis=1), 1.0)
        step *= 2
    return x.reshape(rows, n)


def _tile_last(x):
    rows, n = x.shape
    x = x.reshape(rows // SUBLANES, SUBLANES, n)
    return jnp.broadcast_to(x[:, SUBLANES - 1:, :], x.shape).reshape(rows, n)


def _chain_tiles(y):
    tiles, carry = [], None
    for r in range(0, y.shape[0], SUBLANES):
        t = y[r:r + SUBLANES]
        if carry is not None:
            t = t * carry
        tiles.append(t)
        carry = t[SUBLANES - 1:, :]
    return jnp.concatenate(tiles, axis=0)


def _layer_lower_bound(lb_ref, layer):
    lb = lb_ref[...]
    e = jnp.exp(lb - jnp.max(lb, axis=0, keepdims=True))
    sm = e / jnp.sum(e, axis=0, keepdims=True)
    return jnp.sum(sm[: layer + 1], axis=0, keepdims=True)


def _gates(fz, lb):
    sig = jax.nn.sigmoid(fz)
    f = lb + (1.0 - lb) * sig
    k = (1.0 - lb) * (1.0 - sig)
    return f, k


def _decay_terms(decay, row):
    hi = decay.astype(BF16).astype(F32)
    rest = decay - hi
    mid = rest.astype(BF16).astype(F32)
    lo = rest - mid
    return jnp.where(row == 0, hi, jnp.where(row == 1, mid, jnp.where(row == 2, lo, 0.0)))


def _decay_selector(n_rows, width):
    sub = lax.broadcasted_iota(jnp.int32, (n_rows, width), 0)
    return jnp.concatenate(
        [jnp.zeros((n_rows, width), F32), jnp.where(sub < 3, 1.0, 0.0)], axis=1)


def _head_out(o, g, g_onorm):
    dv = o.shape[-1] // H_REC
    pieces = []
    for h in range(H_REC):
        sl = slice(h * dv, (h + 1) * dv)
        pieces.append(_rms(o[:, sl]))
    return (jnp.concatenate(pieces, axis=-1) * g_onorm) * _silu(g)


def _modulate(x, shift, scale):
    return (_rms(x) * (1.0 + scale) + shift).astype(BF16)


def _mlp_cols(w_up_ref, j):
    blk = w_up_ref.shape[1] // MLP_BLOCKS
    return slice(j * blk, (j + 1) * blk)


def _mlp_up(h2, w_up_ref, j):
    up = jnp.maximum(_bdot(h2, w_up_ref[:, _mlp_cols(w_up_ref, j)]), 0.0)
    return (up * up).astype(BF16)


def _mlp_down(up, w_up_ref, w_down_ref, j):
    return _bdot(up, w_down_ref[_mlp_cols(w_up_ref, j), :])


def _final_norm(x, g2, mlp, g_final_ref):
    return _rms(x + g2 * mlp) * g_final_ref[...]


def _mod_kernel(c_s_ref, c_p_ref, w_ref, b_ref, o_s_ref, o_p_ref):
    k = pl.program_id(0)
    n_s = c_s_ref.shape[0]
    kb = w_ref.shape[0]
    cols = pl.ds(pl.multiple_of(k * kb, kb), kb)
    c = jnp.concatenate([c_s_ref[:, cols], c_p_ref[:, cols]], axis=0)
    part = _bdot(_silu(c).astype(BF16), w_ref[...].astype(BF16))

    @pl.when(k == 0)
    def _():
        o_s_ref[...] = part[:n_s] + b_ref[...]
        o_p_ref[...] = part[n_s:] + b_ref[...]

    @pl.when(k > 0)
    def _():
        o_s_ref[...] += part[:n_s]
        o_p_ref[...] += part[n_s:]


def _modulation(c_s, c_p, w_ada, b_ada):
    (n_s, d), n_p = c_s.shape, c_p.shape[0]
    n = w_ada.shape[1]
    assert n_s % SUBLANES == 0 and n_p % SUBLANES == 0 and d % MOD_TILE_K == 0
    return pl.pallas_call(
        _mod_kernel,
        grid=(d // MOD_TILE_K,),
        in_specs=[
            pl.BlockSpec((n_s, d), lambda k: (0, 0)),
            pl.BlockSpec((n_p, d), lambda k: (0, 0)),
            pl.BlockSpec((MOD_TILE_K, n), lambda k: (k, 0)),
            pl.BlockSpec((1, n), lambda k: (0, 0)),
        ],
        out_specs=[
            pl.BlockSpec((n_s, n), lambda k: (0, 0)),
            pl.BlockSpec((n_p, n), lambda k: (0, 0)),
        ],
        out_shape=[
            jax.ShapeDtypeStruct((n_s, n), F32),
            jax.ShapeDtypeStruct((n_p, n), F32),
        ],
        compiler_params=pltpu.CompilerParams(
            dimension_semantics=("arbitrary",), vmem_limit_bytes=VMEM_LIMIT_BYTES),
        name="adaln_modulation",
    )(c_s, c_p, w_ada, b_ada)


def _split_groups(proj, n_groups):
    width = proj.shape[1] // n_groups
    return tuple(proj[:, p * width:(p + 1) * width] for p in range(n_groups))


def _split_proj(proj):
    return _split_groups(proj, N_PROJ_GROUPS)


def _weight_blocks(w_hbm, w_vmem):
    rows, cols = w_hbm.shape
    br, bc = WEIGHT_STAGE_SHAPE
    assert rows % br == 0 and cols % bc == 0
    return [(w_hbm.at[r:r + br, c:c + bc], w_vmem.at[r:r + br, c:c + bc])
            for r in range(0, rows, br) for c in range(0, cols, bc)]


def _stage_copy(src, stage_ref, sem_ref, i):
    slot = i % WEIGHT_STAGE_SLOTS
    return pltpu.make_async_copy(src, stage_ref.at[slot], sem_ref.at[slot])


def _load_weights_as_bf16(w_hbm_refs, w_vmem_refs, stage_ref, sem_ref):
    blocks = [blk for w_hbm, w_vmem in zip(w_hbm_refs, w_vmem_refs)
              for blk in _weight_blocks(w_hbm, w_vmem)]
    ahead = WEIGHT_STAGE_SLOTS - 1
    for i in range(min(ahead, len(blocks))):
        _stage_copy(blocks[i][0], stage_ref, sem_ref, i).start()
    for i, (src, dst) in enumerate(blocks):
        if i + ahead < len(blocks):
            _stage_copy(blocks[i + ahead][0], stage_ref, sem_ref, i + ahead).start()
        _stage_copy(src, stage_ref, sem_ref, i).wait()
        dst[...] = stage_ref[i % WEIGHT_STAGE_SLOTS].astype(BF16)


def _alternate(*stages):
    stages = list(stages)
    while stages:
        for stage in list(stages):
            try:
                next(stage)
            except StopIteration:
                stages.remove(stage)


def _mlp_stage(x1_ref, mod, w_up_ref, w_down_ref, g_final_ref, y_ref):
    d = x1_ref.shape[1]
    sh2, sc2, g2 = (mod[:, i * d:(i + 1) * d] for i in range(3, N_MOD))
    h2 = _modulate(x1_ref[...], sh2, sc2)
    ups = [_mlp_up(h2, w_up_ref, 0)]
    yield
    mlp = None
    for j in range(MLP_BLOCKS):
        if j + 1 < MLP_BLOCKS:
            ups.append(_mlp_up(h2, w_up_ref, j + 1))
        part = _mlp_down(ups[j], w_up_ref, w_down_ref, j)
        mlp = part if mlp is None else mlp + part
        if j + 2 < MLP_BLOCKS:
            yield
    y_ref[...] = _final_norm(x1_ref[...], g2, mlp, g_final_ref)
    yield


def _prompt_mix_stage(x_ref, mod, lb_ref, w_in_ref, w_conv_ref, g_onorm_ref, w_out_ref,
                      st_ref, ubuf_ref, mix_ref, x1_ref, rec_ref, conv_ref, is_last):
    tl, d = x_ref.shape
    d_rec = lb_ref.shape[1]
    dk = d_rec // H_REC
    n_chunks = tl // CHUNK
    heads = [slice(hd * dk, (hd + 1) * dk) for hd in range(H_REC)]
    sh1, sc1, g1 = (mod[:, i * d:(i + 1) * d] for i in range(3))

    x = x_ref[...]
    h = _modulate(x, sh1, sc1)
    n_rec = 4 * d_rec
    gb, gc, hv = _split_groups(_bdot(h, w_in_ref[:, n_rec:]), 3)
    d_conv = gb.shape[1]
    yield
    q, fz, iv, g = _split_groups(_bdot(h, w_in_ref[:, :n_rec]), 4)
    yield

    lb = _layer_lower_bound(lb_ref, 0)
    row = lax.broadcasted_iota(jnp.int32, (CHUNK, CHUNK), 0)
    col = lax.broadcasted_iota(jnp.int32, (CHUNK, CHUNK), 1)
    causal = row >= col
    f_all, k_all = _gates(fz, lb)
    prod_tiles = _tile_cumprod(f_all)
    q_dec, decay, scores, upd = [], [], [], []
    for c in range(n_chunks):
        rows = slice(c * CHUNK, (c + 1) * CHUNK)
        prod = _chain_tiles(prod_tiles[rows])
        dec = prod[CHUNK - 1:CHUNK, :]
        q_dec.append((q[rows] * prod).astype(BF16))
        k_inv = k_all[rows] / prod
        k_dec = k_inv.astype(BF16)
        k_end = (k_inv * dec).astype(BF16)
        decay.append([jnp.broadcast_to(dec[:, sl], (dk, dk)).T for sl in heads])
        v = iv[rows].astype(BF16)
        scores.append([jnp.where(causal, _dot_nt(q_dec[c][:, sl], k_dec[:, sl]), 0.0).astype(BF16)
                       for sl in heads])
        upd.append([_dot_tn(k_end[:, sl], v[:, sl]) for sl in heads])
    yield

    st = [st_ref[hd] for hd in range(H_REC)]
    o_chunks = []
    for c in range(n_chunks):
        v = iv[c * CHUNK:(c + 1) * CHUNK].astype(BF16)
        o_heads = []
        for hd, sl in enumerate(heads):
            o_heads.append(_bdot(scores[c][hd], v[:, sl])
                           + _bdot(q_dec[c][:, sl], st[hd].astype(BF16)))
            st[hd] = st[hd] * decay[c][hd] + upd[c][hd]
        o_chunks.append(jnp.concatenate(o_heads, axis=-1))
    for hd in range(H_REC):
        st_ref[hd] = st[hd]
    yield

    o_rec = _head_out(jnp.concatenate(o_chunks, axis=0), g, g_onorm_ref[...])
    mix_ref[:, 0:d_rec] = o_rec.astype(BF16)

    u = gc * hv
    ubuf_ref[SUBLANES:SUBLANES + tl, :] = u
    w_conv = w_conv_ref[...]
    y_conv = (w_conv[0:1] * ubuf_ref[SUBLANES - 2:SUBLANES - 2 + tl, :]
              + w_conv[1:2] * ubuf_ref[SUBLANES - 1:SUBLANES - 1 + tl, :]
              + w_conv[2:3] * u)
    mix_ref[:, d_rec:d_rec + d_conv] = (gb * y_conv).astype(BF16)
    tail = u[tl - (CONV_W - 1):tl]
    ubuf_ref[SUBLANES - (CONV_W - 1):SUBLANES, :] = tail
    yield

    x1 = x + g1 * _bdot(mix_ref[...], w_out_ref[...])
    x1_ref[...] = x1

    @pl.when(is_last)
    def _():
        for hd in range(H_REC):
            rec_ref[hd] = st_ref[hd]
        conv_ref[...] = tail


def _prompt_kernel(x_ref, mod_ref, lb_ref, w_in_hbm, w_conv_ref, g_onorm_ref, w_out_hbm,
                   w_up_hbm, w_down_hbm, g_final_ref,
                   y_ref, rec_ref, conv_ref, w_in_out, w_out_out, w_up_out, w_down_out,
                   st_ref, ubuf_ref, mix_ref, x1_ref,
                   w_in_ref, w_out_ref, w_up_ref, w_down_ref, stage_ref, stage_sem, out_sem,
                   *, n_tiles, n_steps):
    s = pl.program_id(0)
    tile = jnp.minimum(s, n_steps - 1)
    l = lax.rem(tile, n_tiles)
    d = x_ref.shape[1]

    w_vmem = (w_in_ref, w_out_ref, w_up_ref, w_down_ref)
    w_outs = (w_in_out, w_out_out, w_up_out, w_down_out)

    def bf16_export(i):
        return pltpu.make_async_copy(w_vmem[i], w_outs[i], out_sem.at[i])

    @pl.when(s == 0)
    def _():
        _load_weights_as_bf16((w_in_hbm, w_out_hbm, w_up_hbm, w_down_hbm), w_vmem,
                              stage_ref, stage_sem)
        for i in range(len(w_vmem)):
            bf16_export(i).start()

    @pl.when(s == n_steps)
    def _():
        for i in range(len(w_vmem)):
            bf16_export(i).wait()

    @pl.when(l == 0)
    def _():
        st_ref[...] = jnp.zeros_like(st_ref)
        ubuf_ref[0:SUBLANES, :] = jnp.zeros((SUBLANES, ubuf_ref.shape[1]), F32)

    def mix_stage():
        mod = mod_ref[pl.ds(lax.div(tile, n_tiles), 1), :]
        return _prompt_mix_stage(x_ref, mod, lb_ref, w_in_ref, w_conv_ref, g_onorm_ref,
                                 w_out_ref, st_ref, ubuf_ref, mix_ref, x1_ref,
                                 rec_ref, conv_ref, l == n_tiles - 1)

    def mlp_stage():
        mod_prev = mod_ref[pl.ds(lax.div(s - 1, n_tiles), 1), :]
        return _mlp_stage(x1_ref, mod_prev, w_up_ref, w_down_ref, g_final_ref, y_ref)

    @pl.when(s == 0)
    def _():
        _alternate(mix_stage())

    @pl.when((s > 0) & (s < n_steps))
    def _():
        _alternate(mix_stage(), mlp_stage())

    @pl.when(s == n_steps)
    def _():
        _alternate(mlp_stage())


def _prompt_layer(x, mod_p, lower_bounds, w_in, w_conv, g_onorm, w_out, w_up, w_down, g_final):
    bsz, seq, d = x.shape
    d_rec = lower_bounds.shape[1]
    dk = d_rec // H_REC
    d_conv = w_conv.shape[1]
    tl = PROMPT_TILE
    n_tiles = seq // tl
    n_steps = bsz * n_tiles
    assert seq % tl == 0 and tl % CHUNK == 0
    mats = (w_in, w_out, w_up, w_down)
    hbm = pl.BlockSpec(memory_space=pl.ANY)

    def mix_tile(s):
        t = jnp.minimum(s, n_steps - 1)
        return lax.div(t, n_tiles), lax.rem(t, n_tiles)

    def mlp_tile(s):
        t = jnp.maximum(s - 1, 0)
        return lax.div(t, n_tiles), lax.rem(t, n_tiles)

    return pl.pallas_call(
        functools.partial(_prompt_kernel, n_tiles=n_tiles, n_steps=n_steps),
        grid=(n_steps + 1,),
        in_specs=[
            pl.BlockSpec((None, tl, d), lambda s: (*mix_tile(s), 0)),
            _const_spec(mod_p.shape),
            _const_spec(lower_bounds.shape),
            hbm,
            _const_spec(w_conv.shape),
            _const_spec(g_onorm.shape),
            hbm,
            hbm,
            hbm,
            _const_spec(g_final.shape),
        ],
        out_specs=[
            pl.BlockSpec((None, tl, d), lambda s: (*mlp_tile(s), 0)),
            pl.BlockSpec((None, H_REC, dk, dk), lambda s: (mix_tile(s)[0], 0, 0, 0)),
            pl.BlockSpec((None, CONV_W - 1, d_conv), lambda s: (mix_tile(s)[0], 0, 0)),
        ] + [hbm] * len(mats),
        out_shape=[
            jax.ShapeDtypeStruct((bsz, seq, d), F32),
            jax.ShapeDtypeStruct((bsz, H_REC, dk, dk), F32),
            jax.ShapeDtypeStruct((bsz, CONV_W - 1, d_conv), F32),
        ] + [jax.ShapeDtypeStruct(w.shape, BF16) for w in mats],
        scratch_shapes=[
            pltpu.VMEM((H_REC, dk, dk), F32),
            pltpu.VMEM((SUBLANES + tl, d_conv), F32),
            pltpu.VMEM((tl, d_rec + d_conv), BF16),
            pltpu.VMEM((tl, d), F32),
        ] + [pltpu.VMEM(w.shape, BF16) for w in mats] + [
            pltpu.VMEM((WEIGHT_STAGE_SLOTS, *WEIGHT_STAGE_SHAPE), F32),
            pltpu.SemaphoreType.DMA((WEIGHT_STAGE_SLOTS,)),
            pltpu.SemaphoreType.DMA((len(mats),)),
        ],
        compiler_params=pltpu.CompilerParams(
            dimension_semantics=("arbitrary",), vmem_limit_bytes=VMEM_LIMIT_BYTES),
        name="prompt_layer",
    )(x, mod_p, lower_bounds, w_in, w_conv, g_onorm, w_out, w_up, w_down, g_final)


def _sample_kernel(x_ref, mod_ref, lb_ref, w_in_ref, w_conv_ref, g_onorm_ref, w_out_ref,
                   w_up_ref, w_down_ref, g_final_ref, rec_in_ref, conv_in_ref,
                   y_ref, rec_ref, conv_ref,
                   modx_ref, cbx_ref, ubuf_ref, mix_ref, *, seq):
    rows, d = x_ref.shape
    tb = rows // seq
    d_rec = lb_ref.shape[1]
    dk = d_rec // H_REC
    d_conv = conv_in_ref.shape[2]

    for b in range(tb):
        r = slice(b * seq, (b + 1) * seq)
        modx_ref[r, :] = jnp.broadcast_to(mod_ref[b:b + 1, :], (seq, mod_ref.shape[1]))
        for j in range(CONV_W - 1):
            cbx_ref[j, r, :] = jnp.broadcast_to(conv_in_ref[b, j:j + 1, :], (seq, d_conv))

    sh1, sc1, g1, sh2, sc2, g2 = (modx_ref[:, i * d:(i + 1) * d] for i in range(N_MOD))

    x = x_ref[...]
    proj = _bdot(_modulate(x, sh1, sc1), w_in_ref[...])
    q, fz, iv, g, gb, gc, hv = _split_proj(proj)

    lb = _layer_lower_bound(lb_ref, 0)
    row = lax.broadcasted_iota(jnp.int32, (rows, rows), 0)
    col = lax.broadcasted_iota(jnp.int32, (rows, rows), 1)
    causal = ((row // seq) == (col // seq)) & (row >= col)
    f, k = _gates(fz, lb)
    prod = _tile_cumprod(f)
    decay = _tile_last(prod)
    q_dec = (q * prod).astype(BF16)
    k_inv = k / prod
    k_dec = k_inv.astype(BF16)
    k_end = k_inv * decay
    vb = iv.astype(BF16)

    dec3 = _decay_terms(decay, _group_pos((rows, d_rec), seq))
    zeros_blk = jnp.zeros((seq, dk), F32)
    rhs_bottom = _decay_selector(seq, dk)

    heads = [slice(hd * dk, (hd + 1) * dk) for hd in range(H_REC)]

    def update_states(elements):
        for b in elements:
            r = slice(b * seq, (b + 1) * seq)
            for hd, sl in enumerate(heads):
                lhs = jnp.concatenate([k_end[r, sl], dec3[r, sl]], axis=0)
                rhs = jnp.concatenate(
                    [jnp.concatenate([iv[r, sl], zeros_blk], axis=1), rhs_bottom], axis=0)
                both = _dot_tn(lhs.astype(BF16), rhs.astype(BF16))
                rec_ref[b, hd] = both[:, dk:] * rec_in_ref[b, hd] + both[:, :dk]

    scores = [jnp.where(causal, _dot_nt(q_dec[:, sl], k_dec[:, sl]), 0.0).astype(BF16)
              for sl in heads]
    o_inter = jnp.concatenate(
        [jnp.concatenate([_bdot(q_dec[b * seq:(b + 1) * seq, sl], rec_in_ref[b, hd].astype(BF16))
                          for hd, sl in enumerate(heads)], axis=-1)
         for b in range(tb)], axis=0)
    o_intra = jnp.concatenate([_bdot(scores[hd], vb[:, sl]) for hd, sl in enumerate(heads)],
                              axis=-1)
    o_rec = _head_out(o_intra + o_inter, g, g_onorm_ref[...])
    mix_ref[:, 0:d_rec] = o_rec.astype(BF16)

    u = gc * hv
    ubuf_ref[SUBLANES:SUBLANES + rows, :] = u
    tok = _group_pos((rows, d_conv), seq)
    u_m1 = jnp.where(tok >= 1, ubuf_ref[SUBLANES - 1:SUBLANES - 1 + rows, :], cbx_ref[1])
    u_m2 = jnp.where(tok >= 2, ubuf_ref[SUBLANES - 2:SUBLANES - 2 + rows, :],
                     jnp.where(tok == 1, cbx_ref[1], cbx_ref[0]))
    w_conv = w_conv_ref[...]
    y_conv = w_conv[0:1] * u_m2 + w_conv[1:2] * u_m1 + w_conv[2:3] * u
    mix_ref[:, d_rec:d_rec + d_conv] = (gb * y_conv).astype(BF16)
    for b in range(tb):
        conv_ref[b] = u[(b + 1) * seq - (CONV_W - 1):(b + 1) * seq]

    x1 = x + g1 * _bdot(mix_ref[...], w_out_ref[...])
    update_states(range(tb // 2))
    h2 = _modulate(x1, sh2, sc2)
    mlp = _mlp_down(_mlp_up(h2, w_up_ref, 0), w_up_ref, w_down_ref, 0)
    for j in range(1, MLP_BLOCKS):
        mlp += _mlp_down(_mlp_up(h2, w_up_ref, j), w_up_ref, w_down_ref, j)
    update_states(range(tb // 2, tb))
    y_ref[...] = _final_norm(x1, g2, mlp, g_final_ref)


def _sample_layer(x2d, seq, mod, lower_bounds, w_in, w_conv, g_onorm, w_out, w_up, w_down,
                  g_final, rec_in, conv_in):
    n_rows, d = x2d.shape
    bsz = n_rows // seq
    d_rec = lower_bounds.shape[1]
    dk = d_rec // H_REC
    d_conv = w_conv.shape[1]
    tb = SAMPLE_TILE_B
    rows = tb * seq
    assert bsz % tb == 0 and seq == SUBLANES and seq >= CONV_W
    return pl.pallas_call(
        functools.partial(_sample_kernel, seq=seq),
        grid=(bsz // tb,),
        in_specs=[
            pl.BlockSpec((rows, d), lambda i: (i, 0)),
            pl.BlockSpec((tb, N_MOD * d), lambda i: (i, 0)),
            _const_spec(lower_bounds.shape),
            _const_spec(w_in.shape),
            _const_spec(w_conv.shape),
            _const_spec(g_onorm.shape),
            _const_spec(w_out.shape),
            _const_spec(w_up.shape),
            _const_spec(w_down.shape),
            _const_spec(g_final.shape),
            pl.BlockSpec((tb, H_REC, dk, dk), lambda i: (i, 0, 0, 0)),
            pl.BlockSpec((tb, CONV_W - 1, d_conv), lambda i: (i, 0, 0)),
        ],
        out_specs=[
            pl.BlockSpec((rows, d), lambda i: (i, 0)),
            pl.BlockSpec((tb, H_REC, dk, dk), lambda i: (i, 0, 0, 0)),
            pl.BlockSpec((tb, CONV_W - 1, d_conv), lambda i: (i, 0, 0)),
        ],
        out_shape=[
            jax.ShapeDtypeStruct((n_rows, d), F32),
            jax.ShapeDtypeStruct((bsz, H_REC, dk, dk), F32),
            jax.ShapeDtypeStruct((bsz, CONV_W - 1, d_conv), F32),
        ],
        scratch_shapes=[
            pltpu.VMEM((rows, N_MOD * d), F32),
            pltpu.VMEM((CONV_W - 1, rows, d_conv), F32),
            pltpu.VMEM((SUBLANES + rows, d_conv), F32),
            pltpu.VMEM((rows, d_rec + d_conv), BF16),
        ],
        compiler_params=pltpu.CompilerParams(
            dimension_semantics=("arbitrary",), vmem_limit_bytes=VMEM_LIMIT_BYTES),
        name="sample_layer",
    )(x2d, mod, lower_bounds, w_in, w_conv, g_onorm, w_out, w_up, w_down, g_final, rec_in, conv_in)


def kernel(x_prompt, x_sample, state_rec, state_conv, c_prompt, c_sample, lower_bounds, w_ada,
           b_ada, w_in, w_conv, g_onorm, w_out, w_up, w_down, g_final):
    depth = w_in.shape[0]
    assert depth == 1, "single-layer trunk"
    bsz_s, seq_s, d = x_sample.shape

    mod_s, mod_p = _modulation(c_sample, c_prompt, w_ada[0], b_ada)

    g_final = g_final.reshape(1, d)
    y_p, rec_p, conv_p, w_in_b, w_out_b, w_up_b, w_down_b = _prompt_layer(
        x_prompt, mod_p, lower_bounds, w_in[0], w_conv[0], g_onorm, w_out[0], w_up[0], w_down[0],
        g_final)
    y_s, rec_s, conv_s = _sample_layer(
        x_sample.reshape(bsz_s * seq_s, d), seq_s, mod_s, lower_bounds, w_in_b, w_conv[0], g_onorm,
        w_out_b, w_up_b, w_down_b, g_final, state_rec[0], state_conv[0])
    return (y_p, y_s.reshape(bsz_s, seq_s, d), rec_p[None], conv_p[None], rec_s[None],
            conv_s[None])
```

```python
import functools

import jax
import jax.numpy as jnp
from jax import lax
from jax.experimental import pallas as pl
from jax.experimental.pallas import tpu as pltpu

F32 = jnp.float32
BF16 = jnp.bfloat16

EPS = 1e-6
H_REC = 4
CONV_W = 3
N_MOD = 6
N_PROJ_GROUPS = 7
CHUNK = 64

SUBLANES = 8
VMEM_LIMIT_BYTES = 56 * 1024 * 1024

PROMPT_TILE = 512
SAMPLE_TILE_B = 16
MOD_TILE_K = 256
MLP_BLOCKS = 4
WEIGHT_STAGE_SHAPE = (512, 512)
WEIGHT_STAGE_SLOTS = 6


def _const_spec(shape):
    zeros = (0,) * len(shape)
    return pl.BlockSpec(shape, lambda *_: zeros, pipeline_mode=pl.Buffered(1))


def _rms(x):
    return x * lax.rsqrt(jnp.mean(x * x, axis=-1, keepdims=True) + EPS)


def _silu(x):
    return x * jax.nn.sigmoid(x)


def _bdot(a, b):
    return jnp.dot(a, b, preferred_element_type=F32)


def _dot_nt(a, b):
    return lax.dot_general(a, b, (((1,), (1,)), ((), ())), preferred_element_type=F32)


def _dot_tn(a, b):
    return lax.dot_general(a, b, (((0,), (0,)), ((), ())), preferred_element_type=F32)


def _group_pos(shape, group):
    return lax.broadcasted_iota(jnp.int32, shape, 0) % group


def _tile_cumprod(x):
    rows, n = x.shape
    x = x.reshape(rows // SUBLANES, SUBLANES, n)
    pos = lax.broadcasted_iota(jnp.int32, x.shape, 1)
    step = 1
    while step < SUBLANES:
        x = x * jnp.where(pos >= step, pltpu.roll(x, step, axis=1), 1.0)
        step *= 2
    return x.reshape(rows, n)


def _tile_last(x):
    rows, n = x.shape
    x = x.reshape(rows // SUBLANES, SUBLANES, n)
    return jnp.broadcast_to(x[:, SUBLANES - 1:, :], x.shape).reshape(rows, n)


def _chain_tiles(y):
    tiles, carry = [], None
    for r in range(0, y.shape[0], SUBLANES):
        t = y[r:r + SUBLANES]
        if carry is not None:
            t = t * carry
        tiles.append(t)
        carry = t[SUBLANES - 1:, :]
    return jnp.concatenate(tiles, axis=0)


def _layer_lower_bound(lb_ref, layer):
    lb = lb_ref[...]
    e = jnp.exp(lb - jnp.max(lb, axis=0, keepdims=True))
    sm = e / jnp.sum(e, axis=0, keepdims=True)
    return jnp.sum(sm[: layer + 1], axis=0, keepdims=True)


def _gates(fz, lb):
    sig = jax.nn.sigmoid(fz)
    f = lb + (1.0 - lb) * sig
    k = (1.0 - lb) * (1.0 - sig)
    return f, k


def _decay_terms(decay, row):
    hi = decay.astype(BF16).astype(F32)
    rest = decay - hi
    mid = rest.astype(BF16).astype(F32)
    lo = rest - mid
    return jnp.where(row == 0, hi, jnp.where(row == 1, mid, jnp.where(row == 2, lo, 0.0)))


def _decay_selector(n_rows, width):
    sub = lax.broadcasted_iota(jnp.int32, (n_rows, width), 0)
    return jnp.concatenate(
        [jnp.zeros((n_rows, width), F32), jnp.where(sub < 3, 1.0, 0.0)], axis=1)


def _head_out(o, g, g_onorm):
    dv = o.shape[-1] // H_REC
    pieces = []
    for h in range(H_REC):
        sl = slice(h * dv, (h + 1) * dv)
        pieces.append(_rms(o[:, sl]))
    return (jnp.concatenate(pieces, axis=-1) * g_onorm) * _silu(g)


def _modulate(x, shift, scale):
    return (_rms(x) * (1.0 + scale) + shift).astype(BF16)


def _mlp_cols(w_up_ref, j):
    blk = w_up_ref.shape[1] // MLP_BLOCKS
    return slice(j * blk, (j + 1) * blk)


def _mlp_up(h2, w_up_ref, j):
    up = jnp.maximum(_bdot(h2, w_up_ref[:, _mlp_cols(w_up_ref, j)]), 0.0)
    return (up * up).astype(BF16)


def _mlp_down(up, w_up_ref, w_down_ref, j):
    return _bdot(up, w_down_ref[_mlp_cols(w_up_ref, j), :])


def _final_norm(x, g2, mlp, g_final_ref):
    return _rms(x + g2 * mlp) * g_final_ref[...]


def _mod_kernel(c_s_ref, c_p_ref, w_ref, b_ref, o_s_ref, o_p_ref):
    k = pl.program_id(0)
    n_s = c_s_ref.shape[0]
    kb = w_ref.shape[0]
    cols = pl.ds(pl.multiple_of(k * kb, kb), kb)
    c = jnp.concatenate([c_s_ref[:, cols], c_p_ref[:, cols]], axis=0)
    part = _bdot(_silu(c).astype(BF16), w_ref[...].astype(BF16))

    @pl.when(k == 0)
    def _():
        o_s_ref[...] = part[:n_s] + b_ref[...]
        o_p_ref[...] = part[n_s:] + b_ref[...]

    @pl.when(k > 0)
    def _():
        o_s_ref[...] += part[:n_s]
        o_p_ref[...] += part[n_s:]


def _modulation(c_s, c_p, w_ada, b_ada):
    (n_s, d), n_p = c_s.shape, c_p.shape[0]
    n = w_ada.shape[1]
    assert n_s % SUBLANES == 0 and n_p % SUBLANES == 0 and d % MOD_TILE_K == 0
    return pl.pallas_call(
        _mod_kernel,
        grid=(d // MOD_TILE_K,),
        in_specs=[
            pl.BlockSpec((n_s, d), lambda k: (0, 0)),
            pl.BlockSpec((n_p, d), lambda k: (0, 0)),
            pl.BlockSpec((MOD_TILE_K, n), lambda k: (k, 0)),
            pl.BlockSpec((1, n), lambda k: (0, 0)),
        ],
        out_specs=[
            pl.BlockSpec((n_s, n), lambda k: (0, 0)),
            pl.BlockSpec((n_p, n), lambda k: (0, 0)),
        ],
        out_shape=[
            jax.ShapeDtypeStruct((n_s, n), F32),
            jax.ShapeDtypeStruct((n_p, n), F32),
        ],
        compiler_params=pltpu.CompilerParams(
            dimension_semantics=("arbitrary",), vmem_limit_bytes=VMEM_LIMIT_BYTES),
        name="adaln_modulation",
    )(c_s, c_p, w_ada, b_ada)


def _split_proj(proj):
    width = proj.shape[1] // N_PROJ_GROUPS
    return tuple(proj[:, p * width:(p + 1) * width] for p in range(N_PROJ_GROUPS))


def _weight_blocks(w_hbm, w_vmem):
    rows, cols = w_hbm.shape
    br, bc = WEIGHT_STAGE_SHAPE
    assert rows % br == 0 and cols % bc == 0
    return [(w_hbm.at[r:r + br, c:c + bc], w_vmem.at[r:r + br, c:c + bc])
            for r in range(0, rows, br) for c in range(0, cols, bc)]


def _stage_copy(src, stage_ref, sem_ref, i):
    slot = i % WEIGHT_STAGE_SLOTS
    return pltpu.make_async_copy(src, stage_ref.at[slot], sem_ref.at[slot])


def _load_weights_as_bf16(w_hbm_refs, w_vmem_refs, stage_ref, sem_ref):
    blocks = [blk for w_hbm, w_vmem in zip(w_hbm_refs, w_vmem_refs)
              for blk in _weight_blocks(w_hbm, w_vmem)]
    ahead = WEIGHT_STAGE_SLOTS - 1
    for i in range(min(ahead, len(blocks))):
        _stage_copy(blocks[i][0], stage_ref, sem_ref, i).start()
    for i, (src, dst) in enumerate(blocks):
        if i + ahead < len(blocks):
            _stage_copy(blocks[i + ahead][0], stage_ref, sem_ref, i + ahead).start()
        _stage_copy(src, stage_ref, sem_ref, i).wait()
        dst[...] = stage_ref[i % WEIGHT_STAGE_SLOTS].astype(BF16)


def _alternate(*stages):
    stages = list(stages)
    while stages:
        for stage in list(stages):
            try:
                next(stage)
            except StopIteration:
                stages.remove(stage)


def _mlp_stage(x1_ref, mod, w_up_ref, w_down_ref, g_final_ref, y_ref):
    d = x1_ref.shape[1]
    sh2, sc2, g2 = (mod[:, i * d:(i + 1) * d] for i in range(3, N_MOD))
    h2 = _modulate(x1_ref[...], sh2, sc2)
    ups = [_mlp_up(h2, w_up_ref, 0)]
    yield
    mlp = None
    for j in range(MLP_BLOCKS):
        if j + 1 < MLP_BLOCKS:
            ups.append(_mlp_up(h2, w_up_ref, j + 1))
        part = _mlp_down(ups[j], w_up_ref, w_down_ref, j)
        mlp = part if mlp is None else mlp + part
        if j + 2 < MLP_BLOCKS:
            yield
    y_ref[...] = _final_norm(x1_ref[...], g2, mlp, g_final_ref)
    yield


def _prompt_mix_stage(x_ref, mod, lb_ref, w_in_ref, w_conv_ref, g_onorm_ref, w_out_ref,
                      st_ref, ubuf_ref, mix_ref, x1_ref, rec_ref, conv_ref, is_last):
    tl, d = x_ref.shape
    d_rec = lb_ref.shape[1]
    dk = d_rec // H_REC
    n_chunks = tl // CHUNK
    heads = [slice(hd * dk, (hd + 1) * dk) for hd in range(H_REC)]
    sh1, sc1, g1 = (mod[:, i * d:(i + 1) * d] for i in range(3))

    x = x_ref[...]
    proj = _bdot(_modulate(x, sh1, sc1), w_in_ref[...])
    q, fz, iv, g, gb, gc, hv = _split_proj(proj)
    d_conv = gb.shape[1]
    yield

    lb = _layer_lower_bound(lb_ref, 0)
    row = lax.broadcasted_iota(jnp.int32, (CHUNK, CHUNK), 0)
    col = lax.broadcasted_iota(jnp.int32, (CHUNK, CHUNK), 1)
    causal = row >= col
    f_all, k_all = _gates(fz, lb)
    prod_tiles = _tile_cumprod(f_all)
    q_dec, decay, scores, upd = [], [], [], []
    for c in range(n_chunks):
        rows = slice(c * CHUNK, (c + 1) * CHUNK)
        prod = _chain_tiles(prod_tiles[rows])
        dec = prod[CHUNK - 1:CHUNK, :]
        q_dec.append((q[rows] * prod).astype(BF16))
        k_inv = k_all[rows] / prod
        k_dec = k_inv.astype(BF16)
        k_end = (k_inv * dec).astype(BF16)
        decay.append([jnp.broadcast_to(dec[:, sl], (dk, dk)).T for sl in heads])
        v = iv[rows].astype(BF16)
        scores.append([jnp.where(causal, _dot_nt(q_dec[c][:, sl], k_dec[:, sl]), 0.0).astype(BF16)
                       for sl in heads])
        upd.append([_dot_tn(k_end[:, sl], v[:, sl]) for sl in heads])
    yield

    st = [st_ref[hd] for hd in range(H_REC)]
    o_chunks = []
    for c in range(n_chunks):
        v = iv[c * CHUNK:(c + 1) * CHUNK].astype(BF16)
        o_heads = []
        for hd, sl in enumerate(heads):
            o_heads.append(_bdot(scores[c][hd], v[:, sl])
                           + _bdot(q_dec[c][:, sl], st[hd].astype(BF16)))
            st[hd] = st[hd] * decay[c][hd] + upd[c][hd]
        o_chunks.append(jnp.concatenate(o_heads, axis=-1))
    for hd in range(H_REC):
        st_ref[hd] = st[hd]
    yield

    o_rec = _head_out(jnp.concatenate(o_chunks, axis=0), g, g_onorm_ref[...])
    mix_ref[:, 0:d_rec] = o_rec.astype(BF16)

    u = gc * hv
    ubuf_ref[SUBLANES:SUBLANES + tl, :] = u
    w_conv = w_conv_ref[...]
    y_conv = (w_conv[0:1] * ubuf_ref[SUBLANES - 2:SUBLANES - 2 + tl, :]
              + w_conv[1:2] * ubuf_ref[SUBLANES - 1:SUBLANES - 1 + tl, :]
              + w_conv[2:3] * u)
    mix_ref[:, d_rec:d_rec + d_conv] = (gb * y_conv).astype(BF16)
    tail = u[tl - (CONV_W - 1):tl]
    ubuf_ref[SUBLANES - (CONV_W - 1):SUBLANES, :] = tail
    yield

    x1 = x + g1 * _bdot(mix_ref[...], w_out_ref[...])
    x1_ref[...] = x1

    @pl.when(is_last)
    def _():
        for hd in range(H_REC):
            rec_ref[hd] = st_ref[hd]
        conv_ref[...] = tail


def _prompt_kernel(x_ref, mod_ref, lb_ref, w_in_hbm, w_conv_ref, g_onorm_ref, w_out_hbm,
                   w_up_hbm, w_down_hbm, g_final_ref,
                   y_ref, rec_ref, conv_ref, w_in_out, w_out_out, w_up_out, w_down_out,
                   st_ref, ubuf_ref, mix_ref, x1_ref,
                   w_in_ref, w_out_ref, w_up_ref, w_down_ref, stage_ref, stage_sem, out_sem,
                   *, n_tiles, n_steps):
    s = pl.program_id(0)
    tile = jnp.minimum(s, n_steps - 1)
    l = lax.rem(tile, n_tiles)
    d = x_ref.shape[1]

    w_vmem = (w_in_ref, w_out_ref, w_up_ref, w_down_ref)
    w_outs = (w_in_out, w_out_out, w_up_out, w_down_out)

    def bf16_export(i):
        return pltpu.make_async_copy(w_vmem[i], w_outs[i], out_sem.at[i])

    @pl.when(s == 0)
    def _():
        _load_weights_as_bf16((w_in_hbm, w_out_hbm, w_up_hbm, w_down_hbm), w_vmem,
                              stage_ref, stage_sem)
        for i in range(len(w_vmem)):
            bf16_export(i).start()

    @pl.when(s == n_steps)
    def _():
        for i in range(len(w_vmem)):
            bf16_export(i).wait()

    @pl.when(l == 0)
    def _():
        st_ref[...] = jnp.zeros_like(st_ref)
        ubuf_ref[0:SUBLANES, :] = jnp.zeros((SUBLANES, ubuf_ref.shape[1]), F32)

    def mix_stage():
        mod = mod_ref[pl.ds(lax.div(tile, n_tiles), 1), :]
        return _prompt_mix_stage(x_ref, mod, lb_ref, w_in_ref, w_conv_ref, g_onorm_ref,
                                 w_out_ref, st_ref, ubuf_ref, mix_ref, x1_ref,
                                 rec_ref, conv_ref, l == n_tiles - 1)

    def mlp_stage():
        mod_prev = mod_ref[pl.ds(lax.div(s - 1, n_tiles), 1), :]
        return _mlp_stage(x1_ref, mod_prev, w_up_ref, w_down_ref, g_final_ref, y_ref)

    @pl.when(s == 0)
    def _():
        _alternate(mix_stage())

    @pl.when((s > 0) & (s < n_steps))
    def _():
        _alternate(mix_stage(), mlp_stage())

    @pl.when(s == n_steps)
    def _():
        _alternate(mlp_stage())


def _prompt_layer(x, mod_p, lower_bounds, w_in, w_conv, g_onorm, w_out, w_up, w_down, g_final):
    bsz, seq, d = x.shape
    d_rec = lower_bounds.shape[1]
    dk = d_rec // H_REC
    d_conv = w_conv.shape[1]
    tl = PROMPT_TILE
    n_tiles = seq // tl
    n_steps = bsz * n_tiles
    assert seq % tl == 0 and tl % CHUNK == 0
    mats = (w_in, w_out, w_up, w_down)
    hbm = pl.BlockSpec(memory_space=pl.ANY)

    def mix_tile(s):
        t = jnp.minimum(s, n_steps - 1)
        return lax.div(t, n_tiles), lax.rem(t, n_tiles)

    def mlp_tile(s):
        t = jnp.maximum(s - 1, 0)
        return lax.div(t, n_tiles), lax.rem(t, n_tiles)

    return pl.pallas_call(
        functools.partial(_prompt_kernel, n_tiles=n_tiles, n_steps=n_steps),
        grid=(n_steps + 1,),
        in_specs=[
            pl.BlockSpec((None, tl, d), lambda s: (*mix_tile(s), 0)),
            _const_spec(mod_p.shape),
            _const_spec(lower_bounds.shape),
            hbm,
            _const_spec(w_conv.shape),
            _const_spec(g_onorm.shape),
            hbm,
            hbm,
            hbm,
            _const_spec(g_final.shape),
        ],
        out_specs=[
            pl.BlockSpec((None, tl, d), lambda s: (*mlp_tile(s), 0)),
            pl.BlockSpec((None, H_REC, dk, dk), lambda s: (mix_tile(s)[0], 0, 0, 0)),
            pl.BlockSpec((None, CONV_W - 1, d_conv), lambda s: (mix_tile(s)[0], 0, 0)),
        ] + [hbm] * len(mats),
        out_shape=[
            jax.ShapeDtypeStruct((bsz, seq, d), F32),
            jax.ShapeDtypeStruct((bsz, H_REC, dk, dk), F32),
            jax.ShapeDtypeStruct((bsz, CONV_W - 1, d_conv), F32),
        ] + [jax.ShapeDtypeStruct(w.shape, BF16) for w in mats],
        scratch_shapes=[
            pltpu.VMEM((H_REC, dk, dk), F32),
            pltpu.VMEM((SUBLANES + tl, d_conv), F32),
            pltpu.VMEM((tl, d_rec + d_conv), BF16),
            pltpu.VMEM((tl, d), F32),
        ] + [pltpu.VMEM(w.shape, BF16) for w in mats] + [
            pltpu.VMEM((WEIGHT_STAGE_SLOTS, *WEIGHT_STAGE_SHAPE), F32),
            pltpu.SemaphoreType.DMA((WEIGHT_STAGE_SLOTS,)),
            pltpu.SemaphoreType.DMA((len(mats),)),
        ],
        compiler_params=pltpu.CompilerParams(
            dimension_semantics=("arbitrary",), vmem_limit_bytes=VMEM_LIMIT_BYTES),
        name="prompt_layer",
    )(x, mod_p, lower_bounds, w_in, w_conv, g_onorm, w_out, w_up, w_down, g_final)


def _sample_kernel(x_ref, mod_ref, lb_ref, w_in_ref, w_conv_ref, g_onorm_ref, w_out_ref,
                   w_up_ref, w_down_ref, g_final_ref, rec_in_ref, conv_in_ref,
                   y_ref, rec_ref, conv_ref,
                   modx_ref, cbx_ref, ubuf_ref, mix_ref, *, seq):
    rows, d = x_ref.shape
    tb = rows // seq
    d_rec = lb_ref.shape[1]
    dk = d_rec // H_REC
    d_conv = conv_in_ref.shape[2]

    for b in range(tb):
        r = slice(b * seq, (b + 1) * seq)
        modx_ref[r, :] = jnp.broadcast_to(mod_ref[b:b + 1, :], (seq, mod_ref.shape[1]))
        for j in range(CONV_W - 1):
            cbx_ref[j, r, :] = jnp.broadcast_to(conv_in_ref[b, j:j + 1, :], (seq, d_conv))

    sh1, sc1, g1, sh2, sc2, g2 = (modx_ref[:, i * d:(i + 1) * d] for i in range(N_MOD))

    x = x_ref[...]
    proj = _bdot(_modulate(x, sh1, sc1), w_in_ref[...])
    q, fz, iv, g, gb, gc, hv = _split_proj(proj)

    lb = _layer_lower_bound(lb_ref, 0)
    row = lax.broadcasted_iota(jnp.int32, (rows, rows), 0)
    col = lax.broadcasted_iota(jnp.int32, (rows, rows), 1)
    causal = ((row // seq) == (col // seq)) & (row >= col)
    f, k = _gates(fz, lb)
    prod = _tile_cumprod(f)
    decay = _tile_last(prod)
    q_dec = (q * prod).astype(BF16)
    k_inv = k / prod
    k_dec = k_inv.astype(BF16)
    k_end = k_inv * decay
    vb = iv.astype(BF16)

    dec3 = _decay_terms(decay, _group_pos((rows, d_rec), seq))
    zeros_blk = jnp.zeros((seq, dk), F32)
    rhs_bottom = _decay_selector(seq, dk)

    heads = [slice(hd * dk, (hd + 1) * dk) for hd in range(H_REC)]

    def update_states(elements):
        for b in elements:
            r = slice(b * seq, (b + 1) * seq)
            for hd, sl in enumerate(heads):
                lhs = jnp.concatenate([k_end[r, sl], dec3[r, sl]], axis=0)
                rhs = jnp.concatenate(
                    [jnp.concatenate([iv[r, sl], zeros_blk], axis=1), rhs_bottom], axis=0)
                both = _dot_tn(lhs.astype(BF16), rhs.astype(BF16))
                rec_ref[b, hd] = both[:, dk:] * rec_in_ref[b, hd] + both[:, :dk]

    scores = [jnp.where(causal, _dot_nt(q_dec[:, sl], k_dec[:, sl]), 0.0).astype(BF16)
              for sl in heads]
    o_inter = jnp.concatenate(
        [jnp.concatenate([_bdot(q_dec[b * seq:(b + 1) * seq, sl], rec_in_ref[b, hd].astype(BF16))
                          for hd, sl in enumerate(heads)], axis=-1)
         for b in range(tb)], axis=0)
    o_intra = jnp.concatenate([_bdot(scores[hd], vb[:, sl]) for hd, sl in enumerate(heads)],
                              axis=-1)
    o_rec = _head_out(o_intra + o_inter, g, g_onorm_ref[...])
    mix_ref[:, 0:d_rec] = o_rec.astype(BF16)

    u = gc * hv
    ubuf_ref[SUBLANES:SUBLANES + rows, :] = u
    tok = _group_pos((rows, d_conv), seq)
    u_m1 = jnp.where(tok >= 1, ubuf_ref[SUBLANES - 1:SUBLANES - 1 + rows, :], cbx_ref[1])
    u_m2 = jnp.where(tok >= 2, ubuf_ref[SUBLANES - 2:SUBLANES - 2 + rows, :],
                     jnp.where(tok == 1, cbx_ref[1], cbx_ref[0]))
    w_conv = w_conv_ref[...]
    y_conv = w_conv[0:1] * u_m2 + w_conv[1:2] * u_m1 + w_conv[2:3] * u
    mix_ref[:, d_rec:d_rec + d_conv] = (gb * y_conv).astype(BF16)
    for b in range(tb):
        conv_ref[b] = u[(b + 1) * seq - (CONV_W - 1):(b + 1) * seq]

    x1 = x + g1 * _bdot(mix_ref[...], w_out_ref[...])
    update_states(range(tb // 2))
    h2 = _modulate(x1, sh2, sc2)
    mlp = _mlp_down(_mlp_up(h2, w_up_ref, 0), w_up_ref, w_down_ref, 0)
    for j in range(1, MLP_BLOCKS):
        mlp += _mlp_down(_mlp_up(h2, w_up_ref, j), w_up_ref, w_down_ref, j)
    update_states(range(tb // 2, tb))
    y_ref[...] = _final_norm(x1, g2, mlp, g_final_ref)


def _sample_layer(x2d, seq, mod, lower_bounds, w_in, w_conv, g_onorm, w_out, w_up, w_down,
                  g_final, rec_in, conv_in):
    n_rows, d = x2d.shape
    bsz = n_rows // seq
    d_rec = lower_bounds.shape[1]
    dk = d_rec // H_REC
    d_conv = w_conv.shape[1]
    tb = SAMPLE_TILE_B
    rows = tb * seq
    assert bsz % tb == 0 and seq == SUBLANES and seq >= CONV_W
    return pl.pallas_call(
        functools.partial(_sample_kernel, seq=seq),
        grid=(bsz // tb,),
        in_specs=[
            pl.BlockSpec((rows, d), lambda i: (i, 0)),
            pl.BlockSpec((tb, N_MOD * d), lambda i: (i, 0)),
            _const_spec(lower_bounds.shape),
            _const_spec(w_in.shape),
            _const_spec(w_conv.shape),
            _const_spec(g_onorm.shape),
            _const_spec(w_out.shape),
            _const_spec(w_up.shape),
            _const_spec(w_down.shape),
            _const_spec(g_final.shape),
            pl.BlockSpec((tb, H_REC, dk, dk), lambda i: (i, 0, 0, 0)),
            pl.BlockSpec((tb, CONV_W - 1, d_conv), lambda i: (i, 0, 0)),
        ],
        out_specs=[
            pl.BlockSpec((rows, d), lambda i: (i, 0)),
            pl.BlockSpec((tb, H_REC, dk, dk), lambda i: (i, 0, 0, 0)),
            pl.BlockSpec((tb, CONV_W - 1, d_conv), lambda i: (i, 0, 0)),
        ],
        out_shape=[
            jax.ShapeDtypeStruct((n_rows, d), F32),
            jax.ShapeDtypeStruct((bsz, H_REC, dk, dk), F32),
            jax.ShapeDtypeStruct((bsz, CONV_W - 1, d_conv), F32),
        ],
        scratch_shapes=[
            pltpu.VMEM((rows, N_MOD * d), F32),
            pltpu.VMEM((CONV_W - 1, rows, d_conv), F32),
            pltpu.VMEM((SUBLANES + rows, d_conv), F32),
            pltpu.VMEM((rows, d_rec + d_conv), BF16),
        ],
        compiler_params=pltpu.CompilerParams(
            dimension_semantics=("arbitrary",), vmem_limit_bytes=VMEM_LIMIT_BYTES),
        name="sample_layer",
    )(x2d, mod, lower_bounds, w_in, w_conv, g_onorm, w_out, w_up, w_down, g_final, rec_in, conv_in)


def kernel(x_prompt, x_sample, state_rec, state_conv, c_prompt, c_sample, lower_bounds, w_ada,
           b_ada, w_in, w_conv, g_onorm, w_out, w_up, w_down, g_final):
    depth = w_in.shape[0]
    assert depth == 1, "single-layer trunk"
    bsz_s, seq_s, d = x_sample.shape

    mod_s, mod_p = _modulation(c_sample, c_prompt, w_ada[0], b_ada)

    g_final = g_final.reshape(1, d)
    y_p, rec_p, conv_p, w_in_b, w_out_b, w_up_b, w_down_b = _prompt_layer(
        x_prompt, mod_p, lower_bounds, w_in[0], w_conv[0], g_onorm, w_out[0], w_up[0], w_down[0],
        g_final)
    y_s, rec_s, conv_s = _sample_layer(
        x_sample.reshape(bsz_s * seq_s, d), seq_s, mod_s, lower_bounds, w_in_b, w_conv[0], g_onorm,
        w_out_b, w_up_b, w_down_b, g_final, state_rec[0], state_conv[0])
    return (y_p, y_s.reshape(bsz_s, seq_s, d), rec_p[None], conv_p[None], rec_s[None],
            conv_s[None])
```

```python
import functools

import jax
import jax.numpy as jnp
from jax import lax
from jax.experimental import pallas as pl
from jax.experimental.pallas import tpu as pltpu

F32 = jnp.float32
BF16 = jnp.bfloat16

EPS = 1e-6
H_REC = 4
CONV_W = 3
N_MOD = 6
N_PROJ_GROUPS = 7
CHUNK = 64

SUBLANES = 8
VMEM_LIMIT_BYTES = 56 * 1024 * 1024

PROMPT_TILE = 512
SAMPLE_TILE_B = 16
MOD_TILE_K = 256
MLP_BLOCKS = 4
WEIGHT_STAGE_SHAPE = (512, 512)
WEIGHT_STAGE_SLOTS = 6


def _const_spec(shape):
    zeros = (0,) * len(shape)
    return pl.BlockSpec(shape, lambda *_: zeros, pipeline_mode=pl.Buffered(1))


def _rms(x):
    return x * lax.rsqrt(jnp.mean(x * x, axis=-1, keepdims=True) + EPS)


def _silu(x):
    return x * jax.nn.sigmoid(x)


def _bdot(a, b):
    return jnp.dot(a, b, preferred_element_type=F32)


def _dot_nt(a, b):
    return lax.dot_general(a, b, (((1,), (1,)), ((), ())), preferred_element_type=F32)


def _dot_tn(a, b):
    return lax.dot_general(a, b, (((0,), (0,)), ((), ())), preferred_element_type=F32)


def _group_pos(shape, group):
    return lax.broadcasted_iota(jnp.int32, shape, 0) % group


def _tile_cumprod(x):
    rows, n = x.shape
    x = x.reshape(rows // SUBLANES, SUBLANES, n)
    pos = lax.broadcasted_iota(jnp.int32, x.shape, 1)
    step = 1
    while step < SUBLANES:
        x = x * jnp.where(pos >= step, pltpu.roll(x, step, axis=1), 1.0)
        step *= 2
    return x.reshape(rows, n)


def _tile_last(x):
    rows, n = x.shape
    x = x.reshape(rows // SUBLANES, SUBLANES, n)
    return jnp.broadcast_to(x[:, SUBLANES - 1:, :], x.shape).reshape(rows, n)


def _chain_tiles(y):
    tiles, carry = [], None
    for r in range(0, y.shape[0], SUBLANES):
        t = y[r:r + SUBLANES]
        if carry is not None:
            t = t * carry
        tiles.append(t)
        carry = t[SUBLANES - 1:, :]
    return jnp.concatenate(tiles, axis=0)


def _layer_lower_bound(lb_ref, layer):
    lb = lb_ref[...]
    e = jnp.exp(lb - jnp.max(lb, axis=0, keepdims=True))
    sm = e / jnp.sum(e, axis=0, keepdims=True)
    return jnp.sum(sm[: layer + 1], axis=0, keepdims=True)


def _gates(fz, lb):
    sig = jax.nn.sigmoid(fz)
    f = lb + (1.0 - lb) * sig
    k = (1.0 - lb) * (1.0 - sig)
    return f, k


def _decay_terms(decay, row):
    hi = decay.astype(BF16).astype(F32)
    rest = decay - hi
    mid = rest.astype(BF16).astype(F32)
    lo = rest - mid
    return jnp.where(row == 0, hi, jnp.where(row == 1, mid, jnp.where(row == 2, lo, 0.0)))


def _decay_selector(n_rows, width):
    sub = lax.broadcasted_iota(jnp.int32, (n_rows, width), 0)
    return jnp.concatenate(
        [jnp.zeros((n_rows, width), F32), jnp.where(sub < 3, 1.0, 0.0)], axis=1)


def _head_out(o, g, g_onorm):
    dv = o.shape[-1] // H_REC
    pieces = []
    for h in range(H_REC):
        sl = slice(h * dv, (h + 1) * dv)
        pieces.append(_rms(o[:, sl]))
    return (jnp.concatenate(pieces, axis=-1) * g_onorm) * _silu(g)


def _modulate(x, shift, scale):
    return (_rms(x) * (1.0 + scale) + shift).astype(BF16)


def _mlp_cols(w_up_ref, j):
    blk = w_up_ref.shape[1] // MLP_BLOCKS
    return slice(j * blk, (j + 1) * blk)


def _mlp_up(h2, w_up_ref, j):
    up = jnp.maximum(_bdot(h2, w_up_ref[:, _mlp_cols(w_up_ref, j)]), 0.0)
    return (up * up).astype(BF16)


def _mlp_down(up, w_up_ref, w_down_ref, j):
    return _bdot(up, w_down_ref[_mlp_cols(w_up_ref, j), :])


def _final_norm(x, g2, mlp, g_final_ref):
    return _rms(x + g2 * mlp) * g_final_ref[...]


def _mod_kernel(c_s_ref, c_p_ref, w_ref, b_ref, o_s_ref, o_p_ref):
    k = pl.program_id(0)
    n_s = c_s_ref.shape[0]
    kb = w_ref.shape[0]
    cols = pl.ds(pl.multiple_of(k * kb, kb), kb)
    c = jnp.concatenate([c_s_ref[:, cols], c_p_ref[:, cols]], axis=0)
    part = _bdot(_silu(c).astype(BF16), w_ref[...].astype(BF16))

    @pl.when(k == 0)
    def _():
        o_s_ref[...] = part[:n_s] + b_ref[...]
        o_p_ref[...] = part[n_s:] + b_ref[...]

    @pl.when(k > 0)
    def _():
        o_s_ref[...] += part[:n_s]
        o_p_ref[...] += part[n_s:]


def _modulation(c_s, c_p, w_ada, b_ada):
    (n_s, d), n_p = c_s.shape, c_p.shape[0]
    n = w_ada.shape[1]
    assert n_s % SUBLANES == 0 and n_p % SUBLANES == 0 and d % MOD_TILE_K == 0
    return pl.pallas_call(
        _mod_kernel,
        grid=(d // MOD_TILE_K,),
        in_specs=[
            pl.BlockSpec((n_s, d), lambda k: (0, 0)),
            pl.BlockSpec((n_p, d), lambda k: (0, 0)),
            pl.BlockSpec((MOD_TILE_K, n), lambda k: (k, 0)),
            pl.BlockSpec((1, n), lambda k: (0, 0)),
        ],
        out_specs=[
            pl.BlockSpec((n_s, n), lambda k: (0, 0)),
            pl.BlockSpec((n_p, n), lambda k: (0, 0)),
        ],
        out_shape=[
            jax.ShapeDtypeStruct((n_s, n), F32),
            jax.ShapeDtypeStruct((n_p, n), F32),
        ],
        compiler_params=pltpu.CompilerParams(
            dimension_semantics=("arbitrary",), vmem_limit_bytes=VMEM_LIMIT_BYTES),
        name="adaln_modulation",
    )(c_s, c_p, w_ada, b_ada)


def _split_proj(proj):
    width = proj.shape[1] // N_PROJ_GROUPS
    return tuple(proj[:, p * width:(p + 1) * width] for p in range(N_PROJ_GROUPS))


def _weight_blocks(w_hbm, w_vmem):
    rows, cols = w_hbm.shape
    br, bc = WEIGHT_STAGE_SHAPE
    assert rows % br == 0 and cols % bc == 0
    return [(w_hbm.at[r:r + br, c:c + bc], w_vmem.at[r:r + br, c:c + bc])
            for r in range(0, rows, br) for c in range(0, cols, bc)]


def _stage_copy(src, stage_ref, sem_ref, i):
    slot = i % WEIGHT_STAGE_SLOTS
    return pltpu.make_async_copy(src, stage_ref.at[slot], sem_ref.at[slot])


def _load_weights_as_bf16(w_hbm_refs, w_vmem_refs, stage_ref, sem_ref):
    blocks = [blk for w_hbm, w_vmem in zip(w_hbm_refs, w_vmem_refs)
              for blk in _weight_blocks(w_hbm, w_vmem)]
    ahead = WEIGHT_STAGE_SLOTS - 1
    for i in range(min(ahead, len(blocks))):
        _stage_copy(blocks[i][0], stage_ref, sem_ref, i).start()
    for i, (src, dst) in enumerate(blocks):
        if i + ahead < len(blocks):
            _stage_copy(blocks[i + ahead][0], stage_ref, sem_ref, i + ahead).start()
        _stage_copy(src, stage_ref, sem_ref, i).wait()
        dst[...] = stage_ref[i % WEIGHT_STAGE_SLOTS].astype(BF16)


def _alternate(*stages):
    stages = list(stages)
    while stages:
        for stage in list(stages):
            try:
                next(stage)
            except StopIteration:
                stages.remove(stage)


def _mlp_stage(x1_ref, mod, w_up_ref, w_down_ref, g_final_ref, y_ref):
    d = x1_ref.shape[1]
    sh2, sc2, g2 = (mod[:, i * d:(i + 1) * d] for i in range(3, N_MOD))
    h2 = _modulate(x1_ref[...], sh2, sc2)
    ups = [_mlp_up(h2, w_up_ref, 0)]
    yield
    mlp = None
    for j in range(MLP_BLOCKS):
        if j + 1 < MLP_BLOCKS:
            ups.append(_mlp_up(h2, w_up_ref, j + 1))
        part = _mlp_down(ups[j], w_up_ref, w_down_ref, j)
        mlp = part if mlp is None else mlp + part
        if j + 2 < MLP_BLOCKS:
            yield
    y_ref[...] = _final_norm(x1_ref[...], g2, mlp, g_final_ref)
    yield


def _prompt_mix_stage(x_ref, mod, lb_ref, w_in_ref, w_conv_ref, g_onorm_ref, w_out_ref,
                      st_ref, ubuf_ref, mix_ref, x1_ref, rec_ref, conv_ref, is_last):
    tl, d = x_ref.shape
    d_rec = lb_ref.shape[1]
    dk = d_rec // H_REC
    n_chunks = tl // CHUNK
    heads = [slice(hd * dk, (hd + 1) * dk) for hd in range(H_REC)]
    sh1, sc1, g1 = (mod[:, i * d:(i + 1) * d] for i in range(3))

    x = x_ref[...]
    proj = _bdot(_modulate(x, sh1, sc1), w_in_ref[...])
    q, fz, iv, g, gb, gc, hv = _split_proj(proj)
    d_conv = gb.shape[1]
    yield

    lb = _layer_lower_bound(lb_ref, 0)
    row = lax.broadcasted_iota(jnp.int32, (CHUNK, CHUNK), 0)
    col = lax.broadcasted_iota(jnp.int32, (CHUNK, CHUNK), 1)
    causal = row >= col
    f_all, k_all = _gates(fz, lb)
    prod_tiles = _tile_cumprod(f_all)
    q_dec, decay, scores, upd = [], [], [], []
    for c in range(n_chunks):
        rows = slice(c * CHUNK, (c + 1) * CHUNK)
        prod = _chain_tiles(prod_tiles[rows])
        dec = prod[CHUNK - 1:CHUNK, :]
        q_dec.append((q[rows] * prod).astype(BF16))
        k_inv = k_all[rows] / prod
        k_dec = k_inv.astype(BF16)
        k_end = (k_inv * dec).astype(BF16)
        decay.append([jnp.broadcast_to(dec[:, sl], (dk, dk)).T for sl in heads])
        v = iv[rows].astype(BF16)
        scores.append([jnp.where(causal, _dot_nt(q_dec[c][:, sl], k_dec[:, sl]), 0.0).astype(BF16)
                       for sl in heads])
        upd.append([_dot_tn(k_end[:, sl], v[:, sl]) for sl in heads])
    yield

    st = [st_ref[hd] for hd in range(H_REC)]
    o_chunks = []
    for c in range(n_chunks):
        v = iv[c * CHUNK:(c + 1) * CHUNK].astype(BF16)
        o_heads = []
        for hd, sl in enumerate(heads):
            o_heads.append(_bdot(scores[c][hd], v[:, sl])
                           + _bdot(q_dec[c][:, sl], st[hd].astype(BF16)))
            st[hd] = st[hd] * decay[c][hd] + upd[c][hd]
        o_chunks.append(jnp.concatenate(o_heads, axis=-1))
    for hd in range(H_REC):
        st_ref[hd] = st[hd]
    yield

    o_rec = _head_out(jnp.concatenate(o_chunks, axis=0), g, g_onorm_ref[...])
    mix_ref[:, 0:d_rec] = o_rec.astype(BF16)

    u = gc * hv
    ubuf_ref[SUBLANES:SUBLANES + tl, :] = u
    w_conv = w_conv_ref[...]
    y_conv = (w_conv[0:1] * ubuf_ref[SUBLANES - 2:SUBLANES - 2 + tl, :]
              + w_conv[1:2] * ubuf_ref[SUBLANES - 1:SUBLANES - 1 + tl, :]
              + w_conv[2:3] * u)
    mix_ref[:, d_rec:d_rec + d_conv] = (gb * y_conv).astype(BF16)
    tail = u[tl - (CONV_W - 1):tl]
    ubuf_ref[SUBLANES - (CONV_W - 1):SUBLANES, :] = tail
    yield

    x1 = x + g1 * _bdot(mix_ref[...], w_out_ref[...])
    x1_ref[...] = x1

    @pl.when(is_last)
    def _():
        for hd in range(H_REC):
            rec_ref[hd] = st_ref[hd]
        conv_ref[...] = tail


def _prompt_kernel(x_ref, mod_ref, lb_ref, w_in_hbm, w_conv_ref, g_onorm_ref, w_out_hbm,
                   w_up_hbm, w_down_hbm, g_final_ref,
                   y_ref, rec_ref, conv_ref, w_in_out, w_out_out, w_up_out, w_down_out,
                   st_ref, ubuf_ref, mix_ref, x1_ref,
                   w_in_ref, w_out_ref, w_up_ref, w_down_ref, stage_ref, stage_sem, out_sem,
                   *, n_tiles, n_steps):
    s = pl.program_id(0)
    tile = jnp.minimum(s, n_steps - 1)
    l = lax.rem(tile, n_tiles)
    d = x_ref.shape[1]

    w_vmem = (w_in_ref, w_out_ref, w_up_ref, w_down_ref)
    w_outs = (w_in_out, w_out_out, w_up_out, w_down_out)

    def bf16_export(i):
        return pltpu.make_async_copy(w_vmem[i], w_outs[i], out_sem.at[i])

    @pl.when(s == 0)
    def _():
        _load_weights_as_bf16((w_in_hbm, w_out_hbm, w_up_hbm, w_down_hbm), w_vmem,
                              stage_ref, stage_sem)
        for i in range(len(w_vmem)):
            bf16_export(i).start()

    @pl.when(s == n_steps)
    def _():
        for i in range(len(w_vmem)):
            bf16_export(i).wait()

    @pl.when(l == 0)
    def _():
        st_ref[...] = jnp.zeros_like(st_ref)
        ubuf_ref[0:SUBLANES, :] = jnp.zeros((SUBLANES, ubuf_ref.shape[1]), F32)

    def mix_stage():
        mod = mod_ref[pl.ds(lax.div(tile, n_tiles), 1), :]
        return _prompt_mix_stage(x_ref, mod, lb_ref, w_in_ref, w_conv_ref, g_onorm_ref,
                                 w_out_ref, st_ref, ubuf_ref, mix_ref, x1_ref,
                                 rec_ref, conv_ref, l == n_tiles - 1)

    def mlp_stage():
        mod_prev = mod_ref[pl.ds(lax.div(s - 1, n_tiles), 1), :]
        return _mlp_stage(x1_ref, mod_prev, w_up_ref, w_down_ref, g_final_ref, y_ref)

    @pl.when(s == 0)
    def _():
        _alternate(mix_stage())

    @pl.when((s > 0) & (s < n_steps))
    def _():
        _alternate(mix_stage(), mlp_stage())

    @pl.when(s == n_steps)
    def _():
        _alternate(mlp_stage())


def _prompt_layer(x, mod_p, lower_bounds, w_in, w_conv, g_onorm, w_out, w_up, w_down, g_final):
    bsz, seq, d = x.shape
    d_rec = lower_bounds.shape[1]
    dk = d_rec // H_REC
    d_conv = w_conv.shape[1]
    tl = PROMPT_TILE
    n_tiles = seq // tl
    n_steps = bsz * n_tiles
    assert seq % tl == 0 and tl % CHUNK == 0
    mats = (w_in, w_out, w_up, w_down)
    hbm = pl.BlockSpec(memory_space=pl.ANY)

    def mix_tile(s):
        t = jnp.minimum(s, n_steps - 1)
        return lax.div(t, n_tiles), lax.rem(t, n_tiles)

    def mlp_tile(s):
        t = jnp.maximum(s - 1, 0)
        return lax.div(t, n_tiles), lax.rem(t, n_tiles)

    return pl.pallas_call(
        functools.partial(_prompt_kernel, n_tiles=n_tiles, n_steps=n_steps),
        grid=(n_steps + 1,),
        in_specs=[
            pl.BlockSpec((None, tl, d), lambda s: (*mix_tile(s), 0)),
            _const_spec(mod_p.shape),
            _const_spec(lower_bounds.shape),
            hbm,
            _const_spec(w_conv.shape),
            _const_spec(g_onorm.shape),
            hbm,
            hbm,
            hbm,
            _const_spec(g_final.shape),
        ],
        out_specs=[
            pl.BlockSpec((None, tl, d), lambda s: (*mlp_tile(s), 0)),
            pl.BlockSpec((None, H_REC, dk, dk), lambda s: (mix_tile(s)[0], 0, 0, 0)),
            pl.BlockSpec((None, CONV_W - 1, d_conv), lambda s: (mix_tile(s)[0], 0, 0)),
        ] + [hbm] * len(mats),
        out_shape=[
            jax.ShapeDtypeStruct((bsz, seq, d), F32),
            jax.ShapeDtypeStruct((bsz, H_REC, dk, dk), F32),
            jax.ShapeDtypeStruct((bsz, CONV_W - 1, d_conv), F32),
        ] + [jax.ShapeDtypeStruct(w.shape, BF16) for w in mats],
        scratch_shapes=[
            pltpu.VMEM((H_REC, dk, dk), F32),
            pltpu.VMEM((SUBLANES + tl, d_conv), F32),
            pltpu.VMEM((tl, d_rec + d_conv), BF16),
            pltpu.VMEM((tl, d), F32),
        ] + [pltpu.VMEM(w.shape, BF16) for w in mats] + [
            pltpu.VMEM((WEIGHT_STAGE_SLOTS, *WEIGHT_STAGE_SHAPE), F32),
            pltpu.SemaphoreType.DMA((WEIGHT_STAGE_SLOTS,)),
            pltpu.SemaphoreType.DMA((len(mats),)),
        ],
        compiler_params=pltpu.CompilerParams(
            dimension_semantics=("arbitrary",), vmem_limit_bytes=VMEM_LIMIT_BYTES),
        name="prompt_layer",
    )(x, mod_p, lower_bounds, w_in, w_conv, g_onorm, w_out, w_up, w_down, g_final)


def _sample_kernel(x_ref, mod_ref, lb_ref, w_in_ref, w_conv_ref, g_onorm_ref, w_out_ref,
                   w_up_ref, w_down_ref, g_final_ref, rec_in_ref, conv_in_ref,
                   y_ref, rec_ref, conv_ref,
                   modx_ref, cbx_ref, ubuf_ref, mix_ref, *, seq):
    rows, d = x_ref.shape
    tb = rows // seq
    d_rec = lb_ref.shape[1]
    dk = d_rec // H_REC
    d_conv = conv_in_ref.shape[2]

    for b in range(tb):
        r = slice(b * seq, (b + 1) * seq)
        modx_ref[r, :] = jnp.broadcast_to(mod_ref[b:b + 1, :], (seq, mod_ref.shape[1]))
        for j in range(CONV_W - 1):
            cbx_ref[j, r, :] = jnp.broadcast_to(conv_in_ref[b, j:j + 1, :], (seq, d_conv))

    sh1, sc1, g1, sh2, sc2, g2 = (modx_ref[:, i * d:(i + 1) * d] for i in range(N_MOD))

    x = x_ref[...]
    proj = _bdot(_modulate(x, sh1, sc1), w_in_ref[...])
    q, fz, iv, g, gb, gc, hv = _split_proj(proj)

    lb = _layer_lower_bound(lb_ref, 0)
    row = lax.broadcasted_iota(jnp.int32, (rows, rows), 0)
    col = lax.broadcasted_iota(jnp.int32, (rows, rows), 1)
    causal = ((row // seq) == (col // seq)) & (row >= col)
    f, k = _gates(fz, lb)
    prod = _tile_cumprod(f)
    decay = _tile_last(prod)
    q_dec = (q * prod).astype(BF16)
    k_inv = k / prod
    k_dec = k_inv.astype(BF16)
    k_end = k_inv * decay
    vb = iv.astype(BF16)

    dec3 = _decay_terms(decay, _group_pos((rows, d_rec), seq))
    zeros_blk = jnp.zeros((seq, dk), F32)
    rhs_bottom = _decay_selector(seq, dk)

    heads = [slice(hd * dk, (hd + 1) * dk) for hd in range(H_REC)]

    def update_states(elements):
        for b in elements:
            r = slice(b * seq, (b + 1) * seq)
            for hd, sl in enumerate(heads):
                lhs = jnp.concatenate([k_end[r, sl], dec3[r, sl]], axis=0)
                rhs = jnp.concatenate(
                    [jnp.concatenate([iv[r, sl], zeros_blk], axis=1), rhs_bottom], axis=0)
                both = _dot_tn(lhs.astype(BF16), rhs.astype(BF16))
                rec_ref[b, hd] = both[:, dk:] * rec_in_ref[b, hd] + both[:, :dk]

    scores = [jnp.where(causal, _dot_nt(q_dec[:, sl], k_dec[:, sl]), 0.0).astype(BF16)
              for sl in heads]
    o_inter = jnp.concatenate(
        [jnp.concatenate([_bdot(q_dec[b * seq:(b + 1) * seq, sl], rec_in_ref[b, hd].astype(BF16))
                          for hd, sl in enumerate(heads)], axis=-1)
         for b in range(tb)], axis=0)
    o_intra = jnp.concatenate([_bdot(scores[hd], vb[:, sl]) for hd, sl in enumerate(heads)],
                              axis=-1)
    o_rec = _head_out(o_intra + o_inter, g, g_onorm_ref[...])
    mix_ref[:, 0:d_rec] = o_rec.astype(BF16)

    u = gc * hv
    ubuf_ref[SUBLANES:SUBLANES + rows, :] = u
    tok = _group_pos((rows, d_conv), seq)
    u_m1 = jnp.where(tok >= 1, ubuf_ref[SUBLANES - 1:SUBLANES - 1 + rows, :], cbx_ref[1])
    u_m2 = jnp.where(tok >= 2, ubuf_ref[SUBLANES - 2:SUBLANES - 2 + rows, :],
                     jnp.where(tok == 1, cbx_ref[1], cbx_ref[0]))
    w_conv = w_conv_ref[...]
    y_conv = w_conv[0:1] * u_m2 + w_conv[1:2] * u_m1 + w_conv[2:3] * u
    mix_ref[:, d_rec:d_rec + d_conv] = (gb * y_conv).astype(BF16)
    for b in range(tb):
        conv_ref[b] = u[(b + 1) * seq - (CONV_W - 1):(b + 1) * seq]

    x1 = x + g1 * _bdot(mix_ref[...], w_out_ref[...])
    update_states(range(tb // 2))
    h2 = _modulate(x1, sh2, sc2)
    mlp = _mlp_down(_mlp_up(h2, w_up_ref, 0), w_up_ref, w_down_ref, 0)
    for j in range(1, MLP_BLOCKS):
        mlp += _mlp_down(_mlp_up(h2, w_up_ref, j), w_up_ref, w_down_ref, j)
    update_states(range(tb // 2, tb))
    y_ref[...] = _final_norm(x1, g2, mlp, g_final_ref)


def _sample_layer(x2d, seq, mod, lower_bounds, w_in, w_conv, g_onorm, w_out, w_up, w_down,
                  g_final, rec_in, conv_in):
    n_rows, d = x2d.shape
    bsz = n_rows // seq
    d_rec = lower_bounds.shape[1]
    dk = d_rec // H_REC
    d_conv = w_conv.shape[1]
    tb = SAMPLE_TILE_B
    rows = tb * seq
    assert bsz % tb == 0 and seq == SUBLANES and seq >= CONV_W
    return pl.pallas_call(
        functools.partial(_sample_kernel, seq=seq),
        grid=(bsz // tb,),
        in_specs=[
            pl.BlockSpec((rows, d), lambda i: (i, 0)),
            pl.BlockSpec((tb, N_MOD * d), lambda i: (i, 0)),
            _const_spec(lower_bounds.shape),
            _const_spec(w_in.shape),
            _const_spec(w_conv.shape),
            _const_spec(g_onorm.shape),
            _const_spec(w_out.shape),
            _const_spec(w_up.shape),
            _const_spec(w_down.shape),
            _const_spec(g_final.shape),
            pl.BlockSpec((tb, H_REC, dk, dk), lambda i: (i, 0, 0, 0)),
            pl.BlockSpec((tb, CONV_W - 1, d_conv), lambda i: (i, 0, 0)),
        ],
        out_specs=[
            pl.BlockSpec((rows, d), lambda i: (i, 0)),
            pl.BlockSpec((tb, H_REC, dk, dk), lambda i: (i, 0, 0, 0)),
            pl.BlockSpec((tb, CONV_W - 1, d_conv), lambda i: (i, 0, 0)),
        ],
        out_shape=[
            jax.ShapeDtypeStruct((n_rows, d), F32),
            jax.ShapeDtypeStruct((bsz, H_REC, dk, dk), F32),
            jax.ShapeDtypeStruct((bsz, CONV_W - 1, d_conv), F32),
        ],
        scratch_shapes=[
            pltpu.VMEM((rows, N_MOD * d), F32),
            pltpu.VMEM((CONV_W - 1, rows, d_conv), F32),
            pltpu.VMEM((SUBLANES + rows, d_conv), F32),
            pltpu.VMEM((rows, d_rec + d_conv), BF16),
        ],
        compiler_params=pltpu.CompilerParams(
            dimension_semantics=("arbitrary",), vmem_limit_bytes=VMEM_LIMIT_BYTES),
        name="sample_layer",
    )(x2d, mod, lower_bounds, w_in, w_conv, g_onorm, w_out, w_up, w_down, g_final, rec_in, conv_in)


def kernel(x_prompt, x_sample, state_rec, state_conv, c_prompt, c_sample, lower_bounds, w_ada,
           b_ada, w_in, w_conv, g_onorm, w_out, w_up, w_down, g_final):
    depth = w_in.shape[0]
    assert depth == 1, "single-layer trunk"
    bsz_s, seq_s, d = x_sample.shape

    mod_s, mod_p = _modulation(c_sample, c_prompt, w_ada[0], b_ada)

    y_p, rec_p, conv_p, w_in_b, w_out_b, w_up_b, w_down_b = _prompt_layer(
        x_prompt, mod_p, lower_bounds, w_in[0], w_conv[0], g_onorm, w_out[0], w_up[0], w_down[0],
        g_final)
    y_s, rec_s, conv_s = _sample_layer(
        x_sample.reshape(bsz_s * seq_s, d), seq_s, mod_s, lower_bounds, w_in_b, w_conv[0], g_onorm,
        w_out_b, w_up_b, w_down_b, g_final, state_rec[0], state_conv[0])
    return (y_p, y_s.reshape(bsz_s, seq_s, d), rec_p[None], conv_p[None], rec_s[None],
            conv_s[None])
```

```python
import functools

import jax
import jax.numpy as jnp
from jax import lax
from jax.experimental import pallas as pl
from jax.experimental.pallas import tpu as pltpu

F32 = jnp.float32
BF16 = jnp.bfloat16

EPS = 1e-6
H_REC = 4
CONV_W = 3
N_MOD = 6
N_PROJ_GROUPS = 7
CHUNK = 64

SUBLANES = 8
VMEM_LIMIT_BYTES = 56 * 1024 * 1024

PROMPT_TILE = 512
SAMPLE_TILE_B = 16
MOD_TILE_K = 256
MLP_BLOCKS = 4
WEIGHT_STAGE_SHAPE = (512, 512)
WEIGHT_STAGE_SLOTS = 6


def _const_spec(shape):
    zeros = (0,) * len(shape)
    return pl.BlockSpec(shape, lambda *_: zeros, pipeline_mode=pl.Buffered(1))


def _rms(x):
    return x * lax.rsqrt(jnp.mean(x * x, axis=-1, keepdims=True) + EPS)


def _silu(x):
    return x * jax.nn.sigmoid(x)


def _bdot(a, b):
    return jnp.dot(a, b, preferred_element_type=F32)


def _dot_nt(a, b):
    return lax.dot_general(a, b, (((1,), (1,)), ((), ())), preferred_element_type=F32)


def _dot_tn(a, b):
    return lax.dot_general(a, b, (((0,), (0,)), ((), ())), preferred_element_type=F32)


def _group_pos(shape, group):
    return lax.broadcasted_iota(jnp.int32, shape, 0) % group


def _tile_cumprod(x):
    rows, n = x.shape
    x = x.reshape(rows // SUBLANES, SUBLANES, n)
    pos = lax.broadcasted_iota(jnp.int32, x.shape, 1)
    step = 1
    while step < SUBLANES:
        x = x * jnp.where(pos >= step, pltpu.roll(x, step, axis=1), 1.0)
        step *= 2
    return x.reshape(rows, n)


def _tile_last(x):
    rows, n = x.shape
    x = x.reshape(rows // SUBLANES, SUBLANES, n)
    return jnp.broadcast_to(x[:, SUBLANES - 1:, :], x.shape).reshape(rows, n)


def _chain_tiles(y):
    tiles, carry = [], None
    for r in range(0, y.shape[0], SUBLANES):
        t = y[r:r + SUBLANES]
        if carry is not None:
            t = t * carry
        tiles.append(t)
        carry = t[SUBLANES - 1:, :]
    return jnp.concatenate(tiles, axis=0)


def _layer_lower_bound(lb_ref, layer):
    lb = lb_ref[...]
    e = jnp.exp(lb - jnp.max(lb, axis=0, keepdims=True))
    sm = e / jnp.sum(e, axis=0, keepdims=True)
    return jnp.sum(sm[: layer + 1], axis=0, keepdims=True)


def _gates(fz, lb):
    sig = jax.nn.sigmoid(fz)
    f = lb + (1.0 - lb) * sig
    k = (1.0 - lb) * (1.0 - sig)
    return f, k


def _decay_terms(decay, row):
    hi = decay.astype(BF16).astype(F32)
    rest = decay - hi
    mid = rest.astype(BF16).astype(F32)
    lo = rest - mid
    return jnp.where(row == 0, hi, jnp.where(row == 1, mid, jnp.where(row == 2, lo, 0.0)))


def _decay_selector(n_rows, width):
    sub = lax.broadcasted_iota(jnp.int32, (n_rows, width), 0)
    return jnp.concatenate(
        [jnp.zeros((n_rows, width), F32), jnp.where(sub < 3, 1.0, 0.0)], axis=1)


def _head_out(o, g, g_onorm):
    dv = o.shape[-1] // H_REC
    pieces = []
    for h in range(H_REC):
        sl = slice(h * dv, (h + 1) * dv)
        pieces.append(_rms(o[:, sl]))
    return (jnp.concatenate(pieces, axis=-1) * g_onorm) * _silu(g)


def _modulate(x, shift, scale):
    return (_rms(x) * (1.0 + scale) + shift).astype(BF16)


def _mlp_cols(w_up_ref, j):
    blk = w_up_ref.shape[1] // MLP_BLOCKS
    return slice(j * blk, (j + 1) * blk)


def _mlp_up(h2, w_up_ref, j):
    up = jnp.maximum(_bdot(h2, w_up_ref[:, _mlp_cols(w_up_ref, j)]), 0.0)
    return (up * up).astype(BF16)


def _mlp_down(up, w_up_ref, w_down_ref, j):
    return _bdot(up, w_down_ref[_mlp_cols(w_up_ref, j), :])


def _final_norm(x, g2, mlp, g_final_ref):
    return _rms(x + g2 * mlp) * g_final_ref[...]


def _mod_kernel(c_s_ref, c_p_ref, w_ref, b_ref, o_s_ref, o_p_ref):
    k = pl.program_id(0)
    n_s = c_s_ref.shape[0]
    kb = w_ref.shape[0]
    cols = pl.ds(pl.multiple_of(k * kb, kb), kb)
    c = jnp.concatenate([c_s_ref[:, cols], c_p_ref[:, cols]], axis=0)
    part = _bdot(_silu(c).astype(BF16), w_ref[...].astype(BF16))

    @pl.when(k == 0)
    def _():
        o_s_ref[...] = part[:n_s] + b_ref[...]
        o_p_ref[...] = part[n_s:] + b_ref[...]

    @pl.when(k > 0)
    def _():
        o_s_ref[...] += part[:n_s]
        o_p_ref[...] += part[n_s:]


def _modulation(c_s, c_p, w_ada, b_ada):
    (n_s, d), n_p = c_s.shape, c_p.shape[0]
    n = w_ada.shape[1]
    assert n_s % SUBLANES == 0 and n_p % SUBLANES == 0 and d % MOD_TILE_K == 0
    return pl.pallas_call(
        _mod_kernel,
        grid=(d // MOD_TILE_K,),
        in_specs=[
            pl.BlockSpec((n_s, d), lambda k: (0, 0)),
            pl.BlockSpec((n_p, d), lambda k: (0, 0)),
            pl.BlockSpec((MOD_TILE_K, n), lambda k: (k, 0)),
            pl.BlockSpec((1, n), lambda k: (0, 0)),
        ],
        out_specs=[
            pl.BlockSpec((n_s, n), lambda k: (0, 0)),
            pl.BlockSpec((n_p, n), lambda k: (0, 0)),
        ],
        out_shape=[
            jax.ShapeDtypeStruct((n_s, n), F32),
            jax.ShapeDtypeStruct((n_p, n), F32),
        ],
        compiler_params=pltpu.CompilerParams(
            dimension_semantics=("arbitrary",), vmem_limit_bytes=VMEM_LIMIT_BYTES),
        name="adaln_modulation",
    )(c_s, c_p, w_ada, b_ada)


def _split_proj(proj):
    width = proj.shape[1] // N_PROJ_GROUPS
    return tuple(proj[:, p * width:(p + 1) * width] for p in range(N_PROJ_GROUPS))


def _weight_blocks(w_hbm, w_vmem):
    rows, cols = w_hbm.shape
    br, bc = WEIGHT_STAGE_SHAPE
    assert rows % br == 0 and cols % bc == 0
    return [(w_hbm.at[r:r + br, c:c + bc], w_vmem.at[r:r + br, c:c + bc])
            for r in range(0, rows, br) for c in range(0, cols, bc)]


def _stage_copy(src, stage_ref, sem_ref, i):
    slot = i % WEIGHT_STAGE_SLOTS
    return pltpu.make_async_copy(src, stage_ref.at[slot], sem_ref.at[slot])


def _load_weights_as_bf16(w_hbm_refs, w_vmem_refs, stage_ref, sem_ref):
    blocks = [blk for w_hbm, w_vmem in zip(w_hbm_refs, w_vmem_refs)
              for blk in _weight_blocks(w_hbm, w_vmem)]
    ahead = WEIGHT_STAGE_SLOTS - 1
    for i in range(min(ahead, len(blocks))):
        _stage_copy(blocks[i][0], stage_ref, sem_ref, i).start()
    for i, (src, dst) in enumerate(blocks):
        if i + ahead < len(blocks):
            _stage_copy(blocks[i + ahead][0], stage_ref, sem_ref, i + ahead).start()
        _stage_copy(src, stage_ref, sem_ref, i).wait()
        dst[...] = stage_ref[i % WEIGHT_STAGE_SLOTS].astype(BF16)


def _alternate(*stages):
    stages = list(stages)
    while stages:
        for stage in list(stages):
            try:
                next(stage)
            except StopIteration:
                stages.remove(stage)


def _mlp_stage(x1_ref, mod, w_up_ref, w_down_ref, g_final_ref, y_ref):
    d = x1_ref.shape[1]
    sh2, sc2, g2 = (mod[:, i * d:(i + 1) * d] for i in range(3, N_MOD))
    h2 = _modulate(x1_ref[...], sh2, sc2)
    ups = [_mlp_up(h2, w_up_ref, 0)]
    yield
    mlp = None
    for j in range(MLP_BLOCKS):
        if j + 1 < MLP_BLOCKS:
            ups.append(_mlp_up(h2, w_up_ref, j + 1))
        part = _mlp_down(ups[j], w_up_ref, w_down_ref, j)
        mlp = part if mlp is None else mlp + part
        if j + 2 < MLP_BLOCKS:
            yield
    y_ref[...] = _final_norm(x1_ref[...], g2, mlp, g_final_ref)
    yield


def _prompt_mix_stage(x_ref, mod, lb_ref, w_in_ref, w_conv_ref, g_onorm_ref, w_out_ref,
                      st_ref, ubuf_ref, mix_ref, x1_ref, rec_ref, conv_ref, is_last):
    tl, d = x_ref.shape
    d_rec = lb_ref.shape[1]
    dk = d_rec // H_REC
    n_chunks = tl // CHUNK
    heads = [slice(hd * dk, (hd + 1) * dk) for hd in range(H_REC)]
    sh1, sc1, g1 = (mod[:, i * d:(i + 1) * d] for i in range(3))

    x = x_ref[...]
    proj = _bdot(_modulate(x, sh1, sc1), w_in_ref[...])
    q, fz, iv, g, gb, gc, hv = _split_proj(proj)
    d_conv = gb.shape[1]
    yield

    lb = _layer_lower_bound(lb_ref, 0)
    row = lax.broadcasted_iota(jnp.int32, (CHUNK, CHUNK), 0)
    col = lax.broadcasted_iota(jnp.int32, (CHUNK, CHUNK), 1)
    causal = row >= col
    f_all, k_all = _gates(fz, lb)
    prod_tiles = _tile_cumprod(f_all)
    q_dec, decay, scores, upd = [], [], [], []
    for c in range(n_chunks):
        rows = slice(c * CHUNK, (c + 1) * CHUNK)
        prod = _chain_tiles(prod_tiles[rows])
        dec = prod[CHUNK - 1:CHUNK, :]
        q_dec.append((q[rows] * prod).astype(BF16))
        k_inv = k_all[rows] / prod
        k_dec = k_inv.astype(BF16)
        k_end = (k_inv * dec).astype(BF16)
        decay.append([jnp.broadcast_to(dec[:, sl], (dk, dk)).T for sl in heads])
        v = iv[rows].astype(BF16)
        scores.append([jnp.where(causal, _dot_nt(q_dec[c][:, sl], k_dec[:, sl]), 0.0).astype(BF16)
                       for sl in heads])
        upd.append([_dot_tn(k_end[:, sl], v[:, sl]) for sl in heads])
    yield

    st = [st_ref[hd] for hd in range(H_REC)]
    o_chunks = []
    for c in range(n_chunks):
        v = iv[c * CHUNK:(c + 1) * CHUNK].astype(BF16)
        o_heads = []
        for hd, sl in enumerate(heads):
            o_heads.append(_bdot(scores[c][hd], v[:, sl])
                           + _bdot(q_dec[c][:, sl], st[hd].astype(BF16)))
            st[hd] = st[hd] * decay[c][hd] + upd[c][hd]
        o_chunks.append(jnp.concatenate(o_heads, axis=-1))
    for hd in range(H_REC):
        st_ref[hd] = st[hd]
    yield

    o_rec = _head_out(jnp.concatenate(o_chunks, axis=0), g, g_onorm_ref[...])
    mix_ref[:, 0:d_rec] = o_rec.astype(BF16)

    u = gc * hv
    ubuf_ref[SUBLANES:SUBLANES + tl, :] = u
    w_conv = w_conv_ref[...]
    y_conv = (w_conv[0:1] * ubuf_ref[SUBLANES - 2:SUBLANES - 2 + tl, :]
              + w_conv[1:2] * ubuf_ref[SUBLANES - 1:SUBLANES - 1 + tl, :]
              + w_conv[2:3] * u)
    mix_ref[:, d_rec:d_rec + d_conv] = (gb * y_conv).astype(BF16)
    tail = u[tl - (CONV_W - 1):tl]
    ubuf_ref[SUBLANES - (CONV_W - 1):SUBLANES, :] = tail
    yield

    x1 = x + g1 * _bdot(mix_ref[...], w_out_ref[...])
    x1_ref[...] = x1

    @pl.when(is_last)
    def _():
        for hd in range(H_REC):
            rec_ref[hd] = st_ref[hd]
        conv_ref[...] = tail


def _prompt_kernel(x_ref, mod_ref, lb_ref, w_in_hbm, w_conv_ref, g_onorm_ref, w_out_hbm,
                   w_up_hbm, w_down_hbm, g_final_ref,
                   y_ref, rec_ref, conv_ref, w_in_out, w_out_out, w_up_out, w_down_out,
                   st_ref, ubuf_ref, mix_ref, x1_ref,
                   w_in_ref, w_out_ref, w_up_ref, w_down_ref, stage_ref, stage_sem, out_sem,
                   *, n_tiles, n_steps):
    s = pl.program_id(0)
    tile = jnp.minimum(s, n_steps - 1)
    l = lax.rem(tile, n_tiles)
    d = x_ref.shape[1]

    w_vmem = (w_in_ref, w_out_ref, w_up_ref, w_down_ref)
    w_outs = (w_in_out, w_out_out, w_up_out, w_down_out)

    def bf16_export(i):
        return pltpu.make_async_copy(w_vmem[i], w_outs[i], out_sem.at[i])

    @pl.when(s == 0)
    def _():
        _load_weights_as_bf16((w_in_hbm, w_out_hbm, w_up_hbm, w_down_hbm), w_vmem,
                              stage_ref, stage_sem)
        for i in range(len(w_vmem)):
            bf16_export(i).start()

    @pl.when(s == n_steps)
    def _():
        for i in range(len(w_vmem)):
            bf16_export(i).wait()

    @pl.when(l == 0)
    def _():
        st_ref[...] = jnp.zeros_like(st_ref)
        ubuf_ref[0:SUBLANES, :] = jnp.zeros((SUBLANES, ubuf_ref.shape[1]), F32)

    def mix_stage():
        mod = mod_ref[pl.ds(lax.div(tile, n_tiles), 1), :]
        return _prompt_mix_stage(x_ref, mod, lb_ref, w_in_ref, w_conv_ref, g_onorm_ref,
                                 w_out_ref, st_ref, ubuf_ref, mix_ref, x1_ref,
                                 rec_ref, conv_ref, l == n_tiles - 1)

    def mlp_stage():
        mod_prev = mod_ref[pl.ds(lax.div(s - 1, n_tiles), 1), :]
        return _mlp_stage(x1_ref, mod_prev, w_up_ref, w_down_ref, g_final_ref, y_ref)

    @pl.when(s == 0)
    def _():
        _alternate(mix_stage())

    @pl.when((s > 0) & (s < n_steps))
    def _():
        _alternate(mix_stage(), mlp_stage())

    @pl.when(s == n_steps)
    def _():
        _alternate(mlp_stage())


def _prompt_layer(x, mod_p, lower_bounds, w_in, w_conv, g_onorm, w_out, w_up, w_down, g_final):
    bsz, seq, d = x.shape
    d_rec = lower_bounds.shape[1]
    dk = d_rec // H_REC
    d_conv = w_conv.shape[1]
    tl = PROMPT_TILE
    n_tiles = seq // tl
    n_steps = bsz * n_tiles
    assert seq % tl == 0 and tl % CHUNK == 0
    mats = (w_in, w_out, w_up, w_down)
    hbm = pl.BlockSpec(memory_space=pl.ANY)

    def mix_tile(s):
        t = jnp.minimum(s, n_steps - 1)
        return lax.div(t, n_tiles), lax.rem(t, n_tiles)

    def mlp_tile(s):
        t = jnp.maximum(s - 1, 0)
        return lax.div(t, n_tiles), lax.rem(t, n_tiles)

    return pl.pallas_call(
        functools.partial(_prompt_kernel, n_tiles=n_tiles, n_steps=n_steps),
        grid=(n_steps + 1,),
        in_specs=[
            pl.BlockSpec((None, tl, d), lambda s: (*mix_tile(s), 0)),
            _const_spec(mod_p.shape),
            _const_spec(lower_bounds.shape),
            hbm,
            _const_spec(w_conv.shape),
            _const_spec(g_onorm.shape),
            hbm,
            hbm,
            hbm,
            _const_spec(g_final.shape),
        ],
        out_specs=[
            pl.BlockSpec((None, tl, d), lambda s: (*mlp_tile(s), 0)),
            pl.BlockSpec((None, H_REC, dk, dk), lambda s: (mix_tile(s)[0], 0, 0, 0)),
            pl.BlockSpec((None, CONV_W - 1, d_conv), lambda s: (mix_tile(s)[0], 0, 0)),
        ] + [hbm] * len(mats),
        out_shape=[
            jax.ShapeDtypeStruct((bsz, seq, d), F32),
            jax.ShapeDtypeStruct((bsz, H_REC, dk, dk), F32),
            jax.ShapeDtypeStruct((bsz, CONV_W - 1, d_conv), F32),
        ] + [jax.ShapeDtypeStruct(w.shape, BF16) for w in mats],
        scratch_shapes=[
            pltpu.VMEM((H_REC, dk, dk), F32),
            pltpu.VMEM((SUBLANES + tl, d_conv), F32),
            pltpu.VMEM((tl, d_rec + d_conv), BF16),
            pltpu.VMEM((tl, d), F32),
        ] + [pltpu.VMEM(w.shape, BF16) for w in mats] + [
            pltpu.VMEM((WEIGHT_STAGE_SLOTS, *WEIGHT_STAGE_SHAPE), F32),
            pltpu.SemaphoreType.DMA((WEIGHT_STAGE_SLOTS,)),
            pltpu.SemaphoreType.DMA((len(mats),)),
        ],
        compiler_params=pltpu.CompilerParams(
            dimension_semantics=("arbitrary",), vmem_limit_bytes=VMEM_LIMIT_BYTES),
        name="prompt_layer",
    )(x, mod_p, lower_bounds, w_in, w_conv, g_onorm, w_out, w_up, w_down, g_final)


def _sample_kernel(x_ref, mod_ref, lb_ref, w_in_ref, w_conv_ref, g_onorm_ref, w_out_ref,
                   w_up_ref, w_down_ref, g_final_ref, rec_in_ref, conv_in_ref,
                   y_ref, rec_ref, conv_ref,
                   modx_ref, cbx_ref, ubuf_ref, mix_ref):
    tb, seq, d = x_ref.shape
    rows = tb * seq
    d_rec = lb_ref.shape[1]
    dk = d_rec // H_REC
    d_conv = conv_in_ref.shape[2]

    for b in range(tb):
        r = slice(b * seq, (b + 1) * seq)
        modx_ref[r, :] = jnp.broadcast_to(mod_ref[b:b + 1, :], (seq, mod_ref.shape[1]))
        for j in range(CONV_W - 1):
            cbx_ref[j, r, :] = jnp.broadcast_to(conv_in_ref[b, j:j + 1, :], (seq, d_conv))

    sh1, sc1, g1, sh2, sc2, g2 = (modx_ref[:, i * d:(i + 1) * d] for i in range(N_MOD))

    x = x_ref[...].reshape(rows, d)
    proj = _bdot(_modulate(x, sh1, sc1), w_in_ref[...])
    q, fz, iv, g, gb, gc, hv = _split_proj(proj)

    lb = _layer_lower_bound(lb_ref, 0)
    row = lax.broadcasted_iota(jnp.int32, (rows, rows), 0)
    col = lax.broadcasted_iota(jnp.int32, (rows, rows), 1)
    causal = ((row // seq) == (col // seq)) & (row >= col)
    f, k = _gates(fz, lb)
    prod = _tile_cumprod(f)
    decay = _tile_last(prod)
    q_dec = (q * prod).astype(BF16)
    k_inv = k / prod
    k_dec = k_inv.astype(BF16)
    k_end = k_inv * decay
    vb = iv.astype(BF16)

    dec3 = _decay_terms(decay, _group_pos((rows, d_rec), seq))
    zeros_blk = jnp.zeros((seq, dk), F32)
    rhs_bottom = _decay_selector(seq, dk)

    heads = [slice(hd * dk, (hd + 1) * dk) for hd in range(H_REC)]

    def update_states(elements):
        for b in elements:
            r = slice(b * seq, (b + 1) * seq)
            for hd, sl in enumerate(heads):
                lhs = jnp.concatenate([k_end[r, sl], dec3[r, sl]], axis=0)
                rhs = jnp.concatenate(
                    [jnp.concatenate([iv[r, sl], zeros_blk], axis=1), rhs_bottom], axis=0)
                both = _dot_tn(lhs.astype(BF16), rhs.astype(BF16))
                rec_ref[b, hd] = both[:, dk:] * rec_in_ref[b, hd] + both[:, :dk]

    scores = [jnp.where(causal, _dot_nt(q_dec[:, sl], k_dec[:, sl]), 0.0).astype(BF16)
              for sl in heads]
    o_inter = jnp.concatenate(
        [jnp.concatenate([_bdot(q_dec[b * seq:(b + 1) * seq, sl], rec_in_ref[b, hd].astype(BF16))
                          for hd, sl in enumerate(heads)], axis=-1)
         for b in range(tb)], axis=0)
    o_intra = jnp.concatenate([_bdot(scores[hd], vb[:, sl]) for hd, sl in enumerate(heads)],
                              axis=-1)
    o_rec = _head_out(o_intra + o_inter, g, g_onorm_ref[...])
    mix_ref[:, 0:d_rec] = o_rec.astype(BF16)

    u = gc * hv
    ubuf_ref[SUBLANES:SUBLANES + rows, :] = u
    tok = _group_pos((rows, d_conv), seq)
    u_m1 = jnp.where(tok >= 1, ubuf_ref[SUBLANES - 1:SUBLANES - 1 + rows, :], cbx_ref[1])
    u_m2 = jnp.where(tok >= 2, ubuf_ref[SUBLANES - 2:SUBLANES - 2 + rows, :],
                     jnp.where(tok == 1, cbx_ref[1], cbx_ref[0]))
    w_conv = w_conv_ref[...]
    y_conv = w_conv[0:1] * u_m2 + w_conv[1:2] * u_m1 + w_conv[2:3] * u
    mix_ref[:, d_rec:d_rec + d_conv] = (gb * y_conv).astype(BF16)
    for b in range(tb):
        conv_ref[b] = u[(b + 1) * seq - (CONV_W - 1):(b + 1) * seq]

    x1 = x + g1 * _bdot(mix_ref[...], w_out_ref[...])
    update_states(range(tb // 2))
    h2 = _modulate(x1, sh2, sc2)
    mlp = _mlp_down(_mlp_up(h2, w_up_ref, 0), w_up_ref, w_down_ref, 0)
    for j in range(1, MLP_BLOCKS):
        mlp += _mlp_down(_mlp_up(h2, w_up_ref, j), w_up_ref, w_down_ref, j)
    update_states(range(tb // 2, tb))
    y_ref[...] = _final_norm(x1, g2, mlp, g_final_ref).reshape(tb, seq, d)


def _sample_layer(x, mod, lower_bounds, w_in, w_conv, g_onorm, w_out, w_up, w_down,
                  g_final, rec_in, conv_in):
    bsz, seq, d = x.shape
    d_rec = lower_bounds.shape[1]
    dk = d_rec // H_REC
    d_conv = w_conv.shape[1]
    tb = SAMPLE_TILE_B
    rows = tb * seq
    assert bsz % tb == 0 and seq == SUBLANES and seq >= CONV_W
    return pl.pallas_call(
        _sample_kernel,
        grid=(bsz // tb,),
        in_specs=[
            pl.BlockSpec((tb, seq, d), lambda i: (i, 0, 0)),
            pl.BlockSpec((tb, N_MOD * d), lambda i: (i, 0)),
            _const_spec(lower_bounds.shape),
            _const_spec(w_in.shape),
            _const_spec(w_conv.shape),
            _const_spec(g_onorm.shape),
            _const_spec(w_out.shape),
            _const_spec(w_up.shape),
            _const_spec(w_down.shape),
            _const_spec(g_final.shape),
            pl.BlockSpec((tb, H_REC, dk, dk), lambda i: (i, 0, 0, 0)),
            pl.BlockSpec((tb, CONV_W - 1, d_conv), lambda i: (i, 0, 0)),
        ],
        out_specs=[
            pl.BlockSpec((tb, seq, d), lambda i: (i, 0, 0)),
            pl.BlockSpec((tb, H_REC, dk, dk), lambda i: (i, 0, 0, 0)),
            pl.BlockSpec((tb, CONV_W - 1, d_conv), lambda i: (i, 0, 0)),
        ],
        out_shape=[
            jax.ShapeDtypeStruct((bsz, seq, d), F32),
            jax.ShapeDtypeStruct((bsz, H_REC, dk, dk), F32),
            jax.ShapeDtypeStruct((bsz, CONV_W - 1, d_conv), F32),
        ],
        scratch_shapes=[
            pltpu.VMEM((rows, N_MOD * d), F32),
            pltpu.VMEM((CONV_W - 1, rows, d_conv), F32),
            pltpu.VMEM((SUBLANES + rows, d_conv), F32),
            pltpu.VMEM((rows, d_rec + d_conv), BF16),
        ],
        compiler_params=pltpu.CompilerParams(
            dimension_semantics=("arbitrary",), vmem_limit_bytes=VMEM_LIMIT_BYTES),
        name="sample_layer",
    )(x, mod, lower_bounds, w_in, w_conv, g_onorm, w_out, w_up, w_down, g_final, rec_in, conv_in)


def kernel(x_prompt, x_sample, state_rec, state_conv, c_prompt, c_sample, lower_bounds, w_ada,
           b_ada, w_in, w_conv, g_onorm, w_out, w_up, w_down, g_final):
    depth = w_in.shape[0]
    assert depth == 1, "single-layer trunk"

    mod_s, mod_p = _modulation(c_sample, c_prompt, w_ada[0], b_ada)

    y_p, rec_p, conv_p, w_in_b, w_out_b, w_up_b, w_down_b = _prompt_layer(
        x_prompt, mod_p, lower_bounds, w_in[0], w_conv[0], g_onorm, w_out[0], w_up[0], w_down[0],
        g_final)
    y_s, rec_s, conv_s = _sample_layer(
        x_sample, mod_s, lower_bounds, w_in_b, w_conv[0], g_onorm, w_out_b, w_up_b, w_down_b,
        g_final, state_rec[0], state_conv[0])
    return y_p, y_s, rec_p[None], conv_p[None], rec_s[None], conv_s[None]
```

```python
import functools

import jax
import jax.numpy as jnp
from jax import lax
from jax.experimental import pallas as pl
from jax.experimental.pallas import tpu as pltpu

F32 = jnp.float32
BF16 = jnp.bfloat16

EPS = 1e-6
H_REC = 4
CONV_W = 3
N_MOD = 6
N_PROJ_GROUPS = 7
CHUNK = 64

SUBLANES = 8
VMEM_LIMIT_BYTES = 56 * 1024 * 1024

PROMPT_TILE = 512
SAMPLE_TILE_B = 16
MOD_TILE_K = 256
MLP_BLOCKS = 4
WEIGHT_STAGE_SHAPE = (512, 512)
WEIGHT_STAGE_SLOTS = 6


def _const_spec(shape):
    zeros = (0,) * len(shape)
    return pl.BlockSpec(shape, lambda *_: zeros, pipeline_mode=pl.Buffered(1))


def _rms(x):
    return x * lax.rsqrt(jnp.mean(x * x, axis=-1, keepdims=True) + EPS)


def _silu(x):
    return x * jax.nn.sigmoid(x)


def _bdot(a, b):
    return jnp.dot(a, b, preferred_element_type=F32)


def _dot_nt(a, b):
    return lax.dot_general(a, b, (((1,), (1,)), ((), ())), preferred_element_type=F32)


def _dot_tn(a, b):
    return lax.dot_general(a, b, (((0,), (0,)), ((), ())), preferred_element_type=F32)


def _group_pos(shape, group):
    return lax.broadcasted_iota(jnp.int32, shape, 0) % group


def _tile_cumprod(x):
    rows, n = x.shape
    x = x.reshape(rows // SUBLANES, SUBLANES, n)
    pos = lax.broadcasted_iota(jnp.int32, x.shape, 1)
    step = 1
    while step < SUBLANES:
        x = x * jnp.where(pos >= step, pltpu.roll(x, step, axis=1), 1.0)
        step *= 2
    return x.reshape(rows, n)


def _tile_last(x):
    rows, n = x.shape
    x = x.reshape(rows // SUBLANES, SUBLANES, n)
    return jnp.broadcast_to(x[:, SUBLANES - 1:, :], x.shape).reshape(rows, n)


def _chain_tiles(y):
    tiles, carry = [], None
    for r in range(0, y.shape[0], SUBLANES):
        t = y[r:r + SUBLANES]
        if carry is not None:
            t = t * carry
        tiles.append(t)
        carry = t[SUBLANES - 1:, :]
    return jnp.concatenate(tiles, axis=0)


def _layer_lower_bound(lb_ref, layer):
    lb = lb_ref[...]
    e = jnp.exp(lb - jnp.max(lb, axis=0, keepdims=True))
    sm = e / jnp.sum(e, axis=0, keepdims=True)
    return jnp.sum(sm[: layer + 1], axis=0, keepdims=True)


def _gates(fz, lb):
    sig = jax.nn.sigmoid(fz)
    f = lb + (1.0 - lb) * sig
    k = (1.0 - lb) * (1.0 - sig)
    return f, k


def _decay_terms(decay, row):
    hi = decay.astype(BF16).astype(F32)
    rest = decay - hi
    mid = rest.astype(BF16).astype(F32)
    lo = rest - mid
    return jnp.where(row == 0, hi, jnp.where(row == 1, mid, jnp.where(row == 2, lo, 0.0)))


def _decay_selector(n_rows, width):
    sub = lax.broadcasted_iota(jnp.int32, (n_rows, width), 0)
    return jnp.concatenate(
        [jnp.zeros((n_rows, width), F32), jnp.where(sub < 3, 1.0, 0.0)], axis=1)


def _head_out(o, g, g_onorm):
    dv = o.shape[-1] // H_REC
    pieces = []
    for h in range(H_REC):
        sl = slice(h * dv, (h + 1) * dv)
        pieces.append(_rms(o[:, sl]))
    return (jnp.concatenate(pieces, axis=-1) * g_onorm) * _silu(g)


def _modulate(x, shift, scale):
    return (_rms(x) * (1.0 + scale) + shift).astype(BF16)


def _mlp_cols(w_up_ref, j):
    blk = w_up_ref.shape[1] // MLP_BLOCKS
    return slice(j * blk, (j + 1) * blk)


def _mlp_up(h2, w_up_ref, j):
    up = jnp.maximum(_bdot(h2, w_up_ref[:, _mlp_cols(w_up_ref, j)]), 0.0)
    return (up * up).astype(BF16)


def _mlp_down(up, w_up_ref, w_down_ref, j):
    return _bdot(up, w_down_ref[_mlp_cols(w_up_ref, j), :])


def _final_norm(x, g2, mlp, g_final_ref):
    return _rms(x + g2 * mlp) * g_final_ref[...]


def _mod_kernel(c_s_ref, c_p_ref, w_ref, b_ref, o_s_ref, o_p_ref):
    k = pl.program_id(0)
    n_s = c_s_ref.shape[0]
    kb = w_ref.shape[0]
    cols = pl.ds(pl.multiple_of(k * kb, kb), kb)
    c = jnp.concatenate([c_s_ref[:, cols], c_p_ref[:, cols]], axis=0)
    part = _bdot(_silu(c).astype(BF16), w_ref[...].astype(BF16))

    @pl.when(k == 0)
    def _():
        o_s_ref[...] = part[:n_s] + b_ref[...]
        o_p_ref[...] = part[n_s:] + b_ref[...]

    @pl.when(k > 0)
    def _():
        o_s_ref[...] += part[:n_s]
        o_p_ref[...] += part[n_s:]


def _modulation(c_s, c_p, w_ada, b_ada):
    (n_s, d), n_p = c_s.shape, c_p.shape[0]
    n = w_ada.shape[1]
    assert n_s % SUBLANES == 0 and n_p % SUBLANES == 0 and d % MOD_TILE_K == 0
    return pl.pallas_call(
        _mod_kernel,
        grid=(d // MOD_TILE_K,),
        in_specs=[
            pl.BlockSpec((n_s, d), lambda k: (0, 0)),
            pl.BlockSpec((n_p, d), lambda k: (0, 0)),
            pl.BlockSpec((MOD_TILE_K, n), lambda k: (k, 0)),
            pl.BlockSpec((1, n), lambda k: (0, 0)),
        ],
        out_specs=[
            pl.BlockSpec((n_s, n), lambda k: (0, 0)),
            pl.BlockSpec((n_p, n), lambda k: (0, 0)),
        ],
        out_shape=[
            jax.ShapeDtypeStruct((n_s, n), F32),
            jax.ShapeDtypeStruct((n_p, n), F32),
        ],
        compiler_params=pltpu.CompilerParams(
            dimension_semantics=("arbitrary",), vmem_limit_bytes=VMEM_LIMIT_BYTES),
        name="adaln_modulation",
    )(c_s, c_p, w_ada, b_ada)


def _split_proj(proj):
    width = proj.shape[1] // N_PROJ_GROUPS
    return tuple(proj[:, p * width:(p + 1) * width] for p in range(N_PROJ_GROUPS))


def _weight_blocks(w_hbm, w_vmem):
    rows, cols = w_hbm.shape
    br, bc = WEIGHT_STAGE_SHAPE
    assert rows % br == 0 and cols % bc == 0
    return [(w_hbm.at[r:r + br, c:c + bc], w_vmem.at[r:r + br, c:c + bc])
            for r in range(0, rows, br) for c in range(0, cols, bc)]


def _stage_copy(src, stage_ref, sem_ref, i):
    slot = i % WEIGHT_STAGE_SLOTS
    return pltpu.make_async_copy(src, stage_ref.at[slot], sem_ref.at[slot])


def _load_weights_as_bf16(w_hbm_refs, w_vmem_refs, stage_ref, sem_ref):
    blocks = [blk for w_hbm, w_vmem in zip(w_hbm_refs, w_vmem_refs)
              for blk in _weight_blocks(w_hbm, w_vmem)]
    ahead = WEIGHT_STAGE_SLOTS - 1
    for i in range(min(ahead, len(blocks))):
        _stage_copy(blocks[i][0], stage_ref, sem_ref, i).start()
    for i, (src, dst) in enumerate(blocks):
        if i + ahead < len(blocks):
            _stage_copy(blocks[i + ahead][0], stage_ref, sem_ref, i + ahead).start()
        _stage_copy(src, stage_ref, sem_ref, i).wait()
        dst[...] = stage_ref[i % WEIGHT_STAGE_SLOTS].astype(BF16)


def _alternate(*stages):
    stages = list(stages)
    while stages:
        for stage in list(stages):
            try:
                next(stage)
            except StopIteration:
                stages.remove(stage)


def _mlp_stage(x1_ref, mod, w_up_ref, w_down_ref, g_final_ref, y_ref):
    d = x1_ref.shape[1]
    sh2, sc2, g2 = (mod[:, i * d:(i + 1) * d] for i in range(3, N_MOD))
    h2 = _modulate(x1_ref[...], sh2, sc2)
    ups = [_mlp_up(h2, w_up_ref, 0)]
    yield
    mlp = None
    for j in range(MLP_BLOCKS):
        if j + 1 < MLP_BLOCKS:
            ups.append(_mlp_up(h2, w_up_ref, j + 1))
        part = _mlp_down(ups[j], w_up_ref, w_down_ref, j)
        mlp = part if mlp is None else mlp + part
        if j + 2 < MLP_BLOCKS:
            yield
    y_ref[...] = _final_norm(x1_ref[...], g2, mlp, g_final_ref)
    yield


def _prompt_mix_stage(x_ref, mod, lb_ref, w_in_ref, w_conv_ref, g_onorm_ref, w_out_ref,
                      st_ref, ubuf_ref, mix_ref, x1_ref, rec_ref, conv_ref, is_last):
    tl, d = x_ref.shape
    d_rec = lb_ref.shape[1]
    dk = d_rec // H_REC
    n_chunks = tl // CHUNK
    heads = [slice(hd * dk, (hd + 1) * dk) for hd in range(H_REC)]
    sh1, sc1, g1 = (mod[:, i * d:(i + 1) * d] for i in range(3))

    x = x_ref[...]
    proj = _bdot(_modulate(x, sh1, sc1), w_in_ref[...])
    q, fz, iv, g, gb, gc, hv = _split_proj(proj)
    d_conv = gb.shape[1]
    yield

    lb = _layer_lower_bound(lb_ref, 0)
    row = lax.broadcasted_iota(jnp.int32, (CHUNK, CHUNK), 0)
    col = lax.broadcasted_iota(jnp.int32, (CHUNK, CHUNK), 1)
    causal = row >= col
    f_all, k_all = _gates(fz, lb)
    prod_tiles = _tile_cumprod(f_all)
    q_dec, decay, scores, upd = [], [], [], []
    for c in range(n_chunks):
        rows = slice(c * CHUNK, (c + 1) * CHUNK)
        prod = _chain_tiles(prod_tiles[rows])
        dec = prod[CHUNK - 1:CHUNK, :]
        q_dec.append((q[rows] * prod).astype(BF16))
        k_inv = k_all[rows] / prod
        k_dec = k_inv.astype(BF16)
        k_end = (k_inv * dec).astype(BF16)
        decay.append([jnp.broadcast_to(dec[:, sl], (dk, dk)).T for sl in heads])
        v = iv[rows].astype(BF16)
        scores.append([jnp.where(causal, _dot_nt(q_dec[c][:, sl], k_dec[:, sl]), 0.0).astype(BF16)
                       for sl in heads])
        upd.append([_dot_tn(k_end[:, sl], v[:, sl]) for sl in heads])
    yield

    st = [st_ref[hd] for hd in range(H_REC)]
    o_chunks = []
    for c in range(n_chunks):
        v = iv[c * CHUNK:(c + 1) * CHUNK].astype(BF16)
        o_heads = []
        for hd, sl in enumerate(heads):
            o_heads.append(_bdot(scores[c][hd], v[:, sl])
                           + _bdot(q_dec[c][:, sl], st[hd].astype(BF16)))
            st[hd] = st[hd] * decay[c][hd] + upd[c][hd]
        o_chunks.append(jnp.concatenate(o_heads, axis=-1))
    for hd in range(H_REC):
        st_ref[hd] = st[hd]
    yield

    o_rec = _head_out(jnp.concatenate(o_chunks, axis=0), g, g_onorm_ref[...])
    mix_ref[:, 0:d_rec] = o_rec.astype(BF16)

    u = gc * hv
    ubuf_ref[SUBLANES:SUBLANES + tl, :] = u
    w_conv = w_conv_ref[0]
    y_conv = (w_conv[0:1] * ubuf_ref[SUBLANES - 2:SUBLANES - 2 + tl, :]
              + w_conv[1:2] * ubuf_ref[SUBLANES - 1:SUBLANES - 1 + tl, :]
              + w_conv[2:3] * u)
    mix_ref[:, d_rec:d_rec + d_conv] = (gb * y_conv).astype(BF16)
    tail = u[tl - (CONV_W - 1):tl]
    ubuf_ref[SUBLANES - (CONV_W - 1):SUBLANES, :] = tail
    yield

    x1 = x + g1 * _bdot(mix_ref[...], w_out_ref[...])
    x1_ref[...] = x1

    @pl.when(is_last)
    def _():
        for hd in range(H_REC):
            rec_ref[hd] = st_ref[hd]
        conv_ref[...] = tail


def _prompt_kernel(x_ref, mod_ref, lb_ref, w_in_hbm, w_conv_ref, g_onorm_ref, w_out_hbm,
                   w_up_hbm, w_down_hbm, g_final_ref,
                   y_ref, rec_ref, conv_ref, w_in_out, w_out_out, w_up_out, w_down_out,
                   st_ref, ubuf_ref, mix_ref, x1_ref,
                   w_in_ref, w_out_ref, w_up_ref, w_down_ref, stage_ref, stage_sem, out_sem,
                   *, n_tiles, n_steps):
    s = pl.program_id(0)
    tile = jnp.minimum(s, n_steps - 1)
    l = lax.rem(tile, n_tiles)
    d = x_ref.shape[1]

    w_vmem = (w_in_ref, w_out_ref, w_up_ref, w_down_ref)
    w_outs = (w_in_out, w_out_out, w_up_out, w_down_out)

    def bf16_export(i):
        return pltpu.make_async_copy(w_vmem[i], w_outs[i], out_sem.at[i])

    @pl.when(s == 0)
    def _():
        _load_weights_as_bf16((w_in_hbm, w_out_hbm, w_up_hbm, w_down_hbm), w_vmem,
                              stage_ref, stage_sem)
        for i in range(len(w_vmem)):
            bf16_export(i).start()

    @pl.when(s == n_steps)
    def _():
        for i in range(len(w_vmem)):
            bf16_export(i).wait()

    @pl.when(l == 0)
    def _():
        st_ref[...] = jnp.zeros_like(st_ref)
        ubuf_ref[0:SUBLANES, :] = jnp.zeros((SUBLANES, ubuf_ref.shape[1]), F32)

    def mix_stage():
        mod = mod_ref[pl.ds(lax.div(tile, n_tiles), 1), :]
        return _prompt_mix_stage(x_ref, mod, lb_ref, w_in_ref, w_conv_ref, g_onorm_ref,
                                 w_out_ref, st_ref, ubuf_ref, mix_ref, x1_ref,
                                 rec_ref, conv_ref, l == n_tiles - 1)

    def mlp_stage():
        mod_prev = mod_ref[pl.ds(lax.div(s - 1, n_tiles), 1), :]
        return _mlp_stage(x1_ref, mod_prev, w_up_ref, w_down_ref, g_final_ref, y_ref)

    @pl.when(s == 0)
    def _():
        _alternate(mix_stage())

    @pl.when((s > 0) & (s < n_steps))
    def _():
        _alternate(mix_stage(), mlp_stage())

    @pl.when(s == n_steps)
    def _():
        _alternate(mlp_stage())


def _prompt_layer(x, mod_p, lower_bounds, w_in, w_conv, g_onorm, w_out, w_up, w_down, g_final):
    bsz, seq, d = x.shape
    d_rec = lower_bounds.shape[1]
    dk = d_rec // H_REC
    d_conv = w_conv.shape[-1]
    tl = PROMPT_TILE
    n_tiles = seq // tl
    n_steps = bsz * n_tiles
    assert seq % tl == 0 and tl % CHUNK == 0
    mats = (w_in, w_out, w_up, w_down)
    hbm = pl.BlockSpec(memory_space=pl.ANY)

    def mix_tile(s):
        t = jnp.minimum(s, n_steps - 1)
        return lax.div(t, n_tiles), lax.rem(t, n_tiles)

    def mlp_tile(s):
        t = jnp.maximum(s - 1, 0)
        return lax.div(t, n_tiles), lax.rem(t, n_tiles)

    return pl.pallas_call(
        functools.partial(_prompt_kernel, n_tiles=n_tiles, n_steps=n_steps),
        grid=(n_steps + 1,),
        in_specs=[
            pl.BlockSpec((None, tl, d), lambda s: (*mix_tile(s), 0)),
            _const_spec(mod_p.shape),
            _const_spec(lower_bounds.shape),
            hbm,
            _const_spec(w_conv.shape),
            _const_spec(g_onorm.shape),
            hbm,
            hbm,
            hbm,
            _const_spec(g_final.shape),
        ],
        out_specs=[
            pl.BlockSpec((None, tl, d), lambda s: (*mlp_tile(s), 0)),
            pl.BlockSpec((None, H_REC, dk, dk), lambda s: (mix_tile(s)[0], 0, 0, 0)),
            pl.BlockSpec((None, CONV_W - 1, d_conv), lambda s: (mix_tile(s)[0], 0, 0)),
        ] + [hbm] * len(mats),
        out_shape=[
            jax.ShapeDtypeStruct((bsz, seq, d), F32),
            jax.ShapeDtypeStruct((bsz, H_REC, dk, dk), F32),
            jax.ShapeDtypeStruct((bsz, CONV_W - 1, d_conv), F32),
        ] + [jax.ShapeDtypeStruct(w.shape, BF16) for w in mats],
        scratch_shapes=[
            pltpu.VMEM((H_REC, dk, dk), F32),
            pltpu.VMEM((SUBLANES + tl, d_conv), F32),
            pltpu.VMEM((tl, d_rec + d_conv), BF16),
            pltpu.VMEM((tl, d), F32),
        ] + [pltpu.VMEM(w.shape, BF16) for w in mats] + [
            pltpu.VMEM((WEIGHT_STAGE_SLOTS, *WEIGHT_STAGE_SHAPE), F32),
            pltpu.SemaphoreType.DMA((WEIGHT_STAGE_SLOTS,)),
            pltpu.SemaphoreType.DMA((len(mats),)),
        ],
        compiler_params=pltpu.CompilerParams(
            dimension_semantics=("arbitrary",), vmem_limit_bytes=VMEM_LIMIT_BYTES),
        name="prompt_layer",
    )(x, mod_p, lower_bounds, w_in, w_conv, g_onorm, w_out, w_up, w_down, g_final)


def _sample_kernel(x_ref, mod_ref, lb_ref, w_in_ref, w_conv_ref, g_onorm_ref, w_out_ref,
                   w_up_ref, w_down_ref, g_final_ref, rec_in_ref, conv_in_ref,
                   y_ref, rec_ref, conv_ref,
                   modx_ref, cbx_ref, ubuf_ref, mix_ref):
    tb, seq, d = x_ref.shape
    rows = tb * seq
    d_rec = lb_ref.shape[1]
    dk = d_rec // H_REC
    d_conv = conv_in_ref.shape[2]

    for b in range(tb):
        r = slice(b * seq, (b + 1) * seq)
        modx_ref[r, :] = jnp.broadcast_to(mod_ref[b:b + 1, :], (seq, mod_ref.shape[1]))
        for j in range(CONV_W - 1):
            cbx_ref[j, r, :] = jnp.broadcast_to(conv_in_ref[b, j:j + 1, :], (seq, d_conv))

    sh1, sc1, g1, sh2, sc2, g2 = (modx_ref[:, i * d:(i + 1) * d] for i in range(N_MOD))

    x = x_ref[...].reshape(rows, d)
    proj = _bdot(_modulate(x, sh1, sc1), w_in_ref[...])
    q, fz, iv, g, gb, gc, hv = _split_proj(proj)

    lb = _layer_lower_bound(lb_ref, 0)
    row = lax.broadcasted_iota(jnp.int32, (rows, rows), 0)
    col = lax.broadcasted_iota(jnp.int32, (rows, rows), 1)
    causal = ((row // seq) == (col // seq)) & (row >= col)
    f, k = _gates(fz, lb)
    prod = _tile_cumprod(f)
    decay = _tile_last(prod)
    q_dec = (q * prod).astype(BF16)
    k_inv = k / prod
    k_dec = k_inv.astype(BF16)
    k_end = k_inv * decay
    vb = iv.astype(BF16)

    dec3 = _decay_terms(decay, _group_pos((rows, d_rec), seq))
    zeros_blk = jnp.zeros((seq, dk), F32)
    rhs_bottom = _decay_selector(seq, dk)

    heads = [slice(hd * dk, (hd + 1) * dk) for hd in range(H_REC)]

    def update_states(elements):
        for b in elements:
            r = slice(b * seq, (b + 1) * seq)
            for hd, sl in enumerate(heads):
                lhs = jnp.concatenate([k_end[r, sl], dec3[r, sl]], axis=0)
                rhs = jnp.concatenate(
                    [jnp.concatenate([iv[r, sl], zeros_blk], axis=1), rhs_bottom], axis=0)
                both = _dot_tn(lhs.astype(BF16), rhs.astype(BF16))
                rec_ref[b, hd] = both[:, dk:] * rec_in_ref[b, hd] + both[:, :dk]

    scores = [jnp.where(causal, _dot_nt(q_dec[:, sl], k_dec[:, sl]), 0.0).astype(BF16)
              for sl in heads]
    o_inter = jnp.concatenate(
        [jnp.concatenate([_bdot(q_dec[b * seq:(b + 1) * seq, sl], rec_in_ref[b, hd].astype(BF16))
                          for hd, sl in enumerate(heads)], axis=-1)
         for b in range(tb)], axis=0)
    o_intra = jnp.concatenate([_bdot(scores[hd], vb[:, sl]) for hd, sl in enumerate(heads)],
                              axis=-1)
    o_rec = _head_out(o_intra + o_inter, g, g_onorm_ref[...])
    mix_ref[:, 0:d_rec] = o_rec.astype(BF16)

    u = gc * hv
    ubuf_ref[SUBLANES:SUBLANES + rows, :] = u
    tok = _group_pos((rows, d_conv), seq)
    u_m1 = jnp.where(tok >= 1, ubuf_ref[SUBLANES - 1:SUBLANES - 1 + rows, :], cbx_ref[1])
    u_m2 = jnp.where(tok >= 2, ubuf_ref[SUBLANES - 2:SUBLANES - 2 + rows, :],
                     jnp.where(tok == 1, cbx_ref[1], cbx_ref[0]))
    w_conv = w_conv_ref[0]
    y_conv = w_conv[0:1] * u_m2 + w_conv[1:2] * u_m1 + w_conv[2:3] * u
    mix_ref[:, d_rec:d_rec + d_conv] = (gb * y_conv).astype(BF16)
    for b in range(tb):
        conv_ref[b] = u[(b + 1) * seq - (CONV_W - 1):(b + 1) * seq]

    x1 = x + g1 * _bdot(mix_ref[...], w_out_ref[...])
    update_states(range(tb // 2))
    h2 = _modulate(x1, sh2, sc2)
    mlp = _mlp_down(_mlp_up(h2, w_up_ref, 0), w_up_ref, w_down_ref, 0)
    for j in range(1, MLP_BLOCKS):
        mlp += _mlp_down(_mlp_up(h2, w_up_ref, j), w_up_ref, w_down_ref, j)
    update_states(range(tb // 2, tb))
    y_ref[...] = _final_norm(x1, g2, mlp, g_final_ref).reshape(tb, seq, d)


def _sample_layer(x, mod, lower_bounds, w_in, w_conv, g_onorm, w_out, w_up, w_down,
                  g_final, rec_in, conv_in):
    bsz, seq, d = x.shape
    d_rec = lower_bounds.shape[1]
    dk = d_rec // H_REC
    d_conv = w_conv.shape[-1]
    tb = SAMPLE_TILE_B
    rows = tb * seq
    assert bsz % tb == 0 and seq == SUBLANES and seq >= CONV_W
    return pl.pallas_call(
        _sample_kernel,
        grid=(bsz // tb,),
        in_specs=[
            pl.BlockSpec((tb, seq, d), lambda i: (i, 0, 0)),
            pl.BlockSpec((tb, N_MOD * d), lambda i: (i, 0)),
            _const_spec(lower_bounds.shape),
            _const_spec(w_in.shape),
            _const_spec(w_conv.shape),
            _const_spec(g_onorm.shape),
            _const_spec(w_out.shape),
            _const_spec(w_up.shape),
            _const_spec(w_down.shape),
            _const_spec(g_final.shape),
            pl.BlockSpec((tb, H_REC, dk, dk), lambda i: (i, 0, 0, 0)),
            pl.BlockSpec((tb, CONV_W - 1, d_conv), lambda i: (i, 0, 0)),
        ],
        out_specs=[
            pl.BlockSpec((tb, seq, d), lambda i: (i, 0, 0)),
            pl.BlockSpec((tb, H_REC, dk, dk), lambda i: (i, 0, 0, 0)),
            pl.BlockSpec((tb, CONV_W - 1, d_conv), lambda i: (i, 0, 0)),
        ],
        out_shape=[
            jax.ShapeDtypeStruct((bsz, seq, d), F32),
            jax.ShapeDtypeStruct((bsz, H_REC, dk, dk), F32),
            jax.ShapeDtypeStruct((bsz, CONV_W - 1, d_conv), F32),
        ],
        scratch_shapes=[
            pltpu.VMEM((rows, N_MOD * d), F32),
            pltpu.VMEM((CONV_W - 1, rows, d_conv), F32),
            pltpu.VMEM((SUBLANES + rows, d_conv), F32),
            pltpu.VMEM((rows, d_rec + d_conv), BF16),
        ],
        compiler_params=pltpu.CompilerParams(
            dimension_semantics=("arbitrary",), vmem_limit_bytes=VMEM_LIMIT_BYTES),
        name="sample_layer",
    )(x, mod, lower_bounds, w_in, w_conv, g_onorm, w_out, w_up, w_down, g_final, rec_in, conv_in)


def kernel(x_prompt, x_sample, state_rec, state_conv, c_prompt, c_sample, lower_bounds, w_ada,
           b_ada, w_in, w_conv, g_onorm, w_out, w_up, w_down, g_final):
    depth = w_in.shape[0]
    assert depth == 1, "single-layer trunk"

    mod_s, mod_p = _modulation(c_sample, c_prompt, w_ada[0], b_ada)

    y_p, rec_p, conv_p, w_in_b, w_out_b, w_up_b, w_down_b = _prompt_layer(
        x_prompt, mod_p, lower_bounds, w_in[0], w_conv, g_onorm, w_out[0], w_up[0], w_down[0],
        g_final)
    y_s, rec_s, conv_s = _sample_layer(
        x_sample, mod_s, lower_bounds, w_in_b, w_conv, g_onorm, w_out_b, w_up_b, w_down_b,
        g_final, state_rec[0], state_conv[0])
    return y_p, y_s, rec_p[None], conv_p[None], rec_s[None], conv_s[None]
```

```python
import functools

import jax
import jax.numpy as jnp
from jax import lax
from jax.experimental import pallas as pl
from jax.experimental.pallas import tpu as pltpu

F32 = jnp.float32
BF16 = jnp.bfloat16

EPS = 1e-6
H_REC = 4
CONV_W = 3
N_MOD = 6
N_PROJ_GROUPS = 7
CHUNK = 64

SUBLANES = 8
VMEM_LIMIT_BYTES = 56 * 1024 * 1024

PROMPT_TILE = 512
SAMPLE_TILE_B = 16
MOD_TILE_K = 256
MLP_BLOCKS = 4
WEIGHT_STAGE_SHAPE = (512, 512)
WEIGHT_STAGE_SLOTS = 6


def _const_spec(shape):
    zeros = (0,) * len(shape)
    return pl.BlockSpec(shape, lambda *_: zeros, pipeline_mode=pl.Buffered(1))


def _rms(x):
    return x * lax.rsqrt(jnp.mean(x * x, axis=-1, keepdims=True) + EPS)


def _silu(x):
    return x * jax.nn.sigmoid(x)


def _bdot(a, b):
    return jnp.dot(a, b, preferred_element_type=F32)


def _dot_nt(a, b):
    return lax.dot_general(a, b, (((1,), (1,)), ((), ())), preferred_element_type=F32)


def _dot_tn(a, b):
    return lax.dot_general(a, b, (((0,), (0,)), ((), ())), preferred_element_type=F32)


def _group_pos(shape, group):
    return lax.broadcasted_iota(jnp.int32, shape, 0) % group


def _tile_cumprod(x):
    rows, n = x.shape
    x = x.reshape(rows // SUBLANES, SUBLANES, n)
    pos = lax.broadcasted_iota(jnp.int32, x.shape, 1)
    step = 1
    while step < SUBLANES:
        x = x * jnp.where(pos >= step, pltpu.roll(x, step, axis=1), 1.0)
        step *= 2
    return x.reshape(rows, n)


def _tile_last(x):
    rows, n = x.shape
    x = x.reshape(rows // SUBLANES, SUBLANES, n)
    return jnp.broadcast_to(x[:, SUBLANES - 1:, :], x.shape).reshape(rows, n)


def _chain_tiles(y):
    tiles, carry = [], None
    for r in range(0, y.shape[0], SUBLANES):
        t = y[r:r + SUBLANES]
        if carry is not None:
            t = t * carry
        tiles.append(t)
        carry = t[SUBLANES - 1:, :]
    return jnp.concatenate(tiles, axis=0)


def _layer_lower_bound(lb_ref, layer):
    lb = lb_ref[...]
    e = jnp.exp(lb - jnp.max(lb, axis=0, keepdims=True))
    sm = e / jnp.sum(e, axis=0, keepdims=True)
    return jnp.sum(sm[: layer + 1], axis=0, keepdims=True)


def _gates(fz, lb):
    sig = jax.nn.sigmoid(fz)
    f = lb + (1.0 - lb) * sig
    k = (1.0 - lb) * (1.0 - sig)
    return f, k


def _decay_terms(decay, row):
    hi = decay.astype(BF16).astype(F32)
    rest = decay - hi
    mid = rest.astype(BF16).astype(F32)
    lo = rest - mid
    return jnp.where(row == 0, hi, jnp.where(row == 1, mid, jnp.where(row == 2, lo, 0.0)))


def _decay_selector(n_rows, width):
    sub = lax.broadcasted_iota(jnp.int32, (n_rows, width), 0)
    return jnp.concatenate(
        [jnp.zeros((n_rows, width), F32), jnp.where(sub < 3, 1.0, 0.0)], axis=1)


def _head_out(o, g, g_onorm):
    dv = o.shape[-1] // H_REC
    pieces = []
    for h in range(H_REC):
        sl = slice(h * dv, (h + 1) * dv)
        pieces.append(_rms(o[:, sl]))
    return (jnp.concatenate(pieces, axis=-1) * g_onorm) * _silu(g)


def _modulate(x, shift, scale):
    return (_rms(x) * (1.0 + scale) + shift).astype(BF16)


def _mlp_cols(w_up_ref, j):
    blk = w_up_ref.shape[1] // MLP_BLOCKS
    return slice(j * blk, (j + 1) * blk)


def _mlp_up(h2, w_up_ref, j):
    up = jnp.maximum(_bdot(h2, w_up_ref[:, _mlp_cols(w_up_ref, j)]), 0.0)
    return (up * up).astype(BF16)


def _mlp_down(up, w_up_ref, w_down_ref, j):
    return _bdot(up, w_down_ref[_mlp_cols(w_up_ref, j), :])


def _final_norm(x, g2, mlp, g_final_ref):
    return _rms(x + g2 * mlp) * g_final_ref[...]


def _mod_kernel(c_s_ref, c_p_ref, w_lo_ref, w_hi_ref, b_ref, o_s_ref, o_p_ref):
    k = pl.program_id(0)
    n_s = c_s_ref.shape[0]
    w = jnp.concatenate([w_lo_ref[...], w_hi_ref[...]], axis=0)
    kb = w.shape[0]
    cols = pl.ds(pl.multiple_of(k * kb, kb), kb)
    c = jnp.concatenate([c_s_ref[:, cols], c_p_ref[:, cols]], axis=0)
    part = _bdot(_silu(c).astype(BF16), w.astype(BF16))

    @pl.when(k == 0)
    def _():
        o_s_ref[...] = part[:n_s] + b_ref[...]
        o_p_ref[...] = part[n_s:] + b_ref[...]

    @pl.when(k > 0)
    def _():
        o_s_ref[...] += part[:n_s]
        o_p_ref[...] += part[n_s:]


def _modulation(c_s, c_p, w_ada, b_ada):
    (n_s, d), n_p = c_s.shape, c_p.shape[0]
    n = w_ada.shape[1]
    assert n_s % SUBLANES == 0 and n_p % SUBLANES == 0 and d % MOD_TILE_K == 0
    half = MOD_TILE_K // 2
    return pl.pallas_call(
        _mod_kernel,
        grid=(d // MOD_TILE_K,),
        in_specs=[
            pl.BlockSpec((n_s, d), lambda k: (0, 0)),
            pl.BlockSpec((n_p, d), lambda k: (0, 0)),
            pl.BlockSpec((half, n), lambda k: (2 * k, 0)),
            pl.BlockSpec((half, n), lambda k: (2 * k + 1, 0)),
            pl.BlockSpec((1, n), lambda k: (0, 0)),
        ],
        out_specs=[
            pl.BlockSpec((n_s, n), lambda k: (0, 0)),
            pl.BlockSpec((n_p, n), lambda k: (0, 0)),
        ],
        out_shape=[
            jax.ShapeDtypeStruct((n_s, n), F32),
            jax.ShapeDtypeStruct((n_p, n), F32),
        ],
        compiler_params=pltpu.CompilerParams(
            dimension_semantics=("arbitrary",), vmem_limit_bytes=VMEM_LIMIT_BYTES),
        name="adaln_modulation",
    )(c_s, c_p, w_ada, w_ada, b_ada)


def _split_proj(proj):
    width = proj.shape[1] // N_PROJ_GROUPS
    return tuple(proj[:, p * width:(p + 1) * width] for p in range(N_PROJ_GROUPS))


def _weight_blocks(w_hbm, w_vmem):
    rows, cols = w_hbm.shape
    br, bc = WEIGHT_STAGE_SHAPE
    assert rows % br == 0 and cols % bc == 0
    return [(w_hbm.at[r:r + br, c:c + bc], w_vmem.at[r:r + br, c:c + bc])
            for r in range(0, rows, br) for c in range(0, cols, bc)]


def _stage_copy(src, stage_ref, sem_ref, i):
    slot = i % WEIGHT_STAGE_SLOTS
    return pltpu.make_async_copy(src, stage_ref.at[slot], sem_ref.at[slot])


def _load_weights_as_bf16(w_hbm_refs, w_vmem_refs, stage_ref, sem_ref):
    blocks = [blk for w_hbm, w_vmem in zip(w_hbm_refs, w_vmem_refs)
              for blk in _weight_blocks(w_hbm, w_vmem)]
    ahead = WEIGHT_STAGE_SLOTS - 1
    for i in range(min(ahead, len(blocks))):
        _stage_copy(blocks[i][0], stage_ref, sem_ref, i).start()
    for i, (src, dst) in enumerate(blocks):
        if i + ahead < len(blocks):
            _stage_copy(blocks[i + ahead][0], stage_ref, sem_ref, i + ahead).start()
        _stage_copy(src, stage_ref, sem_ref, i).wait()
        dst[...] = stage_ref[i % WEIGHT_STAGE_SLOTS].astype(BF16)


def _alternate(*stages):
    stages = list(stages)
    while stages:
        for stage in list(stages):
            try:
                next(stage)
            except StopIteration:
                stages.remove(stage)


def _mlp_stage(x1_ref, mod, w_up_ref, w_down_ref, g_final_ref, y_ref):
    d = x1_ref.shape[1]
    sh2, sc2, g2 = (mod[:, i * d:(i + 1) * d] for i in range(3, N_MOD))
    h2 = _modulate(x1_ref[...], sh2, sc2)
    ups = [_mlp_up(h2, w_up_ref, 0)]
    yield
    mlp = None
    for j in range(MLP_BLOCKS):
        if j + 1 < MLP_BLOCKS:
            ups.append(_mlp_up(h2, w_up_ref, j + 1))
        part = _mlp_down(ups[j], w_up_ref, w_down_ref, j)
        mlp = part if mlp is None else mlp + part
        if j + 2 < MLP_BLOCKS:
            yield
    y_ref[...] = _final_norm(x1_ref[...], g2, mlp, g_final_ref)
    yield


def _prompt_mix_stage(x_ref, mod, lb_ref, w_in_ref, w_conv_ref, g_onorm_ref, w_out_ref,
                      st_ref, ubuf_ref, mix_ref, x1_ref, rec_ref, conv_ref, is_last):
    tl, d = x_ref.shape
    d_rec = lb_ref.shape[1]
    dk = d_rec // H_REC
    n_chunks = tl // CHUNK
    heads = [slice(hd * dk, (hd + 1) * dk) for hd in range(H_REC)]
    sh1, sc1, g1 = (mod[:, i * d:(i + 1) * d] for i in range(3))

    x = x_ref[...]
    proj = _bdot(_modulate(x, sh1, sc1), w_in_ref[...])
    q, fz, iv, g, gb, gc, hv = _split_proj(proj)
    d_conv = gb.shape[1]
    yield

    lb = _layer_lower_bound(lb_ref, 0)
    row = lax.broadcasted_iota(jnp.int32, (CHUNK, CHUNK), 0)
    col = lax.broadcasted_iota(jnp.int32, (CHUNK, CHUNK), 1)
    causal = row >= col
    f_all, k_all = _gates(fz, lb)
    prod_tiles = _tile_cumprod(f_all)
    q_dec, decay, scores, upd = [], [], [], []
    for c in range(n_chunks):
        rows = slice(c * CHUNK, (c + 1) * CHUNK)
        prod = _chain_tiles(prod_tiles[rows])
        dec = prod[CHUNK - 1:CHUNK, :]
        q_dec.append((q[rows] * prod).astype(BF16))
        k_inv = k_all[rows] / prod
        k_dec = k_inv.astype(BF16)
        k_end = (k_inv * dec).astype(BF16)
        decay.append([jnp.broadcast_to(dec[:, sl], (dk, dk)).T for sl in heads])
        v = iv[rows].astype(BF16)
        scores.append([jnp.where(causal, _dot_nt(q_dec[c][:, sl], k_dec[:, sl]), 0.0).astype(BF16)
                       for sl in heads])
        upd.append([_dot_tn(k_end[:, sl], v[:, sl]) for sl in heads])
    yield

    st = [st_ref[hd] for hd in range(H_REC)]
    o_chunks = []
    for c in range(n_chunks):
        v = iv[c * CHUNK:(c + 1) * CHUNK].astype(BF16)
        o_heads = []
        for hd, sl in enumerate(heads):
            o_heads.append(_bdot(scores[c][hd], v[:, sl])
                           + _bdot(q_dec[c][:, sl], st[hd].astype(BF16)))
            st[hd] = st[hd] * decay[c][hd] + upd[c][hd]
        o_chunks.append(jnp.concatenate(o_heads, axis=-1))
    for hd in range(H_REC):
        st_ref[hd] = st[hd]
    yield

    o_rec = _head_out(jnp.concatenate(o_chunks, axis=0), g, g_onorm_ref[...])
    mix_ref[:, 0:d_rec] = o_rec.astype(BF16)

    u = gc * hv
    ubuf_ref[SUBLANES:SUBLANES + tl, :] = u
    w_conv = w_conv_ref[...]
    y_conv = (w_conv[0:1] * ubuf_ref[SUBLANES - 2:SUBLANES - 2 + tl, :]
              + w_conv[1:2] * ubuf_ref[SUBLANES - 1:SUBLANES - 1 + tl, :]
              + w_conv[2:3] * u)
    mix_ref[:, d_rec:d_rec + d_conv] = (gb * y_conv).astype(BF16)
    tail = u[tl - (CONV_W - 1):tl]
    ubuf_ref[SUBLANES - (CONV_W - 1):SUBLANES, :] = tail
    yield

    x1 = x + g1 * _bdot(mix_ref[...], w_out_ref[...])
    x1_ref[...] = x1

    @pl.when(is_last)
    def _():
        for hd in range(H_REC):
            rec_ref[hd] = st_ref[hd]
        conv_ref[...] = tail


def _prompt_kernel(x_ref, mod_ref, lb_ref, w_in_hbm, w_conv_ref, g_onorm_ref, w_out_hbm,
                   w_up_hbm, w_down_hbm, g_final_ref,
                   y_ref, rec_ref, conv_ref, w_in_out, w_out_out, w_up_out, w_down_out,
                   st_ref, ubuf_ref, mix_ref, x1_ref,
                   w_in_ref, w_out_ref, w_up_ref, w_down_ref, stage_ref, stage_sem, out_sem,
                   *, n_tiles, n_steps):
    s = pl.program_id(0)
    tile = jnp.minimum(s, n_steps - 1)
    l = lax.rem(tile, n_tiles)
    d = x_ref.shape[1]

    w_vmem = (w_in_ref, w_out_ref, w_up_ref, w_down_ref)
    w_outs = (w_in_out, w_out_out, w_up_out, w_down_out)

    def bf16_export(i):
        return pltpu.make_async_copy(w_vmem[i], w_outs[i], out_sem.at[i])

    @pl.when(s == 0)
    def _():
        _load_weights_as_bf16((w_in_hbm, w_out_hbm, w_up_hbm, w_down_hbm), w_vmem,
                              stage_ref, stage_sem)
        for i in range(len(w_vmem)):
            bf16_export(i).start()

    @pl.when(s == n_steps)
    def _():
        for i in range(len(w_vmem)):
            bf16_export(i).wait()

    @pl.when(l == 0)
    def _():
        st_ref[...] = jnp.zeros_like(st_ref)
        ubuf_ref[0:SUBLANES, :] = jnp.zeros((SUBLANES, ubuf_ref.shape[1]), F32)

    def mix_stage():
        mod = mod_ref[pl.ds(lax.div(tile, n_tiles), 1), :]
        return _prompt_mix_stage(x_ref, mod, lb_ref, w_in_ref, w_conv_ref, g_onorm_ref,
                                 w_out_ref, st_ref, ubuf_ref, mix_ref, x1_ref,
                                 rec_ref, conv_ref, l == n_tiles - 1)

    def mlp_stage():
        mod_prev = mod_ref[pl.ds(lax.div(s - 1, n_tiles), 1), :]
        return _mlp_stage(x1_ref, mod_prev, w_up_ref, w_down_ref, g_final_ref, y_ref)

    @pl.when(s == 0)
    def _():
        _alternate(mix_stage())

    @pl.when((s > 0) & (s < n_steps))
    def _():
        _alternate(mix_stage(), mlp_stage())

    @pl.when(s == n_steps)
    def _():
        _alternate(mlp_stage())


def _prompt_layer(x, mod_p, lower_bounds, w_in, w_conv, g_onorm, w_out, w_up, w_down, g_final):
    bsz, seq, d = x.shape
    d_rec = lower_bounds.shape[1]
    dk = d_rec // H_REC
    d_conv = w_conv.shape[1]
    tl = PROMPT_TILE
    n_tiles = seq // tl
    n_steps = bsz * n_tiles
    assert seq % tl == 0 and tl % CHUNK == 0
    mats = (w_in, w_out, w_up, w_down)
    hbm = pl.BlockSpec(memory_space=pl.ANY)

    def mix_tile(s):
        t = jnp.minimum(s, n_steps - 1)
        return lax.div(t, n_tiles), lax.rem(t, n_tiles)

    def mlp_tile(s):
        t = jnp.maximum(s - 1, 0)
        return lax.div(t, n_tiles), lax.rem(t, n_tiles)

    return pl.pallas_call(
        functools.partial(_prompt_kernel, n_tiles=n_tiles, n_steps=n_steps),
        grid=(n_steps + 1,),
        in_specs=[
            pl.BlockSpec((None, tl, d), lambda s: (*mix_tile(s), 0)),
            _const_spec(mod_p.shape),
            _const_spec(lower_bounds.shape),
            hbm,
            _const_spec(w_conv.shape),
            _const_spec(g_onorm.shape),
            hbm,
            hbm,
            hbm,
            _const_spec(g_final.shape),
        ],
        out_specs=[
            pl.BlockSpec((None, tl, d), lambda s: (*mlp_tile(s), 0)),
            pl.BlockSpec((None, H_REC, dk, dk), lambda s: (mix_tile(s)[0], 0, 0, 0)),
            pl.BlockSpec((None, CONV_W - 1, d_conv), lambda s: (mix_tile(s)[0], 0, 0)),
        ] + [hbm] * len(mats),
        out_shape=[
            jax.ShapeDtypeStruct((bsz, seq, d), F32),
            jax.ShapeDtypeStruct((bsz, H_REC, dk, dk), F32),
            jax.ShapeDtypeStruct((bsz, CONV_W - 1, d_conv), F32),
        ] + [jax.ShapeDtypeStruct(w.shape, BF16) for w in mats],
        scratch_shapes=[
            pltpu.VMEM((H_REC, dk, dk), F32),
            pltpu.VMEM((SUBLANES + tl, d_conv), F32),
            pltpu.VMEM((tl, d_rec + d_conv), BF16),
            pltpu.VMEM((tl, d), F32),
        ] + [pltpu.VMEM(w.shape, BF16) for w in mats] + [
            pltpu.VMEM((WEIGHT_STAGE_SLOTS, *WEIGHT_STAGE_SHAPE), F32),
            pltpu.SemaphoreType.DMA((WEIGHT_STAGE_SLOTS,)),
            pltpu.SemaphoreType.DMA((len(mats),)),
        ],
        compiler_params=pltpu.CompilerParams(
            dimension_semantics=("arbitrary",), vmem_limit_bytes=VMEM_LIMIT_BYTES),
        name="prompt_layer",
    )(x, mod_p, lower_bounds, w_in, w_conv, g_onorm, w_out, w_up, w_down, g_final)


def _sample_kernel(x_ref, mod_ref, lb_ref, w_in_ref, w_conv_ref, g_onorm_ref, w_out_ref,
                   w_up_ref, w_down_ref, g_final_ref, rec_in_ref, conv_in_ref,
                   y_ref, rec_ref, conv_ref,
                   modx_ref, cbx_ref, ubuf_ref, mix_ref):
    tb, seq, d = x_ref.shape
    rows = tb * seq
    d_rec = lb_ref.shape[1]
    dk = d_rec // H_REC
    d_conv = conv_in_ref.shape[2]

    for b in range(tb):
        r = slice(b * seq, (b + 1) * seq)
        modx_ref[r, :] = jnp.broadcast_to(mod_ref[b:b + 1, :], (seq, mod_ref.shape[1]))
        for j in range(CONV_W - 1):
            cbx_ref[j, r, :] = jnp.broadcast_to(conv_in_ref[b, j:j + 1, :], (seq, d_conv))

    sh1, sc1, g1, sh2, sc2, g2 = (modx_ref[:, i * d:(i + 1) * d] for i in range(N_MOD))

    x = x_ref[...].reshape(rows, d)
    proj = _bdot(_modulate(x, sh1, sc1), w_in_ref[...])
    q, fz, iv, g, gb, gc, hv = _split_proj(proj)

    lb = _layer_lower_bound(lb_ref, 0)
    row = lax.broadcasted_iota(jnp.int32, (rows, rows), 0)
    col = lax.broadcasted_iota(jnp.int32, (rows, rows), 1)
    causal = ((row // seq) == (col // seq)) & (row >= col)
    f, k = _gates(fz, lb)
    prod = _tile_cumprod(f)
    decay = _tile_last(prod)
    q_dec = (q * prod).astype(BF16)
    k_inv = k / prod
    k_dec = k_inv.astype(BF16)
    k_end = k_inv * decay
    vb = iv.astype(BF16)

    dec3 = _decay_terms(decay, _group_pos((rows, d_rec), seq))
    zeros_blk = jnp.zeros((seq, dk), F32)
    rhs_bottom = _decay_selector(seq, dk)

    heads = [slice(hd * dk, (hd + 1) * dk) for hd in range(H_REC)]

    def update_states(elements):
        for b in elements:
            r = slice(b * seq, (b + 1) * seq)
            for hd, sl in enumerate(heads):
                lhs = jnp.concatenate([k_end[r, sl], dec3[r, sl]], axis=0)
                rhs = jnp.concatenate(
                    [jnp.concatenate([iv[r, sl], zeros_blk], axis=1), rhs_bottom], axis=0)
                both = _dot_tn(lhs.astype(BF16), rhs.astype(BF16))
                rec_ref[b, hd] = both[:, dk:] * rec_in_ref[b, hd] + both[:, :dk]

    scores = [jnp.where(causal, _dot_nt(q_dec[:, sl], k_dec[:, sl]), 0.0).astype(BF16)
              for sl in heads]
    o_inter = jnp.concatenate(
        [jnp.concatenate([_bdot(q_dec[b * seq:(b + 1) * seq, sl], rec_in_ref[b, hd].astype(BF16))
                          for hd, sl in enumerate(heads)], axis=-1)
         for b in range(tb)], axis=0)
    o_intra = jnp.concatenate([_bdot(scores[hd], vb[:, sl]) for hd, sl in enumerate(heads)],
                              axis=-1)
    o_rec = _head_out(o_intra + o_inter, g, g_onorm_ref[...])
    mix_ref[:, 0:d_rec] = o_rec.astype(BF16)

    u = gc * hv
    ubuf_ref[SUBLANES:SUBLANES + rows, :] = u
    tok = _group_pos((rows, d_conv), seq)
    u_m1 = jnp.where(tok >= 1, ubuf_ref[SUBLANES - 1:SUBLANES - 1 + rows, :], cbx_ref[1])
    u_m2 = jnp.where(tok >= 2, ubuf_ref[SUBLANES - 2:SUBLANES - 2 + rows, :],
                     jnp.where(tok == 1, cbx_ref[1], cbx_ref[0]))
    w_conv = w_conv_ref[...]
    y_conv = w_conv[0:1] * u_m2 + w_conv[1:2] * u_m1 + w_conv[2:3] * u
    mix_ref[:, d_rec:d_rec + d_conv] = (gb * y_conv).astype(BF16)
    for b in range(tb):
        conv_ref[b] = u[(b + 1) * seq - (CONV_W - 1):(b + 1) * seq]

    x1 = x + g1 * _bdot(mix_ref[...], w_out_ref[...])
    update_states(range(tb // 2))
    h2 = _modulate(x1, sh2, sc2)
    mlp = _mlp_down(_mlp_up(h2, w_up_ref, 0), w_up_ref, w_down_ref, 0)
    for j in range(1, MLP_BLOCKS):
        mlp += _mlp_down(_mlp_up(h2, w_up_ref, j), w_up_ref, w_down_ref, j)
    update_states(range(tb // 2, tb))
    y_ref[...] = _final_norm(x1, g2, mlp, g_final_ref).reshape(tb, seq, d)


def _sample_layer(x, mod, lower_bounds, w_in, w_conv, g_onorm, w_out, w_up, w_down,
                  g_final, rec_in, conv_in):
    bsz, seq, d = x.shape
    d_rec = lower_bounds.shape[1]
    dk = d_rec // H_REC
    d_conv = w_conv.shape[1]
    tb = SAMPLE_TILE_B
    rows = tb * seq
    assert bsz % tb == 0 and seq == SUBLANES and seq >= CONV_W
    return pl.pallas_call(
        _sample_kernel,
        grid=(bsz // tb,),
        in_specs=[
            pl.BlockSpec((tb, seq, d), lambda i: (i, 0, 0)),
            pl.BlockSpec((tb, N_MOD * d), lambda i: (i, 0)),
            _const_spec(lower_bounds.shape),
            _const_spec(w_in.shape),
            _const_spec(w_conv.shape),
            _const_spec(g_onorm.shape),
            _const_spec(w_out.shape),
            _const_spec(w_up.shape),
            _const_spec(w_down.shape),
            _const_spec(g_final.shape),
            pl.BlockSpec((tb, H_REC, dk, dk), lambda i: (i, 0, 0, 0)),
            pl.BlockSpec((tb, CONV_W - 1, d_conv), lambda i: (i, 0, 0)),
        ],
        out_specs=[
            pl.BlockSpec((tb, seq, d), lambda i: (i, 0, 0)),
            pl.BlockSpec((tb, H_REC, dk, dk), lambda i: (i, 0, 0, 0)),
            pl.BlockSpec((tb, CONV_W - 1, d_conv), lambda i: (i, 0, 0)),
        ],
        out_shape=[
            jax.ShapeDtypeStruct((bsz, seq, d), F32),
            jax.ShapeDtypeStruct((bsz, H_REC, dk, dk), F32),
            jax.ShapeDtypeStruct((bsz, CONV_W - 1, d_conv), F32),
        ],
        scratch_shapes=[
            pltpu.VMEM((rows, N_MOD * d), F32),
            pltpu.VMEM((CONV_W - 1, rows, d_conv), F32),
            pltpu.VMEM((SUBLANES + rows, d_conv), F32),
            pltpu.VMEM((rows, d_rec + d_conv), BF16),
        ],
        compiler_params=pltpu.CompilerParams(
            dimension_semantics=("arbitrary",), vmem_limit_bytes=VMEM_LIMIT_BYTES),
        name="sample_layer",
    )(x, mod, lower_bounds, w_in, w_conv, g_onorm, w_out, w_up, w_down, g_final, rec_in, conv_in)


def kernel(x_prompt, x_sample, state_rec, state_conv, c_prompt, c_sample, lower_bounds, w_ada,
           b_ada, w_in, w_conv, g_onorm, w_out, w_up, w_down, g_final):
    depth = w_in.shape[0]
    assert depth == 1, "single-layer trunk"

    mod_s, mod_p = _modulation(c_sample, c_prompt, w_ada[0], b_ada)

    y_p, rec_p, conv_p, w_in_b, w_out_b, w_up_b, w_down_b = _prompt_layer(
        x_prompt, mod_p, lower_bounds, w_in[0], w_conv[0], g_onorm, w_out[0], w_up[0], w_down[0],
        g_final)
    y_s, rec_s, conv_s = _sample_layer(
        x_sample, mod_s, lower_bounds, w_in_b, w_conv[0], g_onorm, w_out_b, w_up_b, w_down_b,
        g_final, state_rec[0], state_conv[0])
    return y_p, y_s, rec_p[None], conv_p[None], rec_s[None], conv_s[None]
```

```python
import functools

import jax
import jax.numpy as jnp
from jax import lax
from jax.experimental import pallas as pl
from jax.experimental.pallas import tpu as pltpu

F32 = jnp.float32
BF16 = jnp.bfloat16

EPS = 1e-6
H_REC = 4
CONV_W = 3
N_MOD = 6
N_PROJ_GROUPS = 7
CHUNK = 64

SUBLANES = 8
VMEM_LIMIT_BYTES = 56 * 1024 * 1024

PROMPT_TILE = 512
SAMPLE_TILE_B = 16
MOD_TILE_K = 256
MLP_BLOCKS = 4
WEIGHT_STAGE_SHAPE = (512, 512)
WEIGHT_STAGE_SLOTS = 6


def _const_spec(shape):
    zeros = (0,) * len(shape)
    return pl.BlockSpec(shape, lambda *_: zeros, pipeline_mode=pl.Buffered(1))


def _rms(x):
    return x * lax.rsqrt(jnp.mean(x * x, axis=-1, keepdims=True) + EPS)


def _silu(x):
    return x * jax.nn.sigmoid(x)


def _bdot(a, b):
    return jnp.dot(a, b, preferred_element_type=F32)


def _dot_nt(a, b):
    return lax.dot_general(a, b, (((1,), (1,)), ((), ())), preferred_element_type=F32)


def _dot_tn(a, b):
    return lax.dot_general(a, b, (((0,), (0,)), ((), ())), preferred_element_type=F32)


def _group_pos(shape, group):
    return lax.broadcasted_iota(jnp.int32, shape, 0) % group


def _tile_cumprod(x):
    rows, n = x.shape
    x = x.reshape(rows // SUBLANES, SUBLANES, n)
    pos = lax.broadcasted_iota(jnp.int32, x.shape, 1)
    step = 1
    while step < SUBLANES:
        x = x * jnp.where(pos >= step, pltpu.roll(x, step, axis=1), 1.0)
        step *= 2
    return x.reshape(rows, n)


def _tile_last(x):
    rows, n = x.shape
    x = x.reshape(rows // SUBLANES, SUBLANES, n)
    return jnp.broadcast_to(x[:, SUBLANES - 1:, :], x.shape).reshape(rows, n)


def _chain_tiles(y):
    tiles, carry = [], None
    for r in range(0, y.shape[0], SUBLANES):
        t = y[r:r + SUBLANES]
        if carry is not None:
            t = t * carry
        tiles.append(t)
        carry = t[SUBLANES - 1:, :]
    return jnp.concatenate(tiles, axis=0)


def _layer_lower_bound(lb_ref, layer):
    lb = lb_ref[...]
    e = jnp.exp(lb - jnp.max(lb, axis=0, keepdims=True))
    sm = e / jnp.sum(e, axis=0, keepdims=True)
    return jnp.sum(sm[: layer + 1], axis=0, keepdims=True)


def _gates(fz, lb):
    sig = jax.nn.sigmoid(fz)
    f = lb + (1.0 - lb) * sig
    k = (1.0 - lb) * (1.0 - sig)
    return f, k


def _head_out(o, g, g_onorm):
    dv = o.shape[-1] // H_REC
    pieces = []
    for h in range(H_REC):
        sl = slice(h * dv, (h + 1) * dv)
        pieces.append(_rms(o[:, sl]))
    return (jnp.concatenate(pieces, axis=-1) * g_onorm) * _silu(g)


def _modulate(x, shift, scale):
    return (_rms(x) * (1.0 + scale) + shift).astype(BF16)


def _mlp_cols(w_up_ref, j):
    blk = w_up_ref.shape[1] // MLP_BLOCKS
    return slice(j * blk, (j + 1) * blk)


def _mlp_up(h2, w_up_ref, j):
    up = jnp.maximum(_bdot(h2, w_up_ref[:, _mlp_cols(w_up_ref, j)]), 0.0)
    return (up * up).astype(BF16)


def _mlp_down(up, w_up_ref, w_down_ref, j):
    return _bdot(up, w_down_ref[_mlp_cols(w_up_ref, j), :])


def _final_norm(x, g2, mlp, g_final_ref):
    return _rms(x + g2 * mlp) * g_final_ref[...]


def _mod_kernel(c_s_ref, c_p_ref, w_ref, b_ref, o_s_ref, o_p_ref):
    k = pl.program_id(0)
    n_s = c_s_ref.shape[0]
    kb = w_ref.shape[0]
    cols = pl.ds(pl.multiple_of(k * kb, kb), kb)
    c = jnp.concatenate([c_s_ref[:, cols], c_p_ref[:, cols]], axis=0)
    part = _bdot(_silu(c).astype(BF16), w_ref[...].astype(BF16))

    @pl.when(k == 0)
    def _():
        o_s_ref[...] = part[:n_s] + b_ref[...]
        o_p_ref[...] = part[n_s:] + b_ref[...]

    @pl.when(k > 0)
    def _():
        o_s_ref[...] += part[:n_s]
        o_p_ref[...] += part[n_s:]


def _modulation(c_s, c_p, w_ada, b_ada):
    (n_s, d), n_p = c_s.shape, c_p.shape[0]
    n = w_ada.shape[1]
    assert n_s % SUBLANES == 0 and n_p % SUBLANES == 0 and d % MOD_TILE_K == 0
    return pl.pallas_call(
        _mod_kernel,
        grid=(d // MOD_TILE_K,),
        in_specs=[
            pl.BlockSpec((n_s, d), lambda k: (0, 0)),
            pl.BlockSpec((n_p, d), lambda k: (0, 0)),
            pl.BlockSpec((MOD_TILE_K, n), lambda k: (k, 0)),
            pl.BlockSpec((1, n), lambda k: (0, 0)),
        ],
        out_specs=[
            pl.BlockSpec((n_s, n), lambda k: (0, 0)),
            pl.BlockSpec((n_p, n), lambda k: (0, 0)),
        ],
        out_shape=[
            jax.ShapeDtypeStruct((n_s, n), F32),
            jax.ShapeDtypeStruct((n_p, n), F32),
        ],
        compiler_params=pltpu.CompilerParams(
            dimension_semantics=("arbitrary",), vmem_limit_bytes=VMEM_LIMIT_BYTES),
        name="adaln_modulation",
    )(c_s, c_p, w_ada, b_ada)


def _split_proj(proj):
    width = proj.shape[1] // N_PROJ_GROUPS
    return tuple(proj[:, p * width:(p + 1) * width] for p in range(N_PROJ_GROUPS))


def _weight_blocks(w_hbm, w_vmem):
    rows, cols = w_hbm.shape
    br, bc = WEIGHT_STAGE_SHAPE
    assert rows % br == 0 and cols % bc == 0
    return [(w_hbm.at[r:r + br, c:c + bc], w_vmem.at[r:r + br, c:c + bc])
            for r in range(0, rows, br) for c in range(0, cols, bc)]


def _stage_copy(src, stage_ref, sem_ref, i):
    slot = i % WEIGHT_STAGE_SLOTS
    return pltpu.make_async_copy(src, stage_ref.at[slot], sem_ref.at[slot])


def _load_weights_as_bf16(w_hbm_refs, w_vmem_refs, stage_ref, sem_ref):
    blocks = [blk for w_hbm, w_vmem in zip(w_hbm_refs, w_vmem_refs)
              for blk in _weight_blocks(w_hbm, w_vmem)]
    ahead = WEIGHT_STAGE_SLOTS - 1
    for i in range(min(ahead, len(blocks))):
        _stage_copy(blocks[i][0], stage_ref, sem_ref, i).start()
    for i, (src, dst) in enumerate(blocks):
        if i + ahead < len(blocks):
            _stage_copy(blocks[i + ahead][0], stage_ref, sem_ref, i + ahead).start()
        _stage_copy(src, stage_ref, sem_ref, i).wait()
        dst[...] = stage_ref[i % WEIGHT_STAGE_SLOTS].astype(BF16)


def _alternate(*stages):
    stages = list(stages)
    while stages:
        for stage in list(stages):
            try:
                next(stage)
            except StopIteration:
                stages.remove(stage)


def _mlp_stage(x1_ref, mod, w_up_ref, w_down_ref, g_final_ref, y_ref):
    d = x1_ref.shape[1]
    sh2, sc2, g2 = (mod[:, i * d:(i + 1) * d] for i in range(3, N_MOD))
    h2 = _modulate(x1_ref[...], sh2, sc2)
    ups = [_mlp_up(h2, w_up_ref, 0)]
    yield
    mlp = None
    for j in range(MLP_BLOCKS):
        if j + 1 < MLP_BLOCKS:
            ups.append(_mlp_up(h2, w_up_ref, j + 1))
        part = _mlp_down(ups[j], w_up_ref, w_down_ref, j)
        mlp = part if mlp is None else mlp + part
        if j + 2 < MLP_BLOCKS:
            yield
    y_ref[...] = _final_norm(x1_ref[...], g2, mlp, g_final_ref)
    yield


def _prompt_mix_stage(x_ref, mod, lb_ref, w_in_ref, w_conv_ref, g_onorm_ref, w_out_ref,
                      st_ref, ubuf_ref, mix_ref, x1_ref, rec_ref, conv_ref, is_last):
    tl, d = x_ref.shape
    d_rec = lb_ref.shape[1]
    dk = d_rec // H_REC
    n_chunks = tl // CHUNK
    heads = [slice(hd * dk, (hd + 1) * dk) for hd in range(H_REC)]
    sh1, sc1, g1 = (mod[:, i * d:(i + 1) * d] for i in range(3))

    x = x_ref[...]
    proj = _bdot(_modulate(x, sh1, sc1), w_in_ref[...])
    q, fz, iv, g, gb, gc, hv = _split_proj(proj)
    d_conv = gb.shape[1]
    yield

    lb = _layer_lower_bound(lb_ref, 0)
    row = lax.broadcasted_iota(jnp.int32, (CHUNK, CHUNK), 0)
    col = lax.broadcasted_iota(jnp.int32, (CHUNK, CHUNK), 1)
    causal = row >= col
    f_all, k_all = _gates(fz, lb)
    prod_tiles = _tile_cumprod(f_all)
    q_dec, decay, scores, upd = [], [], [], []
    for c in range(n_chunks):
        rows = slice(c * CHUNK, (c + 1) * CHUNK)
        prod = _chain_tiles(prod_tiles[rows])
        dec = prod[CHUNK - 1:CHUNK, :]
        q_dec.append((q[rows] * prod).astype(BF16))
        k_inv = k_all[rows] / prod
        k_dec = k_inv.astype(BF16)
        k_end = (k_inv * dec).astype(BF16)
        decay.append([jnp.broadcast_to(dec[:, sl], (dk, dk)).T for sl in heads])
        v = iv[rows].astype(BF16)
        scores.append([jnp.where(causal, _dot_nt(q_dec[c][:, sl], k_dec[:, sl]), 0.0).astype(BF16)
                       for sl in heads])
        upd.append([_dot_tn(k_end[:, sl], v[:, sl]) for sl in heads])
    yield

    st = [st_ref[hd] for hd in range(H_REC)]
    o_chunks = []
    for c in range(n_chunks):
        v = iv[c * CHUNK:(c + 1) * CHUNK].astype(BF16)
        o_heads = []
        for hd, sl in enumerate(heads):
            o_heads.append(_bdot(scores[c][hd], v[:, sl])
                           + _bdot(q_dec[c][:, sl], st[hd].astype(BF16)))
            st[hd] = st[hd] * decay[c][hd] + upd[c][hd]
        o_chunks.append(jnp.concatenate(o_heads, axis=-1))
    for hd in range(H_REC):
        st_ref[hd] = st[hd]
    yield

    o_rec = _head_out(jnp.concatenate(o_chunks, axis=0), g, g_onorm_ref[...])
    mix_ref[:, 0:d_rec] = o_rec.astype(BF16)

    u = gc * hv
    ubuf_ref[SUBLANES:SUBLANES + tl, :] = u
    w_conv = w_conv_ref[...]
    y_conv = (w_conv[0:1] * ubuf_ref[SUBLANES - 2:SUBLANES - 2 + tl, :]
              + w_conv[1:2] * ubuf_ref[SUBLANES - 1:SUBLANES - 1 + tl, :]
              + w_conv[2:3] * u)
    mix_ref[:, d_rec:d_rec + d_conv] = (gb * y_conv).astype(BF16)
    tail = u[tl - (CONV_W - 1):tl]
    ubuf_ref[SUBLANES - (CONV_W - 1):SUBLANES, :] = tail
    yield

    x1 = x + g1 * _bdot(mix_ref[...], w_out_ref[...])
    x1_ref[...] = x1

    @pl.when(is_last)
    def _():
        for hd in range(H_REC):
            rec_ref[hd] = st_ref[hd]
        conv_ref[...] = tail


def _prompt_kernel(x_ref, mod_ref, lb_ref, w_in_hbm, w_conv_ref, g_onorm_ref, w_out_hbm,
                   w_up_hbm, w_down_hbm, g_final_ref,
                   y_ref, rec_ref, conv_ref, w_in_out, w_out_out, w_up_out, w_down_out,
                   st_ref, ubuf_ref, mix_ref, x1_ref,
                   w_in_ref, w_out_ref, w_up_ref, w_down_ref, stage_ref, stage_sem, out_sem,
                   *, n_tiles, n_steps):
    s = pl.program_id(0)
    tile = jnp.minimum(s, n_steps - 1)
    l = lax.rem(tile, n_tiles)
    d = x_ref.shape[1]

    w_vmem = (w_in_ref, w_out_ref, w_up_ref, w_down_ref)
    w_outs = (w_in_out, w_out_out, w_up_out, w_down_out)

    def bf16_export(i):
        return pltpu.make_async_copy(w_vmem[i], w_outs[i], out_sem.at[i])

    @pl.when(s == 0)
    def _():
        _load_weights_as_bf16((w_in_hbm, w_out_hbm, w_up_hbm, w_down_hbm), w_vmem,
                              stage_ref, stage_sem)
        for i in range(len(w_vmem)):
            bf16_export(i).start()

    @pl.when(s == n_steps)
    def _():
        for i in range(len(w_vmem)):
            bf16_export(i).wait()

    @pl.when(l == 0)
    def _():
        st_ref[...] = jnp.zeros_like(st_ref)
        ubuf_ref[0:SUBLANES, :] = jnp.zeros((SUBLANES, ubuf_ref.shape[1]), F32)

    def mix_stage():
        mod = mod_ref[pl.ds(lax.div(tile, n_tiles), 1), :]
        return _prompt_mix_stage(x_ref, mod, lb_ref, w_in_ref, w_conv_ref, g_onorm_ref,
                                 w_out_ref, st_ref, ubuf_ref, mix_ref, x1_ref,
                                 rec_ref, conv_ref, l == n_tiles - 1)

    def mlp_stage():
        mod_prev = mod_ref[pl.ds(lax.div(s - 1, n_tiles), 1), :]
        return _mlp_stage(x1_ref, mod_prev, w_up_ref, w_down_ref, g_final_ref, y_ref)

    @pl.when(s == 0)
    def _():
        _alternate(mix_stage())

    @pl.when((s > 0) & (s < n_steps))
    def _():
        _alternate(mix_stage(), mlp_stage())

    @pl.when(s == n_steps)
    def _():
        _alternate(mlp_stage())


def _prompt_layer(x, mod_p, lower_bounds, w_in, w_conv, g_onorm, w_out, w_up, w_down, g_final):
    bsz, seq, d = x.shape
    d_rec = lower_bounds.shape[1]
    dk = d_rec // H_REC
    d_conv = w_conv.shape[1]
    tl = PROMPT_TILE
    n_tiles = seq // tl
    n_steps = bsz * n_tiles
    assert seq % tl == 0 and tl % CHUNK == 0
    mats = (w_in, w_out, w_up, w_down)
    hbm = pl.BlockSpec(memory_space=pl.ANY)

    def mix_tile(s):
        t = jnp.minimum(s, n_steps - 1)
        return lax.div(t, n_tiles), lax.rem(t, n_tiles)

    def mlp_tile(s):
        t = jnp.maximum(s - 1, 0)
        return lax.div(t, n_tiles), lax.rem(t, n_tiles)

    return pl.pallas_call(
        functools.partial(_prompt_kernel, n_tiles=n_tiles, n_steps=n_steps),
        grid=(n_steps + 1,),
        in_specs=[
            pl.BlockSpec((None, tl, d), lambda s: (*mix_tile(s), 0)),
            _const_spec(mod_p.shape),
            _const_spec(lower_bounds.shape),
            hbm,
            _const_spec(w_conv.shape),
            _const_spec(g_onorm.shape),
            hbm,
            hbm,
            hbm,
            _const_spec(g_final.shape),
        ],
        out_specs=[
            pl.BlockSpec((None, tl, d), lambda s: (*mlp_tile(s), 0)),
            pl.BlockSpec((None, H_REC, dk, dk), lambda s: (mix_tile(s)[0], 0, 0, 0)),
            pl.BlockSpec((None, CONV_W - 1, d_conv), lambda s: (mix_tile(s)[0], 0, 0)),
        ] + [hbm] * len(mats),
        out_shape=[
            jax.ShapeDtypeStruct((bsz, seq, d), F32),
            jax.ShapeDtypeStruct((bsz, H_REC, dk, dk), F32),
            jax.ShapeDtypeStruct((bsz, CONV_W - 1, d_conv), F32),
        ] + [jax.ShapeDtypeStruct(w.shape, BF16) for w in mats],
        scratch_shapes=[
            pltpu.VMEM((H_REC, dk, dk), F32),
            pltpu.VMEM((SUBLANES + tl, d_conv), F32),
            pltpu.VMEM((tl, d_rec + d_conv), BF16),
            pltpu.VMEM((tl, d), F32),
        ] + [pltpu.VMEM(w.shape, BF16) for w in mats] + [
            pltpu.VMEM((WEIGHT_STAGE_SLOTS, *WEIGHT_STAGE_SHAPE), F32),
            pltpu.SemaphoreType.DMA((WEIGHT_STAGE_SLOTS,)),
            pltpu.SemaphoreType.DMA((len(mats),)),
        ],
        compiler_params=pltpu.CompilerParams(
            dimension_semantics=("arbitrary",), vmem_limit_bytes=VMEM_LIMIT_BYTES),
        name="prompt_layer",
    )(x, mod_p, lower_bounds, w_in, w_conv, g_onorm, w_out, w_up, w_down, g_final)


def _sample_kernel(x_ref, mod_ref, lb_ref, w_in_ref, w_conv_ref, g_onorm_ref, w_out_ref,
                   w_up_ref, w_down_ref, g_final_ref, rec_in_ref, conv_in_ref,
                   y_ref, rec_ref, conv_ref,
                   modx_ref, cbx_ref, ubuf_ref, mix_ref):
    tb, seq, d = x_ref.shape
    rows = tb * seq
    d_rec = lb_ref.shape[1]
    dk = d_rec // H_REC
    d_conv = conv_in_ref.shape[2]

    for b in range(tb):
        r = slice(b * seq, (b + 1) * seq)
        modx_ref[r, :] = jnp.broadcast_to(mod_ref[b:b + 1, :], (seq, mod_ref.shape[1]))
        for j in range(CONV_W - 1):
            cbx_ref[j, r, :] = jnp.broadcast_to(conv_in_ref[b, j:j + 1, :], (seq, d_conv))

    sh1, sc1, g1, sh2, sc2, g2 = (modx_ref[:, i * d:(i + 1) * d] for i in range(N_MOD))

    x = x_ref[...].reshape(rows, d)
    proj = _bdot(_modulate(x, sh1, sc1), w_in_ref[...])
    q, fz, iv, g, gb, gc, hv = _split_proj(proj)

    lb = _layer_lower_bound(lb_ref, 0)
    row = lax.broadcasted_iota(jnp.int32, (rows, rows), 0)
    col = lax.broadcasted_iota(jnp.int32, (rows, rows), 1)
    causal = ((row // seq) == (col // seq)) & (row >= col)
    f, k = _gates(fz, lb)
    prod = _tile_cumprod(f)
    decay = _tile_last(prod)
    q_dec = (q * prod).astype(BF16)
    k_inv = k / prod
    k_dec = k_inv.astype(BF16)
    k_end = k_inv * decay
    vb = iv.astype(BF16)

    zeros_blk = jnp.zeros((seq, dk), F32)
    heads = [slice(hd * dk, (hd + 1) * dk) for hd in range(H_REC)]

    def update_states(elements):
        for b in elements:
            r = slice(b * seq, (b + 1) * seq)
            for ha in range(0, H_REC, 2):
                sa, sb = heads[ha], heads[ha + 1]
                lhs = jnp.concatenate([k_end[r, sa], k_end[r, sb]], axis=0)
                rhs = jnp.concatenate([jnp.concatenate([iv[r, sa], zeros_blk], axis=1),
                                       jnp.concatenate([zeros_blk, iv[r, sb]], axis=1)], axis=0)
                both = _dot_tn(lhs.astype(BF16), rhs.astype(BF16))
                for hd, sl, upd in ((ha, sa, both[:, :dk]), (ha + 1, sb, both[:, dk:])):
                    dcol = jnp.broadcast_to(decay[b * seq:b * seq + 1, sl], (dk, dk)).T
                    rec_ref[b, hd] = dcol * rec_in_ref[b, hd] + upd

    scores = [jnp.where(causal, _dot_nt(q_dec[:, sl], k_dec[:, sl]), 0.0).astype(BF16)
              for sl in heads]
    o_inter = jnp.concatenate(
        [jnp.concatenate([_bdot(q_dec[b * seq:(b + 1) * seq, sl], rec_in_ref[b, hd].astype(BF16))
                          for hd, sl in enumerate(heads)], axis=-1)
         for b in range(tb)], axis=0)
    o_intra = jnp.concatenate([_bdot(scores[hd], vb[:, sl]) for hd, sl in enumerate(heads)],
                              axis=-1)
    o_rec = _head_out(o_intra + o_inter, g, g_onorm_ref[...])
    mix_ref[:, 0:d_rec] = o_rec.astype(BF16)

    u = gc * hv
    ubuf_ref[SUBLANES:SUBLANES + rows, :] = u
    tok = _group_pos((rows, d_conv), seq)
    u_m1 = jnp.where(tok >= 1, ubuf_ref[SUBLANES - 1:SUBLANES - 1 + rows, :], cbx_ref[1])
    u_m2 = jnp.where(tok >= 2, ubuf_ref[SUBLANES - 2:SUBLANES - 2 + rows, :],
                     jnp.where(tok == 1, cbx_ref[1], cbx_ref[0]))
    w_conv = w_conv_ref[...]
    y_conv = w_conv[0:1] * u_m2 + w_conv[1:2] * u_m1 + w_conv[2:3] * u
    mix_ref[:, d_rec:d_rec + d_conv] = (gb * y_conv).astype(BF16)
    for b in range(tb):
        conv_ref[b] = u[(b + 1) * seq - (CONV_W - 1):(b + 1) * seq]

    x1 = x + g1 * _bdot(mix_ref[...], w_out_ref[...])
    update_states(range(tb // 2))
    h2 = _modulate(x1, sh2, sc2)
    mlp = _mlp_down(_mlp_up(h2, w_up_ref, 0), w_up_ref, w_down_ref, 0)
    for j in range(1, MLP_BLOCKS):
        mlp += _mlp_down(_mlp_up(h2, w_up_ref, j), w_up_ref, w_down_ref, j)
    update_states(range(tb // 2, tb))
    y_ref[...] = _final_norm(x1, g2, mlp, g_final_ref).reshape(tb, seq, d)


def _sample_layer(x, mod, lower_bounds, w_in, w_conv, g_onorm, w_out, w_up, w_down,
                  g_final, rec_in, conv_in):
    bsz, seq, d = x.shape
    d_rec = lower_bounds.shape[1]
    dk = d_rec // H_REC
    d_conv = w_conv.shape[1]
    tb = SAMPLE_TILE_B
    rows = tb * seq
    assert bsz % tb == 0 and seq == SUBLANES and seq >= CONV_W
    return pl.pallas_call(
        _sample_kernel,
        grid=(bsz // tb,),
        in_specs=[
            pl.BlockSpec((tb, seq, d), lambda i: (i, 0, 0)),
            pl.BlockSpec((tb, N_MOD * d), lambda i: (i, 0)),
            _const_spec(lower_bounds.shape),
            _const_spec(w_in.shape),
            _const_spec(w_conv.shape),
            _const_spec(g_onorm.shape),
            _const_spec(w_out.shape),
            _const_spec(w_up.shape),
            _const_spec(w_down.shape),
            _const_spec(g_final.shape),
            pl.BlockSpec((tb, H_REC, dk, dk), lambda i: (i, 0, 0, 0)),
            pl.BlockSpec((tb, CONV_W - 1, d_conv), lambda i: (i, 0, 0)),
        ],
        out_specs=[
            pl.BlockSpec((tb, seq, d), lambda i: (i, 0, 0)),
            pl.BlockSpec((tb, H_REC, dk, dk), lambda i: (i, 0, 0, 0)),
            pl.BlockSpec((tb, CONV_W - 1, d_conv), lambda i: (i, 0, 0)),
        ],
        out_shape=[
            jax.ShapeDtypeStruct((bsz, seq, d), F32),
            jax.ShapeDtypeStruct((bsz, H_REC, dk, dk), F32),
            jax.ShapeDtypeStruct((bsz, CONV_W - 1, d_conv), F32),
        ],
        scratch_shapes=[
            pltpu.VMEM((rows, N_MOD * d), F32),
            pltpu.VMEM((CONV_W - 1, rows, d_conv), F32),
            pltpu.VMEM((SUBLANES + rows, d_conv), F32),
            pltpu.VMEM((rows, d_rec + d_conv), BF16),
        ],
        compiler_params=pltpu.CompilerParams(
            dimension_semantics=("arbitrary",), vmem_limit_bytes=VMEM_LIMIT_BYTES),
        name="sample_layer",
    )(x, mod, lower_bounds, w_in, w_conv, g_onorm, w_out, w_up, w_down, g_final, rec_in, conv_in)


def kernel(x_prompt, x_sample, state_rec, state_conv, c_prompt, c_sample, lower_bounds, w_ada,
           b_ada, w_in, w_conv, g_onorm, w_out, w_up, w_down, g_final):
    depth = w_in.shape[0]
    assert depth == 1, "single-layer trunk"

    mod_s, mod_p = _modulation(c_sample, c_prompt, w_ada[0], b_ada)

    y_p, rec_p, conv_p, w_in_b, w_out_b, w_up_b, w_down_b = _prompt_layer(
        x_prompt, mod_p, lower_bounds, w_in[0], w_conv[0], g_onorm, w_out[0], w_up[0], w_down[0],
        g_final)
    y_s, rec_s, conv_s = _sample_layer(
        x_sample, mod_s, lower_bounds, w_in_b, w_conv[0], g_onorm, w_out_b, w_up_b, w_down_b,
        g_final, state_rec[0], state_conv[0])
    return y_p, y_s, rec_p[None], conv_p[None], rec_s[None], conv_s[None]
```

```python
import functools

import jax
import jax.numpy as jnp
from jax import lax
from jax.experimental import pallas as pl
from jax.experimental.pallas import tpu as pltpu

F32 = jnp.float32
BF16 = jnp.bfloat16

EPS = 1e-6
H_REC = 4
CONV_W = 3
N_MOD = 6
N_PROJ_GROUPS = 7
CHUNK = 64

SUBLANES = 8
VMEM_LIMIT_BYTES = 56 * 1024 * 1024

PROMPT_TILE = 512
SAMPLE_TILE_B = 16
MOD_TILE_K = 128
MLP_BLOCKS = 4
WEIGHT_STAGE_SHAPE = (512, 512)
WEIGHT_STAGE_SLOTS = 6


def _const_spec(shape):
    zeros = (0,) * len(shape)
    return pl.BlockSpec(shape, lambda *_: zeros, pipeline_mode=pl.Buffered(1))


def _rms(x):
    return x * lax.rsqrt(jnp.mean(x * x, axis=-1, keepdims=True) + EPS)


def _silu(x):
    return x * jax.nn.sigmoid(x)


def _bdot(a, b):
    return jnp.dot(a, b, preferred_element_type=F32)


def _dot_nt(a, b):
    return lax.dot_general(a, b, (((1,), (1,)), ((), ())), preferred_element_type=F32)


def _dot_tn(a, b):
    return lax.dot_general(a, b, (((0,), (0,)), ((), ())), preferred_element_type=F32)


def _group_pos(shape, group):
    return lax.broadcasted_iota(jnp.int32, shape, 0) % group


def _tile_cumprod(x):
    rows, n = x.shape
    x = x.reshape(rows // SUBLANES, SUBLANES, n)
    pos = lax.broadcasted_iota(jnp.int32, x.shape, 1)
    step = 1
    while step < SUBLANES:
        x = x * jnp.where(pos >= step, pltpu.roll(x, step, axis=1), 1.0)
        step *= 2
    return x.reshape(rows, n)


def _tile_last(x):
    rows, n = x.shape
    x = x.reshape(rows // SUBLANES, SUBLANES, n)
    return jnp.broadcast_to(x[:, SUBLANES - 1:, :], x.shape).reshape(rows, n)


def _chain_tiles(y):
    tiles, carry = [], None
    for r in range(0, y.shape[0], SUBLANES):
        t = y[r:r + SUBLANES]
        if carry is not None:
            t = t * carry
        tiles.append(t)
        carry = t[SUBLANES - 1:, :]
    return jnp.concatenate(tiles, axis=0)


def _layer_lower_bound(lb_ref, layer):
    lb = lb_ref[...]
    e = jnp.exp(lb - jnp.max(lb, axis=0, keepdims=True))
    sm = e / jnp.sum(e, axis=0, keepdims=True)
    return jnp.sum(sm[: layer + 1], axis=0, keepdims=True)


def _gates(fz, lb):
    sig = jax.nn.sigmoid(fz)
    f = lb + (1.0 - lb) * sig
    k = (1.0 - lb) * (1.0 - sig)
    return f, k


def _head_out(o, g, g_onorm):
    dv = o.shape[-1] // H_REC
    pieces = []
    for h in range(H_REC):
        sl = slice(h * dv, (h + 1) * dv)
        pieces.append(_rms(o[:, sl]))
    return (jnp.concatenate(pieces, axis=-1) * g_onorm) * _silu(g)


def _modulate(x, shift, scale):
    return (_rms(x) * (1.0 + scale) + shift).astype(BF16)


def _mlp_cols(w_up_ref, j):
    blk = w_up_ref.shape[1] // MLP_BLOCKS
    return slice(j * blk, (j + 1) * blk)


def _mlp_up(h2, w_up_ref, j):
    up = jnp.maximum(_bdot(h2, w_up_ref[:, _mlp_cols(w_up_ref, j)]), 0.0)
    return (up * up).astype(BF16)


def _mlp_down(up, w_up_ref, w_down_ref, j):
    return _bdot(up, w_down_ref[_mlp_cols(w_up_ref, j), :])


def _final_norm(x, g2, mlp, g_final_ref):
    return _rms(x + g2 * mlp) * g_final_ref[...]


def _mod_kernel(c_s_ref, c_p_ref, w_ref, b_ref, o_s_ref, o_p_ref):
    k = pl.program_id(0)
    n_s = c_s_ref.shape[0]
    kb = w_ref.shape[0]
    cols = pl.ds(pl.multiple_of(k * kb, kb), kb)
    c = jnp.concatenate([c_s_ref[:, cols], c_p_ref[:, cols]], axis=0)
    part = _bdot(_silu(c).astype(BF16), w_ref[...].astype(BF16))

    @pl.when(k == 0)
    def _():
        o_s_ref[...] = part[:n_s] + b_ref[...]
        o_p_ref[...] = part[n_s:] + b_ref[...]

    @pl.when(k > 0)
    def _():
        o_s_ref[...] += part[:n_s]
        o_p_ref[...] += part[n_s:]


def _modulation(c_s, c_p, w_ada, b_ada):
    (n_s, d), n_p = c_s.shape, c_p.shape[0]
    n = w_ada.shape[1]
    assert n_s % SUBLANES == 0 and n_p % SUBLANES == 0 and d % MOD_TILE_K == 0
    return pl.pallas_call(
        _mod_kernel,
        grid=(d // MOD_TILE_K,),
        in_specs=[
            pl.BlockSpec((n_s, d), lambda k: (0, 0)),
            pl.BlockSpec((n_p, d), lambda k: (0, 0)),
            pl.BlockSpec((MOD_TILE_K, n), lambda k: (k, 0)),
            pl.BlockSpec((1, n), lambda k: (0, 0)),
        ],
        out_specs=[
            pl.BlockSpec((n_s, n), lambda k: (0, 0)),
            pl.BlockSpec((n_p, n), lambda k: (0, 0)),
        ],
        out_shape=[
            jax.ShapeDtypeStruct((n_s, n), F32),
            jax.ShapeDtypeStruct((n_p, n), F32),
        ],
        compiler_params=pltpu.CompilerParams(
            dimension_semantics=("arbitrary",), vmem_limit_bytes=VMEM_LIMIT_BYTES),
        name="adaln_modulation",
    )(c_s, c_p, w_ada, b_ada)


def _split_proj(proj):
    width = proj.shape[1] // N_PROJ_GROUPS
    return tuple(proj[:, p * width:(p + 1) * width] for p in range(N_PROJ_GROUPS))


def _weight_blocks(w_hbm, w_vmem):
    rows, cols = w_hbm.shape
    br, bc = WEIGHT_STAGE_SHAPE
    assert rows % br == 0 and cols % bc == 0
    return [(w_hbm.at[r:r + br, c:c + bc], w_vmem.at[r:r + br, c:c + bc])
            for r in range(0, rows, br) for c in range(0, cols, bc)]


def _stage_copy(src, stage_ref, sem_ref, i):
    slot = i % WEIGHT_STAGE_SLOTS
    return pltpu.make_async_copy(src, stage_ref.at[slot], sem_ref.at[slot])


def _load_weights_as_bf16(w_hbm_refs, w_vmem_refs, stage_ref, sem_ref):
    blocks = [blk for w_hbm, w_vmem in zip(w_hbm_refs, w_vmem_refs)
              for blk in _weight_blocks(w_hbm, w_vmem)]
    ahead = WEIGHT_STAGE_SLOTS - 1
    for i in range(min(ahead, len(blocks))):
        _stage_copy(blocks[i][0], stage_ref, sem_ref, i).start()
    for i, (src, dst) in enumerate(blocks):
        if i + ahead < len(blocks):
            _stage_copy(blocks[i + ahead][0], stage_ref, sem_ref, i + ahead).start()
        _stage_copy(src, stage_ref, sem_ref, i).wait()
        dst[...] = stage_ref[i % WEIGHT_STAGE_SLOTS].astype(BF16)


def _alternate(*stages):
    stages = list(stages)
    while stages:
        for stage in list(stages):
            try:
                next(stage)
            except StopIteration:
                stages.remove(stage)


def _mlp_stage(x1_ref, mod, w_up_ref, w_down_ref, g_final_ref, y_ref):
    d = x1_ref.shape[1]
    sh2, sc2, g2 = (mod[:, i * d:(i + 1) * d] for i in range(3, N_MOD))
    h2 = _modulate(x1_ref[...], sh2, sc2)
    ups = [_mlp_up(h2, w_up_ref, 0)]
    yield
    mlp = None
    for j in range(MLP_BLOCKS):
        if j + 1 < MLP_BLOCKS:
            ups.append(_mlp_up(h2, w_up_ref, j + 1))
        part = _mlp_down(ups[j], w_up_ref, w_down_ref, j)
        mlp = part if mlp is None else mlp + part
        if j + 2 < MLP_BLOCKS:
            yield
    y_ref[...] = _final_norm(x1_ref[...], g2, mlp, g_final_ref)
    yield


def _prompt_mix_stage(x_ref, mod, lb_ref, w_in_ref, w_conv_ref, g_onorm_ref, w_out_ref,
                      st_ref, ubuf_ref, mix_ref, x1_ref, rec_ref, conv_ref, is_last):
    tl, d = x_ref.shape
    d_rec = lb_ref.shape[1]
    dk = d_rec // H_REC
    n_chunks = tl // CHUNK
    heads = [slice(hd * dk, (hd + 1) * dk) for hd in range(H_REC)]
    sh1, sc1, g1 = (mod[:, i * d:(i + 1) * d] for i in range(3))

    x = x_ref[...]
    proj = _bdot(_modulate(x, sh1, sc1), w_in_ref[...])
    q, fz, iv, g, gb, gc, hv = _split_proj(proj)
    d_conv = gb.shape[1]
    yield

    lb = _layer_lower_bound(lb_ref, 0)
    row = lax.broadcasted_iota(jnp.int32, (CHUNK, CHUNK), 0)
    col = lax.broadcasted_iota(jnp.int32, (CHUNK, CHUNK), 1)
    causal = row >= col
    f_all, k_all = _gates(fz, lb)
    prod_tiles = _tile_cumprod(f_all)
    q_dec, decay, scores, upd = [], [], [], []
    for c in range(n_chunks):
        rows = slice(c * CHUNK, (c + 1) * CHUNK)
        prod = _chain_tiles(prod_tiles[rows])
        dec = prod[CHUNK - 1:CHUNK, :]
        q_dec.append((q[rows] * prod).astype(BF16))
        k_inv = k_all[rows] / prod
        k_dec = k_inv.astype(BF16)
        k_end = (k_inv * dec).astype(BF16)
        decay.append([jnp.broadcast_to(dec[:, sl], (dk, dk)).T for sl in heads])
        v = iv[rows].astype(BF16)
        scores.append([jnp.where(causal, _dot_nt(q_dec[c][:, sl], k_dec[:, sl]), 0.0).astype(BF16)
                       for sl in heads])
        upd.append([_dot_tn(k_end[:, sl], v[:, sl]) for sl in heads])
    yield

    st = [st_ref[hd] for hd in range(H_REC)]
    o_chunks = []
    for c in range(n_chunks):
        v = iv[c * CHUNK:(c + 1) * CHUNK].astype(BF16)
        o_heads = []
        for hd, sl in enumerate(heads):
            o_heads.append(_bdot(scores[c][hd], v[:, sl])
                           + _bdot(q_dec[c][:, sl], st[hd].astype(BF16)))
            st[hd] = st[hd] * decay[c][hd] + upd[c][hd]
        o_chunks.append(jnp.concatenate(o_heads, axis=-1))
    for hd in range(H_REC):
        st_ref[hd] = st[hd]
    yield

    o_rec = _head_out(jnp.concatenate(o_chunks, axis=0), g, g_onorm_ref[...])
    mix_ref[:, 0:d_rec] = o_rec.astype(BF16)

    u = gc * hv
    ubuf_ref[SUBLANES:SUBLANES + tl, :] = u
    w_conv = w_conv_ref[...]
    y_conv = (w_conv[0:1] * ubuf_ref[SUBLANES - 2:SUBLANES - 2 + tl, :]
              + w_conv[1:2] * ubuf_ref[SUBLANES - 1:SUBLANES - 1 + tl, :]
              + w_conv[2:3] * u)
    mix_ref[:, d_rec:d_rec + d_conv] = (gb * y_conv).astype(BF16)
    tail = u[tl - (CONV_W - 1):tl]
    ubuf_ref[SUBLANES - (CONV_W - 1):SUBLANES, :] = tail
    yield

    x1 = x + g1 * _bdot(mix_ref[...], w_out_ref[...])
    x1_ref[...] = x1

    @pl.when(is_last)
    def _():
        for hd in range(H_REC):
            rec_ref[hd] = st_ref[hd]
        conv_ref[...] = tail


def _prompt_kernel(x_ref, mod_ref, lb_ref, w_in_hbm, w_conv_ref, g_onorm_ref, w_out_hbm,
                   w_up_hbm, w_down_hbm, g_final_ref,
                   y_ref, rec_ref, conv_ref, w_in_out, w_out_out, w_up_out, w_down_out,
                   st_ref, ubuf_ref, mix_ref, x1_ref,
                   w_in_ref, w_out_ref, w_up_ref, w_down_ref, stage_ref, stage_sem, out_sem,
                   *, n_tiles, n_steps):
    s = pl.program_id(0)
    tile = jnp.minimum(s, n_steps - 1)
    l = lax.rem(tile, n_tiles)
    d = x_ref.shape[1]

    w_vmem = (w_in_ref, w_out_ref, w_up_ref, w_down_ref)
    w_outs = (w_in_out, w_out_out, w_up_out, w_down_out)

    def bf16_export(i):
        return pltpu.make_async_copy(w_vmem[i], w_outs[i], out_sem.at[i])

    @pl.when(s == 0)
    def _():
        _load_weights_as_bf16((w_in_hbm, w_out_hbm, w_up_hbm, w_down_hbm), w_vmem,
                              stage_ref, stage_sem)
        for i in range(len(w_vmem)):
            bf16_export(i).start()

    @pl.when(s == n_steps)
    def _():
        for i in range(len(w_vmem)):
            bf16_export(i).wait()

    @pl.when(l == 0)
    def _():
        st_ref[...] = jnp.zeros_like(st_ref)
        ubuf_ref[0:SUBLANES, :] = jnp.zeros((SUBLANES, ubuf_ref.shape[1]), F32)

    def mix_stage():
        mod = mod_ref[pl.ds(lax.div(tile, n_tiles), 1), :]
        return _prompt_mix_stage(x_ref, mod, lb_ref, w_in_ref, w_conv_ref, g_onorm_ref,
                                 w_out_ref, st_ref, ubuf_ref, mix_ref, x1_ref,
                                 rec_ref, conv_ref, l == n_tiles - 1)

    def mlp_stage():
        mod_prev = mod_ref[pl.ds(lax.div(s - 1, n_tiles), 1), :]
        return _mlp_stage(x1_ref, mod_prev, w_up_ref, w_down_ref, g_final_ref, y_ref)

    @pl.when(s == 0)
    def _():
        _alternate(mix_stage())

    @pl.when((s > 0) & (s < n_steps))
    def _():
        _alternate(mix_stage(), mlp_stage())

    @pl.when(s == n_steps)
    def _():
        _alternate(mlp_stage())


def _prompt_layer(x, mod_p, lower_bounds, w_in, w_conv, g_onorm, w_out, w_up, w_down, g_final):
    bsz, seq, d = x.shape
    d_rec = lower_bounds.shape[1]
    dk = d_rec // H_REC
    d_conv = w_conv.shape[1]
    tl = PROMPT_TILE
    n_tiles = seq // tl
    n_steps = bsz * n_tiles
    assert seq % tl == 0 and tl % CHUNK == 0
    mats = (w_in, w_out, w_up, w_down)
    hbm = pl.BlockSpec(memory_space=pl.ANY)

    def mix_tile(s):
        t = jnp.minimum(s, n_steps - 1)
        return lax.div(t, n_tiles), lax.rem(t, n_tiles)

    def mlp_tile(s):
        t = jnp.maximum(s - 1, 0)
        return lax.div(t, n_tiles), lax.rem(t, n_tiles)

    return pl.pallas_call(
        functools.partial(_prompt_kernel, n_tiles=n_tiles, n_steps=n_steps),
        grid=(n_steps + 1,),
        in_specs=[
            pl.BlockSpec((None, tl, d), lambda s: (*mix_tile(s), 0)),
            _const_spec(mod_p.shape),
            _const_spec(lower_bounds.shape),
            hbm,
            _const_spec(w_conv.shape),
            _const_spec(g_onorm.shape),
            hbm,
            hbm,
            hbm,
            _const_spec(g_final.shape),
        ],
        out_specs=[
            pl.BlockSpec((None, tl, d), lambda s: (*mlp_tile(s), 0)),
            pl.BlockSpec((None, H_REC, dk, dk), lambda s: (mix_tile(s)[0], 0, 0, 0)),
            pl.BlockSpec((None, CONV_W - 1, d_conv), lambda s: (mix_tile(s)[0], 0, 0)),
        ] + [hbm] * len(mats),
        out_shape=[
            jax.ShapeDtypeStruct((bsz, seq, d), F32),
            jax.ShapeDtypeStruct((bsz, H_REC, dk, dk), F32),
            jax.ShapeDtypeStruct((bsz, CONV_W - 1, d_conv), F32),
        ] + [jax.ShapeDtypeStruct(w.shape, BF16) for w in mats],
        scratch_shapes=[
            pltpu.VMEM((H_REC, dk, dk), F32),
            pltpu.VMEM((SUBLANES + tl, d_conv), F32),
            pltpu.VMEM((tl, d_rec + d_conv), BF16),
            pltpu.VMEM((tl, d), F32),
        ] + [pltpu.VMEM(w.shape, BF16) for w in mats] + [
            pltpu.VMEM((WEIGHT_STAGE_SLOTS, *WEIGHT_STAGE_SHAPE), F32),
            pltpu.SemaphoreType.DMA((WEIGHT_STAGE_SLOTS,)),
            pltpu.SemaphoreType.DMA((len(mats),)),
        ],
        compiler_params=pltpu.CompilerParams(
            dimension_semantics=("arbitrary",), vmem_limit_bytes=VMEM_LIMIT_BYTES),
        name="prompt_layer",
    )(x, mod_p, lower_bounds, w_in, w_conv, g_onorm, w_out, w_up, w_down, g_final)


def _sample_kernel(x_ref, mod_ref, lb_ref, w_in_ref, w_conv_ref, g_onorm_ref, w_out_ref,
                   w_up_ref, w_down_ref, g_final_ref, rec_in_ref, conv_in_ref,
                   y_ref, rec_ref, conv_ref,
                   modx_ref, cbx_ref, ubuf_ref, mix_ref):
    tb, seq, d = x_ref.shape
    rows = tb * seq
    d_rec = lb_ref.shape[1]
    dk = d_rec // H_REC
    d_conv = conv_in_ref.shape[2]

    for b in range(tb):
        r = slice(b * seq, (b + 1) * seq)
        modx_ref[r, :] = jnp.broadcast_to(mod_ref[b:b + 1, :], (seq, mod_ref.shape[1]))
        for j in range(CONV_W - 1):
            cbx_ref[j, r, :] = jnp.broadcast_to(conv_in_ref[b, j:j + 1, :], (seq, d_conv))

    sh1, sc1, g1, sh2, sc2, g2 = (modx_ref[:, i * d:(i + 1) * d] for i in range(N_MOD))

    x = x_ref[...].reshape(rows, d)
    proj = _bdot(_modulate(x, sh1, sc1), w_in_ref[...])
    q, fz, iv, g, gb, gc, hv = _split_proj(proj)

    lb = _layer_lower_bound(lb_ref, 0)
    row = lax.broadcasted_iota(jnp.int32, (rows, rows), 0)
    col = lax.broadcasted_iota(jnp.int32, (rows, rows), 1)
    causal = ((row // seq) == (col // seq)) & (row >= col)
    f, k = _gates(fz, lb)
    prod = _tile_cumprod(f)
    decay = _tile_last(prod)
    q_dec = (q * prod).astype(BF16)
    k_inv = k / prod
    k_dec = k_inv.astype(BF16)
    k_end = k_inv * decay
    vb = iv.astype(BF16)

    zeros_blk = jnp.zeros((seq, dk), F32)
    heads = [slice(hd * dk, (hd + 1) * dk) for hd in range(H_REC)]

    def update_states(elements):
        for b in elements:
            r = slice(b * seq, (b + 1) * seq)
            for ha in range(0, H_REC, 2):
                sa, sb = heads[ha], heads[ha + 1]
                lhs = jnp.concatenate([k_end[r, sa], k_end[r, sb]], axis=0)
                rhs = jnp.concatenate([jnp.concatenate([iv[r, sa], zeros_blk], axis=1),
                                       jnp.concatenate([zeros_blk, iv[r, sb]], axis=1)], axis=0)
                both = _dot_tn(lhs.astype(BF16), rhs.astype(BF16))
                for hd, sl, upd in ((ha, sa, both[:, :dk]), (ha + 1, sb, both[:, dk:])):
                    dcol = jnp.broadcast_to(decay[b * seq:b * seq + 1, sl], (dk, dk)).T
                    rec_ref[b, hd] = dcol * rec_in_ref[b, hd] + upd

    scores = [jnp.where(causal, _dot_nt(q_dec[:, sl], k_dec[:, sl]), 0.0).astype(BF16)
              for sl in heads]
    o_inter = jnp.concatenate(
        [jnp.concatenate([_bdot(q_dec[b * seq:(b + 1) * seq, sl], rec_in_ref[b, hd].astype(BF16))
                          for hd, sl in enumerate(heads)], axis=-1)
         for b in range(tb)], axis=0)
    o_intra = jnp.concatenate([_bdot(scores[hd], vb[:, sl]) for hd, sl in enumerate(heads)],
                              axis=-1)
    o_rec = _head_out(o_intra + o_inter, g, g_onorm_ref[...])
    mix_ref[:, 0:d_rec] = o_rec.astype(BF16)

    u = gc * hv
    ubuf_ref[SUBLANES:SUBLANES + rows, :] = u
    tok = _group_pos((rows, d_conv), seq)
    u_m1 = jnp.where(tok >= 1, ubuf_ref[SUBLANES - 1:SUBLANES - 1 + rows, :], cbx_ref[1])
    u_m2 = jnp.where(tok >= 2, ubuf_ref[SUBLANES - 2:SUBLANES - 2 + rows, :],
                     jnp.where(tok == 1, cbx_ref[1], cbx_ref[0]))
    w_conv = w_conv_ref[...]
    y_conv = w_conv[0:1] * u_m2 + w_conv[1:2] * u_m1 + w_conv[2:3] * u
    mix_ref[:, d_rec:d_rec + d_conv] = (gb * y_conv).astype(BF16)
    for b in range(tb):
        conv_ref[b] = u[(b + 1) * seq - (CONV_W - 1):(b + 1) * seq]

    x1 = x + g1 * _bdot(mix_ref[...], w_out_ref[...])
    update_states(range(tb // 2))
    h2 = _modulate(x1, sh2, sc2)
    mlp = _mlp_down(_mlp_up(h2, w_up_ref, 0), w_up_ref, w_down_ref, 0)
    for j in range(1, MLP_BLOCKS):
        mlp += _mlp_down(_mlp_up(h2, w_up_ref, j), w_up_ref, w_down_ref, j)
    update_states(range(tb // 2, tb))
    y_ref[...] = _final_norm(x1, g2, mlp, g_final_ref).reshape(tb, seq, d)


def _sample_layer(x, mod, lower_bounds, w_in, w_conv, g_onorm, w_out, w_up, w_down,
                  g_final, rec_in, conv_in):
    bsz, seq, d = x.shape
    d_rec = lower_bounds.shape[1]
    dk = d_rec // H_REC
    d_conv = w_conv.shape[1]
    tb = SAMPLE_TILE_B
    rows = tb * seq
    assert bsz % tb == 0 and seq == SUBLANES and seq >= CONV_W
    return pl.pallas_call(
        _sample_kernel,
        grid=(bsz // tb,),
        in_specs=[
            pl.BlockSpec((tb, seq, d), lambda i: (i, 0, 0)),
            pl.BlockSpec((tb, N_MOD * d), lambda i: (i, 0)),
            _const_spec(lower_bounds.shape),
            _const_spec(w_in.shape),
            _const_spec(w_conv.shape),
            _const_spec(g_onorm.shape),
            _const_spec(w_out.shape),
            _const_spec(w_up.shape),
            _const_spec(w_down.shape),
            _const_spec(g_final.shape),
            pl.BlockSpec((tb, H_REC, dk, dk), lambda i: (i, 0, 0, 0)),
            pl.BlockSpec((tb, CONV_W - 1, d_conv), lambda i: (i, 0, 0)),
        ],
        out_specs=[
            pl.BlockSpec((tb, seq, d), lambda i: (i, 0, 0)),
            pl.BlockSpec((tb, H_REC, dk, dk), lambda i: (i, 0, 0, 0)),
            pl.BlockSpec((tb, CONV_W - 1, d_conv), lambda i: (i, 0, 0)),
        ],
        out_shape=[
            jax.ShapeDtypeStruct((bsz, seq, d), F32),
            jax.ShapeDtypeStruct((bsz, H_REC, dk, dk), F32),
            jax.ShapeDtypeStruct((bsz, CONV_W - 1, d_conv), F32),
        ],
        scratch_shapes=[
            pltpu.VMEM((rows, N_MOD * d), F32),
            pltpu.VMEM((CONV_W - 1, rows, d_conv), F32),
            pltpu.VMEM((SUBLANES + rows, d_conv), F32),
            pltpu.VMEM((rows, d_rec + d_conv), BF16),
        ],
        compiler_params=pltpu.CompilerParams(
            dimension_semantics=("arbitrary",), vmem_limit_bytes=VMEM_LIMIT_BYTES),
        name="sample_layer",
    )(x, mod, lower_bounds, w_in, w_conv, g_onorm, w_out, w_up, w_down, g_final, rec_in, conv_in)


def kernel(x_prompt, x_sample, state_rec, state_conv, c_prompt, c_sample, lower_bounds, w_ada,
           b_ada, w_in, w_conv, g_onorm, w_out, w_up, w_down, g_final):
    depth = w_in.shape[0]
    assert depth == 1, "single-layer trunk"

    mod_s, mod_p = _modulation(c_sample, c_prompt, w_ada[0], b_ada)

    y_p, rec_p, conv_p, w_in_b, w_out_b, w_up_b, w_down_b = _prompt_layer(
        x_prompt, mod_p, lower_bounds, w_in[0], w_conv[0], g_onorm, w_out[0], w_up[0], w_down[0],
        g_final)
    y_s, rec_s, conv_s = _sample_layer(
        x_sample, mod_s, lower_bounds, w_in_b, w_conv[0], g_onorm, w_out_b, w_up_b, w_down_b,
        g_final, state_rec[0], state_conv[0])
    return y_p, y_s, rec_p[None], conv_p[None], rec_s[None], conv_s[None]
```

```python
import functools

import jax
import jax.numpy as jnp
from jax import lax
from jax.experimental import pallas as pl
from jax.experimental.pallas import tpu as pltpu

F32 = jnp.float32
BF16 = jnp.bfloat16

EPS = 1e-6
H_REC = 4
CONV_W = 3
N_MOD = 6
N_PROJ_GROUPS = 7
CHUNK = 64

SUBLANES = 8
VMEM_LIMIT_BYTES = 56 * 1024 * 1024

PROMPT_TILE = 512
SAMPLE_TILE_B = 16
MOD_TILE_K = 512
MLP_BLOCKS = 4
WEIGHT_STAGE_SHAPE = (512, 512)
WEIGHT_STAGE_SLOTS = 6


def _const_spec(shape):
    zeros = (0,) * len(shape)
    return pl.BlockSpec(shape, lambda *_: zeros, pipeline_mode=pl.Buffered(1))


def _rms(x):
    return x * lax.rsqrt(jnp.mean(x * x, axis=-1, keepdims=True) + EPS)


def _silu(x):
    return x * jax.nn.sigmoid(x)


def _bdot(a, b):
    return jnp.dot(a, b, preferred_element_type=F32)


def _dot_nt(a, b):
    return lax.dot_general(a, b, (((1,), (1,)), ((), ())), preferred_element_type=F32)


def _dot_tn(a, b):
    return lax.dot_general(a, b, (((0,), (0,)), ((), ())), preferred_element_type=F32)


def _group_pos(shape, group):
    return lax.broadcasted_iota(jnp.int32, shape, 0) % group


def _tile_cumprod(x):
    rows, n = x.shape
    x = x.reshape(rows // SUBLANES, SUBLANES, n)
    pos = lax.broadcasted_iota(jnp.int32, x.shape, 1)
    step = 1
    while step < SUBLANES:
        x = x * jnp.where(pos >= step, pltpu.roll(x, step, axis=1), 1.0)
        step *= 2
    return x.reshape(rows, n)


def _tile_last(x):
    rows, n = x.shape
    x = x.reshape(rows // SUBLANES, SUBLANES, n)
    return jnp.broadcast_to(x[:, SUBLANES - 1:, :], x.shape).reshape(rows, n)


def _chain_tiles(y):
    tiles, carry = [], None
    for r in range(0, y.shape[0], SUBLANES):
        t = y[r:r + SUBLANES]
        if carry is not None:
            t = t * carry
        tiles.append(t)
        carry = t[SUBLANES - 1:, :]
    return jnp.concatenate(tiles, axis=0)


def _layer_lower_bound(lb_ref, layer):
    lb = lb_ref[...]
    e = jnp.exp(lb - jnp.max(lb, axis=0, keepdims=True))
    sm = e / jnp.sum(e, axis=0, keepdims=True)
    return jnp.sum(sm[: layer + 1], axis=0, keepdims=True)


def _gates(fz, lb):
    sig = jax.nn.sigmoid(fz)
    f = lb + (1.0 - lb) * sig
    k = (1.0 - lb) * (1.0 - sig)
    return f, k


def _head_out(o, g, g_onorm):
    dv = o.shape[-1] // H_REC
    pieces = []
    for h in range(H_REC):
        sl = slice(h * dv, (h + 1) * dv)
        pieces.append(_rms(o[:, sl]))
    return (jnp.concatenate(pieces, axis=-1) * g_onorm) * _silu(g)


def _modulate(x, shift, scale):
    return (_rms(x) * (1.0 + scale) + shift).astype(BF16)


def _mlp_cols(w_up_ref, j):
    blk = w_up_ref.shape[1] // MLP_BLOCKS
    return slice(j * blk, (j + 1) * blk)


def _mlp_up(h2, w_up_ref, j):
    up = jnp.maximum(_bdot(h2, w_up_ref[:, _mlp_cols(w_up_ref, j)]), 0.0)
    return (up * up).astype(BF16)


def _mlp_down(up, w_up_ref, w_down_ref, j):
    return _bdot(up, w_down_ref[_mlp_cols(w_up_ref, j), :])


def _final_norm(x, g2, mlp, g_final_ref):
    return _rms(x + g2 * mlp) * g_final_ref[...]


def _mod_kernel(c_s_ref, c_p_ref, w_ref, b_ref, o_s_ref, o_p_ref):
    k = pl.program_id(0)
    n_s = c_s_ref.shape[0]
    kb = w_ref.shape[0]
    cols = pl.ds(pl.multiple_of(k * kb, kb), kb)
    c = jnp.concatenate([c_s_ref[:, cols], c_p_ref[:, cols]], axis=0)
    part = _bdot(_silu(c).astype(BF16), w_ref[...].astype(BF16))

    @pl.when(k == 0)
    def _():
        o_s_ref[...] = part[:n_s] + b_ref[...]
        o_p_ref[...] = part[n_s:] + b_ref[...]

    @pl.when(k > 0)
    def _():
        o_s_ref[...] += part[:n_s]
        o_p_ref[...] += part[n_s:]


def _modulation(c_s, c_p, w_ada, b_ada):
    (n_s, d), n_p = c_s.shape, c_p.shape[0]
    n = w_ada.shape[1]
    assert n_s % SUBLANES == 0 and n_p % SUBLANES == 0 and d % MOD_TILE_K == 0
    return pl.pallas_call(
        _mod_kernel,
        grid=(d // MOD_TILE_K,),
        in_specs=[
            pl.BlockSpec((n_s, d), lambda k: (0, 0)),
            pl.BlockSpec((n_p, d), lambda k: (0, 0)),
            pl.BlockSpec((MOD_TILE_K, n), lambda k: (k, 0)),
            pl.BlockSpec((1, n), lambda k: (0, 0)),
        ],
        out_specs=[
            pl.BlockSpec((n_s, n), lambda k: (0, 0)),
            pl.BlockSpec((n_p, n), lambda k: (0, 0)),
        ],
        out_shape=[
            jax.ShapeDtypeStruct((n_s, n), F32),
            jax.ShapeDtypeStruct((n_p, n), F32),
        ],
        compiler_params=pltpu.CompilerParams(
            dimension_semantics=("arbitrary",), vmem_limit_bytes=VMEM_LIMIT_BYTES),
        name="adaln_modulation",
    )(c_s, c_p, w_ada, b_ada)


def _split_proj(proj):
    width = proj.shape[1] // N_PROJ_GROUPS
    return tuple(proj[:, p * width:(p + 1) * width] for p in range(N_PROJ_GROUPS))


def _weight_blocks(w_hbm, w_vmem):
    rows, cols = w_hbm.shape
    br, bc = WEIGHT_STAGE_SHAPE
    assert rows % br == 0 and cols % bc == 0
    return [(w_hbm.at[r:r + br, c:c + bc], w_vmem.at[r:r + br, c:c + bc])
            for r in range(0, rows, br) for c in range(0, cols, bc)]


def _stage_copy(src, stage_ref, sem_ref, i):
    slot = i % WEIGHT_STAGE_SLOTS
    return pltpu.make_async_copy(src, stage_ref.at[slot], sem_ref.at[slot])


def _load_weights_as_bf16(w_hbm_refs, w_vmem_refs, stage_ref, sem_ref):
    blocks = [blk for w_hbm, w_vmem in zip(w_hbm_refs, w_vmem_refs)
              for blk in _weight_blocks(w_hbm, w_vmem)]
    ahead = WEIGHT_STAGE_SLOTS - 1
    for i in range(min(ahead, len(blocks))):
        _stage_copy(blocks[i][0], stage_ref, sem_ref, i).start()
    for i, (src, dst) in enumerate(blocks):
        if i + ahead < len(blocks):
            _stage_copy(blocks[i + ahead][0], stage_ref, sem_ref, i + ahead).start()
        _stage_copy(src, stage_ref, sem_ref, i).wait()
        dst[...] = stage_ref[i % WEIGHT_STAGE_SLOTS].astype(BF16)


def _alternate(*stages):
    stages = list(stages)
    while stages:
        for stage in list(stages):
            try:
                next(stage)
            except StopIteration:
                stages.remove(stage)


def _mlp_stage(x1_ref, mod, w_up_ref, w_down_ref, g_final_ref, y_ref):
    d = x1_ref.shape[1]
    sh2, sc2, g2 = (mod[:, i * d:(i + 1) * d] for i in range(3, N_MOD))
    h2 = _modulate(x1_ref[...], sh2, sc2)
    ups = [_mlp_up(h2, w_up_ref, 0)]
    yield
    mlp = None
    for j in range(MLP_BLOCKS):
        if j + 1 < MLP_BLOCKS:
            ups.append(_mlp_up(h2, w_up_ref, j + 1))
        part = _mlp_down(ups[j], w_up_ref, w_down_ref, j)
        mlp = part if mlp is None else mlp + part
        if j + 2 < MLP_BLOCKS:
            yield
    y_ref[...] = _final_norm(x1_ref[...], g2, mlp, g_final_ref)
    yield


def _prompt_mix_stage(x_ref, mod, lb_ref, w_in_ref, w_conv_ref, g_onorm_ref, w_out_ref,
                      st_ref, ubuf_ref, mix_ref, x1_ref, rec_ref, conv_ref, is_last):
    tl, d = x_ref.shape
    d_rec = lb_ref.shape[1]
    dk = d_rec // H_REC
    n_chunks = tl // CHUNK
    heads = [slice(hd * dk, (hd + 1) * dk) for hd in range(H_REC)]
    sh1, sc1, g1 = (mod[:, i * d:(i + 1) * d] for i in range(3))

    x = x_ref[...]
    proj = _bdot(_modulate(x, sh1, sc1), w_in_ref[...])
    q, fz, iv, g, gb, gc, hv = _split_proj(proj)
    d_conv = gb.shape[1]
    yield

    lb = _layer_lower_bound(lb_ref, 0)
    row = lax.broadcasted_iota(jnp.int32, (CHUNK, CHUNK), 0)
    col = lax.broadcasted_iota(jnp.int32, (CHUNK, CHUNK), 1)
    causal = row >= col
    f_all, k_all = _gates(fz, lb)
    prod_tiles = _tile_cumprod(f_all)
    q_dec, decay, scores, upd = [], [], [], []
    for c in range(n_chunks):
        rows = slice(c * CHUNK, (c + 1) * CHUNK)
        prod = _chain_tiles(prod_tiles[rows])
        dec = prod[CHUNK - 1:CHUNK, :]
        q_dec.append((q[rows] * prod).astype(BF16))
        k_inv = k_all[rows] / prod
        k_dec = k_inv.astype(BF16)
        k_end = (k_inv * dec).astype(BF16)
        decay.append([jnp.broadcast_to(dec[:, sl], (dk, dk)).T for sl in heads])
        v = iv[rows].astype(BF16)
        scores.append([jnp.where(causal, _dot_nt(q_dec[c][:, sl], k_dec[:, sl]), 0.0).astype(BF16)
                       for sl in heads])
        upd.append([_dot_tn(k_end[:, sl], v[:, sl]) for sl in heads])
    yield

    st = [st_ref[hd] for hd in range(H_REC)]
    o_chunks = []
    for c in range(n_chunks):
        v = iv[c * CHUNK:(c + 1) * CHUNK].astype(BF16)
        o_heads = []
        for hd, sl in enumerate(heads):
            o_heads.append(_bdot(scores[c][hd], v[:, sl])
                           + _bdot(q_dec[c][:, sl], st[hd].astype(BF16)))
            st[hd] = st[hd] * decay[c][hd] + upd[c][hd]
        o_chunks.append(jnp.concatenate(o_heads, axis=-1))
    for hd in range(H_REC):
        st_ref[hd] = st[hd]
    yield

    o_rec = _head_out(jnp.concatenate(o_chunks, axis=0), g, g_onorm_ref[...])
    mix_ref[:, 0:d_rec] = o_rec.astype(BF16)

    u = gc * hv
    ubuf_ref[SUBLANES:SUBLANES + tl, :] = u
    w_conv = w_conv_ref[...]
    y_conv = (w_conv[0:1] * ubuf_ref[SUBLANES - 2:SUBLANES - 2 + tl, :]
              + w_conv[1:2] * ubuf_ref[SUBLANES - 1:SUBLANES - 1 + tl, :]
              + w_conv[2:3] * u)
    mix_ref[:, d_rec:d_rec + d_conv] = (gb * y_conv).astype(BF16)
    tail = u[tl - (CONV_W - 1):tl]
    ubuf_ref[SUBLANES - (CONV_W - 1):SUBLANES, :] = tail
    yield

    x1 = x + g1 * _bdot(mix_ref[...], w_out_ref[...])
    x1_ref[...] = x1

    @pl.when(is_last)
    def _():
        for hd in range(H_REC):
            rec_ref[hd] = st_ref[hd]
        conv_ref[...] = tail


def _prompt_kernel(x_ref, mod_ref, lb_ref, w_in_hbm, w_conv_ref, g_onorm_ref, w_out_hbm,
                   w_up_hbm, w_down_hbm, g_final_ref,
                   y_ref, rec_ref, conv_ref, w_in_out, w_out_out, w_up_out, w_down_out,
                   st_ref, ubuf_ref, mix_ref, x1_ref,
                   w_in_ref, w_out_ref, w_up_ref, w_down_ref, stage_ref, stage_sem, out_sem,
                   *, n_tiles, n_steps):
    s = pl.program_id(0)
    tile = jnp.minimum(s, n_steps - 1)
    l = lax.rem(tile, n_tiles)
    d = x_ref.shape[1]

    w_vmem = (w_in_ref, w_out_ref, w_up_ref, w_down_ref)
    w_outs = (w_in_out, w_out_out, w_up_out, w_down_out)

    def bf16_export(i):
        return pltpu.make_async_copy(w_vmem[i], w_outs[i], out_sem.at[i])

    @pl.when(s == 0)
    def _():
        _load_weights_as_bf16((w_in_hbm, w_out_hbm, w_up_hbm, w_down_hbm), w_vmem,
                              stage_ref, stage_sem)
        for i in range(len(w_vmem)):
            bf16_export(i).start()

    @pl.when(s == n_steps)
    def _():
        for i in range(len(w_vmem)):
            bf16_export(i).wait()

    @pl.when(l == 0)
    def _():
        st_ref[...] = jnp.zeros_like(st_ref)
        ubuf_ref[0:SUBLANES, :] = jnp.zeros((SUBLANES, ubuf_ref.shape[1]), F32)

    def mix_stage():
        mod = mod_ref[pl.ds(lax.div(tile, n_tiles), 1), :]
        return _prompt_mix_stage(x_ref, mod, lb_ref, w_in_ref, w_conv_ref, g_onorm_ref,
                                 w_out_ref, st_ref, ubuf_ref, mix_ref, x1_ref,
                                 rec_ref, conv_ref, l == n_tiles - 1)

    def mlp_stage():
        mod_prev = mod_ref[pl.ds(lax.div(s - 1, n_tiles), 1), :]
        return _mlp_stage(x1_ref, mod_prev, w_up_ref, w_down_ref, g_final_ref, y_ref)

    @pl.when(s == 0)
    def _():
        _alternate(mix_stage())

    @pl.when((s > 0) & (s < n_steps))
    def _():
        _alternate(mix_stage(), mlp_stage())

    @pl.when(s == n_steps)
    def _():
        _alternate(mlp_stage())


def _prompt_layer(x, mod_p, lower_bounds, w_in, w_conv, g_onorm, w_out, w_up, w_down, g_final):
    bsz, seq, d = x.shape
    d_rec = lower_bounds.shape[1]
    dk = d_rec // H_REC
    d_conv = w_conv.shape[1]
    tl = PROMPT_TILE
    n_tiles = seq // tl
    n_steps = bsz * n_tiles
    assert seq % tl == 0 and tl % CHUNK == 0
    mats = (w_in, w_out, w_up, w_down)
    hbm = pl.BlockSpec(memory_space=pl.ANY)

    def mix_tile(s):
        t = jnp.minimum(s, n_steps - 1)
        return lax.div(t, n_tiles), lax.rem(t, n_tiles)

    def mlp_tile(s):
        t = jnp.maximum(s - 1, 0)
        return lax.div(t, n_tiles), lax.rem(t, n_tiles)

    return pl.pallas_call(
        functools.partial(_prompt_kernel, n_tiles=n_tiles, n_steps=n_steps),
        grid=(n_steps + 1,),
        in_specs=[
            pl.BlockSpec((None, tl, d), lambda s: (*mix_tile(s), 0)),
            _const_spec(mod_p.shape),
            _const_spec(lower_bounds.shape),
            hbm,
            _const_spec(w_conv.shape),
            _const_spec(g_onorm.shape),
            hbm,
            hbm,
            hbm,
            _const_spec(g_final.shape),
        ],
        out_specs=[
            pl.BlockSpec((None, tl, d), lambda s: (*mlp_tile(s), 0)),
            pl.BlockSpec((None, H_REC, dk, dk), lambda s: (mix_tile(s)[0], 0, 0, 0)),
            pl.BlockSpec((None, CONV_W - 1, d_conv), lambda s: (mix_tile(s)[0], 0, 0)),
        ] + [hbm] * len(mats),
        out_shape=[
            jax.ShapeDtypeStruct((bsz, seq, d), F32),
            jax.ShapeDtypeStruct((bsz, H_REC, dk, dk), F32),
            jax.ShapeDtypeStruct((bsz, CONV_W - 1, d_conv), F32),
        ] + [jax.ShapeDtypeStruct(w.shape, BF16) for w in mats],
        scratch_shapes=[
            pltpu.VMEM((H_REC, dk, dk), F32),
            pltpu.VMEM((SUBLANES + tl, d_conv), F32),
            pltpu.VMEM((tl, d_rec + d_conv), BF16),
            pltpu.VMEM((tl, d), F32),
        ] + [pltpu.VMEM(w.shape, BF16) for w in mats] + [
            pltpu.VMEM((WEIGHT_STAGE_SLOTS, *WEIGHT_STAGE_SHAPE), F32),
            pltpu.SemaphoreType.DMA((WEIGHT_STAGE_SLOTS,)),
            pltpu.SemaphoreType.DMA((len(mats),)),
        ],
        compiler_params=pltpu.CompilerParams(
            dimension_semantics=("arbitrary",), vmem_limit_bytes=VMEM_LIMIT_BYTES),
        name="prompt_layer",
    )(x, mod_p, lower_bounds, w_in, w_conv, g_onorm, w_out, w_up, w_down, g_final)


def _sample_kernel(x_ref, mod_ref, lb_ref, w_in_ref, w_conv_ref, g_onorm_ref, w_out_ref,
                   w_up_ref, w_down_ref, g_final_ref, rec_in_ref, conv_in_ref,
                   y_ref, rec_ref, conv_ref,
                   modx_ref, cbx_ref, ubuf_ref, mix_ref):
    tb, seq, d = x_ref.shape
    rows = tb * seq
    d_rec = lb_ref.shape[1]
    dk = d_rec // H_REC
    d_conv = conv_in_ref.shape[2]

    for b in range(tb):
        r = slice(b * seq, (b + 1) * seq)
        modx_ref[r, :] = jnp.broadcast_to(mod_ref[b:b + 1, :], (seq, mod_ref.shape[1]))
        for j in range(CONV_W - 1):
            cbx_ref[j, r, :] = jnp.broadcast_to(conv_in_ref[b, j:j + 1, :], (seq, d_conv))

    sh1, sc1, g1, sh2, sc2, g2 = (modx_ref[:, i * d:(i + 1) * d] for i in range(N_MOD))

    x = x_ref[...].reshape(rows, d)
    proj = _bdot(_modulate(x, sh1, sc1), w_in_ref[...])
    q, fz, iv, g, gb, gc, hv = _split_proj(proj)

    lb = _layer_lower_bound(lb_ref, 0)
    row = lax.broadcasted_iota(jnp.int32, (rows, rows), 0)
    col = lax.broadcasted_iota(jnp.int32, (rows, rows), 1)
    causal = ((row // seq) == (col // seq)) & (row >= col)
    f, k = _gates(fz, lb)
    prod = _tile_cumprod(f)
    decay = _tile_last(prod)
    q_dec = (q * prod).astype(BF16)
    k_inv = k / prod
    k_dec = k_inv.astype(BF16)
    k_end = k_inv * decay
    vb = iv.astype(BF16)

    zeros_blk = jnp.zeros((seq, dk), F32)
    heads = [slice(hd * dk, (hd + 1) * dk) for hd in range(H_REC)]

    def update_states(elements):
        for b in elements:
            r = slice(b * seq, (b + 1) * seq)
            for ha in range(0, H_REC, 2):
                sa, sb = heads[ha], heads[ha + 1]
                lhs = jnp.concatenate([k_end[r, sa], k_end[r, sb]], axis=0)
                rhs = jnp.concatenate([jnp.concatenate([iv[r, sa], zeros_blk], axis=1),
                                       jnp.concatenate([zeros_blk, iv[r, sb]], axis=1)], axis=0)
                both = _dot_tn(lhs.astype(BF16), rhs.astype(BF16))
                for hd, sl, upd in ((ha, sa, both[:, :dk]), (ha + 1, sb, both[:, dk:])):
                    dcol = jnp.broadcast_to(decay[b * seq:b * seq + 1, sl], (dk, dk)).T
                    rec_ref[b, hd] = dcol * rec_in_ref[b, hd] + upd

    scores = [jnp.where(causal, _dot_nt(q_dec[:, sl], k_dec[:, sl]), 0.0).astype(BF16)
              for sl in heads]
    o_inter = jnp.concatenate(
        [jnp.concatenate([_bdot(q_dec[b * seq:(b + 1) * seq, sl], rec_in_ref[b, hd].astype(BF16))
                          for hd, sl in enumerate(heads)], axis=-1)
         for b in range(tb)], axis=0)
    o_intra = jnp.concatenate([_bdot(scores[hd], vb[:, sl]) for hd, sl in enumerate(heads)],
                              axis=-1)
    o_rec = _head_out(o_intra + o_inter, g, g_onorm_ref[...])
    mix_ref[:, 0:d_rec] = o_rec.astype(BF16)

    u = gc * hv
    ubuf_ref[SUBLANES:SUBLANES + rows, :] = u
    tok = _group_pos((rows, d_conv), seq)
    u_m1 = jnp.where(tok >= 1, ubuf_ref[SUBLANES - 1:SUBLANES - 1 + rows, :], cbx_ref[1])
    u_m2 = jnp.where(tok >= 2, ubuf_ref[SUBLANES - 2:SUBLANES - 2 + rows, :],
                     jnp.where(tok == 1, cbx_ref[1], cbx_ref[0]))
    w_conv = w_conv_ref[...]
    y_conv = w_conv[0:1] * u_m2 + w_conv[1:2] * u_m1 + w_conv[2:3] * u
    mix_ref[:, d_rec:d_rec + d_conv] = (gb * y_conv).astype(BF16)
    for b in range(tb):
        conv_ref[b] = u[(b + 1) * seq - (CONV_W - 1):(b + 1) * seq]

    x1 = x + g1 * _bdot(mix_ref[...], w_out_ref[...])
    update_states(range(tb // 2))
    h2 = _modulate(x1, sh2, sc2)
    mlp = _mlp_down(_mlp_up(h2, w_up_ref, 0), w_up_ref, w_down_ref, 0)
    for j in range(1, MLP_BLOCKS):
        mlp += _mlp_down(_mlp_up(h2, w_up_ref, j), w_up_ref, w_down_ref, j)
    update_states(range(tb // 2, tb))
    y_ref[...] = _final_norm(x1, g2, mlp, g_final_ref).reshape(tb, seq, d)


def _sample_layer(x, mod, lower_bounds, w_in, w_conv, g_onorm, w_out, w_up, w_down,
                  g_final, rec_in, conv_in):
    bsz, seq, d = x.shape
    d_rec = lower_bounds.shape[1]
    dk = d_rec // H_REC
    d_conv = w_conv.shape[1]
    tb = SAMPLE_TILE_B
    rows = tb * seq
    assert bsz % tb == 0 and seq == SUBLANES and seq >= CONV_W
    return pl.pallas_call(
        _sample_kernel,
        grid=(bsz // tb,),
        in_specs=[
            pl.BlockSpec((tb, seq, d), lambda i: (i, 0, 0)),
            pl.BlockSpec((tb, N_MOD * d), lambda i: (i, 0)),
            _const_spec(lower_bounds.shape),
            _const_spec(w_in.shape),
            _const_spec(w_conv.shape),
            _const_spec(g_onorm.shape),
            _const_spec(w_out.shape),
            _const_spec(w_up.shape),
            _const_spec(w_down.shape),
            _const_spec(g_final.shape),
            pl.BlockSpec((tb, H_REC, dk, dk), lambda i: (i, 0, 0, 0)),
            pl.BlockSpec((tb, CONV_W - 1, d_conv), lambda i: (i, 0, 0)),
        ],
        out_specs=[
            pl.BlockSpec((tb, seq, d), lambda i: (i, 0, 0)),
            pl.BlockSpec((tb, H_REC, dk, dk), lambda i: (i, 0, 0, 0)),
            pl.BlockSpec((tb, CONV_W - 1, d_conv), lambda i: (i, 0, 0)),
        ],
        out_shape=[
            jax.ShapeDtypeStruct((bsz, seq, d), F32),
            jax.ShapeDtypeStruct((bsz, H_REC, dk, dk), F32),
            jax.ShapeDtypeStruct((bsz, CONV_W - 1, d_conv), F32),
        ],
        scratch_shapes=[
            pltpu.VMEM((rows, N_MOD * d), F32),
            pltpu.VMEM((CONV_W - 1, rows, d_conv), F32),
            pltpu.VMEM((SUBLANES + rows, d_conv), F32),
            pltpu.VMEM((rows, d_rec + d_conv), BF16),
        ],
        compiler_params=pltpu.CompilerParams(
            dimension_semantics=("arbitrary",), vmem_limit_bytes=VMEM_LIMIT_BYTES),
        name="sample_layer",
    )(x, mod, lower_bounds, w_in, w_conv, g_onorm, w_out, w_up, w_down, g_final, rec_in, conv_in)


def kernel(x_prompt, x_sample, state_rec, state_conv, c_prompt, c_sample, lower_bounds, w_ada,
           b_ada, w_in, w_conv, g_onorm, w_out, w_up, w_down, g_final):
    depth = w_in.shape[0]
    assert depth == 1, "single-layer trunk"

    mod_s, mod_p = _modulation(c_sample, c_prompt, w_ada[0], b_ada)

    y_p, rec_p, conv_p, w_in_b, w_out_b, w_up_b, w_down_b = _prompt_layer(
        x_prompt, mod_p, lower_bounds, w_in[0], w_conv[0], g_onorm, w_out[0], w_up[0], w_down[0],
        g_final)
    y_s, rec_s, conv_s = _sample_layer(
        x_sample, mod_s, lower_bounds, w_in_b, w_conv[0], g_onorm, w_out_b, w_up_b, w_down_b,
        g_final, state_rec[0], state_conv[0])
    return y_p, y_s, rec_p[None], conv_p[None], rec_s[None], conv_s[None]
```

```python
import functools

import jax
import jax.numpy as jnp
from jax import lax
from jax.experimental import pallas as pl
from jax.experimental.pallas import tpu as pltpu

F32 = jnp.float32
BF16 = jnp.bfloat16

EPS = 1e-6
H_REC = 4
CONV_W = 3
N_MOD = 6
N_PROJ_GROUPS = 7
CHUNK = 64

SUBLANES = 8
VMEM_LIMIT_BYTES = 56 * 1024 * 1024

PROMPT_TILE = 512
SAMPLE_TILE_B = 16
MOD_TILE_K = 256
MLP_BLOCKS = 4
WEIGHT_STAGE_SHAPE = (512, 512)
WEIGHT_STAGE_SLOTS = 6


def _const_spec(shape):
    zeros = (0,) * len(shape)
    return pl.BlockSpec(shape, lambda *_: zeros, pipeline_mode=pl.Buffered(1))


def _rms(x):
    return x * lax.rsqrt(jnp.mean(x * x, axis=-1, keepdims=True) + EPS)


def _silu(x):
    return x * jax.nn.sigmoid(x)


def _bdot(a, b):
    return jnp.dot(a, b, preferred_element_type=F32)


def _dot_nt(a, b):
    return lax.dot_general(a, b, (((1,), (1,)), ((), ())), preferred_element_type=F32)


def _dot_tn(a, b):
    return lax.dot_general(a, b, (((0,), (0,)), ((), ())), preferred_element_type=F32)


def _group_pos(shape, group):
    return lax.broadcasted_iota(jnp.int32, shape, 0) % group


def _tile_cumprod(x):
    rows, n = x.shape
    x = x.reshape(rows // SUBLANES, SUBLANES, n)
    pos = lax.broadcasted_iota(jnp.int32, x.shape, 1)
    step = 1
    while step < SUBLANES:
        x = x * jnp.where(pos >= step, pltpu.roll(x, step, axis=1), 1.0)
        step *= 2
    return x.reshape(rows, n)


def _tile_last(x):
    rows, n = x.shape
    x = x.reshape(rows // SUBLANES, SUBLANES, n)
    return jnp.broadcast_to(x[:, SUBLANES - 1:, :], x.shape).reshape(rows, n)


def _chain_tiles(y):
    tiles, carry = [], None
    for r in range(0, y.shape[0], SUBLANES):
        t = y[r:r + SUBLANES]
        if carry is not None:
            t = t * carry
        tiles.append(t)
        carry = t[SUBLANES - 1:, :]
    return jnp.concatenate(tiles, axis=0)


def _layer_lower_bound(lb_ref, layer):
    lb = lb_ref[...]
    e = jnp.exp(lb - jnp.max(lb, axis=0, keepdims=True))
    sm = e / jnp.sum(e, axis=0, keepdims=True)
    return jnp.sum(sm[: layer + 1], axis=0, keepdims=True)


def _gates(fz, lb):
    sig = jax.nn.sigmoid(fz)
    f = lb + (1.0 - lb) * sig
    k = (1.0 - lb) * (1.0 - sig)
    return f, k


def _head_out(o, g, g_onorm):
    dv = o.shape[-1] // H_REC
    pieces = []
    for h in range(H_REC):
        sl = slice(h * dv, (h + 1) * dv)
        pieces.append(_rms(o[:, sl]))
    return (jnp.concatenate(pieces, axis=-1) * g_onorm) * _silu(g)


def _modulate(x, shift, scale):
    return (_rms(x) * (1.0 + scale) + shift).astype(BF16)


def _mlp_cols(w_up_ref, j):
    blk = w_up_ref.shape[1] // MLP_BLOCKS
    return slice(j * blk, (j + 1) * blk)


def _mlp_up(h2, w_up_ref, j):
    up = jnp.maximum(_bdot(h2, w_up_ref[:, _mlp_cols(w_up_ref, j)]), 0.0)
    return (up * up).astype(BF16)


def _mlp_down(up, w_up_ref, w_down_ref, j):
    return _bdot(up, w_down_ref[_mlp_cols(w_up_ref, j), :])


def _final_norm(x, g2, mlp, g_final_ref):
    return _rms(x + g2 * mlp) * g_final_ref[...]


def _mod_kernel(c_s_ref, c_p_ref, w_ref, b_ref, o_s_ref, o_p_ref):
    k = pl.program_id(0)
    n_s = c_s_ref.shape[0]
    kb = w_ref.shape[0]
    cols = pl.ds(pl.multiple_of(k * kb, kb), kb)
    c = jnp.concatenate([c_s_ref[:, cols], c_p_ref[:, cols]], axis=0)
    part = _bdot(_silu(c).astype(BF16), w_ref[...].astype(BF16))

    @pl.when(k == 0)
    def _():
        o_s_ref[...] = part[:n_s] + b_ref[...]
        o_p_ref[...] = part[n_s:] + b_ref[...]

    @pl.when(k > 0)
    def _():
        o_s_ref[...] += part[:n_s]
        o_p_ref[...] += part[n_s:]


def _modulation(c_s, c_p, w_ada, b_ada):
    (n_s, d), n_p = c_s.shape, c_p.shape[0]
    n = w_ada.shape[1]
    assert n_s % SUBLANES == 0 and n_p % SUBLANES == 0 and d % MOD_TILE_K == 0
    return pl.pallas_call(
        _mod_kernel,
        grid=(d // MOD_TILE_K,),
        in_specs=[
            pl.BlockSpec((n_s, d), lambda k: (0, 0)),
            pl.BlockSpec((n_p, d), lambda k: (0, 0)),
            pl.BlockSpec((MOD_TILE_K, n), lambda k: (k, 0)),
            pl.BlockSpec((1, n), lambda k: (0, 0)),
        ],
        out_specs=[
            pl.BlockSpec((n_s, n), lambda k: (0, 0)),
            pl.BlockSpec((n_p, n), lambda k: (0, 0)),
        ],
        out_shape=[
            jax.ShapeDtypeStruct((n_s, n), F32),
            jax.ShapeDtypeStruct((n_p, n), F32),
        ],
        compiler_params=pltpu.CompilerParams(
            dimension_semantics=("arbitrary",), vmem_limit_bytes=VMEM_LIMIT_BYTES),
        name="adaln_modulation",
    )(c_s, c_p, w_ada, b_ada)


def _split_proj(proj):
    width = proj.shape[1] // N_PROJ_GROUPS
    return tuple(proj[:, p * width:(p + 1) * width] for p in range(N_PROJ_GROUPS))


def _weight_blocks(w_hbm, w_vmem):
    rows, cols = w_hbm.shape
    br, bc = WEIGHT_STAGE_SHAPE
    assert rows % br == 0 and cols % bc == 0
    return [(w_hbm.at[r:r + br, c:c + bc], w_vmem.at[r:r + br, c:c + bc])
            for r in range(0, rows, br) for c in range(0, cols, bc)]


def _stage_copy(src, stage_ref, sem_ref, i):
    slot = i % WEIGHT_STAGE_SLOTS
    return pltpu.make_async_copy(src, stage_ref.at[slot], sem_ref.at[slot])


def _load_weights_as_bf16(w_hbm_refs, w_vmem_refs, stage_ref, sem_ref):
    blocks = [blk for w_hbm, w_vmem in zip(w_hbm_refs, w_vmem_refs)
              for blk in _weight_blocks(w_hbm, w_vmem)]
    ahead = WEIGHT_STAGE_SLOTS - 1
    for i in range(min(ahead, len(blocks))):
        _stage_copy(blocks[i][0], stage_ref, sem_ref, i).start()
    for i, (src, dst) in enumerate(blocks):
        if i + ahead < len(blocks):
            _stage_copy(blocks[i + ahead][0], stage_ref, sem_ref, i + ahead).start()
        _stage_copy(src, stage_ref, sem_ref, i).wait()
        dst[...] = stage_ref[i % WEIGHT_STAGE_SLOTS].astype(BF16)


def _alternate(*stages):
    stages = list(stages)
    while stages:
        for stage in list(stages):
            try:
                next(stage)
            except StopIteration:
                stages.remove(stage)


def _mlp_stage(x1_ref, mod, w_up_ref, w_down_ref, g_final_ref, y_ref):
    d = x1_ref.shape[1]
    sh2, sc2, g2 = (mod[:, i * d:(i + 1) * d] for i in range(3, N_MOD))
    h2 = _modulate(x1_ref[...], sh2, sc2)
    ups = [_mlp_up(h2, w_up_ref, 0)]
    yield
    mlp = None
    for j in range(MLP_BLOCKS):
        if j + 1 < MLP_BLOCKS:
            ups.append(_mlp_up(h2, w_up_ref, j + 1))
        part = _mlp_down(ups[j], w_up_ref, w_down_ref, j)
        mlp = part if mlp is None else mlp + part
        if j + 2 < MLP_BLOCKS:
            yield
    y_ref[...] = _final_norm(x1_ref[...], g2, mlp, g_final_ref)
    yield


def _prompt_mix_stage(x_ref, mod, lb_ref, w_in_ref, w_conv_ref, g_onorm_ref, w_out_ref,
                      st_ref, ubuf_ref, mix_ref, x1_ref, rec_ref, conv_ref, is_last):
    tl, d = x_ref.shape
    d_rec = lb_ref.shape[1]
    dk = d_rec // H_REC
    n_chunks = tl // CHUNK
    heads = [slice(hd * dk, (hd + 1) * dk) for hd in range(H_REC)]
    sh1, sc1, g1 = (mod[:, i * d:(i + 1) * d] for i in range(3))

    x = x_ref[...]
    proj = _bdot(_modulate(x, sh1, sc1), w_in_ref[...])
    q, fz, iv, g, gb, gc, hv = _split_proj(proj)
    d_conv = gb.shape[1]
    yield

    lb = _layer_lower_bound(lb_ref, 0)
    row = lax.broadcasted_iota(jnp.int32, (CHUNK, CHUNK), 0)
    col = lax.broadcasted_iota(jnp.int32, (CHUNK, CHUNK), 1)
    causal = row >= col
    f_all, k_all = _gates(fz, lb)
    prod_tiles = _tile_cumprod(f_all)
    q_dec, decay, scores, upd = [], [], [], []
    for c in range(n_chunks):
        rows = slice(c * CHUNK, (c + 1) * CHUNK)
        prod = _chain_tiles(prod_tiles[rows])
        dec = prod[CHUNK - 1:CHUNK, :]
        q_dec.append((q[rows] * prod).astype(BF16))
        k_inv = k_all[rows] / prod
        k_dec = k_inv.astype(BF16)
        k_end = (k_inv * dec).astype(BF16)
        decay.append([jnp.broadcast_to(dec[:, sl], (dk, dk)).T for sl in heads])
        v = iv[rows].astype(BF16)
        scores.append([jnp.where(causal, _dot_nt(q_dec[c][:, sl], k_dec[:, sl]), 0.0).astype(BF16)
                       for sl in heads])
        upd.append([_dot_tn(k_end[:, sl], v[:, sl]) for sl in heads])
    yield

    st = [st_ref[hd] for hd in range(H_REC)]
    o_chunks = []
    for c in range(n_chunks):
        v = iv[c * CHUNK:(c + 1) * CHUNK].astype(BF16)
        o_heads = []
        for hd, sl in enumerate(heads):
            o_heads.append(_bdot(scores[c][hd], v[:, sl])
                           + _bdot(q_dec[c][:, sl], st[hd].astype(BF16)))
            st[hd] = st[hd] * decay[c][hd] + upd[c][hd]
        o_chunks.append(jnp.concatenate(o_heads, axis=-1))
    for hd in range(H_REC):
        st_ref[hd] = st[hd]
    yield

    o_rec = _head_out(jnp.concatenate(o_chunks, axis=0), g, g_onorm_ref[...])
    mix_ref[:, 0:d_rec] = o_rec.astype(BF16)

    u = gc * hv
    ubuf_ref[SUBLANES:SUBLANES + tl, :] = u
    w_conv = w_conv_ref[...]
    y_conv = (w_conv[0:1] * ubuf_ref[SUBLANES - 2:SUBLANES - 2 + tl, :]
              + w_conv[1:2] * ubuf_ref[SUBLANES - 1:SUBLANES - 1 + tl, :]
              + w_conv[2:3] * u)
    mix_ref[:, d_rec:d_rec + d_conv] = (gb * y_conv).astype(BF16)
    tail = u[tl - (CONV_W - 1):tl]
    ubuf_ref[SUBLANES - (CONV_W - 1):SUBLANES, :] = tail
    yield

    x1 = x + g1 * _bdot(mix_ref[...], w_out_ref[...])
    x1_ref[...] = x1

    @pl.when(is_last)
    def _():
        for hd in range(H_REC):
            rec_ref[hd] = st_ref[hd]
        conv_ref[...] = tail


def _prompt_kernel(x_ref, mod_ref, lb_ref, w_in_hbm, w_conv_ref, g_onorm_ref, w_out_hbm,
                   w_up_hbm, w_down_hbm, g_final_ref,
                   y_ref, rec_ref, conv_ref, w_in_out, w_out_out, w_up_out, w_down_out,
                   st_ref, ubuf_ref, mix_ref, x1_ref,
                   w_in_ref, w_out_ref, w_up_ref, w_down_ref, stage_ref, stage_sem, out_sem,
                   *, n_tiles, n_steps):
    s = pl.program_id(0)
    tile = jnp.minimum(s, n_steps - 1)
    l = lax.rem(tile, n_tiles)
    d = x_ref.shape[1]

    w_vmem = (w_in_ref, w_out_ref, w_up_ref, w_down_ref)
    w_outs = (w_in_out, w_out_out, w_up_out, w_down_out)

    def bf16_export(i):
        return pltpu.make_async_copy(w_vmem[i], w_outs[i], out_sem.at[i])

    @pl.when(s == 0)
    def _():
        _load_weights_as_bf16((w_in_hbm, w_out_hbm, w_up_hbm, w_down_hbm), w_vmem,
                              stage_ref, stage_sem)
        for i in range(len(w_vmem)):
            bf16_export(i).start()

    @pl.when(s == n_steps)
    def _():
        for i in range(len(w_vmem)):
            bf16_export(i).wait()

    @pl.when(l == 0)
    def _():
        st_ref[...] = jnp.zeros_like(st_ref)
        ubuf_ref[0:SUBLANES, :] = jnp.zeros((SUBLANES, ubuf_ref.shape[1]), F32)

    def mix_stage():
        mod = mod_ref[pl.ds(lax.div(tile, n_tiles), 1), :]
        return _prompt_mix_stage(x_ref, mod, lb_ref, w_in_ref, w_conv_ref, g_onorm_ref,
                                 w_out_ref, st_ref, ubuf_ref, mix_ref, x1_ref,
                                 rec_ref, conv_ref, l == n_tiles - 1)

    def mlp_stage():
        mod_prev = mod_ref[pl.ds(lax.div(s - 1, n_tiles), 1), :]
        return _mlp_stage(x1_ref, mod_prev, w_up_ref, w_down_ref, g_final_ref, y_ref)

    @pl.when(s == 0)
    def _():
        _alternate(mix_stage())

    @pl.when((s > 0) & (s < n_steps))
    def _():
        _alternate(mix_stage(), mlp_stage())

    @pl.when(s == n_steps)
    def _():
        _alternate(mlp_stage())


def _prompt_layer(x, mod_p, lower_bounds, w_in, w_conv, g_onorm, w_out, w_up, w_down, g_final):
    bsz, seq, d = x.shape
    d_rec = lower_bounds.shape[1]
    dk = d_rec // H_REC
    d_conv = w_conv.shape[1]
    tl = PROMPT_TILE
    n_tiles = seq // tl
    n_steps = bsz * n_tiles
    assert seq % tl == 0 and tl % CHUNK == 0
    mats = (w_in, w_out, w_up, w_down)
    hbm = pl.BlockSpec(memory_space=pl.ANY)

    def mix_tile(s):
        t = jnp.minimum(s, n_steps - 1)
        return lax.div(t, n_tiles), lax.rem(t, n_tiles)

    def mlp_tile(s):
        t = jnp.maximum(s - 1, 0)
        return lax.div(t, n_tiles), lax.rem(t, n_tiles)

    return pl.pallas_call(
        functools.partial(_prompt_kernel, n_tiles=n_tiles, n_steps=n_steps),
        grid=(n_steps + 1,),
        in_specs=[
            pl.BlockSpec((None, tl, d), lambda s: (*mix_tile(s), 0)),
            _const_spec(mod_p.shape),
            _const_spec(lower_bounds.shape),
            hbm,
            _const_spec(w_conv.shape),
            _const_spec(g_onorm.shape),
            hbm,
            hbm,
            hbm,
            _const_spec(g_final.shape),
        ],
        out_specs=[
            pl.BlockSpec((None, tl, d), lambda s: (*mlp_tile(s), 0)),
            pl.BlockSpec((None, H_REC, dk, dk), lambda s: (mix_tile(s)[0], 0, 0, 0)),
            pl.BlockSpec((None, CONV_W - 1, d_conv), lambda s: (mix_tile(s)[0], 0, 0)),
        ] + [hbm] * len(mats),
        out_shape=[
            jax.ShapeDtypeStruct((bsz, seq, d), F32),
            jax.ShapeDtypeStruct((bsz, H_REC, dk, dk), F32),
            jax.ShapeDtypeStruct((bsz, CONV_W - 1, d_conv), F32),
        ] + [jax.ShapeDtypeStruct(w.shape, BF16) for w in mats],
        scratch_shapes=[
            pltpu.VMEM((H_REC, dk, dk), F32),
            pltpu.VMEM((SUBLANES + tl, d_conv), F32),
            pltpu.VMEM((tl, d_rec + d_conv), BF16),
            pltpu.VMEM((tl, d), F32),
        ] + [pltpu.VMEM(w.shape, BF16) for w in mats] + [
            pltpu.VMEM((WEIGHT_STAGE_SLOTS, *WEIGHT_STAGE_SHAPE), F32),
            pltpu.SemaphoreType.DMA((WEIGHT_STAGE_SLOTS,)),
            pltpu.SemaphoreType.DMA((len(mats),)),
        ],
        compiler_params=pltpu.CompilerParams(
            dimension_semantics=("arbitrary",), vmem_limit_bytes=VMEM_LIMIT_BYTES),
        name="prompt_layer",
    )(x, mod_p, lower_bounds, w_in, w_conv, g_onorm, w_out, w_up, w_down, g_final)


def _sample_kernel(x_ref, mod_ref, lb_ref, w_in_ref, w_conv_ref, g_onorm_ref, w_out_ref,
                   w_up_ref, w_down_ref, g_final_ref, rec_in_ref, conv_in_ref,
                   y_ref, rec_ref, conv_ref,
                   modx_ref, cbx_ref, ubuf_ref, mix_ref):
    tb, seq, d = x_ref.shape
    rows = tb * seq
    d_rec = lb_ref.shape[1]
    dk = d_rec // H_REC
    d_conv = conv_in_ref.shape[2]

    for b in range(tb):
        r = slice(b * seq, (b + 1) * seq)
        modx_ref[r, :] = jnp.broadcast_to(mod_ref[b:b + 1, :], (seq, mod_ref.shape[1]))
        for j in range(CONV_W - 1):
            cbx_ref[j, r, :] = jnp.broadcast_to(conv_in_ref[b, j:j + 1, :], (seq, d_conv))

    sh1, sc1, g1, sh2, sc2, g2 = (modx_ref[:, i * d:(i + 1) * d] for i in range(N_MOD))

    x = x_ref[...].reshape(rows, d)
    proj = _bdot(_modulate(x, sh1, sc1), w_in_ref[...])
    q, fz, iv, g, gb, gc, hv = _split_proj(proj)

    lb = _layer_lower_bound(lb_ref, 0)
    row = lax.broadcasted_iota(jnp.int32, (rows, rows), 0)
    col = lax.broadcasted_iota(jnp.int32, (rows, rows), 1)
    causal = ((row // seq) == (col // seq)) & (row >= col)
    f, k = _gates(fz, lb)
    prod = _tile_cumprod(f)
    decay = _tile_last(prod)
    q_dec = (q * prod).astype(BF16)
    k_inv = k / prod
    k_dec = k_inv.astype(BF16)
    k_end = k_inv * decay
    vb = iv.astype(BF16)

    zeros_blk = jnp.zeros((seq, dk), F32)
    heads = [slice(hd * dk, (hd + 1) * dk) for hd in range(H_REC)]

    def update_states(elements):
        for b in elements:
            r = slice(b * seq, (b + 1) * seq)
            for ha in range(0, H_REC, 2):
                sa, sb = heads[ha], heads[ha + 1]
                lhs = jnp.concatenate([k_end[r, sa], k_end[r, sb]], axis=0)
                rhs = jnp.concatenate([jnp.concatenate([iv[r, sa], zeros_blk], axis=1),
                                       jnp.concatenate([zeros_blk, iv[r, sb]], axis=1)], axis=0)
                both = _dot_tn(lhs.astype(BF16), rhs.astype(BF16))
                for hd, sl, upd in ((ha, sa, both[:, :dk]), (ha + 1, sb, both[:, dk:])):
                    dcol = jnp.broadcast_to(decay[b * seq:b * seq + 1, sl], (dk, dk)).T
                    rec_ref[b, hd] = dcol * rec_in_ref[b, hd] + upd

    scores = [jnp.where(causal, _dot_nt(q_dec[:, sl], k_dec[:, sl]), 0.0).astype(BF16)
              for sl in heads]
    o_inter = jnp.concatenate(
        [jnp.concatenate([_bdot(q_dec[b * seq:(b + 1) * seq, sl], rec_in_ref[b, hd].astype(BF16))
                          for hd, sl in enumerate(heads)], axis=-1)
         for b in range(tb)], axis=0)
    o_intra = jnp.concatenate([_bdot(scores[hd], vb[:, sl]) for hd, sl in enumerate(heads)],
                              axis=-1)
    o_rec = _head_out(o_intra + o_inter, g, g_onorm_ref[...])
    mix_ref[:, 0:d_rec] = o_rec.astype(BF16)

    u = gc * hv
    ubuf_ref[SUBLANES:SUBLANES + rows, :] = u
    tok = _group_pos((rows, d_conv), seq)
    u_m1 = jnp.where(tok >= 1, ubuf_ref[SUBLANES - 1:SUBLANES - 1 + rows, :], cbx_ref[1])
    u_m2 = jnp.where(tok >= 2, ubuf_ref[SUBLANES - 2:SUBLANES - 2 + rows, :],
                     jnp.where(tok == 1, cbx_ref[1], cbx_ref[0]))
    w_conv = w_conv_ref[...]
    y_conv = w_conv[0:1] * u_m2 + w_conv[1:2] * u_m1 + w_conv[2:3] * u
    mix_ref[:, d_rec:d_rec + d_conv] = (gb * y_conv).astype(BF16)
    for b in range(tb):
        conv_ref[b] = u[(b + 1) * seq - (CONV_W - 1):(b + 1) * seq]

    x1 = x + g1 * _bdot(mix_ref[...], w_out_ref[...])
    update_states(range(tb // 2))
    h2 = _modulate(x1, sh2, sc2)
    ups, mlp = [_mlp_up(h2, w_up_ref, 0)], None
    for j in range(MLP_BLOCKS):
        if j + 1 < MLP_BLOCKS:
            ups.append(_mlp_up(h2, w_up_ref, j + 1))
        part = _mlp_down(ups[j], w_up_ref, w_down_ref, j)
        mlp = part if mlp is None else mlp + part
    update_states(range(tb // 2, tb))
    y_ref[...] = _final_norm(x1, g2, mlp, g_final_ref).reshape(tb, seq, d)


def _sample_layer(x, mod, lower_bounds, w_in, w_conv, g_onorm, w_out, w_up, w_down,
                  g_final, rec_in, conv_in):
    bsz, seq, d = x.shape
    d_rec = lower_bounds.shape[1]
    dk = d_rec // H_REC
    d_conv = w_conv.shape[1]
    tb = SAMPLE_TILE_B
    rows = tb * seq
    assert bsz % tb == 0 and seq == SUBLANES and seq >= CONV_W
    return pl.pallas_call(
        _sample_kernel,
        grid=(bsz // tb,),
        in_specs=[
            pl.BlockSpec((tb, seq, d), lambda i: (i, 0, 0)),
            pl.BlockSpec((tb, N_MOD * d), lambda i: (i, 0)),
            _const_spec(lower_bounds.shape),
            _const_spec(w_in.shape),
            _const_spec(w_conv.shape),
            _const_spec(g_onorm.shape),
            _const_spec(w_out.shape),
            _const_spec(w_up.shape),
            _const_spec(w_down.shape),
            _const_spec(g_final.shape),
            pl.BlockSpec((tb, H_REC, dk, dk), lambda i: (i, 0, 0, 0)),
            pl.BlockSpec((tb, CONV_W - 1, d_conv), lambda i: (i, 0, 0)),
        ],
        out_specs=[
            pl.BlockSpec((tb, seq, d), lambda i: (i, 0, 0)),
            pl.BlockSpec((tb, H_REC, dk, dk), lambda i: (i, 0, 0, 0)),
            pl.BlockSpec((tb, CONV_W - 1, d_conv), lambda i: (i, 0, 0)),
        ],
        out_shape=[
            jax.ShapeDtypeStruct((bsz, seq, d), F32),
            jax.ShapeDtypeStruct((bsz, H_REC, dk, dk), F32),
            jax.ShapeDtypeStruct((bsz, CONV_W - 1, d_conv), F32),
        ],
        scratch_shapes=[
            pltpu.VMEM((rows, N_MOD * d), F32),
            pltpu.VMEM((CONV_W - 1, rows, d_conv), F32),
            pltpu.VMEM((SUBLANES + rows, d_conv), F32),
            pltpu.VMEM((rows, d_rec + d_conv), BF16),
        ],
        compiler_params=pltpu.CompilerParams(
            dimension_semantics=("arbitrary",), vmem_limit_bytes=VMEM_LIMIT_BYTES),
        name="sample_layer",
    )(x, mod, lower_bounds, w_in, w_conv, g_onorm, w_out, w_up, w_down, g_final, rec_in, conv_in)


def kernel(x_prompt, x_sample, state_rec, state_conv, c_prompt, c_sample, lower_bounds, w_ada,
           b_ada, w_in, w_conv, g_onorm, w_out, w_up, w_down, g_final):
    depth = w_in.shape[0]
    assert depth == 1, "single-layer trunk"

    mod_s, mod_p = _modulation(c_sample, c_prompt, w_ada[0], b_ada)

    y_p, rec_p, conv_p, w_in_b, w_out_b, w_up_b, w_down_b = _prompt_layer(
        x_prompt, mod_p, lower_bounds, w_in[0], w_conv[0], g_onorm, w_out[0], w_up[0], w_down[0],
        g_final)
    y_s, rec_s, conv_s = _sample_layer(
        x_sample, mod_s, lower_bounds, w_in_b, w_conv[0], g_onorm, w_out_b, w_up_b, w_down_b,
        g_final, state_rec[0], state_conv[0])
    return y_p, y_s, rec_p[None], conv_p[None], rec_s[None], conv_s[None]
```

```python
import functools

import jax
import jax.numpy as jnp
from jax import lax
from jax.experimental import pallas as pl
from jax.experimental.pallas import tpu as pltpu

F32 = jnp.float32
BF16 = jnp.bfloat16

EPS = 1e-6
H_REC = 4
CONV_W = 3
N_MOD = 6
N_PROJ_GROUPS = 7
CHUNK = 64

SUBLANES = 8
VMEM_LIMIT_BYTES = 56 * 1024 * 1024

PROMPT_TILE = 512
SAMPLE_TILE_B = 16
MOD_TILE_K = 256
MLP_BLOCKS = 4
WEIGHT_STAGE_SHAPE = (512, 512)
WEIGHT_STAGE_SLOTS = 6


def _const_spec(shape):
    zeros = (0,) * len(shape)
    return pl.BlockSpec(shape, lambda *_: zeros, pipeline_mode=pl.Buffered(1))


def _rms(x):
    return x * lax.rsqrt(jnp.mean(x * x, axis=-1, keepdims=True) + EPS)


def _silu(x):
    return x * jax.nn.sigmoid(x)


def _bdot(a, b):
    return jnp.dot(a, b, preferred_element_type=F32)


def _dot_nt(a, b):
    return lax.dot_general(a, b, (((1,), (1,)), ((), ())), preferred_element_type=F32)


def _dot_tn(a, b):
    return lax.dot_general(a, b, (((0,), (0,)), ((), ())), preferred_element_type=F32)


def _group_pos(shape, group):
    return lax.broadcasted_iota(jnp.int32, shape, 0) % group


def _tile_cumprod(x):
    rows, n = x.shape
    x = x.reshape(rows // SUBLANES, SUBLANES, n)
    pos = lax.broadcasted_iota(jnp.int32, x.shape, 1)
    step = 1
    while step < SUBLANES:
        x = x * jnp.where(pos >= step, pltpu.roll(x, step, axis=1), 1.0)
        step *= 2
    return x.reshape(rows, n)


def _tile_last(x):
    rows, n = x.shape
    x = x.reshape(rows // SUBLANES, SUBLANES, n)
    return jnp.broadcast_to(x[:, SUBLANES - 1:, :], x.shape).reshape(rows, n)


def _chain_tiles(y):
    tiles, carry = [], None
    for r in range(0, y.shape[0], SUBLANES):
        t = y[r:r + SUBLANES]
        if carry is not None:
            t = t * carry
        tiles.append(t)
        carry = t[SUBLANES - 1:, :]
    return jnp.concatenate(tiles, axis=0)


def _layer_lower_bound(lb_ref, layer):
    lb = lb_ref[...]
    e = jnp.exp(lb - jnp.max(lb, axis=0, keepdims=True))
    sm = e / jnp.sum(e, axis=0, keepdims=True)
    return jnp.sum(sm[: layer + 1], axis=0, keepdims=True)


def _gates(fz, lb):
    sig = jax.nn.sigmoid(fz)
    f = lb + (1.0 - lb) * sig
    k = (1.0 - lb) * (1.0 - sig)
    return f, k


def _head_out(o, g, g_onorm):
    dv = o.shape[-1] // H_REC
    pieces = []
    for h in range(H_REC):
        sl = slice(h * dv, (h + 1) * dv)
        pieces.append(_rms(o[:, sl]))
    return (jnp.concatenate(pieces, axis=-1) * g_onorm) * _silu(g)


def _modulate(x, shift, scale):
    return (_rms(x) * (1.0 + scale) + shift).astype(BF16)


def _mlp_cols(w_up_ref, j):
    blk = w_up_ref.shape[1] // MLP_BLOCKS
    return slice(j * blk, (j + 1) * blk)


def _mlp_up(h2, w_up_ref, j):
    up = jnp.maximum(_bdot(h2, w_up_ref[:, _mlp_cols(w_up_ref, j)]), 0.0)
    return (up * up).astype(BF16)


def _mlp_down(up, w_up_ref, w_down_ref, j):
    return _bdot(up, w_down_ref[_mlp_cols(w_up_ref, j), :])


def _final_norm(x, g2, mlp, g_final_ref):
    return _rms(x + g2 * mlp) * g_final_ref[...]


def _mod_kernel(c_s_ref, c_p_ref, w_ref, b_ref, o_s_ref, o_p_ref):
    k = pl.program_id(0)
    n_s = c_s_ref.shape[0]
    kb = w_ref.shape[0]
    cols = pl.ds(pl.multiple_of(k * kb, kb), kb)
    c = jnp.concatenate([c_s_ref[:, cols], c_p_ref[:, cols]], axis=0)
    part = _bdot(_silu(c).astype(BF16), w_ref[...].astype(BF16))

    @pl.when(k == 0)
    def _():
        o_s_ref[...] = part[:n_s] + b_ref[...]
        o_p_ref[...] = part[n_s:] + b_ref[...]

    @pl.when(k > 0)
    def _():
        o_s_ref[...] += part[:n_s]
        o_p_ref[...] += part[n_s:]


def _modulation(c_s, c_p, w_ada, b_ada):
    (n_s, d), n_p = c_s.shape, c_p.shape[0]
    n = w_ada.shape[1]
    assert n_s % SUBLANES == 0 and n_p % SUBLANES == 0 and d % MOD_TILE_K == 0
    return pl.pallas_call(
        _mod_kernel,
        grid=(d // MOD_TILE_K,),
        in_specs=[
            pl.BlockSpec((n_s, d), lambda k: (0, 0)),
            pl.BlockSpec((n_p, d), lambda k: (0, 0)),
            pl.BlockSpec((MOD_TILE_K, n), lambda k: (k, 0)),
            pl.BlockSpec((1, n), lambda k: (0, 0)),
        ],
        out_specs=[
            pl.BlockSpec((n_s, n), lambda k: (0, 0)),
            pl.BlockSpec((n_p, n), lambda k: (0, 0)),
        ],
        out_shape=[
            jax.ShapeDtypeStruct((n_s, n), F32),
            jax.ShapeDtypeStruct((n_p, n), F32),
        ],
        compiler_params=pltpu.CompilerParams(
            dimension_semantics=("arbitrary",), vmem_limit_bytes=VMEM_LIMIT_BYTES),
        name="adaln_modulation",
    )(c_s, c_p, w_ada, b_ada)


def _split_proj(proj):
    width = proj.shape[1] // N_PROJ_GROUPS
    return tuple(proj[:, p * width:(p + 1) * width] for p in range(N_PROJ_GROUPS))


def _weight_blocks(w_hbm, w_vmem):
    rows, cols = w_hbm.shape
    br, bc = WEIGHT_STAGE_SHAPE
    assert rows % br == 0 and cols % bc == 0
    return [(w_hbm.at[r:r + br, c:c + bc], w_vmem.at[r:r + br, c:c + bc])
            for r in range(0, rows, br) for c in range(0, cols, bc)]


def _stage_copy(src, stage_ref, sem_ref, i):
    slot = i % WEIGHT_STAGE_SLOTS
    return pltpu.make_async_copy(src, stage_ref.at[slot], sem_ref.at[slot])


def _load_weights_as_bf16(w_hbm_refs, w_vmem_refs, stage_ref, sem_ref):
    blocks = [blk for w_hbm, w_vmem in zip(w_hbm_refs, w_vmem_refs)
              for blk in _weight_blocks(w_hbm, w_vmem)]
    ahead = WEIGHT_STAGE_SLOTS - 1
    for i in range(min(ahead, len(blocks))):
        _stage_copy(blocks[i][0], stage_ref, sem_ref, i).start(priority=i % 2)
    for i, (src, dst) in enumerate(blocks):
        if i + ahead < len(blocks):
            _stage_copy(blocks[i + ahead][0], stage_ref, sem_ref, i + ahead).start(
                priority=(i + ahead) % 2)
        _stage_copy(src, stage_ref, sem_ref, i).wait()
        dst[...] = stage_ref[i % WEIGHT_STAGE_SLOTS].astype(BF16)


def _alternate(*stages):
    stages = list(stages)
    while stages:
        for stage in list(stages):
            try:
                next(stage)
            except StopIteration:
                stages.remove(stage)


def _mlp_stage(x1_ref, mod, w_up_ref, w_down_ref, g_final_ref, y_ref):
    d = x1_ref.shape[1]
    sh2, sc2, g2 = (mod[:, i * d:(i + 1) * d] for i in range(3, N_MOD))
    h2 = _modulate(x1_ref[...], sh2, sc2)
    ups = [_mlp_up(h2, w_up_ref, 0)]
    yield
    mlp = None
    for j in range(MLP_BLOCKS):
        if j + 1 < MLP_BLOCKS:
            ups.append(_mlp_up(h2, w_up_ref, j + 1))
        part = _mlp_down(ups[j], w_up_ref, w_down_ref, j)
        mlp = part if mlp is None else mlp + part
        if j + 2 < MLP_BLOCKS:
            yield
    y_ref[...] = _final_norm(x1_ref[...], g2, mlp, g_final_ref)
    yield


def _prompt_mix_stage(x_ref, mod, lb_ref, w_in_ref, w_conv_ref, g_onorm_ref, w_out_ref,
                      st_ref, ubuf_ref, mix_ref, x1_ref, rec_ref, conv_ref, is_last):
    tl, d = x_ref.shape
    d_rec = lb_ref.shape[1]
    dk = d_rec // H_REC
    n_chunks = tl // CHUNK
    heads = [slice(hd * dk, (hd + 1) * dk) for hd in range(H_REC)]
    sh1, sc1, g1 = (mod[:, i * d:(i + 1) * d] for i in range(3))

    x = x_ref[...]
    proj = _bdot(_modulate(x, sh1, sc1), w_in_ref[...])
    q, fz, iv, g, gb, gc, hv = _split_proj(proj)
    d_conv = gb.shape[1]
    yield

    lb = _layer_lower_bound(lb_ref, 0)
    row = lax.broadcasted_iota(jnp.int32, (CHUNK, CHUNK), 0)
    col = lax.broadcasted_iota(jnp.int32, (CHUNK, CHUNK), 1)
    causal = row >= col
    f_all, k_all = _gates(fz, lb)
    prod_tiles = _tile_cumprod(f_all)
    q_dec, decay, scores, upd = [], [], [], []
    for c in range(n_chunks):
        rows = slice(c * CHUNK, (c + 1) * CHUNK)
        prod = _chain_tiles(prod_tiles[rows])
        dec = prod[CHUNK - 1:CHUNK, :]
        q_dec.append((q[rows] * prod).astype(BF16))
        k_inv = k_all[rows] / prod
        k_dec = k_inv.astype(BF16)
        k_end = (k_inv * dec).astype(BF16)
        decay.append([jnp.broadcast_to(dec[:, sl], (dk, dk)).T for sl in heads])
        v = iv[rows].astype(BF16)
        scores.append([jnp.where(causal, _dot_nt(q_dec[c][:, sl], k_dec[:, sl]), 0.0).astype(BF16)
                       for sl in heads])
        upd.append([_dot_tn(k_end[:, sl], v[:, sl]) for sl in heads])
    yield

    st = [st_ref[hd] for hd in range(H_REC)]
    o_chunks = []
    for c in range(n_chunks):
        v = iv[c * CHUNK:(c + 1) * CHUNK].astype(BF16)
        o_heads = []
        for hd, sl in enumerate(heads):
            o_heads.append(_bdot(scores[c][hd], v[:, sl])
                           + _bdot(q_dec[c][:, sl], st[hd].astype(BF16)))
            st[hd] = st[hd] * decay[c][hd] + upd[c][hd]
        o_chunks.append(jnp.concatenate(o_heads, axis=-1))
    for hd in range(H_REC):
        st_ref[hd] = st[hd]
    yield

    o_rec = _head_out(jnp.concatenate(o_chunks, axis=0), g, g_onorm_ref[...])
    mix_ref[:, 0:d_rec] = o_rec.astype(BF16)

    u = gc * hv
    ubuf_ref[SUBLANES:SUBLANES + tl, :] = u
    w_conv = w_conv_ref[...]
    y_conv = (w_conv[0:1] * ubuf_ref[SUBLANES - 2:SUBLANES - 2 + tl, :]
              + w_conv[1:2] * ubuf_ref[SUBLANES - 1:SUBLANES - 1 + tl, :]
              + w_conv[2:3] * u)
    mix_ref[:, d_rec:d_rec + d_conv] = (gb * y_conv).astype(BF16)
    tail = u[tl - (CONV_W - 1):tl]
    ubuf_ref[SUBLANES - (CONV_W - 1):SUBLANES, :] = tail
    yield

    x1 = x + g1 * _bdot(mix_ref[...], w_out_ref[...])
    x1_ref[...] = x1

    @pl.when(is_last)
    def _():
        for hd in range(H_REC):
            rec_ref[hd] = st_ref[hd]
        conv_ref[...] = tail


def _prompt_kernel(x_ref, mod_ref, lb_ref, w_in_hbm, w_conv_ref, g_onorm_ref, w_out_hbm,
                   w_up_hbm, w_down_hbm, g_final_ref,
                   y_ref, rec_ref, conv_ref, w_in_out, w_out_out, w_up_out, w_down_out,
                   st_ref, ubuf_ref, mix_ref, x1_ref,
                   w_in_ref, w_out_ref, w_up_ref, w_down_ref, stage_ref, stage_sem, out_sem,
                   *, n_tiles, n_steps):
    s = pl.program_id(0)
    tile = jnp.minimum(s, n_steps - 1)
    l = lax.rem(tile, n_tiles)
    d = x_ref.shape[1]

    w_vmem = (w_in_ref, w_out_ref, w_up_ref, w_down_ref)
    w_outs = (w_in_out, w_out_out, w_up_out, w_down_out)

    def bf16_export(i):
        return pltpu.make_async_copy(w_vmem[i], w_outs[i], out_sem.at[i])

    @pl.when(s == 0)
    def _():
        _load_weights_as_bf16((w_in_hbm, w_out_hbm, w_up_hbm, w_down_hbm), w_vmem,
                              stage_ref, stage_sem)
        for i in range(len(w_vmem)):
            bf16_export(i).start()

    @pl.when(s == n_steps)
    def _():
        for i in range(len(w_vmem)):
            bf16_export(i).wait()

    @pl.when(l == 0)
    def _():
        st_ref[...] = jnp.zeros_like(st_ref)
        ubuf_ref[0:SUBLANES, :] = jnp.zeros((SUBLANES, ubuf_ref.shape[1]), F32)

    def mix_stage():
        mod = mod_ref[pl.ds(lax.div(tile, n_tiles), 1), :]
        return _prompt_mix_stage(x_ref, mod, lb_ref, w_in_ref, w_conv_ref, g_onorm_ref,
                                 w_out_ref, st_ref, ubuf_ref, mix_ref, x1_ref,
                                 rec_ref, conv_ref, l == n_tiles - 1)

    def mlp_stage():
        mod_prev = mod_ref[pl.ds(lax.div(s - 1, n_tiles), 1), :]
        return _mlp_stage(x1_ref, mod_prev, w_up_ref, w_down_ref, g_final_ref, y_ref)

    @pl.when(s == 0)
    def _():
        _alternate(mix_stage())

    @pl.when((s > 0) & (s < n_steps))
    def _():
        _alternate(mix_stage(), mlp_stage())

    @pl.when(s == n_steps)
    def _():
        _alternate(mlp_stage())


def _prompt_layer(x, mod_p, lower_bounds, w_in, w_conv, g_onorm, w_out, w_up, w_down, g_final):
    bsz, seq, d = x.shape
    d_rec = lower_bounds.shape[1]
    dk = d_rec // H_REC
    d_conv = w_conv.shape[1]
    tl = PROMPT_TILE
    n_tiles = seq // tl
    n_steps = bsz * n_tiles
    assert seq % tl == 0 and tl % CHUNK == 0
    mats = (w_in, w_out, w_up, w_down)
    hbm = pl.BlockSpec(memory_space=pl.ANY)

    def mix_tile(s):
        t = jnp.minimum(s, n_steps - 1)
        return lax.div(t, n_tiles), lax.rem(t, n_tiles)

    def mlp_tile(s):
        t = jnp.maximum(s - 1, 0)
        return lax.div(t, n_tiles), lax.rem(t, n_tiles)

    return pl.pallas_call(
        functools.partial(_prompt_kernel, n_tiles=n_tiles, n_steps=n_steps),
        grid=(n_steps + 1,),
        in_specs=[
            pl.BlockSpec((None, tl, d), lambda s: (*mix_tile(s), 0)),
            _const_spec(mod_p.shape),
            _const_spec(lower_bounds.shape),
            hbm,
            _const_spec(w_conv.shape),
            _const_spec(g_onorm.shape),
            hbm,
            hbm,
            hbm,
            _const_spec(g_final.shape),
        ],
        out_specs=[
            pl.BlockSpec((None, tl, d), lambda s: (*mlp_tile(s), 0)),
            pl.BlockSpec((None, H_REC, dk, dk), lambda s: (mix_tile(s)[0], 0, 0, 0)),
            pl.BlockSpec((None, CONV_W - 1, d_conv), lambda s: (mix_tile(s)[0], 0, 0)),
        ] + [hbm] * len(mats),
        out_shape=[
            jax.ShapeDtypeStruct((bsz, seq, d), F32),
            jax.ShapeDtypeStruct((bsz, H_REC, dk, dk), F32),
            jax.ShapeDtypeStruct((bsz, CONV_W - 1, d_conv), F32),
        ] + [jax.ShapeDtypeStruct(w.shape, BF16) for w in mats],
        scratch_shapes=[
            pltpu.VMEM((H_REC, dk, dk), F32),
            pltpu.VMEM((SUBLANES + tl, d_conv), F32),
            pltpu.VMEM((tl, d_rec + d_conv), BF16),
            pltpu.VMEM((tl, d), F32),
        ] + [pltpu.VMEM(w.shape, BF16) for w in mats] + [
            pltpu.VMEM((WEIGHT_STAGE_SLOTS, *WEIGHT_STAGE_SHAPE), F32),
            pltpu.SemaphoreType.DMA((WEIGHT_STAGE_SLOTS,)),
            pltpu.SemaphoreType.DMA((len(mats),)),
        ],
        compiler_params=pltpu.CompilerParams(
            dimension_semantics=("arbitrary",), vmem_limit_bytes=VMEM_LIMIT_BYTES),
        name="prompt_layer",
    )(x, mod_p, lower_bounds, w_in, w_conv, g_onorm, w_out, w_up, w_down, g_final)


def _sample_kernel(x_ref, mod_ref, lb_ref, w_in_ref, w_conv_ref, g_onorm_ref, w_out_ref,
                   w_up_ref, w_down_ref, g_final_ref, rec_in_ref, conv_in_ref,
                   y_ref, rec_ref, conv_ref,
                   modx_ref, cbx_ref, ubuf_ref, mix_ref):
    tb, seq, d = x_ref.shape
    rows = tb * seq
    d_rec = lb_ref.shape[1]
    dk = d_rec // H_REC
    d_conv = conv_in_ref.shape[2]

    for b in range(tb):
        r = slice(b * seq, (b + 1) * seq)
        modx_ref[r, :] = jnp.broadcast_to(mod_ref[b:b + 1, :], (seq, mod_ref.shape[1]))
        for j in range(CONV_W - 1):
            cbx_ref[j, r, :] = jnp.broadcast_to(conv_in_ref[b, j:j + 1, :], (seq, d_conv))

    sh1, sc1, g1, sh2, sc2, g2 = (modx_ref[:, i * d:(i + 1) * d] for i in range(N_MOD))

    x = x_ref[...].reshape(rows, d)
    proj = _bdot(_modulate(x, sh1, sc1), w_in_ref[...])
    q, fz, iv, g, gb, gc, hv = _split_proj(proj)

    lb = _layer_lower_bound(lb_ref, 0)
    row = lax.broadcasted_iota(jnp.int32, (rows, rows), 0)
    col = lax.broadcasted_iota(jnp.int32, (rows, rows), 1)
    causal = ((row // seq) == (col // seq)) & (row >= col)
    f, k = _gates(fz, lb)
    prod = _tile_cumprod(f)
    decay = _tile_last(prod)
    q_dec = (q * prod).astype(BF16)
    k_inv = k / prod
    k_dec = k_inv.astype(BF16)
    k_end = k_inv * decay
    vb = iv.astype(BF16)

    zeros_blk = jnp.zeros((seq, dk), F32)
    heads = [slice(hd * dk, (hd + 1) * dk) for hd in range(H_REC)]

    def update_states(elements):
        for b in elements:
            r = slice(b * seq, (b + 1) * seq)
            for ha in range(0, H_REC, 2):
                sa, sb = heads[ha], heads[ha + 1]
                lhs = jnp.concatenate([k_end[r, sa], k_end[r, sb]], axis=0)
                rhs = jnp.concatenate([jnp.concatenate([iv[r, sa], zeros_blk], axis=1),
                                       jnp.concatenate([zeros_blk, iv[r, sb]], axis=1)], axis=0)
                both = _dot_tn(lhs.astype(BF16), rhs.astype(BF16))
                for hd, sl, upd in ((ha, sa, both[:, :dk]), (ha + 1, sb, both[:, dk:])):
                    dcol = jnp.broadcast_to(decay[b * seq:b * seq + 1, sl], (dk, dk)).T
                    rec_ref[b, hd] = dcol * rec_in_ref[b, hd] + upd

    scores = [jnp.where(causal, _dot_nt(q_dec[:, sl], k_dec[:, sl]), 0.0).astype(BF16)
              for sl in heads]
    o_inter = jnp.concatenate(
        [jnp.concatenate([_bdot(q_dec[b * seq:(b + 1) * seq, sl], rec_in_ref[b, hd].astype(BF16))
                          for hd, sl in enumerate(heads)], axis=-1)
         for b in range(tb)], axis=0)
    o_intra = jnp.concatenate([_bdot(scores[hd], vb[:, sl]) for hd, sl in enumerate(heads)],
                              axis=-1)
    o_rec = _head_out(o_intra + o_inter, g, g_onorm_ref[...])
    mix_ref[:, 0:d_rec] = o_rec.astype(BF16)

    u = gc * hv
    ubuf_ref[SUBLANES:SUBLANES + rows, :] = u
    tok = _group_pos((rows, d_conv), seq)
    u_m1 = jnp.where(tok >= 1, ubuf_ref[SUBLANES - 1:SUBLANES - 1 + rows, :], cbx_ref[1])
    u_m2 = jnp.where(tok >= 2, ubuf_ref[SUBLANES - 2:SUBLANES - 2 + rows, :],
                     jnp.where(tok == 1, cbx_ref[1], cbx_ref[0]))
    w_conv = w_conv_ref[...]
    y_conv = w_conv[0:1] * u_m2 + w_conv[1:2] * u_m1 + w_conv[2:3] * u
    mix_ref[:, d_rec:d_rec + d_conv] = (gb * y_conv).astype(BF16)
    for b in range(tb):
        conv_ref[b] = u[(b + 1) * seq - (CONV_W - 1):(b + 1) * seq]

    x1 = x + g1 * _bdot(mix_ref[...], w_out_ref[...])
    update_states(range(tb // 2))
    h2 = _modulate(x1, sh2, sc2)
    ups, mlp = [_mlp_up(h2, w_up_ref, 0)], None
    for j in range(MLP_BLOCKS):
        if j + 1 < MLP_BLOCKS:
            ups.append(_mlp_up(h2, w_up_ref, j + 1))
        part = _mlp_down(ups[j], w_up_ref, w_down_ref, j)
        mlp = part if mlp is None else mlp + part
    update_states(range(tb // 2, tb))
    y_ref[...] = _final_norm(x1, g2, mlp, g_final_ref).reshape(tb, seq, d)


def _sample_layer(x, mod, lower_bounds, w_in, w_conv, g_onorm, w_out, w_up, w_down,
                  g_final, rec_in, conv_in):
    bsz, seq, d = x.shape
    d_rec = lower_bounds.shape[1]
    dk = d_rec // H_REC
    d_conv = w_conv.shape[1]
    tb = SAMPLE_TILE_B
    rows = tb * seq
    assert bsz % tb == 0 and seq == SUBLANES and seq >= CONV_W
    return pl.pallas_call(
        _sample_kernel,
        grid=(bsz // tb,),
        in_specs=[
            pl.BlockSpec((tb, seq, d), lambda i: (i, 0, 0)),
            pl.BlockSpec((tb, N_MOD * d), lambda i: (i, 0)),
            _const_spec(lower_bounds.shape),
            _const_spec(w_in.shape),
            _const_spec(w_conv.shape),
            _const_spec(g_onorm.shape),
            _const_spec(w_out.shape),
            _const_spec(w_up.shape),
            _const_spec(w_down.shape),
            _const_spec(g_final.shape),
            pl.BlockSpec((tb, H_REC, dk, dk), lambda i: (i, 0, 0, 0)),
            pl.BlockSpec((tb, CONV_W - 1, d_conv), lambda i: (i, 0, 0)),
        ],
        out_specs=[
            pl.BlockSpec((tb, seq, d), lambda i: (i, 0, 0)),
            pl.BlockSpec((tb, H_REC, dk, dk), lambda i: (i, 0, 0, 0)),
            pl.BlockSpec((tb, CONV_W - 1, d_conv), lambda i: (i, 0, 0)),
        ],
        out_shape=[
            jax.ShapeDtypeStruct((bsz, seq, d), F32),
            jax.ShapeDtypeStruct((bsz, H_REC, dk, dk), F32),
            jax.ShapeDtypeStruct((bsz, CONV_W - 1, d_conv), F32),
        ],
        scratch_shapes=[
            pltpu.VMEM((rows, N_MOD * d), F32),
            pltpu.VMEM((CONV_W - 1, rows, d_conv), F32),
            pltpu.VMEM((SUBLANES + rows, d_conv), F32),
            pltpu.VMEM((rows, d_rec + d_conv), BF16),
        ],
        compiler_params=pltpu.CompilerParams(
            dimension_semantics=("arbitrary",), vmem_limit_bytes=VMEM_LIMIT_BYTES),
        name="sample_layer",
    )(x, mod, lower_bounds, w_in, w_conv, g_onorm, w_out, w_up, w_down, g_final, rec_in, conv_in)


def kernel(x_prompt, x_sample, state_rec, state_conv, c_prompt, c_sample, lower_bounds, w_ada,
           b_ada, w_in, w_conv, g_onorm, w_out, w_up, w_down, g_final):
    depth = w_in.shape[0]
    assert depth == 1, "single-layer trunk"

    mod_s, mod_p = _modulation(c_sample, c_prompt, w_ada[0], b_ada)

    y_p, rec_p, conv_p, w_in_b, w_out_b, w_up_b, w_down_b = _prompt_layer(
        x_prompt, mod_p, lower_bounds, w_in[0], w_conv[0], g_onorm, w_out[0], w_up[0], w_down[0],
        g_final)
    y_s, rec_s, conv_s = _sample_layer(
        x_sample, mod_s, lower_bounds, w_in_b, w_conv[0], g_onorm, w_out_b, w_up_b, w_down_b,
        g_final, state_rec[0], state_conv[0])
    return y_p, y_s, rec_p[None], conv_p[None], rec_s[None], conv_s[None]
```
